```python
import math
import numpy as np
import jax, jax.numpy as jnp
from jax import lax

D_MODEL = 1024
BATCH = 8
SEQ = 2048
DEPTH = 1

SSM_D_INNER = 1024
SSM_HEAD_DIM = 64
SSM_HEADS = SSM_D_INNER // SSM_HEAD_DIM
SSM_GROUPS = 2
SSM_STATE = 128
SSM_CONV = 4
SSM_CHUNK = 128
SSM_XBC = SSM_D_INNER + 2 * SSM_GROUPS * SSM_STATE
SSM_NORM_EPS = 1e-5
NSA_HEADS = 16
NSA_KV_HEADS = 4
NSA_HEAD_DIM = 64
NSA_Q = NSA_HEADS * NSA_HEAD_DIM
NSA_KV = NSA_KV_HEADS * NSA_HEAD_DIM
CMP_BLOCK = 32
CMP_STRIDE = 16
CMP_HIDDEN = NSA_HEAD_DIM
SLC_BLOCK = 64
SLC_TOP_N = 16
N_LOCAL_BLOCKS = 2
FORCED_SCORE = 1e4
WINDOW = 512
SLC_Q_BLOCK = 16
WIN_Q_BLOCK = 128
ROPE_THETA = 10000.0
D_MIX = SSM_D_INNER + NSA_Q
IN_SPLITS = (SSM_D_INNER, SSM_XBC, SSM_HEADS, NSA_Q, 6 * NSA_KV, 3 * NSA_HEADS)
D_IN_PROJ = 5184
D_FF = 2816
FFN_CONV = 3
NORM_EPS = 1e-6

kernel_name = "hymba_ssd_nsa_convffn_layer"


def rms_norm(x, g, eps=NORM_EPS):
    xf = x.astype(jnp.float32)
    y = xf * lax.rsqrt(jnp.mean(xf * xf, axis=-1, keepdims=True) + eps)
    return (y * g.astype(jnp.float32)).astype(x.dtype)


def causal_dwconv(x, w, b):
    width, c = w.shape
    y = lax.conv_general_dilated(x, w[:, None, :].astype(x.dtype), window_strides=(1,),
                                 padding=[(width - 1, 0)],
                                 dimension_numbers=("NWC", "WIO", "NWC"),
                                 feature_group_count=c)
    return y + b.astype(x.dtype)


def rope(x, cos, sin):
    x1, x2 = jnp.split(x, 2, axis=-1)
    c = cos[None, :, None, :].astype(x.dtype)
    s = sin[None, :, None, :].astype(x.dtype)
    return jnp.concatenate([x1 * c - x2 * s, x2 * c + x1 * s], axis=-1)


def masked_softmax(s, mask):
    s = jnp.where(mask, s.astype(jnp.float32), -1e30)
    p = jax.nn.softmax(s, axis=-1)
    return jnp.where(mask, p, 0.0)


def ssd_mixer(z, xbc, dt, conv_w, conv_b, dt_bias, a_log, d_skip, norm_g):
    f32 = jnp.float32
    b, s, _ = z.shape
    G, R, P, N, L = SSM_GROUPS, SSM_HEADS // SSM_GROUPS, SSM_HEAD_DIM, SSM_STATE, SSM_CHUNK
    nc = s // L
    xbc = jax.nn.silu(causal_dwconv(xbc, conv_w, conv_b)).astype(f32)
    xs, bm, cm = jnp.split(xbc, [SSM_D_INNER, SSM_D_INNER + G * N], axis=-1)
    xs = xs.reshape(b, nc, L, G, R, P)
    bm = bm.reshape(b, nc, L, G, N)
    cm = cm.reshape(b, nc, L, G, N)
    dt = jax.nn.softplus(dt.astype(f32) + dt_bias.astype(f32))
    a = -jnp.exp(a_log.astype(f32))
    da = (dt * a).reshape(b, nc, L, G, R).transpose(0, 1, 3, 4, 2)
    xdt = xs * dt.reshape(b, nc, L, G, R)[..., None]
    acs = jnp.cumsum(da, axis=-1)
    tri = jnp.tril(jnp.ones((L, L), dtype=bool))
    seg = jnp.exp(jnp.where(tri, acs[..., :, None] - acs[..., None, :], -jnp.inf))
    cb = jnp.einsum("bclgn,bcsgn->bcgls", cm, bm)
    y_diag = jnp.einsum("bcgrls,bcsgrp->bclgrp", cb[:, :, :, None] * seg, xdt)
    decay_states = jnp.exp(acs[..., -1:] - acs)
    states = jnp.einsum("bclgn,bcgrl,bclgrp->bcgrpn", bm, decay_states, xdt)
    chunk_decay = jnp.exp(acs[..., -1])

    def step(h, inp):
        st, dec = inp
        return h * dec[..., None, None] + st, h

    h0 = jnp.zeros((b, G, R, P, N), f32)
    _, prev = lax.scan(step, h0, (jnp.moveaxis(states, 1, 0), jnp.moveaxis(chunk_decay, 1, 0)))
    prev = jnp.moveaxis(prev, 0, 1)
    y_off = jnp.einsum("bclgn,bcgrpn,bcgrl->bclgrp", cm, prev, jnp.exp(acs))
    y = y_diag + y_off + xs * d_skip.astype(f32).reshape(G, R, 1)
    y = y.reshape(b, s, SSM_D_INNER) * jax.nn.silu(z.astype(f32))
    yg = y.reshape(b, s, G, SSM_D_INNER // G)
    yg = yg * lax.rsqrt(jnp.mean(yg * yg, axis=-1, keepdims=True) + SSM_NORM_EPS)
    y = yg.reshape(b, s, SSM_D_INNER) * norm_g.astype(f32)
    return y.astype(z.dtype)


def nsa_mixer(q, kv, gates, cmp_k_pos, cmp_k_w1, cmp_k_b1, cmp_k_w2,
              cmp_v_pos, cmp_v_w1, cmp_v_b1, cmp_v_w2):
    f32 = jnp.float32
    b, s, _ = q.shape
    G, R, Dh = NSA_KV_HEADS, NSA_HEADS // NSA_KV_HEADS, NSA_HEAD_DIM
    dtype = q.dtype
    scale = Dh ** -0.5
    t_pos = jnp.arange(s)
    inv_freq = 1.0 / (ROPE_THETA ** (jnp.arange(0, Dh, 2, dtype=f32) / Dh))
    ang = t_pos.astype(f32)[:, None] * inv_freq[None, :]
    cos, sin = jnp.cos(ang), jnp.sin(ang)

    q = rope(q.reshape(b, s, NSA_HEADS, Dh), cos, sin)
    q = q.reshape(b, s, G, R, Dh).transpose(0, 2, 3, 1, 4)
    k_c, v_c, k_s, v_s, k_w, v_w = [t.reshape(b, s, G, Dh) for t in jnp.split(kv, 6, axis=-1)]
    k_c, k_s, k_w = rope(k_c, cos, sin), rope(k_s, cos, sin), rope(k_w, cos, sin)
    k_c, v_c, k_s, v_s, k_w, v_w = [t.transpose(0, 2, 1, 3) for t in (k_c, v_c, k_s, v_s, k_w, v_w)]

    n_cmp = (s - CMP_BLOCK) // CMP_STRIDE + 1
    blk_idx = np.arange(n_cmp)[:, None] * CMP_STRIDE + np.arange(CMP_BLOCK)[None, :]

    def compress(t, pos_emb, w1, b1, w2):
        blocks = t[:, :, blk_idx] + pos_emb.astype(t.dtype)
        h = jax.nn.silu(blocks.reshape(b, G, n_cmp, CMP_BLOCK * Dh) @ w1 + b1)
        return h @ w2

    kc = compress(k_c, cmp_k_pos, cmp_k_w1, cmp_k_b1, cmp_k_w2)
    vc = compress(v_c, cmp_v_pos, cmp_v_w1, cmp_v_b1, cmp_v_w2)
    mask_c = jnp.asarray(blk_idx[:, -1])[None, :] <= t_pos[:, None]
    s_c = jnp.einsum("bgrtd,bgnd->bgrtn", q, kc).astype(f32) * scale
    p_c = masked_softmax(s_c, mask_c)
    o_c = jnp.einsum("bgrtn,bgnd->bgrtd", p_c.astype(dtype), vc)

    n_slc = s // SLC_BLOCK
    cs = np.arange(n_cmp) * CMP_STRIDE
    ss = np.arange(n_slc) * SLC_BLOCK
    overlap = np.clip(np.minimum(cs[:, None] + CMP_BLOCK, ss[None, :] + SLC_BLOCK)
                      - np.maximum(cs[:, None], ss[None, :]), 0, None) / CMP_BLOCK
    imp = jnp.einsum("bgrtn,nj->bgtj", p_c, jnp.asarray(overlap, f32))
    j = jnp.arange(n_slc)
    cur = t_pos // SLC_BLOCK
    valid = j[None, :] * SLC_BLOCK <= t_pos[:, None]
    lag = cur[:, None] - j[None, :]
    forced = (j[None, :] == 0) | ((lag >= 0) & (lag < N_LOCAL_BLOCKS))
    score = jnp.where(forced, FORCED_SCORE, jnp.where(valid, imp, -1.0))
    k_eff = min(SLC_TOP_N, n_slc)
    top_val, top_idx = lax.top_k(score, k_eff)
    top_ok = top_val >= 0.0

    kb = k_s.reshape(b, G, n_slc, SLC_BLOCK, Dh)
    vb = v_s.reshape(b, G, n_slc, SLC_BLOCK, Dh)
    nq = s // SLC_Q_BLOCK
    q_ch = jnp.moveaxis(q.reshape(b, G, R, nq, SLC_Q_BLOCK, Dh), 3, 0)
    idx_ch = jnp.moveaxis(top_idx.reshape(b, G, nq, SLC_Q_BLOCK, k_eff), 2, 0)
    ok_ch = jnp.moveaxis(top_ok.reshape(b, G, nq, SLC_Q_BLOCK, k_eff), 2, 0)
    tq_ch = t_pos.reshape(nq, SLC_Q_BLOCK)
    gather = jax.vmap(jax.vmap(lambda blocks, idx: blocks[idx]))

    def slc_block(args):
        qb, ib, okb, tb = args
        kg = gather(kb, ib).reshape(b, G, SLC_Q_BLOCK, k_eff * SLC_BLOCK, Dh)
        vg = gather(vb, ib).reshape(b, G, SLC_Q_BLOCK, k_eff * SLC_BLOCK, Dh)
        kpos = (ib[..., None] * SLC_BLOCK + jnp.arange(SLC_BLOCK)).reshape(b, G, SLC_Q_BLOCK, -1)
        m = (kpos <= tb[None, None, :, None]) & jnp.repeat(okb, SLC_BLOCK, axis=-1)
        sc = jnp.einsum("bgrqd,bgqkd->bgrqk", qb, kg).astype(f32) * scale
        p = masked_softmax(sc, m[:, :, None])
        return jnp.einsum("bgrqk,bgqkd->bgrqd", p.astype(dtype), vg)

    o_s = lax.map(slc_block, (q_ch, idx_ch, ok_ch, tq_ch))
    o_s = jnp.moveaxis(o_s, 0, 3).reshape(b, G, R, s, Dh)

    kw = jnp.pad(k_w, ((0, 0), (0, 0), (WINDOW, 0), (0, 0)))
    vw = jnp.pad(v_w, ((0, 0), (0, 0), (WINDOW, 0), (0, 0)))
    nw = s // WIN_Q_BLOCK
    span = WINDOW + WIN_Q_BLOCK
    qw_ch = jnp.moveaxis(q.reshape(b, G, R, nw, WIN_Q_BLOCK, Dh), 3, 0)

    def win_block(args):
        qb, i = args
        start = i * WIN_Q_BLOCK
        ks = lax.dynamic_slice_in_dim(kw, start, span, axis=2)
        vs = lax.dynamic_slice_in_dim(vw, start, span, axis=2)
        tq = start + jnp.arange(WIN_Q_BLOCK)
        sk = start - WINDOW + jnp.arange(span)
        m = (sk[None, :] >= 0) & (sk[None, :] <= tq[:, None]) & (tq[:, None] - sk[None, :] < WINDOW)
        sc = jnp.einsum("bgrqd,bgkd->bgrqk", qb, ks).astype(f32) * scale
        p = masked_softmax(sc, m)
        return jnp.einsum("bgrqk,bgkd->bgrqd", p.astype(dtype), vs)

    o_w = lax.map(win_block, (qw_ch, jnp.arange(nw)))
    o_w = jnp.moveaxis(o_w, 0, 3).reshape(b, G, R, s, Dh)

    gt = jax.nn.sigmoid(gates.astype(f32)).reshape(b, s, G, R, 3).transpose(0, 2, 3, 1, 4).astype(dtype)
    o = gt[..., 0:1] * o_c + gt[..., 1:2] * o_s + gt[..., 2:3] * o_w
    return o.transpose(0, 3, 1, 2, 4).reshape(b, s, NSA_Q)


def setup_inputs(seed: int = 0) -> dict:
    key = jax.random.key(seed)
    ks = jax.random.split(key, 32)
    f32 = jnp.float32
    L = DEPTH

    def nrm(k, shape, scale):
        return jax.random.normal(k, shape, f32) * scale

    dt0 = jnp.exp(jax.random.uniform(ks[5], (L, SSM_HEADS), f32, math.log(1e-3), math.log(1e-1)))
    flat = CMP_BLOCK * NSA_HEAD_DIM
    return {
        "x": nrm(ks[0], (BATCH, SEQ, D_MODEL), 1.0),
        "norm1_g": 1.0 + nrm(ks[1], (L, D_MODEL), 0.02),
        "w_in": nrm(ks[2], (L, D_MODEL, D_IN_PROJ), D_MODEL ** -0.5),
        "ssm_conv_w": nrm(ks[3], (L, SSM_CONV, SSM_XBC), SSM_CONV ** -0.5),
        "ssm_conv_b": nrm(ks[4], (L, SSM_XBC), 0.01),
        "ssm_dt_bias": dt0 + jnp.log(-jnp.expm1(-dt0)),
        "ssm_a_log": jnp.log(jax.random.uniform(ks[6], (L, SSM_HEADS), f32, 1.0, 16.0)),
        "ssm_d": 1.0 + nrm(ks[7], (L, SSM_HEADS), 0.1),
        "ssm_norm_g": 1.0 + nrm(ks[8], (L, SSM_D_INNER), 0.02),
        "cmp_k_pos": nrm(ks[9], (L, CMP_BLOCK, NSA_HEAD_DIM), 0.02),
        "cmp_k_w1": nrm(ks[10], (L, flat, CMP_HIDDEN), flat ** -0.5),
        "cmp_k_b1": nrm(ks[11], (L, CMP_HIDDEN), 0.01),
        "cmp_k_w2": nrm(ks[12], (L, CMP_HIDDEN, NSA_HEAD_DIM), CMP_HIDDEN ** -0.5),
        "cmp_v_pos": nrm(ks[13], (L, CMP_BLOCK, NSA_HEAD_DIM), 0.02),
        "cmp_v_w1": nrm(ks[14], (L, flat, CMP_HIDDEN), flat ** -0.5),
        "cmp_v_b1": nrm(ks[15], (L, CMP_HIDDEN), 0.01),
        "cmp_v_w2": nrm(ks[16], (L, CMP_HIDDEN, NSA_HEAD_DIM), CMP_HIDDEN ** -0.5),
        "attn_norm_g": 1.0 + nrm(ks[17], (L, NSA_Q), 0.02),
        "w_out": nrm(ks[18], (L, D_MIX, D_MODEL), D_MIX ** -0.5),
        "norm2_g": 1.0 + nrm(ks[19], (L, D_MODEL), 0.02),
        "ffn_w_up": nrm(ks[20], (L, D_MODEL, 2 * D_FF), D_MODEL ** -0.5),
        "ffn_conv_w": nrm(ks[21], (L, FFN_CONV, 2 * D_FF), FFN_CONV ** -0.5),
        "ffn_conv_b": nrm(ks[22], (L, 2 * D_FF), 0.01),
        "ffn_w_down": nrm(ks[23], (L, D_FF, D_MODEL), D_FF ** -0.5),
        "final_norm_g": 1.0 + nrm(ks[24], (D_MODEL,), 0.02),
    }


def reference(x, norm1_g, w_in, ssm_conv_w, ssm_conv_b, ssm_dt_bias, ssm_a_log, ssm_d, ssm_norm_g,
              cmp_k_pos, cmp_k_w1, cmp_k_b1, cmp_k_w2, cmp_v_pos, cmp_v_w1, cmp_v_b1, cmp_v_w2,
              attn_norm_g, w_out, norm2_g, ffn_w_up, ffn_conv_w, ffn_conv_b, ffn_w_down, final_norm_g):
    split_points = np.cumsum(IN_SPLITS)[:-1].tolist()
    for l in range(DEPTH):
        h = rms_norm(x, norm1_g[l])
        proj = h @ w_in[l]
        z, xbc, dt, q, kv, gates = jnp.split(proj, split_points, axis=-1)
        y_ssm = ssd_mixer(z, xbc, dt, ssm_conv_w[l], ssm_conv_b[l], ssm_dt_bias[l],
                          ssm_a_log[l], ssm_d[l], ssm_norm_g[l])
        y_nsa = rms_norm(nsa_mixer(q, kv, gates, cmp_k_pos[l], cmp_k_w1[l], cmp_k_b1[l], cmp_k_w2[l],
                                   cmp_v_pos[l], cmp_v_w1[l], cmp_v_b1[l], cmp_v_w2[l]), attn_norm_g[l])
        x = x + jnp.concatenate([y_ssm, y_nsa], axis=-1) @ w_out[l]
        h = rms_norm(x, norm2_g[l])
        u = causal_dwconv(h @ ffn_w_up[l], ffn_conv_w[l], ffn_conv_b[l])
        ug, uv = jnp.split(u, 2, axis=-1)
        x = x + (jax.nn.silu(ug) * uv) @ ffn_w_down[l]
    return rms_norm(x, final_norm_g)
```

```python
import functools
import math

import numpy as np
import jax
import jax.numpy as jnp
from jax import lax
from jax.experimental import pallas as pl
from jax.experimental.pallas import tpu as pltpu

F32 = jnp.float32
BF16 = jnp.bfloat16

D_MODEL = 1024
SSM_D_INNER = 1024
SSM_HEAD_DIM = 64
SSM_HEADS = 16
SSM_GROUPS = 2
SSM_STATE = 128
SSM_CONV = 4
SSM_CHUNK = 128
SSM_XBC = SSM_D_INNER + 2 * SSM_GROUPS * SSM_STATE
SSM_NORM_EPS = 1e-5
NSA_HEADS = 16
NSA_KV_HEADS = 4
NSA_REP = NSA_HEADS // NSA_KV_HEADS
NSA_HEAD_DIM = 64
NSA_Q = NSA_HEADS * NSA_HEAD_DIM
NSA_KV = NSA_KV_HEADS * NSA_HEAD_DIM
CMP_BLOCK = 32
CMP_STRIDE = 16
SLC_BLOCK = 64
SLC_TOP_N = 16
N_LOCAL_BLOCKS = 2
FORCED_SCORE = 1e4
WINDOW = 512
ROPE_THETA = 10000.0
D_FF = 2816
FFN_CONV = 3
NORM_EPS = 1e-6
NEG_BIG = -1e30

LANES = 128
BF16_SUBLANES = 16
TAIL = LANES
GATE_OFF = SSM_HEADS
D_PROJ = SSM_D_INNER + SSM_XBC + NSA_Q + 6 * NSA_KV + TAIL
COL_Z, COL_XBC, COL_Q, COL_KV, COL_TAIL = 0, 1024, 2560, 3584, 5120

TM_PROJ = 256
TQ = 128
TK = 128
TM_FFN = 512
TN_FFN = 256
VMEM_LIMIT = 56 * 1024 * 1024

NT_DIMS = (((1,), (1,)), ((), ()))


def _cparams(sem):
    return pltpu.CompilerParams(dimension_semantics=sem, vmem_limit_bytes=VMEM_LIMIT)


def _rms(x, g, eps):
    return x * lax.rsqrt(jnp.mean(x * x, axis=-1, keepdims=True) + eps) * g


def _silu(x):
    return x * jax.nn.sigmoid(x)


def _inproj_kernel(x_ref, g_ref, w_ref, cos_ref, sin_ref,
                   z_ref, xbc_ref, q_ref, kc_ref, vc_ref, ks_ref, vs_ref, kw_ref, vw_ref,
                   tail_ref, tailt_ref, gates_ref):
    tm = x_ref.shape[0]
    h = _rms(x_ref[...], g_ref[...], NORM_EPS).astype(BF16)

    def mm(lo, hi):
        return jnp.dot(h, w_ref[:, lo:hi], preferred_element_type=F32)

    z_ref[...] = mm(COL_Z, COL_XBC)
    xbc_ref[...] = mm(COL_XBC, COL_Q)

    cos = cos_ref[...]
    sin = sin_ref[...]
    lane = lax.broadcasted_iota(jnp.int32, (tm, LANES), 1)
    first_half = (lane % NSA_HEAD_DIM) < (NSA_HEAD_DIM // 2)

    def rope(xc):
        partner = jnp.where(first_half, pltpu.roll(xc, LANES - 32, 1), pltpu.roll(xc, 32, 1))
        return xc * cos + partner * sin

    q = mm(COL_Q, COL_KV)
    scale = NSA_HEAD_DIM ** -0.5
    for c in range(NSA_Q // LANES):
        r = rope(q[:, c * LANES:(c + 1) * LANES]) * scale
        q_ref[2 * c] = r[:, :NSA_HEAD_DIM].astype(BF16)
        q_ref[2 * c + 1] = r[:, NSA_HEAD_DIM:].astype(BF16)

    kv = mm(COL_KV, COL_TAIL)

    def seg(i):
        return kv[:, i * NSA_KV:(i + 1) * NSA_KV]

    def rope_seg(x):
        return jnp.concatenate([rope(x[:, :LANES]), rope(x[:, LANES:])], axis=1)

    def store_heads(ref, x):
        for g in range(NSA_KV_HEADS):
            ref[g] = x[:, g * NSA_HEAD_DIM:(g + 1) * NSA_HEAD_DIM].astype(BF16)

    kc_ref[...] = rope_seg(seg(0))
    vc_ref[...] = seg(1)
    store_heads(ks_ref, rope_seg(seg(2)))
    store_heads(vs_ref, seg(3))
    store_heads(kw_ref, rope_seg(seg(4)))
    store_heads(vw_ref, seg(5))

    tail = mm(COL_TAIL, D_PROJ)
    tail_ref[...] = tail
    tailt_ref[...] = tail.T
    gsig = jax.nn.sigmoid(tail)
    for g in range(NSA_KV_HEADS):
        gates_ref[g] = pltpu.roll(gsig, LANES - GATE_OFF - 3 * NSA_REP * g, 1)


def _in_proj(x2, norm_g, w, cos_t, sin_t, seq):
    t = x2.shape[0]
    tm = TM_PROJ
    nseq = seq // tm
    row = lambda i: (i, 0)
    const = lambda i: (0, 0)
    heads = lambda i: (0, i, 0)
    out_shape = (
        jax.ShapeDtypeStruct((t, SSM_D_INNER), F32),
        jax.ShapeDtypeStruct((t, SSM_XBC), F32),
        jax.ShapeDtypeStruct((NSA_HEADS, t, NSA_HEAD_DIM), BF16),
        jax.ShapeDtypeStruct((t, NSA_KV), F32),
        jax.ShapeDtypeStruct((t, NSA_KV), F32),
        jax.ShapeDtypeStruct((NSA_KV_HEADS, t, NSA_HEAD_DIM), BF16),
        jax.ShapeDtypeStruct((NSA_KV_HEADS, t, NSA_HEAD_DIM), BF16),
        jax.ShapeDtypeStruct((NSA_KV_HEADS, t, NSA_HEAD_DIM), BF16),
        jax.ShapeDtypeStruct((NSA_KV_HEADS, t, NSA_HEAD_DIM), BF16),
        jax.ShapeDtypeStruct((t, TAIL), F32),
        jax.ShapeDtypeStruct((TAIL, t), F32),
        jax.ShapeDtypeStruct((NSA_KV_HEADS, t, LANES), F32),
    )
    kvh = pl.BlockSpec((NSA_KV_HEADS, tm, NSA_HEAD_DIM), heads)
    out_specs = (
        pl.BlockSpec((tm, SSM_D_INNER), row),
        pl.BlockSpec((tm, SSM_XBC), row),
        pl.BlockSpec((NSA_HEADS, tm, NSA_HEAD_DIM), heads),
        pl.BlockSpec((tm, NSA_KV), row),
        pl.BlockSpec((tm, NSA_KV), row),
        kvh, kvh, kvh, kvh,
        pl.BlockSpec((tm, TAIL), row),
        pl.BlockSpec((TAIL, tm), lambda i: (0, i)),
        pl.BlockSpec((NSA_KV_HEADS, tm, LANES), heads),
    )
    in_specs = [
        pl.BlockSpec((tm, D_MODEL), row),
        pl.BlockSpec((1, D_MODEL), const),
        pl.BlockSpec((D_MODEL, D_PROJ), const),
        pl.BlockSpec((tm, LANES), lambda i: (i % nseq, 0)),
        pl.BlockSpec((tm, LANES), lambda i: (i % nseq, 0)),
    ]
    return pl.pallas_call(
        _inproj_kernel, grid=(t // tm,), in_specs=in_specs, out_specs=out_specs,
        out_shape=out_shape, compiler_params=_cparams(("arbitrary",)), name="in_proj",
    )(x2, norm_g, w, cos_t, sin_t)


def _ssd_kernel(xbc_ref, z_ref, tail_ref, tailt_ref, cw_ref, cb_ref, dtb_ref, dtbt_ref,
                alog_ref, alogt_ref, dskip_ref, ng_ref,
                y_ref, xcat_ref, state_ref, ybuf_ref):
    L, P, N, H, G = SSM_CHUNK, SSM_HEAD_DIM, SSM_STATE, SSM_HEADS, SSM_GROUPS
    HG = H // G
    halo = 8

    @pl.when(pl.program_id(1) == 0)
    def _():
        xcat_ref[0:halo, :] = jnp.zeros((halo, SSM_XBC), F32)
        state_ref[...] = jnp.zeros_like(state_ref)

    xcat_ref[halo:halo + L, :] = xbc_ref[...]
    conv = cb_ref[...]
    for k in range(SSM_CONV):
        off = halo - (SSM_CONV - 1) + k
        conv = conv + cw_ref[k:k + 1, :] * xcat_ref[off:off + L, :]
    xcat_ref[0:halo, :] = xcat_ref[L:L + halo, :]
    u = _silu(conv)
    xs = u[:, :SSM_D_INNER]
    bm = u[:, SSM_D_INNER:SSM_D_INNER + G * N]
    cm = u[:, SSM_D_INNER + G * N:]

    def softplus(v):
        return jnp.maximum(v, 0.0) + jnp.log1p(jnp.exp(-jnp.abs(v)))

    dt = softplus(tail_ref[:, 0:H] + dtb_ref[...])
    dtt = softplus(tailt_ref[0:H, :] + dtbt_ref[...])
    da = dt * (-jnp.exp(alog_ref[...]))
    dat = dtt * (-jnp.exp(alogt_ref[...]))
    ri = lax.broadcasted_iota(jnp.int32, (L, L), 0)
    ci = lax.broadcasted_iota(jnp.int32, (L, L), 1)
    tri = ci <= ri
    hi = lax.Precision.HIGHEST
    acs = jnp.dot(tri.astype(F32), da, precision=hi, preferred_element_type=F32)
    acst = jnp.dot(dat, (ri <= ci).astype(F32), precision=hi, preferred_element_type=F32)
    last = acs[L - 1:L, :]
    w_state = dt * jnp.exp(last - acs)
    eacs = jnp.exp(acs)
    cdec = jnp.exp(last)

    hrow = lax.broadcasted_iota(jnp.int32, (H, SSM_D_INNER), 0)
    hcol = lax.broadcasted_iota(jnp.int32, (H, SSM_D_INNER), 1)
    expand = (hcol // P == hrow).astype(F32)

    def ex(v):
        return jnp.dot(v, expand, precision=hi, preferred_element_type=F32)

    small = jnp.concatenate([cdec, dskip_ref[...], jnp.zeros((6, H), F32)], axis=0)
    small_e = ex(small)
    cdec_e = small_e[0:1, :]
    dskip_e = small_e[1:2, :]
    xdt = (xs * ex(dt)).astype(BF16)
    wst = (xs * ex(w_state)).astype(BF16)
    eacs_e = ex(eacs)

    for g in range(G):
        bm_g = bm[:, g * N:(g + 1) * N]
        cm_g = cm[:, g * N:(g + 1) * N].astype(BF16)
        cb = lax.dot_general(cm_g, bm_g.astype(BF16), NT_DIMS, preferred_element_type=F32)
        cols = slice(g * HG * P, (g + 1) * HG * P)
        st = state_ref[:, cols]
        y_off = jnp.dot(cm_g, st.astype(BF16), preferred_element_type=F32) * eacs_e[:, cols]
        ybuf_ref[:, cols] = y_off
        bmt = bm_g.T.astype(BF16)
        state_ref[:, cols] = st * cdec_e[:, cols] + jnp.dot(bmt, wst[:, cols], preferred_element_type=F32)
        for r in range(HG):
            hh = g * HG + r
            diff = acs[:, hh:hh + 1] - acst[hh:hh + 1, :]
            seg = jnp.exp(jnp.where(tri, diff, -jnp.inf))
            lmat = (cb * seg).astype(BF16)
            hc = slice(hh * P, (hh + 1) * P)
            ybuf_ref[:, hc] = ybuf_ref[:, hc] + jnp.dot(lmat, xdt[:, hc], preferred_element_type=F32)

    y = (ybuf_ref[...] + xs * dskip_e) * _silu(z_ref[...])
    gw = SSM_D_INNER // G
    parts = []
    for g in range(G):
        yg = y[:, g * gw:(g + 1) * gw]
        parts.append(yg * lax.rsqrt(jnp.mean(yg * yg, axis=-1, keepdims=True) + SSM_NORM_EPS))
    y_ref[...] = (jnp.concatenate(parts, axis=1) * ng_ref[...]).astype(y_ref.dtype)


def _ssd(xbc, z, tail, tailt, conv_w, conv_b, dt_bias, a_log, d_skip, norm_g, batch, seq):
    t = xbc.shape[0]
    L = SSM_CHUNK
    nc = seq // L
    row = lambda b, c: (b * nc + c, 0)
    const = lambda b, c: (0, 0)
    H = SSM_HEADS
    in_specs = [
        pl.BlockSpec((L, SSM_XBC), row),
        pl.BlockSpec((L, SSM_D_INNER), row),
        pl.BlockSpec((L, TAIL), row),
        pl.BlockSpec((TAIL, L), lambda b, c: (0, b * nc + c)),
        pl.BlockSpec((SSM_CONV, SSM_XBC), const),
        pl.BlockSpec((1, SSM_XBC), const),
        pl.BlockSpec((1, H), const),
        pl.BlockSpec((H, 1), const),
        pl.BlockSpec((1, H), const),
        pl.BlockSpec((H, 1), const),
        pl.BlockSpec((1, H), const),
        pl.BlockSpec((1, SSM_D_INNER), const),
    ]
    return pl.pallas_call(
        _ssd_kernel, grid=(batch, nc), in_specs=in_specs,
        out_specs=pl.BlockSpec((L, SSM_D_INNER), row),
        out_shape=jax.ShapeDtypeStruct((t, SSM_D_INNER), BF16),
        scratch_shapes=[pltpu.VMEM((L + 8, SSM_XBC), F32),
                        pltpu.VMEM((SSM_STATE, SSM_D_INNER), F32),
                        pltpu.VMEM((L, SSM_D_INNER), F32)],
        compiler_params=_cparams(("arbitrary", "arbitrary")), name="ssd",
    )(xbc, z, tail, tailt, conv_w, conv_b, dt_bias.reshape(1, H), dt_bias.reshape(H, 1),
      a_log.reshape(1, H), a_log.reshape(H, 1), d_skip.reshape(1, H), norm_g.reshape(1, -1))


def _compress_kernel(k_ref, v_ref, kpos_ref, vpos_ref, kw1_ref, vw1_ref, kb1_ref, vb1_ref,
                     kw2_ref, vw2_ref, kc_ref, vc_ref):
    def one(t_ref, pos_ref, w1_ref, b1_ref, w2_ref, out_ref):
        t = t_ref[...]
        n = t.shape[0]
        lo = jnp.dot((t + pos_ref[0:1, :]).astype(BF16), w1_ref[0], preferred_element_type=F32)
        hi = jnp.dot((t + pos_ref[1:2, :]).astype(BF16), w1_ref[1], preferred_element_type=F32)
        hid = _silu(lo + pltpu.roll(hi, n - 1, 0) + b1_ref[...])
        out = jnp.dot(hid.astype(BF16), w2_ref[...], preferred_element_type=F32)
        for g in range(NSA_KV_HEADS):
            out_ref[g] = out[:, g * NSA_HEAD_DIM:(g + 1) * NSA_HEAD_DIM].astype(BF16)

    one(k_ref, kpos_ref, kw1_ref, kb1_ref, kw2_ref, kc_ref)
    one(v_ref, vpos_ref, vw1_ref, vb1_ref, vw2_ref, vc_ref)


def _compress_weights(pos, w1, b1, w2):
    G, D = NSA_KV_HEADS, NSA_HEAD_DIM
    eye = jnp.eye(G, dtype=F32)
    half = CMP_BLOCK // 2
    w1r = w1.reshape(2, half, D, D)
    w1big = jnp.einsum("hldj,gk->hlgdkj", w1r, eye).reshape(2, half * G * D, G * D).astype(BF16)
    posr = pos.reshape(2, half, 1, D)
    posbig = jnp.broadcast_to(posr, (2, half, G, D)).reshape(2, half * G * D)
    b1big = jnp.tile(b1.reshape(1, D), (1, G))
    w2big = jnp.einsum("dj,gk->gdkj", w2, eye).reshape(G * D, G * D).astype(BF16)
    return posbig, w1big, b1big, w2big


def _compress(kc_rope, vc, kparams, vparams, batch, seq):
    rows = seq // CMP_STRIDE
    width = CMP_STRIDE * NSA_KV
    k16 = kc_rope.reshape(batch * rows, width)
    v16 = vc.reshape(batch * rows, width)
    kpos, kw1, kb1, kw2 = _compress_weights(*kparams)
    vpos, vw1, vb1, vw2 = _compress_weights(*vparams)
    c2 = lambda b: (0, 0)
    c3 = lambda b: (0, 0, 0)
    tok = pl.BlockSpec((rows, width), lambda b: (b, 0))
    in_specs = [tok, tok,
                pl.BlockSpec((2, width), c2), pl.BlockSpec((2, width), c2),
                pl.BlockSpec((2, width, NSA_KV), c3), pl.BlockSpec((2, width, NSA_KV), c3),
                pl.BlockSpec((1, NSA_KV), c2), pl.BlockSpec((1, NSA_KV), c2),
                pl.BlockSpec((NSA_KV, NSA_KV), c2), pl.BlockSpec((NSA_KV, NSA_KV), c2)]
    out = jax.ShapeDtypeStruct((NSA_KV_HEADS, batch * rows, NSA_HEAD_DIM), BF16)
    ospec = pl.BlockSpec((NSA_KV_HEADS, rows, NSA_HEAD_DIM), lambda b: (0, b, 0))
    return pl.pallas_call(
        _compress_kernel, grid=(batch,), in_specs=in_specs, out_specs=(ospec, ospec),
        out_shape=(out, out), compiler_params=_cparams(("arbitrary",)), name="compress",
    )(k16, v16, kpos, vpos, kw1, vw1, kb1, vb1, kw2, vw2)


def _nsa_kernel(q_ref, kc_ref, vc_ref, ks_ref, vs_ref, kw_ref, vw_ref, gates_ref,
                ovt_ref, eye_ref, blk_ref, o_ref, mask_ref, m_ref, l_ref, acc_ref):
    R, D = NSA_REP, NSA_HEAD_DIM
    rows = R * TQ
    qi = pl.program_id(2)
    t0 = qi * TQ
    q = q_ref[...].reshape(rows, D)
    rowi = lax.broadcasted_iota(jnp.int32, (rows, TK), 0)
    col = lax.broadcasted_iota(jnp.int32, (rows, TK), 1)
    tpos = t0 + rowi % TQ

    n_cmp = kc_ref.shape[1] - 1
    sc = lax.dot_general(q, kc_ref[0], NT_DIMS, preferred_element_type=F32)
    mask_c = (col * CMP_STRIDE + (CMP_BLOCK - 1) <= tpos) & (col < n_cmp)
    s_m = jnp.where(mask_c, sc, NEG_BIG)
    mx = jnp.max(s_m, axis=1, keepdims=True)
    p = jnp.where(mask_c, jnp.exp(s_m - mx), 0.0)
    den = jnp.sum(p, axis=1, keepdims=True)
    pc = p / jnp.where(den > 0.0, den, 1.0)
    o_c = jnp.dot(pc.astype(BF16), vc_ref[0], preferred_element_type=F32)

    psum = pc[0:TQ]
    for r in range(1, R):
        psum = psum + pc[r * TQ:(r + 1) * TQ]
    n_slc = ovt_ref.shape[0]
    imp_t = lax.dot_general(ovt_ref[...], psum, NT_DIMS, precision=lax.Precision.HIGHEST,
                            preferred_element_type=F32)
    jj = lax.broadcasted_iota(jnp.int32, (n_slc, TQ), 0)
    tt = t0 + lax.broadcasted_iota(jnp.int32, (n_slc, TQ), 1)
    lag = tt // SLC_BLOCK - jj
    forced = (jj == 0) | ((lag >= 0) & (lag < N_LOCAL_BLOCKS))
    valid = jj * SLC_BLOCK <= tt
    score = jnp.where(forced, FORCED_SCORE, jnp.where(valid, imp_t, -1.0))
    rank = jnp.zeros((n_slc, TQ), F32)
    for j2 in range(n_slc):
        sj = score[j2:j2 + 1, :]
        beats = (sj > score) | ((sj == score) & (jj > j2))
        rank = rank + jnp.where(beats, 1.0, 0.0)
    sel_t = jnp.where((rank < float(min(SLC_TOP_N, n_slc))) & (score >= 0.0), 1.0, 0.0)
    sel = lax.dot_general(eye_ref[...], sel_t.astype(BF16), NT_DIMS, preferred_element_type=F32)
    mask_ref[...] = jnp.dot(sel.astype(BF16), blk_ref[...], preferred_element_type=F32)

    def attend(k_ref, v_ref, lo, hi, keep_fn):
        m_ref[...] = jnp.full(m_ref.shape, NEG_BIG, F32)
        l_ref[...] = jnp.zeros(l_ref.shape, F32)
        acc_ref[...] = jnp.zeros(acc_ref.shape, F32)

        def body(kb, carry):
            start = pl.multiple_of(kb * TK, TK)
            k = k_ref[0, pl.ds(start, TK), :]
            v = v_ref[0, pl.ds(start, TK), :]
            s = lax.dot_general(q, k, NT_DIMS, preferred_element_type=F32)
            keep = keep_fn(start, start + col)
            s = jnp.where(keep, s, NEG_BIG)
            m_old = m_ref[...]
            m_new = jnp.maximum(m_old, jnp.max(s, axis=1, keepdims=True))
            alpha = jnp.exp(m_old - m_new)
            pr = jnp.where(keep, jnp.exp(s - m_new), 0.0)
            l_ref[...] = alpha * l_ref[...] + jnp.sum(pr, axis=1, keepdims=True)
            acc_ref[...] = alpha * acc_ref[...] + jnp.dot(pr.astype(BF16), v, preferred_element_type=F32)
            m_ref[...] = m_new
            return carry

        lax.fori_loop(lo, hi, body, 0)
        return acc_ref[...] / l_ref[...]

    def keep_slc(start, kpos):
        me = mask_ref[:, pl.ds(start, TK)]
        me = jnp.concatenate([me] * R, axis=0)
        return (me > 0.5) & (kpos <= tpos)

    def keep_win(start, kpos):
        return (kpos <= tpos) & (tpos - kpos < WINDOW)

    o_s = attend(ks_ref, vs_ref, 0, qi + 1, keep_slc)
    o_w = attend(kw_ref, vw_ref, jnp.maximum(qi - WINDOW // TK, 0), qi + 1, keep_win)

    gt = gates_ref[0]
    outs = []
    for r in range(R):
        rs = slice(r * TQ, (r + 1) * TQ)
        outs.append(gt[:, 3 * r:3 * r + 1] * o_c[rs] + gt[:, 3 * r + 1:3 * r + 2] * o_s[rs]
                    + gt[:, 3 * r + 2:3 * r + 3] * o_w[rs])
    o_ref[...] = jnp.concatenate(outs, axis=1)


def _nsa_constants(seq):
    n_cmp = (seq - CMP_BLOCK) // CMP_STRIDE + 1
    n_slc = seq // SLC_BLOCK
    cs = np.arange(n_cmp) * CMP_STRIDE
    ss = np.arange(n_slc) * SLC_BLOCK
    overlap = np.clip(np.minimum(cs[:, None] + CMP_BLOCK, ss[None, :] + SLC_BLOCK)
                      - np.maximum(cs[:, None], ss[None, :]), 0, None) / CMP_BLOCK
    ovt = np.zeros((n_slc, n_cmp + 1), np.float32)
    ovt[:, :n_cmp] = overlap.T
    eye = np.eye(TQ, dtype=np.float32)
    blk = (np.arange(seq)[None, :] // SLC_BLOCK == np.arange(n_slc)[:, None]).astype(np.float32)
    return jnp.asarray(ovt), jnp.asarray(eye, BF16), jnp.asarray(blk, BF16)


def _nsa(q, kc, vc, ks, vs, kw, vw, gates, batch, seq):
    t = batch * seq
    G, R, D = NSA_KV_HEADS, NSA_REP, NSA_HEAD_DIM
    nq = seq // TQ
    ncr = seq // CMP_STRIDE
    ovt, eye, blk = _nsa_constants(seq)
    n_slc = seq // SLC_BLOCK
    assert ncr == TK, "compressed branch assumes one key tile"
    kvspec = pl.BlockSpec((1, seq, D), lambda b, g, i: (g, b, 0))
    cspec = pl.BlockSpec((1, ncr, D), lambda b, g, i: (g, b, 0))
    const = lambda b, g, i: (0, 0)
    in_specs = [
        pl.BlockSpec((R, TQ, D), lambda b, g, i: (g, b * nq + i, 0)),
        cspec, cspec, kvspec, kvspec, kvspec, kvspec,
        pl.BlockSpec((1, TQ, LANES), lambda b, g, i: (g, b * nq + i, 0)),
        pl.BlockSpec((n_slc, ncr), const),
        pl.BlockSpec((TQ, TQ), const),
        pl.BlockSpec((n_slc, seq), const),
    ]
    return pl.pallas_call(
        _nsa_kernel, grid=(batch, G, nq), in_specs=in_specs,
        out_specs=pl.BlockSpec((TQ, R * D), lambda b, g, i: (b * nq + i, g)),
        out_shape=jax.ShapeDtypeStruct((t, NSA_Q), F32),
        scratch_shapes=[pltpu.VMEM((TQ, seq), F32),
                        pltpu.VMEM((R * TQ, 1), F32),
                        pltpu.VMEM((R * TQ, 1), F32),
                        pltpu.VMEM((R * TQ, D), F32)],
        compiler_params=_cparams(("arbitrary", "arbitrary", "arbitrary")), name="nsa",
    )(q, kc, vc, ks, vs, kw, vw, gates, ovt, eye, blk)


def _outproj_kernel(y_ref, o_ref, x_ref, w_ref, ag_ref, ng_ref, x1_ref, h2_ref):
    yn = _rms(o_ref[...], ag_ref[...], NORM_EPS).astype(BF16)
    x1 = (x_ref[...]
          + jnp.dot(y_ref[...], w_ref[0:SSM_D_INNER, :], preferred_element_type=F32)
          + jnp.dot(yn, w_ref[SSM_D_INNER:, :], preferred_element_type=F32))
    x1_ref[...] = x1
    h2_ref[...] = _rms(x1, ng_ref[...], NORM_EPS).astype(BF16)


def _out_proj(y_ssm, o_nsa, x2, w_out, attn_g, norm2_g):
    t = x2.shape[0]
    tm = TM_PROJ
    row = lambda i: (i, 0)
    const = lambda i: (0, 0)
    tok = pl.BlockSpec((tm, D_MODEL), row)
    vec = pl.BlockSpec((1, D_MODEL), const)
    return pl.pallas_call(
        _outproj_kernel, grid=(t // tm,),
        in_specs=[tok, tok, tok, pl.BlockSpec((SSM_D_INNER + NSA_Q, D_MODEL), const), vec, vec],
        out_specs=(tok, tok),
        out_shape=(jax.ShapeDtypeStruct((t, D_MODEL), F32), jax.ShapeDtypeStruct((t, D_MODEL), BF16)),
        compiler_params=_cparams(("arbitrary",)), name="out_proj",
    )(y_ssm, o_nsa, x2, w_out, attn_g, norm2_g)


def _ffn_kernel(h_ref, halo_ref, x1_ref, wg_ref, wv_ref, cwg_ref, cwv_ref, cbg_ref, cbv_ref,
                wd_ref, fg_ref, out_ref, acc_ref, *, tiles_per_seq):
    i = pl.program_id(0)
    j = pl.program_id(1)
    tm = h_ref.shape[0]
    pad = halo_ref.shape[0]

    @pl.when(j == 0)
    def _():
        acc_ref[...] = jnp.zeros_like(acc_ref)

    halo = halo_ref[...]
    halo = jnp.where(i % tiles_per_seq == 0, jnp.zeros_like(halo), halo)
    hc = jnp.concatenate([halo, h_ref[...]], axis=0)

    def branch(w_ref, cw_ref, cb_ref):
        u = jnp.dot(hc, w_ref[...], preferred_element_type=F32)
        out = cb_ref[...]
        for k in range(FFN_CONV):
            off = pad - (FFN_CONV - 1) + k
            out = out + cw_ref[k:k + 1, :] * u[off:off + tm, :]
        return out

    act = _silu(branch(wg_ref, cwg_ref, cbg_ref)) * branch(wv_ref, cwv_ref, cbv_ref)
    acc_ref[...] += jnp.dot(act.astype(BF16), wd_ref[...], preferred_element_type=F32)

    @pl.when(j == pl.num_programs(1) - 1)
    def _():
        out_ref[...] = _rms(x1_ref[...] + acc_ref[...], fg_ref[...], NORM_EPS)


def _ffn(h2, x1, w_up, conv_w, conv_b, w_down, final_g, seq):
    t = h2.shape[0]
    tm, tn = TM_FFN, TN_FFN
    nj = D_FF // tn
    pad = BF16_SUBLANES
    tok = pl.BlockSpec((tm, D_MODEL), lambda i, j: (i, 0))
    in_specs = [
        tok,
        pl.BlockSpec((pad, D_MODEL), lambda i, j: (jnp.maximum(i * (tm // pad) - 1, 0), 0)),
        tok,
        pl.BlockSpec((D_MODEL, tn), lambda i, j: (0, j)),
        pl.BlockSpec((D_MODEL, tn), lambda i, j: (0, nj + j)),
        pl.BlockSpec((FFN_CONV, tn), lambda i, j: (0, j)),
        pl.BlockSpec((FFN_CONV, tn), lambda i, j: (0, nj + j)),
        pl.BlockSpec((1, tn), lambda i, j: (0, j)),
        pl.BlockSpec((1, tn), lambda i, j: (0, nj + j)),
        pl.BlockSpec((tn, D_MODEL), lambda i, j: (j, 0)),
        pl.BlockSpec((1, D_MODEL), lambda i, j: (0, 0)),
    ]
    return pl.pallas_call(
        functools.partial(_ffn_kernel, tiles_per_seq=seq // tm), grid=(t // tm, nj),
        in_specs=in_specs, out_specs=tok,
        out_shape=jax.ShapeDtypeStruct((t, D_MODEL), F32),
        scratch_shapes=[pltpu.VMEM((tm, D_MODEL), F32)],
        compiler_params=_cparams(("arbitrary", "arbitrary")), name="ffn",
    )(h2, h2, x1, w_up, w_up, conv_w, conv_w, conv_b, conv_b, w_down, final_g)


def _rope_tables(seq):
    half = NSA_HEAD_DIM // 2
    inv_freq = 1.0 / (ROPE_THETA ** (jnp.arange(0, NSA_HEAD_DIM, 2, dtype=F32) / NSA_HEAD_DIM))
    ang = jnp.arange(seq).astype(F32)[:, None] * inv_freq[None, :]
    cos, sin = jnp.cos(ang), jnp.sin(ang)
    reps = LANES // NSA_HEAD_DIM
    cos_t = jnp.tile(jnp.concatenate([cos, cos], axis=1), (1, reps))
    sin_t = jnp.tile(jnp.concatenate([-sin, sin], axis=1), (1, reps))
    return cos_t, sin_t


def _arrange_w_in(w):
    o_z, o_xbc, o_dt, o_q, o_kv, o_g = np.cumsum([0, SSM_D_INNER, SSM_XBC, SSM_HEADS, NSA_Q, 6 * NSA_KV]).tolist()
    end = o_g + 3 * NSA_HEADS
    padw = TAIL - SSM_HEADS - 3 * NSA_HEADS
    return jnp.concatenate([w[:, o_z:o_dt], w[:, o_q:o_g], w[:, o_dt:o_q], w[:, o_g:end],
                            jnp.zeros((w.shape[0], padw), w.dtype)], axis=1).astype(BF16)


def kernel(x, norm1_g, w_in, ssm_conv_w, ssm_conv_b, ssm_dt_bias, ssm_a_log, ssm_d, ssm_norm_g, cmp_k_pos, cmp_k_w1, cmp_k_b1, cmp_k_w2, cmp_v_pos, cmp_v_w1, cmp_v_b1, cmp_v_w2, attn_norm_g, w_out, norm2_g, ffn_w_up, ffn_conv_w, ffn_conv_b, ffn_w_down, final_norm_g):
    batch, seq, d = x.shape
    assert w_in.shape[0] == 1, "single-layer problem"
    l = 0
    cos_t, sin_t = _rope_tables(seq)
    x2 = x.reshape(batch * seq, d)
    z, xbc, q, kc_r, vc_r, ks, vs, kw, vw, tail, tailt, gates = _in_proj(
        x2, norm1_g[l].reshape(1, d), _arrange_w_in(w_in[l]), cos_t, sin_t, seq)
    y_ssm = _ssd(xbc, z, tail, tailt, ssm_conv_w[l], ssm_conv_b[l].reshape(1, -1), ssm_dt_bias[l],
                 ssm_a_log[l], ssm_d[l], ssm_norm_g[l], batch, seq)
    kc, vc = _compress(kc_r, vc_r,
                       (cmp_k_pos[l], cmp_k_w1[l], cmp_k_b1[l], cmp_k_w2[l]),
                       (cmp_v_pos[l], cmp_v_w1[l], cmp_v_b1[l], cmp_v_w2[l]), batch, seq)
    o_nsa = _nsa(q, kc, vc, ks, vs, kw, vw, gates, batch, seq)
    x1, h2 = _out_proj(y_ssm, o_nsa, x2, w_out[l].astype(BF16), attn_norm_g[l].reshape(1, d),
                       norm2_g[l].reshape(1, d))
    out = _ffn(h2, x1, ffn_w_up[l].astype(BF16), ffn_conv_w[l], ffn_conv_b[l].reshape(1, -1),
               ffn_w_down[l].astype(BF16), final_norm_g.reshape(1, d), seq)
    return out.reshape(batch, seq, d)
```

```python
import functools
import math

import numpy as np
import jax
import jax.numpy as jnp
from jax import lax
from jax.experimental import pallas as pl
from jax.experimental.pallas import tpu as pltpu

F32 = jnp.float32
BF16 = jnp.bfloat16

D_MODEL = 1024
SSM_D_INNER = 1024
SSM_HEAD_DIM = 64
SSM_HEADS = 16
SSM_GROUPS = 2
SSM_STATE = 128
SSM_CONV = 4
SSM_CHUNK = 128
SSM_XBC = SSM_D_INNER + 2 * SSM_GROUPS * SSM_STATE
SSM_NORM_EPS = 1e-5
NSA_HEADS = 16
NSA_KV_HEADS = 4
NSA_REP = NSA_HEADS // NSA_KV_HEADS
NSA_HEAD_DIM = 64
NSA_Q = NSA_HEADS * NSA_HEAD_DIM
NSA_KV = NSA_KV_HEADS * NSA_HEAD_DIM
CMP_BLOCK = 32
CMP_STRIDE = 16
SLC_BLOCK = 64
SLC_TOP_N = 16
N_LOCAL_BLOCKS = 2
FORCED_SCORE = 1e4
WINDOW = 512
ROPE_THETA = 10000.0
D_FF = 2816
FFN_CONV = 3
NORM_EPS = 1e-6
NEG_BIG = -1e30

LANES = 128
BF16_SUBLANES = 16
TAIL = LANES
GATE_OFF = SSM_HEADS
GATE_SLOT = 16
D_PROJ = SSM_D_INNER + SSM_XBC + NSA_Q + 6 * NSA_KV + TAIL
COL_Z, COL_XBC, COL_Q, COL_KV, COL_TAIL = 0, 1024, 2560, 3584, 5120

TM_PROJ = 256
TQ = 256
TK = 256
ONES_ROWS = 16
TM_FFN = 512
TN_FFN = 256
VMEM_LIMIT = 56 * 1024 * 1024

NT_DIMS = (((1,), (1,)), ((), ()))


def _cparams(sem):
    return pltpu.CompilerParams(dimension_semantics=sem, vmem_limit_bytes=VMEM_LIMIT)


def _rms(x, g, eps):
    return x * lax.rsqrt(jnp.mean(x * x, axis=-1, keepdims=True) + eps) * g


def _silu(x):
    return x * jax.nn.sigmoid(x)


def _inproj_kernel(x_ref, g_ref, w_ref, cos_ref, sin_ref,
                   z_ref, xbc_ref, qt_ref, kc_ref, vc_ref, ks_ref, vst_ref, kw_ref, vwt_ref,
                   tail_ref, tailt_ref):
    tm = x_ref.shape[0]
    h = _rms(x_ref[...], g_ref[...], NORM_EPS).astype(BF16)

    def mm(lo, hi):
        return jnp.dot(h, w_ref[:, lo:hi], preferred_element_type=F32)

    z_ref[...] = mm(COL_Z, COL_XBC)
    xbc_ref[...] = mm(COL_XBC, COL_Q)

    cos = cos_ref[...]
    sin = sin_ref[...]
    lane = lax.broadcasted_iota(jnp.int32, (tm, LANES), 1)
    first_half = (lane % NSA_HEAD_DIM) < (NSA_HEAD_DIM // 2)

    def rope(xc):
        partner = jnp.where(first_half, pltpu.roll(xc, LANES - 32, 1), pltpu.roll(xc, 32, 1))
        return xc * cos + partner * sin

    q = mm(COL_Q, COL_KV)
    scale = NSA_HEAD_DIM ** -0.5
    for c in range(NSA_Q // LANES):
        rt = (rope(q[:, c * LANES:(c + 1) * LANES]) * scale).T.astype(BF16)
        qt_ref[2 * c] = rt[:NSA_HEAD_DIM]
        qt_ref[2 * c + 1] = rt[NSA_HEAD_DIM:]

    kv = mm(COL_KV, COL_TAIL)

    def seg(i):
        return kv[:, i * NSA_KV:(i + 1) * NSA_KV]

    def rope_seg(x):
        return jnp.concatenate([rope(x[:, :LANES]), rope(x[:, LANES:])], axis=1)

    def store_heads(ref, x):
        for g in range(NSA_KV_HEADS):
            ref[g] = x[:, g * NSA_HEAD_DIM:(g + 1) * NSA_HEAD_DIM].astype(BF16)

    def store_heads_t(ref, x):
        xt = x.T.astype(BF16)
        for g in range(NSA_KV_HEADS):
            ref[g] = xt[g * NSA_HEAD_DIM:(g + 1) * NSA_HEAD_DIM]

    kc_ref[...] = rope_seg(seg(0))
    vc_ref[...] = seg(1)
    store_heads(ks_ref, rope_seg(seg(2)))
    store_heads_t(vst_ref, seg(3))
    store_heads(kw_ref, rope_seg(seg(4)))
    store_heads_t(vwt_ref, seg(5))

    tail = mm(COL_TAIL, D_PROJ)
    tail_ref[...] = tail
    tailt_ref[...] = tail.T


def _in_proj(x2, norm_g, w, cos_t, sin_t, seq):
    t = x2.shape[0]
    tm = TM_PROJ
    nseq = seq // tm
    row = lambda i: (i, 0)
    const = lambda i: (0, 0)
    heads = lambda i: (0, i, 0)
    heads_t = lambda i: (0, 0, i)
    out_shape = (
        jax.ShapeDtypeStruct((t, SSM_D_INNER), F32),
        jax.ShapeDtypeStruct((t, SSM_XBC), F32),
        jax.ShapeDtypeStruct((NSA_HEADS, NSA_HEAD_DIM, t), BF16),
        jax.ShapeDtypeStruct((t, NSA_KV), F32),
        jax.ShapeDtypeStruct((t, NSA_KV), F32),
        jax.ShapeDtypeStruct((NSA_KV_HEADS, t, NSA_HEAD_DIM), BF16),
        jax.ShapeDtypeStruct((NSA_KV_HEADS, NSA_HEAD_DIM, t), BF16),
        jax.ShapeDtypeStruct((NSA_KV_HEADS, t, NSA_HEAD_DIM), BF16),
        jax.ShapeDtypeStruct((NSA_KV_HEADS, NSA_HEAD_DIM, t), BF16),
        jax.ShapeDtypeStruct((t, TAIL), F32),
        jax.ShapeDtypeStruct((TAIL, t), F32),
    )
    kvh = pl.BlockSpec((NSA_KV_HEADS, tm, NSA_HEAD_DIM), heads)
    kvh_t = pl.BlockSpec((NSA_KV_HEADS, NSA_HEAD_DIM, tm), heads_t)
    out_specs = (
        pl.BlockSpec((tm, SSM_D_INNER), row),
        pl.BlockSpec((tm, SSM_XBC), row),
        pl.BlockSpec((NSA_HEADS, NSA_HEAD_DIM, tm), heads_t),
        pl.BlockSpec((tm, NSA_KV), row),
        pl.BlockSpec((tm, NSA_KV), row),
        kvh, kvh_t, kvh, kvh_t,
        pl.BlockSpec((tm, TAIL), row),
        pl.BlockSpec((TAIL, tm), lambda i: (0, i)),
    )
    in_specs = [
        pl.BlockSpec((tm, D_MODEL), row),
        pl.BlockSpec((1, D_MODEL), const),
        pl.BlockSpec((D_MODEL, D_PROJ), const),
        pl.BlockSpec((tm, LANES), lambda i: (i % nseq, 0)),
        pl.BlockSpec((tm, LANES), lambda i: (i % nseq, 0)),
    ]
    return pl.pallas_call(
        _inproj_kernel, grid=(t // tm,), in_specs=in_specs, out_specs=out_specs,
        out_shape=out_shape, compiler_params=_cparams(("arbitrary",)), name="in_proj",
    )(x2, norm_g, w, cos_t, sin_t)


def _ssd_kernel(xbc_ref, z_ref, tail_ref, tailt_ref, cw_ref, cb_ref, dtb_ref, dtbt_ref,
                alog_ref, alogt_ref, dskip_ref, ng_ref,
                y_ref, xcat_ref, state_ref, ybuf_ref):
    L, P, N, H, G = SSM_CHUNK, SSM_HEAD_DIM, SSM_STATE, SSM_HEADS, SSM_GROUPS
    HG = H // G
    halo = 8

    @pl.when(pl.program_id(1) == 0)
    def _():
        xcat_ref[0:halo, :] = jnp.zeros((halo, SSM_XBC), F32)
        state_ref[...] = jnp.zeros_like(state_ref)

    xcat_ref[halo:halo + L, :] = xbc_ref[...]
    conv = cb_ref[...]
    for k in range(SSM_CONV):
        off = halo - (SSM_CONV - 1) + k
        conv = conv + cw_ref[k:k + 1, :] * xcat_ref[off:off + L, :]
    xcat_ref[0:halo, :] = xcat_ref[L:L + halo, :]
    u = _silu(conv)
    xs = u[:, :SSM_D_INNER]
    bm = u[:, SSM_D_INNER:SSM_D_INNER + G * N]
    cm = u[:, SSM_D_INNER + G * N:]

    def softplus(v):
        return jnp.maximum(v, 0.0) + jnp.log1p(jnp.exp(-jnp.abs(v)))

    dt = softplus(tail_ref[:, 0:H] + dtb_ref[...])
    dtt = softplus(tailt_ref[0:H, :] + dtbt_ref[...])
    da = dt * (-jnp.exp(alog_ref[...]))
    dat = dtt * (-jnp.exp(alogt_ref[...]))
    ri = lax.broadcasted_iota(jnp.int32, (L, L), 0)
    ci = lax.broadcasted_iota(jnp.int32, (L, L), 1)
    tri = ci <= ri
    hi = lax.Precision.HIGHEST
    acs = jnp.dot(tri.astype(F32), da, precision=hi, preferred_element_type=F32)
    acst = jnp.dot(dat, (ri <= ci).astype(F32), precision=hi, preferred_element_type=F32)
    last = acs[L - 1:L, :]
    w_state = dt * jnp.exp(last - acs)
    eacs = jnp.exp(acs)
    cdec = jnp.exp(last)

    hrow = lax.broadcasted_iota(jnp.int32, (H, SSM_D_INNER), 0)
    hcol = lax.broadcasted_iota(jnp.int32, (H, SSM_D_INNER), 1)
    expand = (hcol // P == hrow).astype(F32)

    def ex(v):
        return jnp.dot(v, expand, precision=hi, preferred_element_type=F32)

    small = jnp.concatenate([cdec, dskip_ref[...], jnp.zeros((6, H), F32)], axis=0)
    small_e = ex(small)
    cdec_e = small_e[0:1, :]
    dskip_e = small_e[1:2, :]
    xdt = (xs * ex(dt)).astype(BF16)
    wst = (xs * ex(w_state)).astype(BF16)
    eacs_e = ex(eacs)

    for g in range(G):
        bm_g = bm[:, g * N:(g + 1) * N]
        cm_g = cm[:, g * N:(g + 1) * N].astype(BF16)
        cb = lax.dot_general(cm_g, bm_g.astype(BF16), NT_DIMS, preferred_element_type=F32)
        cols = slice(g * HG * P, (g + 1) * HG * P)
        st = state_ref[:, cols]
        y_off = jnp.dot(cm_g, st.astype(BF16), preferred_element_type=F32) * eacs_e[:, cols]
        ybuf_ref[:, cols] = y_off
        bmt = bm_g.T.astype(BF16)
        state_ref[:, cols] = st * cdec_e[:, cols] + jnp.dot(bmt, wst[:, cols], preferred_element_type=F32)
        for r in range(HG):
            hh = g * HG + r
            diff = acs[:, hh:hh + 1] - acst[hh:hh + 1, :]
            seg = jnp.exp(jnp.where(tri, diff, -jnp.inf))
            lmat = (cb * seg).astype(BF16)
            hc = slice(hh * P, (hh + 1) * P)
            ybuf_ref[:, hc] = ybuf_ref[:, hc] + jnp.dot(lmat, xdt[:, hc], preferred_element_type=F32)

    y = (ybuf_ref[...] + xs * dskip_e) * _silu(z_ref[...])
    gw = SSM_D_INNER // G
    parts = []
    for g in range(G):
        yg = y[:, g * gw:(g + 1) * gw]
        parts.append(yg * lax.rsqrt(jnp.mean(yg * yg, axis=-1, keepdims=True) + SSM_NORM_EPS))
    y_ref[...] = (jnp.concatenate(parts, axis=1) * ng_ref[...]).astype(y_ref.dtype)


def _ssd(xbc, z, tail, tailt, conv_w, conv_b, dt_bias, a_log, d_skip, norm_g, batch, seq):
    t = xbc.shape[0]
    L = SSM_CHUNK
    nc = seq // L
    row = lambda b, c: (b * nc + c, 0)
    const = lambda b, c: (0, 0)
    H = SSM_HEADS
    in_specs = [
        pl.BlockSpec((L, SSM_XBC), row),
        pl.BlockSpec((L, SSM_D_INNER), row),
        pl.BlockSpec((L, TAIL), row),
        pl.BlockSpec((TAIL, L), lambda b, c: (0, b * nc + c)),
        pl.BlockSpec((SSM_CONV, SSM_XBC), const),
        pl.BlockSpec((1, SSM_XBC), const),
        pl.BlockSpec((1, H), const),
        pl.BlockSpec((H, 1), const),
        pl.BlockSpec((1, H), const),
        pl.BlockSpec((H, 1), const),
        pl.BlockSpec((1, H), const),
        pl.BlockSpec((1, SSM_D_INNER), const),
    ]
    return pl.pallas_call(
        _ssd_kernel, grid=(batch, nc), in_specs=in_specs,
        out_specs=pl.BlockSpec((L, SSM_D_INNER), row),
        out_shape=jax.ShapeDtypeStruct((t, SSM_D_INNER), BF16),
        scratch_shapes=[pltpu.VMEM((L + 8, SSM_XBC), F32),
                        pltpu.VMEM((SSM_STATE, SSM_D_INNER), F32),
                        pltpu.VMEM((L, SSM_D_INNER), F32)],
        compiler_params=_cparams(("arbitrary", "arbitrary")), name="ssd",
    )(xbc, z, tail, tailt, conv_w, conv_b, dt_bias.reshape(1, H), dt_bias.reshape(H, 1),
      a_log.reshape(1, H), a_log.reshape(H, 1), d_skip.reshape(1, H), norm_g.reshape(1, -1))


def _compress_kernel(k_ref, v_ref, kpos_ref, vpos_ref, kw1_ref, vw1_ref, kb1_ref, vb1_ref,
                     kw2_ref, vw2_ref, kc_ref, vct_ref):
    def hidden(t_ref, pos_ref, w1_ref, b1_ref):
        t = t_ref[...]
        n = t.shape[0]
        lo = jnp.dot((t + pos_ref[0:1, :]).astype(BF16), w1_ref[0], preferred_element_type=F32)
        hi = jnp.dot((t + pos_ref[1:2, :]).astype(BF16), w1_ref[1], preferred_element_type=F32)
        return _silu(lo + pltpu.roll(hi, n - 1, 0) + b1_ref[...]).astype(BF16)

    kc = jnp.dot(hidden(k_ref, kpos_ref, kw1_ref, kb1_ref), kw2_ref[...], preferred_element_type=F32)
    vct = lax.dot_general(vw2_ref[...], hidden(v_ref, vpos_ref, vw1_ref, vb1_ref), NT_DIMS,
                          preferred_element_type=F32).astype(BF16)
    for g in range(NSA_KV_HEADS):
        hs = slice(g * NSA_HEAD_DIM, (g + 1) * NSA_HEAD_DIM)
        kc_ref[g] = kc[:, hs].astype(BF16)
        vct_ref[g] = vct[hs]


def _compress_weights(pos, w1, b1, w2):
    G, D = NSA_KV_HEADS, NSA_HEAD_DIM
    eye = jnp.eye(G, dtype=F32)
    half = CMP_BLOCK // 2
    w1r = w1.reshape(2, half, D, D)
    w1big = jnp.einsum("hldj,gk->hlgdkj", w1r, eye).reshape(2, half * G * D, G * D).astype(BF16)
    posr = pos.reshape(2, half, 1, D)
    posbig = jnp.broadcast_to(posr, (2, half, G, D)).reshape(2, half * G * D)
    b1big = jnp.tile(b1.reshape(1, D), (1, G))
    w2big = jnp.einsum("dj,gk->gdkj", w2, eye).reshape(G * D, G * D).astype(BF16)
    return posbig, w1big, b1big, w2big


def _compress(kc_rope, vc, kparams, vparams, batch, seq):
    rows = seq // CMP_STRIDE
    width = CMP_STRIDE * NSA_KV
    k16 = kc_rope.reshape(batch * rows, width)
    v16 = vc.reshape(batch * rows, width)
    kpos, kw1, kb1, kw2 = _compress_weights(*kparams)
    vpos, vw1, vb1, vw2 = _compress_weights(*vparams)
    vw2 = vw2.T
    c2 = lambda b: (0, 0)
    c3 = lambda b: (0, 0, 0)
    tok = pl.BlockSpec((rows, width), lambda b: (b, 0))
    in_specs = [tok, tok,
                pl.BlockSpec((2, width), c2), pl.BlockSpec((2, width), c2),
                pl.BlockSpec((2, width, NSA_KV), c3), pl.BlockSpec((2, width, NSA_KV), c3),
                pl.BlockSpec((1, NSA_KV), c2), pl.BlockSpec((1, NSA_KV), c2),
                pl.BlockSpec((NSA_KV, NSA_KV), c2), pl.BlockSpec((NSA_KV, NSA_KV), c2)]
    out = jax.ShapeDtypeStruct((NSA_KV_HEADS, batch * rows, NSA_HEAD_DIM), BF16)
    out_t = jax.ShapeDtypeStruct((NSA_KV_HEADS, NSA_HEAD_DIM, batch * rows), BF16)
    ospec = pl.BlockSpec((NSA_KV_HEADS, rows, NSA_HEAD_DIM), lambda b: (0, b, 0))
    ospec_t = pl.BlockSpec((NSA_KV_HEADS, NSA_HEAD_DIM, rows), lambda b: (0, 0, b))
    return pl.pallas_call(
        _compress_kernel, grid=(batch,), in_specs=in_specs, out_specs=(ospec, ospec_t),
        out_shape=(out, out_t), compiler_params=_cparams(("arbitrary",)), name="compress",
    )(k16, v16, kpos, vpos, kw1, vw1, kb1, vb1, kw2, vw2)


def _nsa_kernel(qt_ref, kc_ref, vct_ref, ks_ref, vst_ref, kw_ref, vwt_ref, tailt_ref, ovt_ref,
                o_ref, bias_ref, m_ref, acc_ref):
    R, D = NSA_REP, NSA_HEAD_DIM
    nl = R * TQ
    g = pl.program_id(1)
    qi = pl.program_id(2)
    t0 = qi * TQ
    qt = jnp.concatenate([qt_ref[r] for r in range(R)], axis=1)
    tpos_row = t0 + lax.broadcasted_iota(jnp.int32, (1, nl), 1) % TQ

    ncr = kc_ref.shape[1]
    sc = jnp.dot(kc_ref[0], qt, preferred_element_type=F32)
    ni = lax.broadcasted_iota(jnp.int32, (ncr, nl), 0)
    mask_c = (ni * CMP_STRIDE + (CMP_BLOCK - 1) <= tpos_row) & (ni < ncr - 1)
    s_m = jnp.where(mask_c, sc, NEG_BIG)
    mx = jnp.max(s_m, axis=0, keepdims=True)
    p = jnp.where(mask_c, jnp.exp(s_m - mx), 0.0)
    den = jnp.sum(p, axis=0, keepdims=True)
    pc = p / jnp.where(den > 0.0, den, 1.0)
    o_c = jnp.dot(vct_ref[0], pc.astype(BF16), preferred_element_type=F32)

    psum = pc[:, 0:TQ]
    for r in range(1, R):
        psum = psum + pc[:, r * TQ:(r + 1) * TQ]
    n_slc = ovt_ref.shape[0]
    imp_t = jnp.dot(ovt_ref[...], psum, precision=lax.Precision.HIGHEST,
                    preferred_element_type=F32)
    jj = lax.broadcasted_iota(jnp.int32, (n_slc, TQ), 0)
    tt = t0 + lax.broadcasted_iota(jnp.int32, (n_slc, TQ), 1)
    lag = tt // SLC_BLOCK - jj
    forced = (jj == 0) | ((lag >= 0) & (lag < N_LOCAL_BLOCKS))
    valid = jj * SLC_BLOCK <= tt
    score = jnp.where(forced, FORCED_SCORE, jnp.where(valid, imp_t, -1.0))
    rank = jnp.zeros((n_slc, TQ), F32)
    for j2 in range(n_slc):
        sj = score[j2:j2 + 1, :]
        beats = (sj > score) | ((sj == score) & (jj > j2))
        rank = rank + jnp.where(beats, 1.0, 0.0)
    selected = (rank < float(min(SLC_TOP_N, n_slc))) & (score >= 0.0)
    bias_t = jnp.where(selected, 0.0, NEG_BIG)
    bias_ref[...] = jnp.concatenate([bias_t] * R, axis=1)

    ones = jnp.ones((ONES_ROWS, TK), BF16)
    ki = lax.broadcasted_iota(jnp.int32, (TK, nl), 0)

    def scores(k_ref, kb):
        start = pl.multiple_of(kb * TK, TK)
        return jnp.dot(k_ref[0, pl.ds(start, TK), :], qt, preferred_element_type=F32)

    def pv(vt_ref, kb, p):
        start = pl.multiple_of(kb * TK, TK)
        vt1 = jnp.concatenate([vt_ref[0, :, pl.ds(start, TK)], ones], axis=0)
        return jnp.dot(vt1, p.astype(BF16), preferred_element_type=F32)

    def first_chunk(vt_ref, kb, s):
        m = jnp.max(s, axis=0, keepdims=True)
        m_ref[...] = m
        acc_ref[...] = pv(vt_ref, kb, jnp.exp(s - m))

    def next_chunk(vt_ref, kb, s):
        m_old = m_ref[...]
        m_new = jnp.maximum(m_old, jnp.max(s, axis=0, keepdims=True))
        acc_ref[...] = jnp.exp(m_old - m_new) * acc_ref[...] + pv(vt_ref, kb, jnp.exp(s - m_new))
        m_ref[...] = m_new

    def result():
        acc = acc_ref[...]
        return acc[0:D] / acc[D:D + 1]

    def slc_bias(kb):
        per = TK // SLC_BLOCK
        rows = [jnp.broadcast_to(bias_ref[pl.ds(kb * per + j, 1), :], (SLC_BLOCK, nl)) for j in range(per)]
        return jnp.concatenate(rows, axis=0)

    causal = jnp.where(t0 + ki <= tpos_row, 0.0, NEG_BIG)

    first_chunk(vst_ref, qi, scores(ks_ref, qi) + slc_bias(qi) + causal)

    def slc_body(kb, carry):
        next_chunk(vst_ref, kb, scores(ks_ref, kb) + slc_bias(kb))
        return carry

    lax.fori_loop(0, qi, slc_body, 0)
    o_s = result()

    first_chunk(vwt_ref, qi, scores(kw_ref, qi) + causal)

    @pl.when(qi >= 1)
    def _():
        next_chunk(vwt_ref, qi - 1, scores(kw_ref, qi - 1))

    @pl.when(qi >= 2)
    def _():
        far = jnp.where(tpos_row - (t0 - 2 * TK + ki) < WINDOW, 0.0, NEG_BIG)
        next_chunk(vwt_ref, qi - 2, scores(kw_ref, qi - 2) + far)

    o_w = result()

    gate_row = pl.multiple_of(GATE_OFF + GATE_SLOT * g, 8)
    sig = jax.nn.sigmoid(tailt_ref[pl.ds(gate_row, GATE_SLOT), :])

    def gate(c):
        return jnp.concatenate([sig[3 * r + c:3 * r + c + 1, :] for r in range(R)], axis=1)

    ot = gate(0) * o_c + gate(1) * o_s + gate(2) * o_w
    stacked = jnp.concatenate([ot[:, r * TQ:(r + 1) * TQ] for r in range(R)], axis=0)
    o_ref[...] = stacked.T


def _overlap_t(seq):
    n_cmp = (seq - CMP_BLOCK) // CMP_STRIDE + 1
    n_slc = seq // SLC_BLOCK
    cs = np.arange(n_cmp) * CMP_STRIDE
    ss = np.arange(n_slc) * SLC_BLOCK
    overlap = np.clip(np.minimum(cs[:, None] + CMP_BLOCK, ss[None, :] + SLC_BLOCK)
                      - np.maximum(cs[:, None], ss[None, :]), 0, None) / CMP_BLOCK
    ovt = np.zeros((n_slc, n_cmp + 1), np.float32)
    ovt[:, :n_cmp] = overlap.T
    return jnp.asarray(ovt)


def _nsa(qt, kc, vct, ks, vst, kw, vwt, tailt, batch, seq):
    t = batch * seq
    G, R, D = NSA_KV_HEADS, NSA_REP, NSA_HEAD_DIM
    nq = seq // TQ
    ncr = seq // CMP_STRIDE
    n_slc = seq // SLC_BLOCK
    assert TQ == TK and WINDOW == 2 * TK, "window branch visits exactly three key chunks"
    kspec = pl.BlockSpec((1, seq, D), lambda b, g, i: (g, b, 0))
    vtspec = pl.BlockSpec((1, D, seq), lambda b, g, i: (g, 0, b))
    in_specs = [
        pl.BlockSpec((R, D, TQ), lambda b, g, i: (g, 0, b * nq + i)),
        pl.BlockSpec((1, ncr, D), lambda b, g, i: (g, b, 0)),
        pl.BlockSpec((1, D, ncr), lambda b, g, i: (g, 0, b)),
        kspec, vtspec, kspec, vtspec,
        pl.BlockSpec((TAIL, TQ), lambda b, g, i: (0, b * nq + i)),
        pl.BlockSpec((n_slc, ncr), lambda b, g, i: (0, 0)),
    ]
    return pl.pallas_call(
        _nsa_kernel, grid=(batch, G, nq), in_specs=in_specs,
        out_specs=pl.BlockSpec((TQ, R * D), lambda b, g, i: (b * nq + i, g)),
        out_shape=jax.ShapeDtypeStruct((t, NSA_Q), F32),
        scratch_shapes=[pltpu.VMEM((n_slc, R * TQ), F32),
                        pltpu.VMEM((1, R * TQ), F32),
                        pltpu.VMEM((D + ONES_ROWS, R * TQ), F32)],
        compiler_params=_cparams(("arbitrary", "arbitrary", "arbitrary")), name="nsa",
    )(qt, kc, vct, ks, vst, kw, vwt, tailt, _overlap_t(seq))


def _outproj_kernel(y_ref, o_ref, x_ref, w_ref, ag_ref, ng_ref, x1_ref, h2_ref):
    yn = _rms(o_ref[...], ag_ref[...], NORM_EPS).astype(BF16)
    x1 = (x_ref[...]
          + jnp.dot(y_ref[...], w_ref[0:SSM_D_INNER, :], preferred_element_type=F32)
          + jnp.dot(yn, w_ref[SSM_D_INNER:, :], preferred_element_type=F32))
    x1_ref[...] = x1
    h2_ref[...] = _rms(x1, ng_ref[...], NORM_EPS).astype(BF16)


def _out_proj(y_ssm, o_nsa, x2, w_out, attn_g, norm2_g):
    t = x2.shape[0]
    tm = TM_PROJ
    row = lambda i: (i, 0)
    const = lambda i: (0, 0)
    tok = pl.BlockSpec((tm, D_MODEL), row)
    vec = pl.BlockSpec((1, D_MODEL), const)
    return pl.pallas_call(
        _outproj_kernel, grid=(t // tm,),
        in_specs=[tok, tok, tok, pl.BlockSpec((SSM_D_INNER + NSA_Q, D_MODEL), const), vec, vec],
        out_specs=(tok, tok),
        out_shape=(jax.ShapeDtypeStruct((t, D_MODEL), F32), jax.ShapeDtypeStruct((t, D_MODEL), BF16)),
        compiler_params=_cparams(("arbitrary",)), name="out_proj",
    )(y_ssm, o_nsa, x2, w_out, attn_g, norm2_g)


def _ffn_kernel(h_ref, halo_ref, x1_ref, wg_ref, wv_ref, cwg_ref, cwv_ref, cbg_ref, cbv_ref,
                wd_ref, fg_ref, out_ref, acc_ref, *, tiles_per_seq):
    i = pl.program_id(0)
    j = pl.program_id(1)
    tm = h_ref.shape[0]
    pad = halo_ref.shape[0]

    @pl.when(j == 0)
    def _():
        acc_ref[...] = jnp.zeros_like(acc_ref)

    halo = halo_ref[...]
    halo = jnp.where(i % tiles_per_seq == 0, jnp.zeros_like(halo), halo)
    hc = jnp.concatenate([halo, h_ref[...]], axis=0)

    def branch(w_ref, cw_ref, cb_ref):
        u = jnp.dot(hc, w_ref[...], preferred_element_type=F32)
        out = cb_ref[...]
        for k in range(FFN_CONV):
            off = pad - (FFN_CONV - 1) + k
            out = out + cw_ref[k:k + 1, :] * u[off:off + tm, :]
        return out

    act = _silu(branch(wg_ref, cwg_ref, cbg_ref)) * branch(wv_ref, cwv_ref, cbv_ref)
    acc_ref[...] += jnp.dot(act.astype(BF16), wd_ref[...], preferred_element_type=F32)

    @pl.when(j == pl.num_programs(1) - 1)
    def _():
        out_ref[...] = _rms(x1_ref[...] + acc_ref[...], fg_ref[...], NORM_EPS)


def _ffn(h2, x1, w_up, conv_w, conv_b, w_down, final_g, seq):
    t = h2.shape[0]
    tm, tn = TM_FFN, TN_FFN
    nj = D_FF // tn
    pad = BF16_SUBLANES
    tok = pl.BlockSpec((tm, D_MODEL), lambda i, j: (i, 0))
    in_specs = [
        tok,
        pl.BlockSpec((pad, D_MODEL), lambda i, j: (jnp.maximum(i * (tm // pad) - 1, 0), 0)),
        tok,
        pl.BlockSpec((D_MODEL, tn), lambda i, j: (0, j)),
        pl.BlockSpec((D_MODEL, tn), lambda i, j: (0, nj + j)),
        pl.BlockSpec((FFN_CONV, tn), lambda i, j: (0, j)),
        pl.BlockSpec((FFN_CONV, tn), lambda i, j: (0, nj + j)),
        pl.BlockSpec((1, tn), lambda i, j: (0, j)),
        pl.BlockSpec((1, tn), lambda i, j: (0, nj + j)),
        pl.BlockSpec((tn, D_MODEL), lambda i, j: (j, 0)),
        pl.BlockSpec((1, D_MODEL), lambda i, j: (0, 0)),
    ]
    return pl.pallas_call(
        functools.partial(_ffn_kernel, tiles_per_seq=seq // tm), grid=(t // tm, nj),
        in_specs=in_specs, out_specs=tok,
        out_shape=jax.ShapeDtypeStruct((t, D_MODEL), F32),
        scratch_shapes=[pltpu.VMEM((tm, D_MODEL), F32)],
        compiler_params=_cparams(("arbitrary", "arbitrary")), name="ffn",
    )(h2, h2, x1, w_up, w_up, conv_w, conv_w, conv_b, conv_b, w_down, final_g)


def _rope_tables(seq):
    half = NSA_HEAD_DIM // 2
    inv_freq = 1.0 / (ROPE_THETA ** (jnp.arange(0, NSA_HEAD_DIM, 2, dtype=F32) / NSA_HEAD_DIM))
    ang = jnp.arange(seq).astype(F32)[:, None] * inv_freq[None, :]
    cos, sin = jnp.cos(ang), jnp.sin(ang)
    reps = LANES // NSA_HEAD_DIM
    cos_t = jnp.tile(jnp.concatenate([cos, cos], axis=1), (1, reps))
    sin_t = jnp.tile(jnp.concatenate([-sin, sin], axis=1), (1, reps))
    return cos_t, sin_t


def _arrange_w_in(w):
    o_z, o_xbc, o_dt, o_q, o_kv, o_g = np.cumsum([0, SSM_D_INNER, SSM_XBC, SSM_HEADS, NSA_Q, 6 * NSA_KV]).tolist()
    per = 3 * NSA_REP
    gate_cols = []
    for g in range(NSA_KV_HEADS):
        gate_cols += [w[:, o_g + g * per:o_g + (g + 1) * per], jnp.zeros((w.shape[0], GATE_SLOT - per), w.dtype)]
    padw = TAIL - GATE_OFF - NSA_KV_HEADS * GATE_SLOT
    return jnp.concatenate([w[:, o_z:o_dt], w[:, o_q:o_g], w[:, o_dt:o_q]] + gate_cols
                           + [jnp.zeros((w.shape[0], padw), w.dtype)], axis=1).astype(BF16)


def kernel(x, norm1_g, w_in, ssm_conv_w, ssm_conv_b, ssm_dt_bias, ssm_a_log, ssm_d, ssm_norm_g, cmp_k_pos, cmp_k_w1, cmp_k_b1, cmp_k_w2, cmp_v_pos, cmp_v_w1, cmp_v_b1, cmp_v_w2, attn_norm_g, w_out, norm2_g, ffn_w_up, ffn_conv_w, ffn_conv_b, ffn_w_down, final_norm_g):
    batch, seq, d = x.shape
    assert w_in.shape[0] == 1, "single-layer problem"
    l = 0
    cos_t, sin_t = _rope_tables(seq)
    x2 = x.reshape(batch * seq, d)
    z, xbc, qt, kc_r, vc_r, ks, vst, kw, vwt, tail, tailt = _in_proj(
        x2, norm1_g[l].reshape(1, d), _arrange_w_in(w_in[l]), cos_t, sin_t, seq)
    y_ssm = _ssd(xbc, z, tail, tailt, ssm_conv_w[l], ssm_conv_b[l].reshape(1, -1), ssm_dt_bias[l],
                 ssm_a_log[l], ssm_d[l], ssm_norm_g[l], batch, seq)
    kc, vct = _compress(kc_r, vc_r,
                        (cmp_k_pos[l], cmp_k_w1[l], cmp_k_b1[l], cmp_k_w2[l]),
                        (cmp_v_pos[l], cmp_v_w1[l], cmp_v_b1[l], cmp_v_w2[l]), batch, seq)
    o_nsa = _nsa(qt, kc, vct, ks, vst, kw, vwt, tailt, batch, seq)
    x1, h2 = _out_proj(y_ssm, o_nsa, x2, w_out[l].astype(BF16), attn_norm_g[l].reshape(1, d),
                       norm2_g[l].reshape(1, d))
    out = _ffn(h2, x1, ffn_w_up[l].astype(BF16), ffn_conv_w[l], ffn_conv_b[l].reshape(1, -1),
               ffn_w_down[l].astype(BF16), final_norm_g.reshape(1, d), seq)
    return out.reshape(batch, seq, d)
```

```python
import functools
import math

import numpy as np
import jax
import jax.numpy as jnp
from jax import lax
from jax.experimental import pallas as pl
from jax.experimental.pallas import tpu as pltpu

F32 = jnp.float32
BF16 = jnp.bfloat16

D_MODEL = 1024
SSM_D_INNER = 1024
SSM_HEAD_DIM = 64
SSM_HEADS = 16
SSM_GROUPS = 2
SSM_STATE = 128
SSM_CONV = 4
SSM_CHUNK = 128
SSM_XBC = SSM_D_INNER + 2 * SSM_GROUPS * SSM_STATE
SSM_NORM_EPS = 1e-5
NSA_HEADS = 16
NSA_KV_HEADS = 4
NSA_REP = NSA_HEADS // NSA_KV_HEADS
NSA_HEAD_DIM = 64
NSA_Q = NSA_HEADS * NSA_HEAD_DIM
NSA_KV = NSA_KV_HEADS * NSA_HEAD_DIM
CMP_BLOCK = 32
CMP_STRIDE = 16
SLC_BLOCK = 64
SLC_TOP_N = 16
N_LOCAL_BLOCKS = 2
FORCED_SCORE = 1e4
WINDOW = 512
ROPE_THETA = 10000.0
D_FF = 2816
FFN_CONV = 3
NORM_EPS = 1e-6
NEG_BIG = -1e30

LANES = 128
BF16_SUBLANES = 16
TAIL = LANES
GATE_OFF = SSM_HEADS
GATE_SLOT = 16
D_PROJ = SSM_D_INNER + SSM_XBC + NSA_Q + 6 * NSA_KV + TAIL
COL_Z, COL_XBC, COL_Q, COL_KV, COL_TAIL = 0, 1024, 2560, 3584, 5120

TM_PROJ = 256
TQ = 256
TK = 256
ONES_ROWS = 16
KSEL = LANES
BIAS_ROWS = 16
MT_NONE, MT_FAR, MT_ALL, MT_CAUSAL = 0, TK, 2 * TK, 3 * TK
TM_FFN = 512
TN_FFN = 256
VMEM_LIMIT = 56 * 1024 * 1024

NT_DIMS = (((1,), (1,)), ((), ()))


def _cparams(sem):
    return pltpu.CompilerParams(dimension_semantics=sem, vmem_limit_bytes=VMEM_LIMIT)


def _rms(x, g, eps):
    return x * lax.rsqrt(jnp.mean(x * x, axis=-1, keepdims=True) + eps) * g


def _silu(x):
    return x * jax.nn.sigmoid(x)


def _inproj_kernel(x_ref, g_ref, w_ref, cos_ref, sin_ref,
                   z_ref, xbc_ref, qt_ref, kc_ref, vc_ref, ks_ref, vst_ref, kw_ref, vwt_ref,
                   tail_ref, tailt_ref, *, tiles_per_seq):
    tm = x_ref.shape[0]
    h = _rms(x_ref[...], g_ref[...], NORM_EPS).astype(BF16)

    def mm(lo, hi):
        return jnp.dot(h, w_ref[:, lo:hi], preferred_element_type=F32)

    z_ref[...] = mm(COL_Z, COL_XBC)
    xbc_ref[...] = mm(COL_XBC, COL_Q)

    cos = cos_ref[...]
    sin = sin_ref[...]
    lane = lax.broadcasted_iota(jnp.int32, (tm, LANES), 1)
    first_half = (lane % NSA_HEAD_DIM) < (NSA_HEAD_DIM // 2)

    def rope(xc):
        partner = jnp.where(first_half, pltpu.roll(xc, LANES - 32, 1), pltpu.roll(xc, 32, 1))
        return xc * cos + partner * sin

    q = mm(COL_Q, COL_KV)
    scale = NSA_HEAD_DIM ** -0.5 * math.log2(math.e)
    for c in range(NSA_Q // LANES):
        rt = (rope(q[:, c * LANES:(c + 1) * LANES]) * scale).T.astype(BF16)
        qt_ref[2 * c] = rt[:NSA_HEAD_DIM]
        qt_ref[2 * c + 1] = rt[NSA_HEAD_DIM:]

    kv = mm(COL_KV, COL_TAIL)

    def seg(i):
        return kv[:, i * NSA_KV:(i + 1) * NSA_KV]

    def rope_seg(x):
        return jnp.concatenate([rope(x[:, :LANES]), rope(x[:, LANES:])], axis=1)

    def store_heads(ref, x):
        for g in range(NSA_KV_HEADS):
            ref[g] = x[:, g * NSA_HEAD_DIM:(g + 1) * NSA_HEAD_DIM].astype(BF16)

    def store_heads_t(ref, x):
        xt = x.T.astype(BF16)
        for g in range(NSA_KV_HEADS):
            ref[g] = xt[g * NSA_HEAD_DIM:(g + 1) * NSA_HEAD_DIM]

    kc_ref[...] = rope_seg(seg(0))
    vc_ref[...] = seg(1)
    pos = (pl.program_id(0) % tiles_per_seq) * tm + lax.broadcasted_iota(jnp.int32, (tm, NSA_HEAD_DIM), 0)
    block_in_chunk = (pos // SLC_BLOCK) % (TK // SLC_BLOCK)
    onehot = jnp.where(lax.broadcasted_iota(jnp.int32, (tm, NSA_HEAD_DIM), 1) == block_in_chunk, 1.0, 0.0)
    ks = rope_seg(seg(2))
    for g in range(NSA_KV_HEADS):
        kg = ks[:, g * NSA_HEAD_DIM:(g + 1) * NSA_HEAD_DIM]
        ks_ref[g] = jnp.concatenate([kg, onehot], axis=1).astype(BF16)
    store_heads_t(vst_ref, seg(3))
    store_heads(kw_ref, rope_seg(seg(4)))
    store_heads_t(vwt_ref, seg(5))

    tail = mm(COL_TAIL, D_PROJ)
    tail_ref[...] = tail
    tailt_ref[...] = tail.T


def _in_proj(x2, norm_g, w, cos_t, sin_t, seq):
    t = x2.shape[0]
    tm = TM_PROJ
    nseq = seq // tm
    row = lambda i: (i, 0)
    const = lambda i: (0, 0)
    heads = lambda i: (0, i, 0)
    heads_t = lambda i: (0, 0, i)
    out_shape = (
        jax.ShapeDtypeStruct((t, SSM_D_INNER), F32),
        jax.ShapeDtypeStruct((t, SSM_XBC), F32),
        jax.ShapeDtypeStruct((NSA_HEADS, NSA_HEAD_DIM, t), BF16),
        jax.ShapeDtypeStruct((t, NSA_KV), F32),
        jax.ShapeDtypeStruct((t, NSA_KV), F32),
        jax.ShapeDtypeStruct((NSA_KV_HEADS, t, KSEL), BF16),
        jax.ShapeDtypeStruct((NSA_KV_HEADS, NSA_HEAD_DIM, t), BF16),
        jax.ShapeDtypeStruct((NSA_KV_HEADS, t, NSA_HEAD_DIM), BF16),
        jax.ShapeDtypeStruct((NSA_KV_HEADS, NSA_HEAD_DIM, t), BF16),
        jax.ShapeDtypeStruct((t, TAIL), F32),
        jax.ShapeDtypeStruct((TAIL, t), F32),
    )
    kvh = pl.BlockSpec((NSA_KV_HEADS, tm, NSA_HEAD_DIM), heads)
    kvh_t = pl.BlockSpec((NSA_KV_HEADS, NSA_HEAD_DIM, tm), heads_t)
    out_specs = (
        pl.BlockSpec((tm, SSM_D_INNER), row),
        pl.BlockSpec((tm, SSM_XBC), row),
        pl.BlockSpec((NSA_HEADS, NSA_HEAD_DIM, tm), heads_t),
        pl.BlockSpec((tm, NSA_KV), row),
        pl.BlockSpec((tm, NSA_KV), row),
        pl.BlockSpec((NSA_KV_HEADS, tm, KSEL), heads), kvh_t, kvh, kvh_t,
        pl.BlockSpec((tm, TAIL), row),
        pl.BlockSpec((TAIL, tm), lambda i: (0, i)),
    )
    in_specs = [
        pl.BlockSpec((tm, D_MODEL), row),
        pl.BlockSpec((1, D_MODEL), const),
        pl.BlockSpec((D_MODEL, D_PROJ), const),
        pl.BlockSpec((tm, LANES), lambda i: (i % nseq, 0)),
        pl.BlockSpec((tm, LANES), lambda i: (i % nseq, 0)),
    ]
    return pl.pallas_call(
        functools.partial(_inproj_kernel, tiles_per_seq=nseq), grid=(t // tm,), in_specs=in_specs,
        out_specs=out_specs,
        out_shape=out_shape, compiler_params=_cparams(("arbitrary",)), name="in_proj",
    )(x2, norm_g, w, cos_t, sin_t)


def _ssd_kernel(xbc_ref, z_ref, tail_ref, tailt_ref, cw_ref, cb_ref, dtb_ref, dtbt_ref,
                alog_ref, alogt_ref, dskip_ref, ng_ref,
                y_ref, xcat_ref, state_ref, ybuf_ref):
    L, P, N, H, G = SSM_CHUNK, SSM_HEAD_DIM, SSM_STATE, SSM_HEADS, SSM_GROUPS
    HG = H // G
    halo = 8

    @pl.when(pl.program_id(1) == 0)
    def _():
        xcat_ref[0:halo, :] = jnp.zeros((halo, SSM_XBC), F32)
        state_ref[...] = jnp.zeros_like(state_ref)

    xcat_ref[halo:halo + L, :] = xbc_ref[...]
    conv = cb_ref[...]
    for k in range(SSM_CONV):
        off = halo - (SSM_CONV - 1) + k
        conv = conv + cw_ref[k:k + 1, :] * xcat_ref[off:off + L, :]
    xcat_ref[0:halo, :] = xcat_ref[L:L + halo, :]
    u = _silu(conv)
    xs = u[:, :SSM_D_INNER]
    bm = u[:, SSM_D_INNER:SSM_D_INNER + G * N]
    cm = u[:, SSM_D_INNER + G * N:]

    def softplus(v):
        return jnp.maximum(v, 0.0) + jnp.log1p(jnp.exp(-jnp.abs(v)))

    dt = softplus(tail_ref[:, 0:H] + dtb_ref[...])
    dtt = softplus(tailt_ref[0:H, :] + dtbt_ref[...])
    da = dt * (-jnp.exp(alog_ref[...]))
    dat = dtt * (-jnp.exp(alogt_ref[...]))
    ri = lax.broadcasted_iota(jnp.int32, (L, L), 0)
    ci = lax.broadcasted_iota(jnp.int32, (L, L), 1)
    tri = ci <= ri
    hi = lax.Precision.HIGHEST
    acs = jnp.dot(tri.astype(F32), da, precision=hi, preferred_element_type=F32)
    acst = jnp.dot(dat, (ri <= ci).astype(F32), precision=hi, preferred_element_type=F32)
    last = acs[L - 1:L, :]
    w_state = dt * jnp.exp(last - acs)
    eacs = jnp.exp(acs)
    cdec = jnp.exp(last)

    hrow = lax.broadcasted_iota(jnp.int32, (H, SSM_D_INNER), 0)
    hcol = lax.broadcasted_iota(jnp.int32, (H, SSM_D_INNER), 1)
    expand = (hcol // P == hrow).astype(F32)

    def ex(v):
        return jnp.dot(v, expand, precision=hi, preferred_element_type=F32)

    small = jnp.concatenate([cdec, dskip_ref[...], jnp.zeros((6, H), F32)], axis=0)
    small_e = ex(small)
    cdec_e = small_e[0:1, :]
    dskip_e = small_e[1:2, :]
    xdt = (xs * ex(dt)).astype(BF16)
    wst = (xs * ex(w_state)).astype(BF16)
    eacs_e = ex(eacs)

    for g in range(G):
        bm_g = bm[:, g * N:(g + 1) * N]
        cm_g = cm[:, g * N:(g + 1) * N].astype(BF16)
        cb = lax.dot_general(cm_g, bm_g.astype(BF16), NT_DIMS, preferred_element_type=F32)
        cols = slice(g * HG * P, (g + 1) * HG * P)
        st = state_ref[:, cols]
        y_off = jnp.dot(cm_g, st.astype(BF16), preferred_element_type=F32) * eacs_e[:, cols]
        ybuf_ref[:, cols] = y_off
        bmt = bm_g.T.astype(BF16)
        state_ref[:, cols] = st * cdec_e[:, cols] + jnp.dot(bmt, wst[:, cols], preferred_element_type=F32)
        for r in range(HG):
            hh = g * HG + r
            diff = acs[:, hh:hh + 1] - acst[hh:hh + 1, :]
            seg = jnp.exp(jnp.where(tri, diff, -jnp.inf))
            lmat = (cb * seg).astype(BF16)
            hc = slice(hh * P, (hh + 1) * P)
            ybuf_ref[:, hc] = ybuf_ref[:, hc] + jnp.dot(lmat, xdt[:, hc], preferred_element_type=F32)

    y = (ybuf_ref[...] + xs * dskip_e) * _silu(z_ref[...])
    gw = SSM_D_INNER // G
    parts = []
    for g in range(G):
        yg = y[:, g * gw:(g + 1) * gw]
        parts.append(yg * lax.rsqrt(jnp.mean(yg * yg, axis=-1, keepdims=True) + SSM_NORM_EPS))
    y_ref[...] = (jnp.concatenate(parts, axis=1) * ng_ref[...]).astype(y_ref.dtype)


def _ssd(xbc, z, tail, tailt, conv_w, conv_b, dt_bias, a_log, d_skip, norm_g, batch, seq):
    t = xbc.shape[0]
    L = SSM_CHUNK
    nc = seq // L
    row = lambda b, c: (b * nc + c, 0)
    const = lambda b, c: (0, 0)
    H = SSM_HEADS
    in_specs = [
        pl.BlockSpec((L, SSM_XBC), row),
        pl.BlockSpec((L, SSM_D_INNER), row),
        pl.BlockSpec((L, TAIL), row),
        pl.BlockSpec((TAIL, L), lambda b, c: (0, b * nc + c)),
        pl.BlockSpec((SSM_CONV, SSM_XBC), const),
        pl.BlockSpec((1, SSM_XBC), const),
        pl.BlockSpec((1, H), const),
        pl.BlockSpec((H, 1), const),
        pl.BlockSpec((1, H), const),
        pl.BlockSpec((H, 1), const),
        pl.BlockSpec((1, H), const),
        pl.BlockSpec((1, SSM_D_INNER), const),
    ]
    return pl.pallas_call(
        _ssd_kernel, grid=(batch, nc), in_specs=in_specs,
        out_specs=pl.BlockSpec((L, SSM_D_INNER), row),
        out_shape=jax.ShapeDtypeStruct((t, SSM_D_INNER), BF16),
        scratch_shapes=[pltpu.VMEM((L + 8, SSM_XBC), F32),
                        pltpu.VMEM((SSM_STATE, SSM_D_INNER), F32),
                        pltpu.VMEM((L, SSM_D_INNER), F32)],
        compiler_params=_cparams(("arbitrary", "arbitrary")), name="ssd",
    )(xbc, z, tail, tailt, conv_w, conv_b, dt_bias.reshape(1, H), dt_bias.reshape(H, 1),
      a_log.reshape(1, H), a_log.reshape(H, 1), d_skip.reshape(1, H), norm_g.reshape(1, -1))


def _compress_kernel(k_ref, v_ref, kpos_ref, vpos_ref, kw1_ref, vw1_ref, kb1_ref, vb1_ref,
                     kw2_ref, vw2_ref, kc_ref, vct_ref):
    def hidden(t_ref, pos_ref, w1_ref, b1_ref):
        t = t_ref[...]
        n = t.shape[0]
        lo = jnp.dot((t + pos_ref[0:1, :]).astype(BF16), w1_ref[0], preferred_element_type=F32)
        hi = jnp.dot((t + pos_ref[1:2, :]).astype(BF16), w1_ref[1], preferred_element_type=F32)
        return _silu(lo + pltpu.roll(hi, n - 1, 0) + b1_ref[...]).astype(BF16)

    kc = jnp.dot(hidden(k_ref, kpos_ref, kw1_ref, kb1_ref), kw2_ref[...], preferred_element_type=F32)
    vct = lax.dot_general(vw2_ref[...], hidden(v_ref, vpos_ref, vw1_ref, vb1_ref), NT_DIMS,
                          preferred_element_type=F32).astype(BF16)
    for g in range(NSA_KV_HEADS):
        hs = slice(g * NSA_HEAD_DIM, (g + 1) * NSA_HEAD_DIM)
        kc_ref[g] = kc[:, hs].astype(BF16)
        vct_ref[g] = vct[hs]


def _compress_weights(pos, w1, b1, w2):
    G, D = NSA_KV_HEADS, NSA_HEAD_DIM
    eye = jnp.eye(G, dtype=F32)
    half = CMP_BLOCK // 2
    w1r = w1.reshape(2, half, D, D)
    w1big = jnp.einsum("hldj,gk->hlgdkj", w1r, eye).reshape(2, half * G * D, G * D).astype(BF16)
    posr = pos.reshape(2, half, 1, D)
    posbig = jnp.broadcast_to(posr, (2, half, G, D)).reshape(2, half * G * D)
    b1big = jnp.tile(b1.reshape(1, D), (1, G))
    w2big = jnp.einsum("dj,gk->gdkj", w2, eye).reshape(G * D, G * D).astype(BF16)
    return posbig, w1big, b1big, w2big


def _compress(kc_rope, vc, kparams, vparams, batch, seq):
    rows = seq // CMP_STRIDE
    width = CMP_STRIDE * NSA_KV
    k16 = kc_rope.reshape(batch * rows, width)
    v16 = vc.reshape(batch * rows, width)
    kpos, kw1, kb1, kw2 = _compress_weights(*kparams)
    vpos, vw1, vb1, vw2 = _compress_weights(*vparams)
    vw2 = vw2.T
    c2 = lambda b: (0, 0)
    c3 = lambda b: (0, 0, 0)
    tok = pl.BlockSpec((rows, width), lambda b: (b, 0))
    in_specs = [tok, tok,
                pl.BlockSpec((2, width), c2), pl.BlockSpec((2, width), c2),
                pl.BlockSpec((2, width, NSA_KV), c3), pl.BlockSpec((2, width, NSA_KV), c3),
                pl.BlockSpec((1, NSA_KV), c2), pl.BlockSpec((1, NSA_KV), c2),
                pl.BlockSpec((NSA_KV, NSA_KV), c2), pl.BlockSpec((NSA_KV, NSA_KV), c2)]
    out = jax.ShapeDtypeStruct((NSA_KV_HEADS, batch * rows, NSA_HEAD_DIM), BF16)
    out_t = jax.ShapeDtypeStruct((NSA_KV_HEADS, NSA_HEAD_DIM, batch * rows), BF16)
    ospec = pl.BlockSpec((NSA_KV_HEADS, rows, NSA_HEAD_DIM), lambda b: (0, b, 0))
    ospec_t = pl.BlockSpec((NSA_KV_HEADS, NSA_HEAD_DIM, rows), lambda b: (0, 0, b))
    return pl.pallas_call(
        _compress_kernel, grid=(batch,), in_specs=in_specs, out_specs=(ospec, ospec_t),
        out_shape=(out, out_t), compiler_params=_cparams(("arbitrary",)), name="compress",
    )(k16, v16, kpos, vpos, kw1, vw1, kb1, vb1, kw2, vw2)


def _nsa_kernel(qt_ref, kc_ref, vct_ref, ks_ref, vst_ref, kw_ref, vwt_ref, tailt_ref, ovt_ref, mtab_ref,
                o_ref, biasq_ref, m_ref, acc_ref, part_ref):
    R, D = NSA_REP, NSA_HEAD_DIM
    nl = R * TQ
    g = pl.program_id(1)
    qi = pl.program_id(2)
    t0 = qi * TQ
    qt = jnp.concatenate([qt_ref[r] for r in range(R)], axis=1)
    tpos_row = t0 + lax.broadcasted_iota(jnp.int32, (1, nl), 1) % TQ

    ncr = kc_ref.shape[1]
    sc = jnp.dot(kc_ref[0], qt, preferred_element_type=F32)
    ni = lax.broadcasted_iota(jnp.int32, (ncr, nl), 0)
    mask_c = (ni * CMP_STRIDE + (CMP_BLOCK - 1) <= tpos_row) & (ni < ncr - 1)
    s_m = jnp.where(mask_c, sc, NEG_BIG)
    mx = jnp.max(s_m, axis=0, keepdims=True)
    p = jnp.where(mask_c, jnp.exp2(s_m - mx), 0.0)
    den = jnp.sum(p, axis=0, keepdims=True)
    pc = p / jnp.where(den > 0.0, den, 1.0)
    o_c = jnp.dot(vct_ref[0], pc.astype(BF16), preferred_element_type=F32)

    psum = pc[:, 0:TQ]
    for r in range(1, R):
        psum = psum + pc[:, r * TQ:(r + 1) * TQ]
    n_slc = ovt_ref.shape[0]
    imp_t = jnp.dot(ovt_ref[...], psum, precision=lax.Precision.HIGHEST,
                    preferred_element_type=F32)
    jj = lax.broadcasted_iota(jnp.int32, (n_slc, TQ), 0)
    tt = t0 + lax.broadcasted_iota(jnp.int32, (n_slc, TQ), 1)
    lag = tt // SLC_BLOCK - jj
    forced = (jj == 0) | ((lag >= 0) & (lag < N_LOCAL_BLOCKS))
    valid = jj * SLC_BLOCK <= tt
    score = jnp.where(forced, FORCED_SCORE, jnp.where(valid, imp_t, -1.0))
    rank = jnp.zeros((n_slc, TQ), F32)
    for j2 in range(n_slc):
        sj = score[j2:j2 + 1, :]
        beats = (sj > score) | ((sj == score) & (jj > j2))
        rank = rank + jnp.where(beats, 1.0, 0.0)
    selected = (rank < float(min(SLC_TOP_N, n_slc))) & (score >= 0.0)
    bias_t = jnp.where(selected, 0.0, NEG_BIG)
    per = TK // SLC_BLOCK
    zrows = jnp.zeros((BIAS_ROWS - per, TQ), F32)
    for kb in range(n_slc // per):
        blk = jnp.concatenate([bias_t[kb * per:(kb + 1) * per], zrows], axis=0)
        biasq_ref[kb * BIAS_ROWS:(kb + 1) * BIAS_ROWS, :] = jnp.concatenate([blk] * R, axis=1).astype(BF16)
    qpad = jnp.zeros((KSEL - D - BIAS_ROWS, nl), BF16)

    def q_sel(kb):
        rows = biasq_ref[pl.ds(pl.multiple_of(kb * BIAS_ROWS, BIAS_ROWS), BIAS_ROWS), :]
        return jnp.concatenate([qt, rows, qpad], axis=0)

    def k_rows(k_ref, start):
        return k_ref[0, pl.ds(pl.multiple_of(start, TK), TK), :]

    def vt_cols(vt_ref, start, n):
        return vt_ref[0, :, pl.ds(pl.multiple_of(start, TK), n)]

    def pv(vt, p):
        vt1 = jnp.concatenate([vt, jnp.ones((ONES_ROWS, vt.shape[1]), BF16)], axis=0)
        return jnp.dot(vt1, p.astype(BF16), preferred_element_type=F32)

    def mask_tile(off):
        return jnp.concatenate([mtab_ref[pl.ds(pl.multiple_of(off, TK), TK), :]] * R, axis=1)

    def col_max(s):
        return jnp.max(s, axis=0, keepdims=True)

    causal = mask_tile(MT_CAUSAL)

    s = jnp.dot(k_rows(ks_ref, t0), q_sel(qi), preferred_element_type=F32) + causal
    m = col_max(s)
    m_ref[...] = m
    acc_ref[...] = pv(vt_cols(vst_ref, t0, TK), jnp.exp2(s - m))

    far0 = jnp.maximum(t0 - 2 * TK, 0)
    mid0 = jnp.maximum(t0 - TK, 0)
    s_far = (jnp.dot(k_rows(kw_ref, far0), qt, preferred_element_type=F32)
             + mask_tile(jnp.where(qi >= 2, MT_FAR, MT_NONE)))
    s_mid = (jnp.dot(k_rows(kw_ref, mid0), qt, preferred_element_type=F32)
             + mask_tile(jnp.where(qi >= 1, MT_ALL, MT_NONE)))
    s_dia = jnp.dot(k_rows(kw_ref, t0), qt, preferred_element_type=F32) + causal
    mw = jnp.maximum(jnp.maximum(col_max(s_far), col_max(s_mid)), col_max(s_dia))
    pw = jnp.concatenate([jnp.exp2(s_far - mw), jnp.exp2(s_mid - mw), jnp.exp2(s_dia - mw)], axis=0)
    vtw = jnp.concatenate([vt_cols(vwt_ref, far0, TK), vt_cols(vwt_ref, mid0, TK), vt_cols(vwt_ref, t0, TK)],
                          axis=1)
    accw = pv(vtw, pw)
    o_w = accw[0:D] / accw[D:D + 1]

    gate_row = pl.multiple_of(GATE_OFF + GATE_SLOT * g, 8)
    sig = jax.nn.sigmoid(tailt_ref[pl.ds(gate_row, GATE_SLOT), :])

    def gate(c):
        return jnp.concatenate([sig[3 * r + c:3 * r + c + 1, :] for r in range(R)], axis=1)

    part_ref[...] = gate(0) * o_c + gate(2) * o_w

    def update(s_list, vt):
        m_old = m_ref[...]
        m_new = m_old
        for s_i in s_list:
            m_new = jnp.maximum(m_new, col_max(s_i))
        p = jnp.concatenate([jnp.exp2(s_i - m_new) for s_i in s_list], axis=0)
        acc_ref[...] = jnp.exp2(m_old - m_new) * acc_ref[...] + pv(vt, p)
        m_ref[...] = m_new

    def slc_scores(kb):
        return jnp.dot(k_rows(ks_ref, kb * TK), q_sel(kb), preferred_element_type=F32)

    def pair_body(i, carry):
        update([slc_scores(2 * i), slc_scores(2 * i + 1)], vt_cols(vst_ref, 2 * i * TK, 2 * TK))
        return carry

    lax.fori_loop(0, qi // 2, pair_body, 0)

    @pl.when(qi % 2 == 1)
    def _():
        update([slc_scores(qi - 1)], vt_cols(vst_ref, (qi - 1) * TK, TK))

    acc = acc_ref[...]
    ot = part_ref[...] + gate(1) * (acc[0:D] / acc[D:D + 1])
    stacked = jnp.concatenate([ot[:, r * TQ:(r + 1) * TQ] for r in range(R)], axis=0)
    o_ref[...] = stacked.T


def _overlap_t(seq):
    n_cmp = (seq - CMP_BLOCK) // CMP_STRIDE + 1
    n_slc = seq // SLC_BLOCK
    cs = np.arange(n_cmp) * CMP_STRIDE
    ss = np.arange(n_slc) * SLC_BLOCK
    overlap = np.clip(np.minimum(cs[:, None] + CMP_BLOCK, ss[None, :] + SLC_BLOCK)
                      - np.maximum(cs[:, None], ss[None, :]), 0, None) / CMP_BLOCK
    ovt = np.zeros((n_slc, n_cmp + 1), np.float32)
    ovt[:, :n_cmp] = overlap.T
    return jnp.asarray(ovt)


def _mask_tiles():
    ki = np.arange(TK)[:, None]
    qi = np.arange(TQ)[None, :]
    neg = np.float32(NEG_BIG)
    none = np.full((TK, TQ), neg, np.float32)
    far = np.where(ki > qi, np.float32(0), neg)
    full = np.zeros((TK, TQ), np.float32)
    causal = np.where(ki <= qi, np.float32(0), neg)
    return jnp.asarray(np.concatenate([none, far, full, causal], axis=0))


def _nsa(qt, kc, vct, ks, vst, kw, vwt, tailt, batch, seq):
    t = batch * seq
    G, R, D = NSA_KV_HEADS, NSA_REP, NSA_HEAD_DIM
    nq = seq // TQ
    ncr = seq // CMP_STRIDE
    n_slc = seq // SLC_BLOCK
    assert TQ == TK and WINDOW == 2 * TK, "window branch visits exactly three key chunks"
    assert TK // SLC_BLOCK <= BIAS_ROWS and D + BIAS_ROWS <= KSEL
    const = lambda b, g, i: (0, 0)
    vtspec = pl.BlockSpec((1, D, seq), lambda b, g, i: (g, 0, b))
    in_specs = [
        pl.BlockSpec((R, D, TQ), lambda b, g, i: (g, 0, b * nq + i)),
        pl.BlockSpec((1, ncr, D), lambda b, g, i: (g, b, 0)),
        pl.BlockSpec((1, D, ncr), lambda b, g, i: (g, 0, b)),
        pl.BlockSpec((1, seq, KSEL), lambda b, g, i: (g, b, 0)),
        vtspec,
        pl.BlockSpec((1, seq, D), lambda b, g, i: (g, b, 0)),
        vtspec,
        pl.BlockSpec((TAIL, TQ), lambda b, g, i: (0, b * nq + i)),
        pl.BlockSpec((n_slc, ncr), const),
        pl.BlockSpec((4 * TK, TQ), const),
    ]
    return pl.pallas_call(
        _nsa_kernel, grid=(batch, G, nq), in_specs=in_specs,
        out_specs=pl.BlockSpec((TQ, R * D), lambda b, g, i: (b * nq + i, g)),
        out_shape=jax.ShapeDtypeStruct((t, NSA_Q), F32),
        scratch_shapes=[pltpu.VMEM((seq // TK * BIAS_ROWS, R * TQ), BF16),
                        pltpu.VMEM((1, R * TQ), F32),
                        pltpu.VMEM((D + ONES_ROWS, R * TQ), F32),
                        pltpu.VMEM((D, R * TQ), F32)],
        compiler_params=_cparams(("arbitrary", "arbitrary", "arbitrary")), name="nsa",
    )(qt, kc, vct, ks, vst, kw, vwt, tailt, _overlap_t(seq), _mask_tiles())


def _outproj_kernel(y_ref, o_ref, x_ref, w_ref, ag_ref, ng_ref, x1_ref, h2_ref):
    yn = _rms(o_ref[...], ag_ref[...], NORM_EPS).astype(BF16)
    x1 = (x_ref[...]
          + jnp.dot(y_ref[...], w_ref[0:SSM_D_INNER, :], preferred_element_type=F32)
          + jnp.dot(yn, w_ref[SSM_D_INNER:, :], preferred_element_type=F32))
    x1_ref[...] = x1
    h2_ref[...] = _rms(x1, ng_ref[...], NORM_EPS).astype(BF16)


def _out_proj(y_ssm, o_nsa, x2, w_out, attn_g, norm2_g):
    t = x2.shape[0]
    tm = TM_PROJ
    row = lambda i: (i, 0)
    const = lambda i: (0, 0)
    tok = pl.BlockSpec((tm, D_MODEL), row)
    vec = pl.BlockSpec((1, D_MODEL), const)
    return pl.pallas_call(
        _outproj_kernel, grid=(t // tm,),
        in_specs=[tok, tok, tok, pl.BlockSpec((SSM_D_INNER + NSA_Q, D_MODEL), const), vec, vec],
        out_specs=(tok, tok),
        out_shape=(jax.ShapeDtypeStruct((t, D_MODEL), F32), jax.ShapeDtypeStruct((t, D_MODEL), BF16)),
        compiler_params=_cparams(("arbitrary",)), name="out_proj",
    )(y_ssm, o_nsa, x2, w_out, attn_g, norm2_g)


def _ffn_kernel(h_ref, halo_ref, x1_ref, wup_ref, cw_ref, cb_ref, wd_ref, fg_ref, out_ref, act_ref,
                *, tiles_per_seq):
    i = pl.program_id(0)
    tm = h_ref.shape[0]
    pad = halo_ref.shape[0]
    tn = TN_FFN
    halo = halo_ref[...]
    halo = jnp.where(i % tiles_per_seq == 0, jnp.zeros_like(halo), halo)
    hc = jnp.concatenate([halo, h_ref[...]], axis=0)

    def branch(c0):
        cols = slice(c0, c0 + tn)
        u = jnp.dot(hc, wup_ref[:, cols], preferred_element_type=F32)
        out = cb_ref[:, cols]
        for k in range(FFN_CONV):
            off = pad - (FFN_CONV - 1) + k
            out = out + cw_ref[k:k + 1, cols] * u[off:off + tm, :]
        return out

    for j in range(D_FF // tn):
        act = _silu(branch(j * tn)) * branch(D_FF + j * tn)
        act_ref[:, j * tn:(j + 1) * tn] = act.astype(BF16)
    down = jnp.dot(act_ref[...], wd_ref[...], preferred_element_type=F32)
    out_ref[...] = _rms(x1_ref[...] + down, fg_ref[...], NORM_EPS)


def _ffn(h2, x1, w_up, conv_w, conv_b, w_down, final_g, seq):
    t = h2.shape[0]
    tm = TM_FFN
    pad = BF16_SUBLANES
    tok = pl.BlockSpec((tm, D_MODEL), lambda i: (i, 0))

    def resident(shape):
        return pl.BlockSpec(shape, lambda i: (0, 0), pipeline_mode=pl.Buffered(1))

    in_specs = [
        tok,
        pl.BlockSpec((pad, D_MODEL), lambda i: (jnp.maximum(i * (tm // pad) - 1, 0), 0)),
        tok,
        resident((D_MODEL, 2 * D_FF)),
        resident((FFN_CONV, 2 * D_FF)),
        resident((1, 2 * D_FF)),
        resident((D_FF, D_MODEL)),
        resident((1, D_MODEL)),
    ]
    return pl.pallas_call(
        functools.partial(_ffn_kernel, tiles_per_seq=seq // tm), grid=(t // tm,),
        in_specs=in_specs, out_specs=tok,
        out_shape=jax.ShapeDtypeStruct((t, D_MODEL), F32),
        scratch_shapes=[pltpu.VMEM((tm, D_FF), BF16)],
        compiler_params=_cparams(("arbitrary",)), name="ffn",
    )(h2, h2, x1, w_up, conv_w, conv_b, w_down, final_g)


def _rope_tables(seq):
    half = NSA_HEAD_DIM // 2
    inv_freq = 1.0 / (ROPE_THETA ** (jnp.arange(0, NSA_HEAD_DIM, 2, dtype=F32) / NSA_HEAD_DIM))
    ang = jnp.arange(seq).astype(F32)[:, None] * inv_freq[None, :]
    cos, sin = jnp.cos(ang), jnp.sin(ang)
    reps = LANES // NSA_HEAD_DIM
    cos_t = jnp.tile(jnp.concatenate([cos, cos], axis=1), (1, reps))
    sin_t = jnp.tile(jnp.concatenate([-sin, sin], axis=1), (1, reps))
    return cos_t, sin_t


def _arrange_w_in(w):
    o_z, o_xbc, o_dt, o_q, o_kv, o_g = np.cumsum([0, SSM_D_INNER, SSM_XBC, SSM_HEADS, NSA_Q, 6 * NSA_KV]).tolist()
    per = 3 * NSA_REP
    gate_cols = []
    for g in range(NSA_KV_HEADS):
        gate_cols += [w[:, o_g + g * per:o_g + (g + 1) * per], jnp.zeros((w.shape[0], GATE_SLOT - per), w.dtype)]
    padw = TAIL - GATE_OFF - NSA_KV_HEADS * GATE_SLOT
    return jnp.concatenate([w[:, o_z:o_dt], w[:, o_q:o_g], w[:, o_dt:o_q]] + gate_cols
                           + [jnp.zeros((w.shape[0], padw), w.dtype)], axis=1).astype(BF16)


def kernel(x, norm1_g, w_in, ssm_conv_w, ssm_conv_b, ssm_dt_bias, ssm_a_log, ssm_d, ssm_norm_g, cmp_k_pos, cmp_k_w1, cmp_k_b1, cmp_k_w2, cmp_v_pos, cmp_v_w1, cmp_v_b1, cmp_v_w2, attn_norm_g, w_out, norm2_g, ffn_w_up, ffn_conv_w, ffn_conv_b, ffn_w_down, final_norm_g):
    batch, seq, d = x.shape
    assert w_in.shape[0] == 1, "single-layer problem"
    l = 0
    cos_t, sin_t = _rope_tables(seq)
    x2 = x.reshape(batch * seq, d)
    z, xbc, qt, kc_r, vc_r, ks, vst, kw, vwt, tail, tailt = _in_proj(
        x2, norm1_g[l].reshape(1, d), _arrange_w_in(w_in[l]), cos_t, sin_t, seq)
    y_ssm = _ssd(xbc, z, tail, tailt, ssm_conv_w[l], ssm_conv_b[l].reshape(1, -1), ssm_dt_bias[l],
                 ssm_a_log[l], ssm_d[l], ssm_norm_g[l], batch, seq)
    kc, vct = _compress(kc_r, vc_r,
                        (cmp_k_pos[l], cmp_k_w1[l], cmp_k_b1[l], cmp_k_w2[l]),
                        (cmp_v_pos[l], cmp_v_w1[l], cmp_v_b1[l], cmp_v_w2[l]), batch, seq)
    o_nsa = _nsa(qt, kc, vct, ks, vst, kw, vwt, tailt, batch, seq)
    x1, h2 = _out_proj(y_ssm, o_nsa, x2, w_out[l].astype(BF16), attn_norm_g[l].reshape(1, d),
                       norm2_g[l].reshape(1, d))
    out = _ffn(h2, x1, ffn_w_up[l].astype(BF16), ffn_conv_w[l], ffn_conv_b[l].reshape(1, -1),
               ffn_w_down[l].astype(BF16), final_norm_g.reshape(1, d), seq)
    return out.reshape(batch, seq, d)
```

```python
import functools
import math

import numpy as np
import jax
import jax.numpy as jnp
from jax import lax
from jax.experimental import pallas as pl
from jax.experimental.pallas import tpu as pltpu

F32 = jnp.float32
BF16 = jnp.bfloat16

D_MODEL = 1024
SSM_D_INNER = 1024
SSM_HEAD_DIM = 64
SSM_HEADS = 16
SSM_GROUPS = 2
SSM_STATE = 128
SSM_CONV = 4
SSM_CHUNK = 128
SSM_XBC = SSM_D_INNER + 2 * SSM_GROUPS * SSM_STATE
SSM_NORM_EPS = 1e-5
NSA_HEADS = 16
NSA_KV_HEADS = 4
NSA_REP = NSA_HEADS // NSA_KV_HEADS
NSA_HEAD_DIM = 64
NSA_Q = NSA_HEADS * NSA_HEAD_DIM
NSA_KV = NSA_KV_HEADS * NSA_HEAD_DIM
CMP_BLOCK = 32
CMP_STRIDE = 16
SLC_BLOCK = 64
SLC_TOP_N = 16
N_LOCAL_BLOCKS = 2
FORCED_SCORE = 1e4
WINDOW = 512
ROPE_THETA = 10000.0
D_FF = 2816
FFN_CONV = 3
NORM_EPS = 1e-6
NEG_BIG = -1e30

LANES = 128
BF16_SUBLANES = 16
TAIL = LANES
GATE_OFF = SSM_HEADS
GATE_SLOT = 16

TM_PROJ = 256
TQ = 256
TK = 256
ONES_ROWS = 16
KSEL = LANES
BIAS_ROWS = 16
MT_NONE, MT_FAR, MT_ALL, MT_CAUSAL = 0, TK, 2 * TK, 3 * TK
TM_FFN = 512
TN_FFN = 256
VMEM_LIMIT = 56 * 1024 * 1024

NT_DIMS = (((1,), (1,)), ((), ()))


def _cparams(sem):
    return pltpu.CompilerParams(dimension_semantics=sem, vmem_limit_bytes=VMEM_LIMIT)


def _rms(x, g, eps):
    return x * lax.rsqrt(jnp.mean(x * x, axis=-1, keepdims=True) + eps) * g


def _silu(x):
    return x * jax.nn.sigmoid(x)


def _inproj_kernel(x_ref, g_ref, wa_ref, wb_ref, wt_ref, cos_ref, sin_ref,
                   zs_ref, xbc_ref, qt_ref, kc_ref, vc_ref, ks_ref, vst_ref, kw_ref, vwt_ref,
                   tail_ref, tailt_ref, kbuf_ref, vbuf_ref, *, tiles_per_seq):
    tm = x_ref.shape[0]
    seq_tile = pl.program_id(0) % tiles_per_seq
    h = _rms(x_ref[...], g_ref[...], NORM_EPS).astype(BF16)

    def mm(w_ref, lo, hi):
        return jnp.dot(h, w_ref[:, lo:hi], preferred_element_type=F32)

    zs_ref[...] = _silu(mm(wa_ref, 0, SSM_D_INNER))
    xbc_ref[...] = mm(wa_ref, SSM_D_INNER, SSM_D_INNER + SSM_XBC)

    cos = cos_ref[...]
    sin = sin_ref[...]
    lane = lax.broadcasted_iota(jnp.int32, (tm, LANES), 1)
    first_half = (lane % NSA_HEAD_DIM) < (NSA_HEAD_DIM // 2)

    def rope(xc):
        partner = jnp.where(first_half, pltpu.roll(xc, LANES - 32, 1), pltpu.roll(xc, 32, 1))
        return xc * cos + partner * sin

    q = mm(wb_ref, 0, NSA_Q)
    scale = NSA_HEAD_DIM ** -0.5 * math.log2(math.e)
    for c in range(NSA_Q // LANES):
        rt = (rope(q[:, c * LANES:(c + 1) * LANES]) * scale).T.astype(BF16)
        qt_ref[2 * c] = rt[:NSA_HEAD_DIM]
        qt_ref[2 * c + 1] = rt[NSA_HEAD_DIM:]

    kv = mm(wb_ref, NSA_Q, NSA_Q + 6 * NSA_KV)

    def seg(i):
        return kv[:, i * NSA_KV:(i + 1) * NSA_KV]

    def rope_seg(x):
        return jnp.concatenate([rope(x[:, :LANES]), rope(x[:, LANES:])], axis=1)

    def store_heads(ref, x):
        for g in range(NSA_KV_HEADS):
            ref[g] = x[:, g * NSA_HEAD_DIM:(g + 1) * NSA_HEAD_DIM].astype(BF16)

    def store_heads_t(ref, x):
        xt = x.T.astype(BF16)
        for g in range(NSA_KV_HEADS):
            ref[g] = xt[g * NSA_HEAD_DIM:(g + 1) * NSA_HEAD_DIM]

    kc_tile = rope_seg(seg(0))
    vc_tile = seg(1)
    for c in range(NSA_KV // LANES):
        kbuf_ref[c] = kc_tile[:, c * LANES:(c + 1) * LANES]
        vbuf_ref[c] = vc_tile[:, c * LANES:(c + 1) * LANES]
    gw = CMP_STRIDE * NSA_HEAD_DIM
    for l in range(CMP_STRIDE):
        for c in range(NSA_KV // LANES):
            k_rows = kbuf_ref[c, pl.ds(l, tm // CMP_STRIDE, stride=CMP_STRIDE), :]
            v_rows = vbuf_ref[c, pl.ds(l, tm // CMP_STRIDE, stride=CMP_STRIDE), :]
            for half in range(LANES // NSA_HEAD_DIM):
                g = c * (LANES // NSA_HEAD_DIM) + half
                src = slice(half * NSA_HEAD_DIM, (half + 1) * NSA_HEAD_DIM)
                dst = slice(g * gw + l * NSA_HEAD_DIM, g * gw + (l + 1) * NSA_HEAD_DIM)
                kc_ref[:, dst] = k_rows[:, src]
                vc_ref[:, dst] = v_rows[:, src]
    pos = seq_tile * tm + lax.broadcasted_iota(jnp.int32, (tm, NSA_HEAD_DIM), 0)
    block_in_chunk = (pos // SLC_BLOCK) % (TK // SLC_BLOCK)
    onehot = jnp.where(lax.broadcasted_iota(jnp.int32, (tm, NSA_HEAD_DIM), 1) == block_in_chunk, 1.0, 0.0)
    ks = rope_seg(seg(2))
    for g in range(NSA_KV_HEADS):
        kg = ks[:, g * NSA_HEAD_DIM:(g + 1) * NSA_HEAD_DIM]
        ks_ref[g] = jnp.concatenate([kg, onehot], axis=1).astype(BF16)
    store_heads_t(vst_ref, seg(3))
    store_heads(kw_ref, rope_seg(seg(4)))
    store_heads_t(vwt_ref, seg(5))

    tail = mm(wt_ref, 0, TAIL)
    tail_ref[...] = tail
    tailt_ref[...] = tail.T


def _in_proj(x2, norm_g, w_parts, cos_t, sin_t, seq):
    t = x2.shape[0]
    tm = TM_PROJ
    nseq = seq // tm
    row = lambda i: (i, 0)
    const = lambda i: (0, 0)
    heads = lambda i: (0, i, 0)
    heads_t = lambda i: (0, 0, i)
    out_shape = (
        jax.ShapeDtypeStruct((t, SSM_D_INNER), F32),
        jax.ShapeDtypeStruct((t, SSM_XBC), F32),
        jax.ShapeDtypeStruct((NSA_HEADS, NSA_HEAD_DIM, t), BF16),
        jax.ShapeDtypeStruct((t // CMP_STRIDE, CMP_STRIDE * NSA_KV), F32),
        jax.ShapeDtypeStruct((t // CMP_STRIDE, CMP_STRIDE * NSA_KV), F32),
        jax.ShapeDtypeStruct((NSA_KV_HEADS, t, KSEL), BF16),
        jax.ShapeDtypeStruct((NSA_KV_HEADS, NSA_HEAD_DIM, t), BF16),
        jax.ShapeDtypeStruct((NSA_KV_HEADS, t, NSA_HEAD_DIM), BF16),
        jax.ShapeDtypeStruct((NSA_KV_HEADS, NSA_HEAD_DIM, t), BF16),
        jax.ShapeDtypeStruct((t, TAIL), F32),
        jax.ShapeDtypeStruct((TAIL, t), F32),
    )
    kvh = pl.BlockSpec((NSA_KV_HEADS, tm, NSA_HEAD_DIM), heads)
    kvh_t = pl.BlockSpec((NSA_KV_HEADS, NSA_HEAD_DIM, tm), heads_t)
    out_specs = (
        pl.BlockSpec((tm, SSM_D_INNER), row),
        pl.BlockSpec((tm, SSM_XBC), row),
        pl.BlockSpec((NSA_HEADS, NSA_HEAD_DIM, tm), heads_t),
        pl.BlockSpec((tm // CMP_STRIDE, CMP_STRIDE * NSA_KV), row),
        pl.BlockSpec((tm // CMP_STRIDE, CMP_STRIDE * NSA_KV), row),
        pl.BlockSpec((NSA_KV_HEADS, tm, KSEL), heads), kvh_t, kvh, kvh_t,
        pl.BlockSpec((tm, TAIL), row),
        pl.BlockSpec((TAIL, tm), lambda i: (0, i)),
    )
    in_specs = [
        pl.BlockSpec((tm, D_MODEL), row),
        pl.BlockSpec((1, D_MODEL), const),
        pl.BlockSpec((D_MODEL, SSM_D_INNER + SSM_XBC), const),
        pl.BlockSpec((D_MODEL, NSA_Q + 6 * NSA_KV), const),
        pl.BlockSpec((D_MODEL, TAIL), const),
        pl.BlockSpec((tm, LANES), lambda i: (i % nseq, 0)),
        pl.BlockSpec((tm, LANES), lambda i: (i % nseq, 0)),
    ]
    return pl.pallas_call(
        functools.partial(_inproj_kernel, tiles_per_seq=nseq), grid=(t // tm,), in_specs=in_specs,
        out_specs=out_specs,
        out_shape=out_shape,
        scratch_shapes=[pltpu.VMEM((NSA_KV // LANES, tm, LANES), F32),
                        pltpu.VMEM((NSA_KV // LANES, tm, LANES), F32)],
        compiler_params=_cparams(("arbitrary",)), name="in_proj",
    )(x2, norm_g, *w_parts, cos_t, sin_t)


def _ssd_kernel(xbc_ref, zs_ref, tail_ref, tailt_ref, cw_ref, cb_ref, dtb_ref, dtbt_ref,
                alog_ref, alogt_ref, dskip_ref, ng_ref,
                y_ref, xcat_ref, state_ref, ybuf_ref):
    L, P, N, H, G = SSM_CHUNK, SSM_HEAD_DIM, SSM_STATE, SSM_HEADS, SSM_GROUPS
    HG = H // G
    halo = 8

    @pl.when(pl.program_id(1) == 0)
    def _():
        xcat_ref[0:halo, :] = jnp.zeros((halo, SSM_XBC), F32)
        state_ref[...] = jnp.zeros_like(state_ref)

    xcat_ref[halo:halo + L, :] = xbc_ref[...]
    xcat = xcat_ref[...]
    conv = cb_ref[...] + cw_ref[SSM_CONV - 1:SSM_CONV, :] * xcat[halo:halo + L]
    for k in range(SSM_CONV - 1):
        shifted = pltpu.roll(xcat, SSM_CONV - 1 - k, 0)
        conv = conv + cw_ref[k:k + 1, :] * shifted[halo:halo + L]
    xcat_ref[0:halo, :] = xcat[L:L + halo]
    u = _silu(conv)
    xs = u[:, :SSM_D_INNER]
    bm = u[:, SSM_D_INNER:SSM_D_INNER + G * N]
    cm = u[:, SSM_D_INNER + G * N:]

    def softplus(v):
        return jnp.maximum(v, 0.0) + jnp.log1p(jnp.exp(-jnp.abs(v)))

    dt = softplus(tail_ref[:, 0:H] + dtb_ref[...])
    dtt = softplus(tailt_ref[0:H, :] + dtbt_ref[...])
    da = dt * (-jnp.exp(alog_ref[...]))
    dat = dtt * (-jnp.exp(alogt_ref[...]))
    ri = lax.broadcasted_iota(jnp.int32, (L, L), 0)
    ci = lax.broadcasted_iota(jnp.int32, (L, L), 1)
    tri = ci <= ri
    hi = lax.Precision.HIGHEST
    acs = jnp.dot(tri.astype(F32), da, precision=hi, preferred_element_type=F32)
    acst = jnp.dot(dat, (ri <= ci).astype(F32), precision=hi, preferred_element_type=F32)
    last = acs[L - 1:L, :]
    w_state = dt * jnp.exp(last - acs)
    eacs = jnp.exp(acs)
    cdec = jnp.exp(last)

    hrow = lax.broadcasted_iota(jnp.int32, (H, SSM_D_INNER), 0)
    hcol = lax.broadcasted_iota(jnp.int32, (H, SSM_D_INNER), 1)
    expand = jnp.where(hcol // P == hrow, 1.0, 0.0).astype(BF16)

    def split3(v):
        v_hi = v.astype(BF16)
        r1 = v - v_hi.astype(F32)
        v_mid = r1.astype(BF16)
        v_lo = (r1 - v_mid.astype(F32)).astype(BF16)
        return jnp.concatenate([v_hi, v_mid, v_lo], axis=1)

    small = jnp.concatenate([cdec, dskip_ref[...], jnp.zeros((6, H), F32)], axis=0)
    per_head = jnp.concatenate([dt, w_state, eacs, small], axis=0)
    spread = jnp.dot(split3(per_head), jnp.concatenate([expand] * 3, axis=0), preferred_element_type=F32)
    cdec_e = spread[3 * L:3 * L + 1, :]
    dskip_e = spread[3 * L + 1:3 * L + 2, :]
    xdt = (xs * spread[0:L]).astype(BF16)
    wst = (xs * spread[L:2 * L]).astype(BF16)
    eacs_e = spread[2 * L:3 * L]

    for g in range(G):
        bm_g = bm[:, g * N:(g + 1) * N]
        cm_g = cm[:, g * N:(g + 1) * N].astype(BF16)
        cb = lax.dot_general(cm_g, bm_g.astype(BF16), NT_DIMS, preferred_element_type=F32)
        cols = slice(g * HG * P, (g + 1) * HG * P)
        st = state_ref[:, cols]
        y_off = jnp.dot(cm_g, st.astype(BF16), preferred_element_type=F32) * eacs_e[:, cols]
        ybuf_ref[:, cols] = y_off
        bmt = bm_g.T.astype(BF16)
        state_ref[:, cols] = st * cdec_e[:, cols] + jnp.dot(bmt, wst[:, cols], preferred_element_type=F32)
        for r in range(HG):
            hh = g * HG + r
            diff = acs[:, hh:hh + 1] - acst[hh:hh + 1, :]
            seg = jnp.exp(jnp.where(tri, diff, -jnp.inf))
            lmat = (cb * seg).astype(BF16)
            hc = slice(hh * P, (hh + 1) * P)
            ybuf_ref[:, hc] = ybuf_ref[:, hc] + jnp.dot(lmat, xdt[:, hc], preferred_element_type=F32)

    y = (ybuf_ref[...] + xs * dskip_e) * zs_ref[...]
    gw = SSM_D_INNER // G
    parts = []
    for g in range(G):
        yg = y[:, g * gw:(g + 1) * gw]
        parts.append(yg * lax.rsqrt(jnp.mean(yg * yg, axis=-1, keepdims=True) + SSM_NORM_EPS))
    y_ref[...] = (jnp.concatenate(parts, axis=1) * ng_ref[...]).astype(y_ref.dtype)


def _ssd(xbc, zs, tail, tailt, conv_w, conv_b, dt_bias, a_log, d_skip, norm_g, batch, seq):
    t = xbc.shape[0]
    L = SSM_CHUNK
    nc = seq // L
    row = lambda b, c: (b * nc + c, 0)
    const = lambda b, c: (0, 0)
    H = SSM_HEADS
    in_specs = [
        pl.BlockSpec((L, SSM_XBC), row),
        pl.BlockSpec((L, SSM_D_INNER), row),
        pl.BlockSpec((L, TAIL), row),
        pl.BlockSpec((TAIL, L), lambda b, c: (0, b * nc + c)),
        pl.BlockSpec((SSM_CONV, SSM_XBC), const),
        pl.BlockSpec((1, SSM_XBC), const),
        pl.BlockSpec((1, H), const),
        pl.BlockSpec((H, 1), const),
        pl.BlockSpec((1, H), const),
        pl.BlockSpec((H, 1), const),
        pl.BlockSpec((1, H), const),
        pl.BlockSpec((1, SSM_D_INNER), const),
    ]
    return pl.pallas_call(
        _ssd_kernel, grid=(batch, nc), in_specs=in_specs,
        out_specs=pl.BlockSpec((L, SSM_D_INNER), row),
        out_shape=jax.ShapeDtypeStruct((t, SSM_D_INNER), BF16),
        scratch_shapes=[pltpu.VMEM((L + 8, SSM_XBC), F32),
                        pltpu.VMEM((SSM_STATE, SSM_D_INNER), F32),
                        pltpu.VMEM((L, SSM_D_INNER), F32)],
        compiler_params=_cparams(("arbitrary", "arbitrary")), name="ssd",
    )(xbc, zs, tail, tailt, conv_w, conv_b, dt_bias.reshape(1, H), dt_bias.reshape(H, 1),
      a_log.reshape(1, H), a_log.reshape(H, 1), d_skip.reshape(1, H), norm_g.reshape(1, -1))


def _compress_kernel(k_ref, v_ref, kpos_ref, vpos_ref, kw1_ref, vw1_ref, kb1_ref, vb1_ref,
                     kw2_ref, vw2_ref, kc_ref, vct_ref):
    D = NSA_HEAD_DIM
    gw = CMP_STRIDE * D

    def hidden(t_ref, pos_ref, w1_ref, b1_ref, g):
        t = t_ref[:, g * gw:(g + 1) * gw]
        n = t.shape[0]
        lo = jnp.dot((t + pos_ref[0:1, :]).astype(BF16), w1_ref[...], preferred_element_type=F32)
        hi = jnp.dot((t + pos_ref[1:2, :]).astype(BF16), w1_ref[...], preferred_element_type=F32)
        pre = lo + pltpu.roll(pltpu.roll(hi, n - 1, 0), D, 1)
        return _silu(pre[:, 0:D] + b1_ref[...]).astype(BF16)

    for g in range(NSA_KV_HEADS):
        kc = jnp.dot(hidden(k_ref, kpos_ref, kw1_ref, kb1_ref, g), kw2_ref[...], preferred_element_type=F32)
        kc_ref[g] = kc.astype(BF16)
        vct = lax.dot_general(vw2_ref[...], hidden(v_ref, vpos_ref, vw1_ref, vb1_ref, g), NT_DIMS,
                              preferred_element_type=F32)
        vct_ref[g] = vct.astype(BF16)


def _compress_weights(pos, w1, b1, w2):
    half = CMP_BLOCK // 2 * NSA_HEAD_DIM
    w1cat = jnp.concatenate([w1[:half], w1[half:]], axis=1).astype(BF16)
    return pos.reshape(2, half), w1cat, b1.reshape(1, -1), w2.astype(BF16)


def _compress(k16, v16, kparams, vparams, batch, seq):
    rows = seq // CMP_STRIDE
    width = CMP_STRIDE * NSA_KV
    D = NSA_HEAD_DIM
    gw = CMP_STRIDE * D
    kpos, kw1, kb1, kw2 = _compress_weights(*kparams)
    vpos, vw1, vb1, vw2 = _compress_weights(*vparams)
    vw2 = vw2.T
    c2 = lambda b: (0, 0)
    tok = pl.BlockSpec((rows, width), lambda b: (b, 0))
    in_specs = [tok, tok,
                pl.BlockSpec((2, gw), c2), pl.BlockSpec((2, gw), c2),
                pl.BlockSpec((gw, 2 * D), c2), pl.BlockSpec((gw, 2 * D), c2),
                pl.BlockSpec((1, D), c2), pl.BlockSpec((1, D), c2),
                pl.BlockSpec((D, D), c2), pl.BlockSpec((D, D), c2)]
    out = jax.ShapeDtypeStruct((NSA_KV_HEADS, batch * rows, D), BF16)
    out_t = jax.ShapeDtypeStruct((NSA_KV_HEADS, D, batch * rows), BF16)
    ospec = pl.BlockSpec((NSA_KV_HEADS, rows, D), lambda b: (0, b, 0))
    ospec_t = pl.BlockSpec((NSA_KV_HEADS, D, rows), lambda b: (0, 0, b))
    return pl.pallas_call(
        _compress_kernel, grid=(batch,), in_specs=in_specs, out_specs=(ospec, ospec_t),
        out_shape=(out, out_t), compiler_params=_cparams(("arbitrary",)), name="compress",
    )(k16, v16, kpos, vpos, kw1, vw1, kb1, vb1, kw2, vw2)


def _nsa_kernel(qt_ref, kc_ref, vct_ref, ks_ref, vst_ref, kw_ref, vwt_ref, tailt_ref, ovt_ref, mtab_ref, cmask_ref,
                o_ref, biasq_ref, m_ref, acc_ref, part_ref, sbuf0_ref, sbuf1_ref):
    R, D = NSA_REP, NSA_HEAD_DIM
    nl = R * TQ
    sbuf_refs = (sbuf0_ref, sbuf1_ref)
    n_stages = (ks_ref.shape[1] // TK - 1) // 2
    g = pl.program_id(1)
    qi = pl.program_id(2)
    t0 = qi * TQ
    qt = jnp.concatenate([qt_ref[r] for r in range(R)], axis=1)
    tpos_row = t0 + lax.broadcasted_iota(jnp.int32, (1, nl), 1) % TQ

    def k_rows(k_ref, start):
        return k_ref[0, pl.ds(pl.multiple_of(start, TK), TK), :]

    def vt_cols(vt_ref, start, n):
        return vt_ref[0, :, pl.ds(pl.multiple_of(start, TK), n)]

    def pv(vt, p):
        vt1 = jnp.concatenate([vt, jnp.ones((ONES_ROWS, vt.shape[1]), BF16)], axis=0)
        return jnp.dot(vt1, p, preferred_element_type=F32)

    def mask_tile(off):
        return jnp.concatenate([mtab_ref[pl.ds(pl.multiple_of(off, TK), TK), :]] * R, axis=1)

    def online(state, s, vt):
        m_chunk = jnp.max(s, axis=0, keepdims=True)
        m_new = m_chunk if state is None else jnp.maximum(state[0], m_chunk)
        contrib = pv(vt, jnp.exp2(s - m_new).astype(BF16))
        if state is None:
            return m_new, contrib
        return m_new, jnp.exp2(state[0] - m_new) * state[1] + contrib

    causal = mask_tile(MT_CAUSAL)
    far0 = jnp.maximum(t0 - 2 * TK, 0)
    mid0 = jnp.maximum(t0 - TK, 0)
    s_dia = jnp.dot(k_rows(kw_ref, t0), qt, preferred_element_type=F32) + causal
    s_mid = (jnp.dot(k_rows(kw_ref, mid0), qt, preferred_element_type=F32)
             + mask_tile(jnp.where(qi >= 1, MT_ALL, MT_NONE)))
    s_far = (jnp.dot(k_rows(kw_ref, far0), qt, preferred_element_type=F32)
             + mask_tile(jnp.where(qi >= 2, MT_FAR, MT_NONE)))

    ncr = kc_ref.shape[1]
    cmask = cmask_ref[pl.ds(pl.multiple_of(qi * ncr, ncr), ncr), :]
    s_m = jnp.dot(kc_ref[0], qt, preferred_element_type=F32) + jnp.concatenate([cmask] * R, axis=1)
    mx = jnp.max(s_m, axis=0, keepdims=True)
    p = jnp.exp2(s_m - mx)
    den = jnp.sum(p, axis=0, keepdims=True)
    sees_any = tpos_row >= CMP_BLOCK - 1
    pc = p * jnp.where(sees_any, 1.0 / den, 0.0)
    o_c = jnp.dot(vct_ref[0], pc.astype(BF16), preferred_element_type=F32)

    psum = pc[:, 0:TQ]
    for r in range(1, R):
        psum = psum + pc[:, r * TQ:(r + 1) * TQ]
    n_slc = ovt_ref.shape[0]
    imp_t = jnp.dot(ovt_ref[...], psum, precision=lax.Precision.HIGHEST,
                    preferred_element_type=F32)
    jj = lax.broadcasted_iota(jnp.int32, (n_slc, TQ), 0)
    tt = t0 + lax.broadcasted_iota(jnp.int32, (n_slc, TQ), 1)
    lag = tt // SLC_BLOCK - jj
    forced = (jj == 0) | ((lag >= 0) & (lag < N_LOCAL_BLOCKS))
    valid = jj * SLC_BLOCK <= tt
    score = jnp.where(forced, FORCED_SCORE, jnp.where(valid, imp_t, -1.0))
    rows8 = 8
    j8 = lax.broadcasted_iota(jnp.int32, (rows8, TQ), 0)
    groups = [score[a:a + rows8] for a in range(0, n_slc, rows8)]
    ranks = [jnp.zeros((rows8, TQ), F32) for _ in groups]
    for j2 in range(n_slc):
        sj = score[j2:j2 + 1, :]
        for gi, sg in enumerate(groups):
            lo = gi * rows8
            if lo + rows8 - 1 < j2:
                beats = sj > sg
            elif lo > j2:
                beats = sj >= sg
            else:
                beats = (sj > sg) | ((sj == sg) & (j8 + lo > j2))
            ranks[gi] = ranks[gi] + jnp.where(beats, 1.0, 0.0)
    rank = jnp.concatenate(ranks, axis=0)
    selected = (rank < float(min(SLC_TOP_N, n_slc))) & (score >= 0.0)
    bias_t = jnp.where(selected, 0.0, NEG_BIG)
    per = TK // SLC_BLOCK
    zrows = jnp.zeros((BIAS_ROWS - per, TQ), F32)
    for kb in range(n_slc // per):
        blk = jnp.concatenate([bias_t[kb * per:(kb + 1) * per], zrows], axis=0)
        biasq_ref[kb * BIAS_ROWS:(kb + 1) * BIAS_ROWS, :] = jnp.concatenate([blk] * R, axis=1).astype(BF16)
    qpad = jnp.zeros((KSEL - D - BIAS_ROWS, nl), BF16)

    def q_sel(kb):
        rows = biasq_ref[pl.ds(pl.multiple_of(kb * BIAS_ROWS, BIAS_ROWS), BIAS_ROWS), :]
        return jnp.concatenate([qt, rows, qpad], axis=0)

    def slc_scores(kb):
        return jnp.dot(k_rows(ks_ref, kb * TK), q_sel(kb), preferred_element_type=F32)

    def pair_scores(pair):
        return jnp.concatenate([slc_scores(2 * pair), slc_scores(2 * pair + 1)], axis=0)

    npairs = qi // 2
    s_sel = slc_scores(qi) + causal
    sbuf_refs[0][...] = pair_scores(0)

    win = online(None, s_dia, vt_cols(vwt_ref, t0, TK))
    win = online(win, s_mid, vt_cols(vwt_ref, mid0, TK))
    accw = online(win, s_far, vt_cols(vwt_ref, far0, TK))[1]
    o_w = accw[0:D] / accw[D:D + 1]

    gate_row = pl.multiple_of(GATE_OFF + GATE_SLOT * g, 8)
    sig = jax.nn.sigmoid(tailt_ref[pl.ds(gate_row, GATE_SLOT), :])

    def gate(c):
        return jnp.concatenate([sig[3 * r + c:3 * r + c + 1, :] for r in range(R)], axis=1)

    part_ref[...] = gate(0) * o_c + gate(2) * o_w

    m_sel, acc_sel = online(None, s_sel, vt_cols(vst_ref, t0, TK))
    m_ref[...] = m_sel
    acc_ref[...] = acc_sel

    def absorb(chunks):
        state = (m_ref[...], acc_ref[...])
        for kb, s in chunks:
            state = online(state, s, vt_cols(vst_ref, kb * TK, TK))
        m_ref[...] = state[0]
        acc_ref[...] = state[1]

    for k in range(n_stages):
        cur, nxt = sbuf_refs[k % 2], sbuf_refs[(k + 1) % 2]

        @pl.when(npairs > k)
        def _(k=k, cur=cur, nxt=nxt):
            if k + 1 < n_stages:
                nxt[...] = pair_scores(jnp.minimum(k + 1, npairs - 1))
            absorb([(2 * k + c, cur[c * TK:(c + 1) * TK, :]) for c in range(2)])

    @pl.when(qi % 2 == 1)
    def _():
        absorb([(qi - 1, slc_scores(qi - 1))])

    acc = acc_ref[...]
    ot = part_ref[...] + gate(1) * (acc[0:D] / acc[D:D + 1])
    stacked = jnp.concatenate([ot[:, r * TQ:(r + 1) * TQ] for r in range(R)], axis=0)
    o_ref[...] = stacked.T.astype(o_ref.dtype)


def _overlap_t(seq):
    n_cmp = (seq - CMP_BLOCK) // CMP_STRIDE + 1
    n_slc = seq // SLC_BLOCK
    cs = np.arange(n_cmp) * CMP_STRIDE
    ss = np.arange(n_slc) * SLC_BLOCK
    overlap = np.clip(np.minimum(cs[:, None] + CMP_BLOCK, ss[None, :] + SLC_BLOCK)
                      - np.maximum(cs[:, None], ss[None, :]), 0, None) / CMP_BLOCK
    ovt = np.zeros((n_slc, n_cmp + 1), np.float32)
    ovt[:, :n_cmp] = overlap.T
    return jnp.asarray(ovt)


def _mask_tiles():
    ki = np.arange(TK)[:, None]
    qi = np.arange(TQ)[None, :]
    neg = np.float32(NEG_BIG)
    none = np.full((TK, TQ), neg, np.float32)
    far = np.where(ki > qi, np.float32(0), neg)
    full = np.zeros((TK, TQ), np.float32)
    causal = np.where(ki <= qi, np.float32(0), neg)
    return jnp.asarray(np.concatenate([none, far, full, causal], axis=0))


def _cmp_masks(seq):
    ncr = seq // CMP_STRIDE
    n_cmp = (seq - CMP_BLOCK) // CMP_STRIDE + 1
    n = np.arange(ncr)[None, :, None]
    t = (np.arange(seq // TQ)[:, None, None] * TQ + np.arange(TQ)[None, None, :])
    visible = (n * CMP_STRIDE + CMP_BLOCK - 1 <= t) & (n < n_cmp)
    return jnp.asarray(np.where(visible, np.float32(0), np.float32(NEG_BIG)).reshape(-1, TQ))


def _nsa(qt, kc, vct, ks, vst, kw, vwt, tailt, batch, seq):
    t = batch * seq
    G, R, D = NSA_KV_HEADS, NSA_REP, NSA_HEAD_DIM
    nq = seq // TQ
    ncr = seq // CMP_STRIDE
    n_slc = seq // SLC_BLOCK
    assert TQ == TK and WINDOW == 2 * TK, "window branch visits exactly three key chunks"
    assert TK // SLC_BLOCK <= BIAS_ROWS and D + BIAS_ROWS <= KSEL
    const = lambda b, g, i: (0, 0)
    vtspec = pl.BlockSpec((1, D, seq), lambda b, g, i: (g, 0, b))
    in_specs = [
        pl.BlockSpec((R, D, TQ), lambda b, g, i: (g, 0, b * nq + i)),
        pl.BlockSpec((1, ncr, D), lambda b, g, i: (g, b, 0)),
        pl.BlockSpec((1, D, ncr), lambda b, g, i: (g, 0, b)),
        pl.BlockSpec((1, seq, KSEL), lambda b, g, i: (g, b, 0)),
        vtspec,
        pl.BlockSpec((1, seq, D), lambda b, g, i: (g, b, 0)),
        vtspec,
        pl.BlockSpec((TAIL, TQ), lambda b, g, i: (0, b * nq + i)),
        pl.BlockSpec((n_slc, ncr), const),
        pl.BlockSpec((4 * TK, TQ), const),
        pl.BlockSpec((nq * ncr, TQ), const),
    ]
    return pl.pallas_call(
        _nsa_kernel, grid=(batch, G, nq), in_specs=in_specs,
        out_specs=pl.BlockSpec((TQ, R * D), lambda b, g, i: (b * nq + i, g)),
        out_shape=jax.ShapeDtypeStruct((t, NSA_Q), BF16),
        scratch_shapes=[pltpu.VMEM((seq // TK * BIAS_ROWS, R * TQ), BF16),
                        pltpu.VMEM((1, R * TQ), F32),
                        pltpu.VMEM((D + ONES_ROWS, R * TQ), F32),
                        pltpu.VMEM((D, R * TQ), F32),
                        pltpu.VMEM((2 * TK, R * TQ), F32),
                        pltpu.VMEM((2 * TK, R * TQ), F32)],
        compiler_params=_cparams(("arbitrary", "arbitrary", "arbitrary")), name="nsa",
    )(qt, kc, vct, ks, vst, kw, vwt, tailt, _overlap_t(seq), _mask_tiles(), _cmp_masks(seq))


def _outproj_kernel(y_ref, o_ref, x_ref, w_ref, ag_ref, ng_ref, x1_ref, h2_ref):
    yn = _rms(o_ref[...].astype(F32), ag_ref[...], NORM_EPS).astype(BF16)
    x1 = (x_ref[...]
          + jnp.dot(y_ref[...], w_ref[0:SSM_D_INNER, :], preferred_element_type=F32)
          + jnp.dot(yn, w_ref[SSM_D_INNER:, :], preferred_element_type=F32))
    x1_ref[...] = x1
    h2_ref[...] = _rms(x1, ng_ref[...], NORM_EPS).astype(BF16)


def _out_proj(y_ssm, o_nsa, x2, w_out, attn_g, norm2_g):
    t = x2.shape[0]
    tm = TM_PROJ
    row = lambda i: (i, 0)
    const = lambda i: (0, 0)
    tok = pl.BlockSpec((tm, D_MODEL), row)
    vec = pl.BlockSpec((1, D_MODEL), const)
    return pl.pallas_call(
        _outproj_kernel, grid=(t // tm,),
        in_specs=[tok, tok, tok, pl.BlockSpec((SSM_D_INNER + NSA_Q, D_MODEL), const), vec, vec],
        out_specs=(tok, tok),
        out_shape=(jax.ShapeDtypeStruct((t, D_MODEL), F32), jax.ShapeDtypeStruct((t, D_MODEL), BF16)),
        compiler_params=_cparams(("arbitrary",)), name="out_proj",
    )(y_ssm, o_nsa, x2, w_out, attn_g, norm2_g)


def _ffn_kernel(h_ref, halo_ref, x1_ref, wup_ref, cw_ref, cb_ref, wd_ref, fg_ref, out_ref, act_ref,
                *, tiles_per_seq):
    i = pl.program_id(0)
    tm = h_ref.shape[0]
    pad = halo_ref.shape[0]
    tn = TN_FFN
    halo = halo_ref[...]
    halo = jnp.where(i % tiles_per_seq == 0, jnp.zeros_like(halo), halo)
    hc = jnp.concatenate([halo, h_ref[...]], axis=0)

    def branch(c0):
        cols = slice(c0, c0 + tn)
        u = jnp.dot(hc, wup_ref[:, cols], preferred_element_type=F32)
        out = cb_ref[:, cols]
        for k in range(FFN_CONV):
            off = pad - (FFN_CONV - 1) + k
            out = out + cw_ref[k:k + 1, cols] * u[off:off + tm, :]
        return out

    for j in range(D_FF // tn):
        act = _silu(branch(j * tn)) * branch(D_FF + j * tn)
        act_ref[:, j * tn:(j + 1) * tn] = act.astype(BF16)
    down = jnp.dot(act_ref[...], wd_ref[...], preferred_element_type=F32)
    out_ref[...] = _rms(x1_ref[...] + down, fg_ref[...], NORM_EPS)


def _ffn(h2, x1, w_up, conv_w, conv_b, w_down, final_g, seq):
    t = h2.shape[0]
    tm = TM_FFN
    pad = BF16_SUBLANES
    tok = pl.BlockSpec((tm, D_MODEL), lambda i: (i, 0))

    def resident(shape):
        return pl.BlockSpec(shape, lambda i: (0, 0), pipeline_mode=pl.Buffered(1))

    in_specs = [
        tok,
        pl.BlockSpec((pad, D_MODEL), lambda i: (jnp.maximum(i * (tm // pad) - 1, 0), 0)),
        tok,
        resident((D_MODEL, 2 * D_FF)),
        resident((FFN_CONV, 2 * D_FF)),
        resident((1, 2 * D_FF)),
        resident((D_FF, D_MODEL)),
        resident((1, D_MODEL)),
    ]
    return pl.pallas_call(
        functools.partial(_ffn_kernel, tiles_per_seq=seq // tm), grid=(t // tm,),
        in_specs=in_specs, out_specs=tok,
        out_shape=jax.ShapeDtypeStruct((t, D_MODEL), F32),
        scratch_shapes=[pltpu.VMEM((tm, D_FF), BF16)],
        compiler_params=_cparams(("arbitrary",)), name="ffn",
    )(h2, h2, x1, w_up, conv_w, conv_b, w_down, final_g)


def _rope_tables(seq):
    half = NSA_HEAD_DIM // 2
    inv_freq = 1.0 / (ROPE_THETA ** (jnp.arange(0, NSA_HEAD_DIM, 2, dtype=F32) / NSA_HEAD_DIM))
    ang = jnp.arange(seq).astype(F32)[:, None] * inv_freq[None, :]
    cos, sin = jnp.cos(ang), jnp.sin(ang)
    reps = LANES // NSA_HEAD_DIM
    cos_t = jnp.tile(jnp.concatenate([cos, cos], axis=1), (1, reps))
    sin_t = jnp.tile(jnp.concatenate([-sin, sin], axis=1), (1, reps))
    return cos_t, sin_t


def _split_w_in(w):
    o_z, o_xbc, o_dt, o_q, o_kv, o_g = np.cumsum([0, SSM_D_INNER, SSM_XBC, SSM_HEADS, NSA_Q, 6 * NSA_KV]).tolist()
    per = 3 * NSA_REP
    tail_cols = [w[:, o_dt:o_q]]
    for g in range(NSA_KV_HEADS):
        tail_cols += [w[:, o_g + g * per:o_g + (g + 1) * per], jnp.zeros((w.shape[0], GATE_SLOT - per), w.dtype)]
    tail_cols.append(jnp.zeros((w.shape[0], TAIL - GATE_OFF - NSA_KV_HEADS * GATE_SLOT), w.dtype))
    return (w[:, o_z:o_dt].astype(BF16), w[:, o_q:o_g].astype(BF16),
            jnp.concatenate(tail_cols, axis=1).astype(BF16))


def kernel(x, norm1_g, w_in, ssm_conv_w, ssm_conv_b, ssm_dt_bias, ssm_a_log, ssm_d, ssm_norm_g, cmp_k_pos, cmp_k_w1, cmp_k_b1, cmp_k_w2, cmp_v_pos, cmp_v_w1, cmp_v_b1, cmp_v_w2, attn_norm_g, w_out, norm2_g, ffn_w_up, ffn_conv_w, ffn_conv_b, ffn_w_down, final_norm_g):
    batch, seq, d = x.shape
    assert w_in.shape[0] == 1, "single-layer problem"
    l = 0
    cos_t, sin_t = _rope_tables(seq)
    x2 = x.reshape(batch * seq, d)
    zs, xbc, qt, kc_r, vc_r, ks, vst, kw, vwt, tail, tailt = _in_proj(
        x2, norm1_g[l].reshape(1, d), _split_w_in(w_in.reshape(d, -1)), cos_t, sin_t, seq)
    y_ssm = _ssd(xbc, zs, tail, tailt, ssm_conv_w[l], ssm_conv_b[l].reshape(1, -1), ssm_dt_bias[l],
                 ssm_a_log[l], ssm_d[l], ssm_norm_g[l], batch, seq)
    kc, vct = _compress(kc_r, vc_r,
                        (cmp_k_pos[l], cmp_k_w1[l], cmp_k_b1[l], cmp_k_w2[l]),
                        (cmp_v_pos[l], cmp_v_w1[l], cmp_v_b1[l], cmp_v_w2[l]), batch, seq)
    o_nsa = _nsa(qt, kc, vct, ks, vst, kw, vwt, tailt, batch, seq)
    x1, h2 = _out_proj(y_ssm, o_nsa, x2, w_out.reshape(-1, d).astype(BF16), attn_norm_g[l].reshape(1, d),
                       norm2_g[l].reshape(1, d))
    out = _ffn(h2, x1, ffn_w_up.reshape(d, -1).astype(BF16), ffn_conv_w[l], ffn_conv_b[l].reshape(1, -1),
               ffn_w_down.reshape(-1, d).astype(BF16), final_norm_g.reshape(1, d), seq)
    return out.reshape(batch, seq, d)
```

```python
import functools
import math

import numpy as np
import jax
import jax.numpy as jnp
from jax import lax
from jax.experimental import pallas as pl
from jax.experimental.pallas import tpu as pltpu

F32 = jnp.float32
BF16 = jnp.bfloat16

D_MODEL = 1024
SSM_D_INNER = 1024
SSM_HEAD_DIM = 64
SSM_HEADS = 16
SSM_GROUPS = 2
SSM_STATE = 128
SSM_CONV = 4
SSM_CHUNK = 128
SSM_XBC = SSM_D_INNER + 2 * SSM_GROUPS * SSM_STATE
SSM_NORM_EPS = 1e-5
NSA_HEADS = 16
NSA_KV_HEADS = 4
NSA_REP = NSA_HEADS // NSA_KV_HEADS
NSA_HEAD_DIM = 64
NSA_Q = NSA_HEADS * NSA_HEAD_DIM
NSA_KV = NSA_KV_HEADS * NSA_HEAD_DIM
CMP_BLOCK = 32
CMP_STRIDE = 16
SLC_BLOCK = 64
SLC_TOP_N = 16
N_LOCAL_BLOCKS = 2
FORCED_SCORE = 1e4
WINDOW = 512
ROPE_THETA = 10000.0
D_FF = 2816
FFN_CONV = 3
NORM_EPS = 1e-6
NEG_BIG = -1e30

LANES = 128
BF16_SUBLANES = 16
TAIL = LANES
GATE_OFF = SSM_HEADS
GATE_SLOT = 16

TM_PROJ = 256
TQ = 256
TK = 256
ONES_ROWS = 16
KSEL = LANES
BIAS_ROWS = 16
MT_NONE, MT_FAR, MT_ALL, MT_CAUSAL = 0, TK, 2 * TK, 3 * TK
TM_FFN = 512
TN_FFN = 256
VMEM_LIMIT = 56 * 1024 * 1024

NT_DIMS = (((1,), (1,)), ((), ()))


def _cparams(sem):
    return pltpu.CompilerParams(dimension_semantics=sem, vmem_limit_bytes=VMEM_LIMIT)


def _rms(x, g, eps):
    return x * lax.rsqrt(jnp.mean(x * x, axis=-1, keepdims=True) + eps) * g


def _silu(x):
    return x * jax.nn.sigmoid(x)


def _inproj_kernel(x_ref, g_ref, wa_ref, wb_ref, wt_ref, cos_ref, sin_ref,
                   zs_ref, xbc_ref, qt_ref, kc_ref, vc_ref, ks_ref, vst_ref, kw_ref, vwt_ref,
                   tail_ref, tailt_ref, kbuf_ref, vbuf_ref, *, tiles_per_seq):
    tm = x_ref.shape[0]
    seq_tile = pl.program_id(0) % tiles_per_seq
    h = _rms(x_ref[...], g_ref[...], NORM_EPS).astype(BF16)

    def mm(w_ref, lo, hi):
        return jnp.dot(h, w_ref[:, lo:hi], preferred_element_type=F32)

    zs_ref[...] = _silu(mm(wa_ref, 0, SSM_D_INNER))
    xbc_ref[...] = mm(wa_ref, SSM_D_INNER, SSM_D_INNER + SSM_XBC)

    cos = cos_ref[...]
    sin = sin_ref[...]
    lane = lax.broadcasted_iota(jnp.int32, (tm, LANES), 1)
    first_half = (lane % NSA_HEAD_DIM) < (NSA_HEAD_DIM // 2)

    def rope(xc):
        partner = jnp.where(first_half, pltpu.roll(xc, LANES - 32, 1), pltpu.roll(xc, 32, 1))
        return xc * cos + partner * sin

    q = mm(wb_ref, 0, NSA_Q)
    scale = NSA_HEAD_DIM ** -0.5 * math.log2(math.e)
    for c in range(NSA_Q // LANES):
        rt = (rope(q[:, c * LANES:(c + 1) * LANES]) * scale).T.astype(BF16)
        qt_ref[2 * c] = rt[:NSA_HEAD_DIM]
        qt_ref[2 * c + 1] = rt[NSA_HEAD_DIM:]

    kv = mm(wb_ref, NSA_Q, NSA_Q + 6 * NSA_KV)

    def seg(i):
        return kv[:, i * NSA_KV:(i + 1) * NSA_KV]

    def rope_seg(x):
        return jnp.concatenate([rope(x[:, :LANES]), rope(x[:, LANES:])], axis=1)

    def store_heads(ref, x):
        for g in range(NSA_KV_HEADS):
            ref[g] = x[:, g * NSA_HEAD_DIM:(g + 1) * NSA_HEAD_DIM].astype(BF16)

    def store_heads_t(ref, x):
        xt = x.T.astype(BF16)
        for g in range(NSA_KV_HEADS):
            ref[g] = xt[g * NSA_HEAD_DIM:(g + 1) * NSA_HEAD_DIM]

    kc_tile = rope_seg(seg(0))
    vc_tile = seg(1)
    for c in range(NSA_KV // LANES):
        kbuf_ref[c] = kc_tile[:, c * LANES:(c + 1) * LANES]
        vbuf_ref[c] = vc_tile[:, c * LANES:(c + 1) * LANES]
    gw = CMP_STRIDE * NSA_HEAD_DIM
    for l in range(CMP_STRIDE):
        for c in range(NSA_KV // LANES):
            k_rows = kbuf_ref[c, pl.ds(l, tm // CMP_STRIDE, stride=CMP_STRIDE), :]
            v_rows = vbuf_ref[c, pl.ds(l, tm // CMP_STRIDE, stride=CMP_STRIDE), :]
            for half in range(LANES // NSA_HEAD_DIM):
                g = c * (LANES // NSA_HEAD_DIM) + half
                src = slice(half * NSA_HEAD_DIM, (half + 1) * NSA_HEAD_DIM)
                dst = slice(g * gw + l * NSA_HEAD_DIM, g * gw + (l + 1) * NSA_HEAD_DIM)
                kc_ref[:, dst] = k_rows[:, src]
                vc_ref[:, dst] = v_rows[:, src]
    pos = seq_tile * tm + lax.broadcasted_iota(jnp.int32, (tm, NSA_HEAD_DIM), 0)
    block_in_chunk = (pos // SLC_BLOCK) % (TK // SLC_BLOCK)
    onehot = jnp.where(lax.broadcasted_iota(jnp.int32, (tm, NSA_HEAD_DIM), 1) == block_in_chunk, 1.0, 0.0)
    ks = rope_seg(seg(2))
    for g in range(NSA_KV_HEADS):
        kg = ks[:, g * NSA_HEAD_DIM:(g + 1) * NSA_HEAD_DIM]
        ks_ref[g] = jnp.concatenate([kg, onehot], axis=1).astype(BF16)
    store_heads_t(vst_ref, seg(3))
    store_heads(kw_ref, rope_seg(seg(4)))
    store_heads_t(vwt_ref, seg(5))

    tail = mm(wt_ref, 0, TAIL)
    tail_ref[...] = tail
    tailt_ref[...] = tail.T


def _in_proj(x2, norm_g, w_parts, cos_t, sin_t, seq):
    t = x2.shape[0]
    tm = TM_PROJ
    nseq = seq // tm
    row = lambda i: (i, 0)
    const = lambda i: (0, 0)
    heads = lambda i: (0, i, 0)
    heads_t = lambda i: (0, 0, i)
    out_shape = (
        jax.ShapeDtypeStruct((t, SSM_D_INNER), F32),
        jax.ShapeDtypeStruct((t, SSM_XBC), F32),
        jax.ShapeDtypeStruct((NSA_HEADS, NSA_HEAD_DIM, t), BF16),
        jax.ShapeDtypeStruct((t // CMP_STRIDE, CMP_STRIDE * NSA_KV), F32),
        jax.ShapeDtypeStruct((t // CMP_STRIDE, CMP_STRIDE * NSA_KV), F32),
        jax.ShapeDtypeStruct((NSA_KV_HEADS, t, KSEL), BF16),
        jax.ShapeDtypeStruct((NSA_KV_HEADS, NSA_HEAD_DIM, t), BF16),
        jax.ShapeDtypeStruct((NSA_KV_HEADS, t, NSA_HEAD_DIM), BF16),
        jax.ShapeDtypeStruct((NSA_KV_HEADS, NSA_HEAD_DIM, t), BF16),
        jax.ShapeDtypeStruct((t, TAIL), F32),
        jax.ShapeDtypeStruct((TAIL, t), F32),
    )
    kvh = pl.BlockSpec((NSA_KV_HEADS, tm, NSA_HEAD_DIM), heads)
    kvh_t = pl.BlockSpec((NSA_KV_HEADS, NSA_HEAD_DIM, tm), heads_t)
    out_specs = (
        pl.BlockSpec((tm, SSM_D_INNER), row),
        pl.BlockSpec((tm, SSM_XBC), row),
        pl.BlockSpec((NSA_HEADS, NSA_HEAD_DIM, tm), heads_t),
        pl.BlockSpec((tm // CMP_STRIDE, CMP_STRIDE * NSA_KV), row),
        pl.BlockSpec((tm // CMP_STRIDE, CMP_STRIDE * NSA_KV), row),
        pl.BlockSpec((NSA_KV_HEADS, tm, KSEL), heads), kvh_t, kvh, kvh_t,
        pl.BlockSpec((tm, TAIL), row),
        pl.BlockSpec((TAIL, tm), lambda i: (0, i)),
    )
    in_specs = [
        pl.BlockSpec((tm, D_MODEL), row),
        pl.BlockSpec((1, D_MODEL), const),
        pl.BlockSpec((D_MODEL, SSM_D_INNER + SSM_XBC), const),
        pl.BlockSpec((D_MODEL, NSA_Q + 6 * NSA_KV), const),
        pl.BlockSpec((D_MODEL, TAIL), const),
        pl.BlockSpec((tm, LANES), lambda i: (i % nseq, 0)),
        pl.BlockSpec((tm, LANES), lambda i: (i % nseq, 0)),
    ]
    return pl.pallas_call(
        functools.partial(_inproj_kernel, tiles_per_seq=nseq), grid=(t // tm,), in_specs=in_specs,
        out_specs=out_specs,
        out_shape=out_shape,
        scratch_shapes=[pltpu.VMEM((NSA_KV // LANES, tm, LANES), F32),
                        pltpu.VMEM((NSA_KV // LANES, tm, LANES), F32)],
        compiler_params=_cparams(("arbitrary",)), name="in_proj",
    )(x2, norm_g, *w_parts, cos_t, sin_t)


def _ssd_kernel(xbc_ref, zs_ref, tail_ref, tailt_ref, cw_ref, cb_ref, dtb_ref, dtbt_ref,
                alog_ref, alogt_ref, dskip_ref, ng_ref,
                y_ref, xcat_ref, state_ref, ybuf_ref):
    L, P, N, H, G = SSM_CHUNK, SSM_HEAD_DIM, SSM_STATE, SSM_HEADS, SSM_GROUPS
    HG = H // G
    halo = 8

    @pl.when(pl.program_id(1) == 0)
    def _():
        xcat_ref[0:halo, :] = jnp.zeros((halo, SSM_XBC), F32)
        state_ref[...] = jnp.zeros_like(state_ref)

    xcat_ref[halo:halo + L, :] = xbc_ref[...]
    xcat = xcat_ref[...]
    conv = cb_ref[...] + cw_ref[SSM_CONV - 1:SSM_CONV, :] * xcat[halo:halo + L]
    for k in range(SSM_CONV - 1):
        shifted = pltpu.roll(xcat, SSM_CONV - 1 - k, 0)
        conv = conv + cw_ref[k:k + 1, :] * shifted[halo:halo + L]
    xcat_ref[0:halo, :] = xcat[L:L + halo]
    u = _silu(conv)
    xs = u[:, :SSM_D_INNER]
    bm = u[:, SSM_D_INNER:SSM_D_INNER + G * N]
    cm = u[:, SSM_D_INNER + G * N:]

    def softplus(v):
        return jnp.maximum(v, 0.0) + jnp.log1p(jnp.exp(-jnp.abs(v)))

    dt = softplus(tail_ref[:, 0:H] + dtb_ref[...])
    dtt = softplus(tailt_ref[0:H, :] + dtbt_ref[...])
    da = dt * (-jnp.exp(alog_ref[...]))
    dat = dtt * (-jnp.exp(alogt_ref[...]))
    ri = lax.broadcasted_iota(jnp.int32, (L, L), 0)
    ci = lax.broadcasted_iota(jnp.int32, (L, L), 1)
    tri = ci <= ri
    hi = lax.Precision.HIGHEST
    acs = jnp.dot(tri.astype(F32), da, precision=hi, preferred_element_type=F32)
    acst = jnp.dot(dat, (ri <= ci).astype(F32), precision=hi, preferred_element_type=F32)
    last = acs[L - 1:L, :]
    w_state = dt * jnp.exp(last - acs)
    eacs = jnp.exp(acs)
    cdec = jnp.exp(last)

    hrow = lax.broadcasted_iota(jnp.int32, (H, SSM_D_INNER), 0)
    hcol = lax.broadcasted_iota(jnp.int32, (H, SSM_D_INNER), 1)
    expand = jnp.where(hcol // P == hrow, 1.0, 0.0).astype(BF16)

    def split3(v):
        v_hi = v.astype(BF16)
        r1 = v - v_hi.astype(F32)
        v_mid = r1.astype(BF16)
        v_lo = (r1 - v_mid.astype(F32)).astype(BF16)
        return jnp.concatenate([v_hi, v_mid, v_lo], axis=1)

    small = jnp.concatenate([cdec, dskip_ref[...], jnp.zeros((6, H), F32)], axis=0)
    per_head = jnp.concatenate([dt, w_state, eacs, small], axis=0)
    spread = jnp.dot(split3(per_head), jnp.concatenate([expand] * 3, axis=0), preferred_element_type=F32)
    cdec_e = spread[3 * L:3 * L + 1, :]
    dskip_e = spread[3 * L + 1:3 * L + 2, :]
    xdt = (xs * spread[0:L]).astype(BF16)
    wst = (xs * spread[L:2 * L]).astype(BF16)
    eacs_e = spread[2 * L:3 * L]

    for g in range(G):
        bm_g = bm[:, g * N:(g + 1) * N]
        cm_g = cm[:, g * N:(g + 1) * N].astype(BF16)
        cb = lax.dot_general(cm_g, bm_g.astype(BF16), NT_DIMS, preferred_element_type=F32)
        cols = slice(g * HG * P, (g + 1) * HG * P)
        st = state_ref[:, cols]
        y_off = jnp.dot(cm_g, st.astype(BF16), preferred_element_type=F32) * eacs_e[:, cols]
        ybuf_ref[:, cols] = y_off
        bmt = bm_g.T.astype(BF16)
        state_ref[:, cols] = st * cdec_e[:, cols] + jnp.dot(bmt, wst[:, cols], preferred_element_type=F32)
        for r in range(HG):
            hh = g * HG + r
            diff = acs[:, hh:hh + 1] - acst[hh:hh + 1, :]
            seg = jnp.exp(jnp.where(tri, diff, -jnp.inf))
            lmat = (cb * seg).astype(BF16)
            hc = slice(hh * P, (hh + 1) * P)
            ybuf_ref[:, hc] = ybuf_ref[:, hc] + jnp.dot(lmat, xdt[:, hc], preferred_element_type=F32)

    y = (ybuf_ref[...] + xs * dskip_e) * zs_ref[...]
    gw = SSM_D_INNER // G
    parts = []
    for g in range(G):
        yg = y[:, g * gw:(g + 1) * gw]
        parts.append(yg * lax.rsqrt(jnp.mean(yg * yg, axis=-1, keepdims=True) + SSM_NORM_EPS))
    y_ref[...] = (jnp.concatenate(parts, axis=1) * ng_ref[...]).astype(y_ref.dtype)


def _ssd(xbc, zs, tail, tailt, conv_w, conv_b, dt_bias, a_log, d_skip, norm_g, batch, seq):
    t = xbc.shape[0]
    L = SSM_CHUNK
    nc = seq // L
    row = lambda b, c: (b * nc + c, 0)
    const = lambda b, c: (0, 0)
    H = SSM_HEADS
    in_specs = [
        pl.BlockSpec((L, SSM_XBC), row),
        pl.BlockSpec((L, SSM_D_INNER), row),
        pl.BlockSpec((L, TAIL), row),
        pl.BlockSpec((TAIL, L), lambda b, c: (0, b * nc + c)),
        pl.BlockSpec((SSM_CONV, SSM_XBC), const),
        pl.BlockSpec((1, SSM_XBC), const),
        pl.BlockSpec((1, H), const),
        pl.BlockSpec((H, 1), const),
        pl.BlockSpec((1, H), const),
        pl.BlockSpec((H, 1), const),
        pl.BlockSpec((1, H), const),
        pl.BlockSpec((1, SSM_D_INNER), const),
    ]
    return pl.pallas_call(
        _ssd_kernel, grid=(batch, nc), in_specs=in_specs,
        out_specs=pl.BlockSpec((L, SSM_D_INNER), row),
        out_shape=jax.ShapeDtypeStruct((t, SSM_D_INNER), BF16),
        scratch_shapes=[pltpu.VMEM((L + 8, SSM_XBC), F32),
                        pltpu.VMEM((SSM_STATE, SSM_D_INNER), F32),
                        pltpu.VMEM((L, SSM_D_INNER), F32)],
        compiler_params=_cparams(("arbitrary", "arbitrary")), name="ssd",
    )(xbc, zs, tail, tailt, conv_w, conv_b, dt_bias.reshape(1, H), dt_bias.reshape(H, 1),
      a_log.reshape(1, H), a_log.reshape(H, 1), d_skip.reshape(1, H), norm_g.reshape(1, -1))


def _compress_kernel(k_ref, v_ref, kpos_ref, vpos_ref, kw1_ref, vw1_ref, kb1_ref, vb1_ref,
                     kw2_ref, vw2_ref, kc_ref, vct_ref):
    D = NSA_HEAD_DIM
    gw = CMP_STRIDE * D

    def hidden(t_ref, pos_ref, w1_ref, b1_ref, g):
        t = t_ref[:, g * gw:(g + 1) * gw]
        n = t.shape[0]
        lo = jnp.dot((t + pos_ref[0:1, :]).astype(BF16), w1_ref[...], preferred_element_type=F32)
        hi = jnp.dot((t + pos_ref[1:2, :]).astype(BF16), w1_ref[...], preferred_element_type=F32)
        pre = lo + pltpu.roll(pltpu.roll(hi, n - 1, 0), D, 1)
        return _silu(pre[:, 0:D] + b1_ref[...]).astype(BF16)

    for g in range(NSA_KV_HEADS):
        kc = jnp.dot(hidden(k_ref, kpos_ref, kw1_ref, kb1_ref, g), kw2_ref[...], preferred_element_type=F32)
        kc_ref[g] = kc.astype(BF16)
        vct = lax.dot_general(vw2_ref[...], hidden(v_ref, vpos_ref, vw1_ref, vb1_ref, g), NT_DIMS,
                              preferred_element_type=F32)
        vct_ref[g] = vct.astype(BF16)


def _compress_weights(pos, w1, b1, w2):
    half = CMP_BLOCK // 2 * NSA_HEAD_DIM
    w1cat = jnp.concatenate([w1[:half], w1[half:]], axis=1).astype(BF16)
    return pos.reshape(2, half), w1cat, b1.reshape(1, -1), w2.astype(BF16)


def _compress(k16, v16, kparams, vparams, batch, seq):
    rows = seq // CMP_STRIDE
    width = CMP_STRIDE * NSA_KV
    D = NSA_HEAD_DIM
    gw = CMP_STRIDE * D
    kpos, kw1, kb1, kw2 = _compress_weights(*kparams)
    vpos, vw1, vb1, vw2 = _compress_weights(*vparams)
    vw2 = vw2.T
    c2 = lambda b: (0, 0)
    tok = pl.BlockSpec((rows, width), lambda b: (b, 0))
    in_specs = [tok, tok,
                pl.BlockSpec((2, gw), c2), pl.BlockSpec((2, gw), c2),
                pl.BlockSpec((gw, 2 * D), c2), pl.BlockSpec((gw, 2 * D), c2),
                pl.BlockSpec((1, D), c2), pl.BlockSpec((1, D), c2),
                pl.BlockSpec((D, D), c2), pl.BlockSpec((D, D), c2)]
    out = jax.ShapeDtypeStruct((NSA_KV_HEADS, batch * rows, D), BF16)
    out_t = jax.ShapeDtypeStruct((NSA_KV_HEADS, D, batch * rows), BF16)
    ospec = pl.BlockSpec((NSA_KV_HEADS, rows, D), lambda b: (0, b, 0))
    ospec_t = pl.BlockSpec((NSA_KV_HEADS, D, rows), lambda b: (0, 0, b))
    return pl.pallas_call(
        _compress_kernel, grid=(batch,), in_specs=in_specs, out_specs=(ospec, ospec_t),
        out_shape=(out, out_t), compiler_params=_cparams(("arbitrary",)), name="compress",
    )(k16, v16, kpos, vpos, kw1, vw1, kb1, vb1, kw2, vw2)


def _nsa_kernel(qt_ref, kc_ref, vct_ref, ks_ref, vst_ref, kw_ref, vwt_ref, tailt_ref, ovt_ref, mtab_ref, cmask_ref,
                o_ref, biasq_ref, m_ref, acc_ref, part_ref, sbuf0_ref, sbuf1_ref, mbuf0_ref, mbuf1_ref):
    R, D = NSA_REP, NSA_HEAD_DIM
    nl = R * TQ
    sbuf_refs = (sbuf0_ref, sbuf1_ref)
    mbuf_refs = (mbuf0_ref, mbuf1_ref)
    n_stages = (ks_ref.shape[1] // TK - 1) // 2
    g = pl.program_id(1)
    qi = pl.program_id(2)
    t0 = qi * TQ
    qt = jnp.concatenate([qt_ref[r] for r in range(R)], axis=1)
    tpos_row = t0 + lax.broadcasted_iota(jnp.int32, (1, nl), 1) % TQ

    def k_rows(k_ref, start):
        return k_ref[0, pl.ds(pl.multiple_of(start, TK), TK), :]

    def vt_cols(vt_ref, start, n):
        return vt_ref[0, :, pl.ds(pl.multiple_of(start, TK), n)]

    def pv(vt, p):
        vt1 = jnp.concatenate([vt, jnp.ones((ONES_ROWS, vt.shape[1]), BF16)], axis=0)
        return jnp.dot(vt1, p, preferred_element_type=F32)

    def mask_tile(off):
        return jnp.concatenate([mtab_ref[pl.ds(pl.multiple_of(off, TK), TK), :]] * R, axis=1)

    def col_max(s):
        return jnp.max(s, axis=0, keepdims=True)

    def online(state, s, m_chunk, vt):
        m_new = m_chunk if state is None else jnp.maximum(state[0], m_chunk)
        contrib = pv(vt, jnp.exp2(s - m_new).astype(BF16))
        if state is None:
            return m_new, contrib
        return m_new, jnp.exp2(state[0] - m_new) * state[1] + contrib

    causal = mask_tile(MT_CAUSAL)
    far0 = jnp.maximum(t0 - 2 * TK, 0)
    mid0 = jnp.maximum(t0 - TK, 0)
    s_dia = jnp.dot(k_rows(kw_ref, t0), qt, preferred_element_type=F32) + causal
    m_dia = col_max(s_dia)
    s_mid = (jnp.dot(k_rows(kw_ref, mid0), qt, preferred_element_type=F32)
             + mask_tile(jnp.where(qi >= 1, MT_ALL, MT_NONE)))
    m_mid = col_max(s_mid)
    s_far = (jnp.dot(k_rows(kw_ref, far0), qt, preferred_element_type=F32)
             + mask_tile(jnp.where(qi >= 2, MT_FAR, MT_NONE)))
    m_far = col_max(s_far)

    ncr = kc_ref.shape[1]
    cmask = cmask_ref[pl.ds(pl.multiple_of(qi * ncr, ncr), ncr), :]
    s_m = jnp.dot(kc_ref[0], qt, preferred_element_type=F32) + jnp.concatenate([cmask] * R, axis=1)
    mx = jnp.max(s_m, axis=0, keepdims=True)
    p = jnp.exp2(s_m - mx)
    den = jnp.sum(p, axis=0, keepdims=True)
    sees_any = tpos_row >= CMP_BLOCK - 1
    pc = p * jnp.where(sees_any, 1.0 / den, 0.0)
    o_c = jnp.dot(vct_ref[0], pc.astype(BF16), preferred_element_type=F32)

    psum = pc[:, 0:TQ]
    for r in range(1, R):
        psum = psum + pc[:, r * TQ:(r + 1) * TQ]
    n_slc = ovt_ref.shape[0]
    imp_t = jnp.dot(ovt_ref[...], psum, precision=lax.Precision.HIGHEST,
                    preferred_element_type=F32)
    jj = lax.broadcasted_iota(jnp.int32, (n_slc, TQ), 0)
    tt = t0 + lax.broadcasted_iota(jnp.int32, (n_slc, TQ), 1)
    lag = tt // SLC_BLOCK - jj
    forced = (jj == 0) | ((lag >= 0) & (lag < N_LOCAL_BLOCKS))
    valid = jj * SLC_BLOCK <= tt
    score = jnp.where(forced, FORCED_SCORE, jnp.where(valid, imp_t, -1.0))
    rows8 = 8
    j8 = lax.broadcasted_iota(jnp.int32, (rows8, TQ), 0)
    groups = [score[a:a + rows8] for a in range(0, n_slc, rows8)]
    ranks = [jnp.zeros((rows8, TQ), F32) for _ in groups]
    window_steps = {0: (s_dia, m_dia, t0), n_slc // 3: (s_mid, m_mid, mid0), 2 * n_slc // 3: (s_far, m_far, far0)}
    win = None
    for j2 in range(n_slc):
        if j2 in window_steps:
            s_w, m_w, start_w = window_steps[j2]
            win = online(win, s_w, m_w, vt_cols(vwt_ref, start_w, TK))
        sj = score[j2:j2 + 1, :]
        for gi, sg in enumerate(groups):
            lo = gi * rows8
            if lo + rows8 - 1 < j2:
                beats = sj > sg
            elif lo > j2:
                beats = sj >= sg
            else:
                beats = (sj > sg) | ((sj == sg) & (j8 + lo > j2))
            ranks[gi] = ranks[gi] + jnp.where(beats, 1.0, 0.0)
    rank = jnp.concatenate(ranks, axis=0)
    selected = (rank < float(min(SLC_TOP_N, n_slc))) & (score >= 0.0)
    bias_t = jnp.where(selected, 0.0, NEG_BIG)
    per = TK // SLC_BLOCK
    zrows = jnp.zeros((BIAS_ROWS - per, TQ), F32)
    for kb in range(n_slc // per):
        blk = jnp.concatenate([bias_t[kb * per:(kb + 1) * per], zrows], axis=0)
        biasq_ref[kb * BIAS_ROWS:(kb + 1) * BIAS_ROWS, :] = jnp.concatenate([blk] * R, axis=1).astype(BF16)
    qpad = jnp.zeros((KSEL - D - BIAS_ROWS, nl), BF16)

    def q_sel(kb):
        rows = biasq_ref[pl.ds(pl.multiple_of(kb * BIAS_ROWS, BIAS_ROWS), BIAS_ROWS), :]
        return jnp.concatenate([qt, rows, qpad], axis=0)

    def slc_scores(kb):
        return jnp.dot(k_rows(ks_ref, kb * TK), q_sel(kb), preferred_element_type=F32)

    def fill(buf_ref, mbuf_ref, chunks):
        for c, kb in enumerate(chunks):
            s = slc_scores(kb)
            buf_ref[c * TK:(c + 1) * TK, :] = s
            mbuf_ref[c:c + 1, :] = col_max(s)

    npairs = qi // 2
    s_sel = slc_scores(qi) + causal
    m_sel_chunk = col_max(s_sel)
    fill(sbuf_refs[0], mbuf_refs[0], (0, 1))

    accw = win[1]
    o_w = accw[0:D] / accw[D:D + 1]

    gate_row = pl.multiple_of(GATE_OFF + GATE_SLOT * g, 8)
    sig = jax.nn.sigmoid(tailt_ref[pl.ds(gate_row, GATE_SLOT), :])

    def gate(c):
        return jnp.concatenate([sig[3 * r + c:3 * r + c + 1, :] for r in range(R)], axis=1)

    part_ref[...] = gate(0) * o_c + gate(2) * o_w

    m_sel, acc_sel = online(None, s_sel, m_sel_chunk, vt_cols(vst_ref, t0, TK))
    m_ref[...] = m_sel
    acc_ref[...] = acc_sel

    def absorb(buf_ref, mbuf_ref, first, n):
        state = (m_ref[...], acc_ref[...])
        for c in range(n):
            state = online(state, buf_ref[c * TK:(c + 1) * TK, :], mbuf_ref[c:c + 1, :],
                           vt_cols(vst_ref, (first + c) * TK, TK))
        m_ref[...] = state[0]
        acc_ref[...] = state[1]

    for k in range(n_stages):
        @pl.when(npairs > k)
        def _(k=k):
            fill(sbuf_refs[(k + 1) % 2], mbuf_refs[(k + 1) % 2],
                 (jnp.minimum(2 * k + 2, qi - 1), jnp.minimum(2 * k + 3, qi - 1)))
            absorb(sbuf_refs[k % 2], mbuf_refs[k % 2], 2 * k, 2)

    for parity in range(2):
        @pl.when((qi % 2 == 1) & (npairs % 2 == parity))
        def _(parity=parity):
            absorb(sbuf_refs[parity], mbuf_refs[parity], qi - 1, 1)

    acc = acc_ref[...]
    ot = part_ref[...] + gate(1) * (acc[0:D] / acc[D:D + 1])
    stacked = jnp.concatenate([ot[:, r * TQ:(r + 1) * TQ] for r in range(R)], axis=0)
    o_ref[...] = stacked.T.astype(o_ref.dtype)


def _overlap_t(seq):
    n_cmp = (seq - CMP_BLOCK) // CMP_STRIDE + 1
    n_slc = seq // SLC_BLOCK
    cs = np.arange(n_cmp) * CMP_STRIDE
    ss = np.arange(n_slc) * SLC_BLOCK
    overlap = np.clip(np.minimum(cs[:, None] + CMP_BLOCK, ss[None, :] + SLC_BLOCK)
                      - np.maximum(cs[:, None], ss[None, :]), 0, None) / CMP_BLOCK
    ovt = np.zeros((n_slc, n_cmp + 1), np.float32)
    ovt[:, :n_cmp] = overlap.T
    return jnp.asarray(ovt)


def _mask_tiles():
    ki = np.arange(TK)[:, None]
    qi = np.arange(TQ)[None, :]
    neg = np.float32(NEG_BIG)
    none = np.full((TK, TQ), neg, np.float32)
    far = np.where(ki > qi, np.float32(0), neg)
    full = np.zeros((TK, TQ), np.float32)
    causal = np.where(ki <= qi, np.float32(0), neg)
    return jnp.asarray(np.concatenate([none, far, full, causal], axis=0))


def _cmp_masks(seq):
    ncr = seq // CMP_STRIDE
    n_cmp = (seq - CMP_BLOCK) // CMP_STRIDE + 1
    n = np.arange(ncr)[None, :, None]
    t = (np.arange(seq // TQ)[:, None, None] * TQ + np.arange(TQ)[None, None, :])
    visible = (n * CMP_STRIDE + CMP_BLOCK - 1 <= t) & (n < n_cmp)
    return jnp.asarray(np.where(visible, np.float32(0), np.float32(NEG_BIG)).reshape(-1, TQ))


def _nsa(qt, kc, vct, ks, vst, kw, vwt, tailt, batch, seq):
    t = batch * seq
    G, R, D = NSA_KV_HEADS, NSA_REP, NSA_HEAD_DIM
    nq = seq // TQ
    ncr = seq // CMP_STRIDE
    n_slc = seq // SLC_BLOCK
    assert TQ == TK and WINDOW == 2 * TK, "window branch visits exactly three key chunks"
    assert TK // SLC_BLOCK <= BIAS_ROWS and D + BIAS_ROWS <= KSEL
    const = lambda b, g, i: (0, 0)
    vtspec = pl.BlockSpec((1, D, seq), lambda b, g, i: (g, 0, b))
    in_specs = [
        pl.BlockSpec((R, D, TQ), lambda b, g, i: (g, 0, b * nq + i)),
        pl.BlockSpec((1, ncr, D), lambda b, g, i: (g, b, 0)),
        pl.BlockSpec((1, D, ncr), lambda b, g, i: (g, 0, b)),
        pl.BlockSpec((1, seq, KSEL), lambda b, g, i: (g, b, 0)),
        vtspec,
        pl.BlockSpec((1, seq, D), lambda b, g, i: (g, b, 0)),
        vtspec,
        pl.BlockSpec((TAIL, TQ), lambda b, g, i: (0, b * nq + i)),
        pl.BlockSpec((n_slc, ncr), const),
        pl.BlockSpec((4 * TK, TQ), const),
        pl.BlockSpec((nq * ncr, TQ), const),
    ]
    return pl.pallas_call(
        _nsa_kernel, grid=(batch, G, nq), in_specs=in_specs,
        out_specs=pl.BlockSpec((TQ, R * D), lambda b, g, i: (b * nq + i, g)),
        out_shape=jax.ShapeDtypeStruct((t, NSA_Q), BF16),
        scratch_shapes=[pltpu.VMEM((seq // TK * BIAS_ROWS, R * TQ), BF16),
                        pltpu.VMEM((1, R * TQ), F32),
                        pltpu.VMEM((D + ONES_ROWS, R * TQ), F32),
                        pltpu.VMEM((D, R * TQ), F32),
                        pltpu.VMEM((2 * TK, R * TQ), F32),
                        pltpu.VMEM((2 * TK, R * TQ), F32),
                        pltpu.VMEM((8, R * TQ), F32),
                        pltpu.VMEM((8, R * TQ), F32)],
        compiler_params=_cparams(("arbitrary", "arbitrary", "arbitrary")), name="nsa",
    )(qt, kc, vct, ks, vst, kw, vwt, tailt, _overlap_t(seq), _mask_tiles(), _cmp_masks(seq))


def _outproj_kernel(y_ref, o_ref, x_ref, w_ref, ag_ref, ng_ref, x1_ref, h2_ref):
    yn = _rms(o_ref[...].astype(F32), ag_ref[...], NORM_EPS).astype(BF16)
    x1 = (x_ref[...]
          + jnp.dot(y_ref[...], w_ref[0:SSM_D_INNER, :], preferred_element_type=F32)
          + jnp.dot(yn, w_ref[SSM_D_INNER:, :], preferred_element_type=F32))
    x1_ref[...] = x1
    h2_ref[...] = _rms(x1, ng_ref[...], NORM_EPS).astype(BF16)


def _out_proj(y_ssm, o_nsa, x2, w_out, attn_g, norm2_g):
    t = x2.shape[0]
    tm = TM_PROJ
    row = lambda i: (i, 0)
    const = lambda i: (0, 0)
    tok = pl.BlockSpec((tm, D_MODEL), row)
    vec = pl.BlockSpec((1, D_MODEL), const)
    return pl.pallas_call(
        _outproj_kernel, grid=(t // tm,),
        in_specs=[tok, tok, tok, pl.BlockSpec((SSM_D_INNER + NSA_Q, D_MODEL), const), vec, vec],
        out_specs=(tok, tok),
        out_shape=(jax.ShapeDtypeStruct((t, D_MODEL), F32), jax.ShapeDtypeStruct((t, D_MODEL), BF16)),
        compiler_params=_cparams(("arbitrary",)), name="out_proj",
    )(y_ssm, o_nsa, x2, w_out, attn_g, norm2_g)


def _ffn_kernel(h_ref, halo_ref, x1_ref, wup_ref, cw_ref, cb_ref, wd_ref, fg_ref, out_ref, act_ref,
                *, tiles_per_seq):
    i = pl.program_id(0)
    tm = h_ref.shape[0]
    pad = halo_ref.shape[0]
    tn = TN_FFN
    halo = halo_ref[...]
    halo = jnp.where(i % tiles_per_seq == 0, jnp.zeros_like(halo), halo)
    hc = jnp.concatenate([halo, h_ref[...]], axis=0)

    def branch(c0):
        cols = slice(c0, c0 + tn)
        u = jnp.dot(hc, wup_ref[:, cols], preferred_element_type=F32)
        out = cb_ref[:, cols]
        for k in range(FFN_CONV):
            off = pad - (FFN_CONV - 1) + k
            out = out + cw_ref[k:k + 1, cols] * u[off:off + tm, :]
        return out

    for j in range(D_FF // tn):
        act = _silu(branch(j * tn)) * branch(D_FF + j * tn)
        act_ref[:, j * tn:(j + 1) * tn] = act.astype(BF16)
    down = jnp.dot(act_ref[...], wd_ref[...], preferred_element_type=F32)
    out_ref[...] = _rms(x1_ref[...] + down, fg_ref[...], NORM_EPS)


def _ffn(h2, x1, w_up, conv_w, conv_b, w_down, final_g, seq):
    t = h2.shape[0]
    tm = TM_FFN
    pad = BF16_SUBLANES
    tok = pl.BlockSpec((tm, D_MODEL), lambda i: (i, 0))

    def resident(shape):
        return pl.BlockSpec(shape, lambda i: (0, 0), pipeline_mode=pl.Buffered(1))

    in_specs = [
        tok,
        pl.BlockSpec((pad, D_MODEL), lambda i: (jnp.maximum(i * (tm // pad) - 1, 0), 0)),
        tok,
        resident((D_MODEL, 2 * D_FF)),
        resident((FFN_CONV, 2 * D_FF)),
        resident((1, 2 * D_FF)),
        resident((D_FF, D_MODEL)),
        resident((1, D_MODEL)),
    ]
    return pl.pallas_call(
        functools.partial(_ffn_kernel, tiles_per_seq=seq // tm), grid=(t // tm,),
        in_specs=in_specs, out_specs=tok,
        out_shape=jax.ShapeDtypeStruct((t, D_MODEL), F32),
        scratch_shapes=[pltpu.VMEM((tm, D_FF), BF16)],
        compiler_params=_cparams(("arbitrary",)), name="ffn",
    )(h2, h2, x1, w_up, conv_w, conv_b, w_down, final_g)


def _rope_tables(seq):
    half = NSA_HEAD_DIM // 2
    inv_freq = 1.0 / (ROPE_THETA ** (jnp.arange(0, NSA_HEAD_DIM, 2, dtype=F32) / NSA_HEAD_DIM))
    ang = jnp.arange(seq).astype(F32)[:, None] * inv_freq[None, :]
    cos, sin = jnp.cos(ang), jnp.sin(ang)
    reps = LANES // NSA_HEAD_DIM
    cos_t = jnp.tile(jnp.concatenate([cos, cos], axis=1), (1, reps))
    sin_t = jnp.tile(jnp.concatenate([-sin, sin], axis=1), (1, reps))
    return cos_t, sin_t


def _split_w_in(w):
    o_z, o_xbc, o_dt, o_q, o_kv, o_g = np.cumsum([0, SSM_D_INNER, SSM_XBC, SSM_HEADS, NSA_Q, 6 * NSA_KV]).tolist()
    per = 3 * NSA_REP
    tail_cols = [w[:, o_dt:o_q]]
    for g in range(NSA_KV_HEADS):
        tail_cols += [w[:, o_g + g * per:o_g + (g + 1) * per], jnp.zeros((w.shape[0], GATE_SLOT - per), w.dtype)]
    tail_cols.append(jnp.zeros((w.shape[0], TAIL - GATE_OFF - NSA_KV_HEADS * GATE_SLOT), w.dtype))
    return (w[:, o_z:o_dt].astype(BF16), w[:, o_q:o_g].astype(BF16),
            jnp.concatenate(tail_cols, axis=1).astype(BF16))


def kernel(x, norm1_g, w_in, ssm_conv_w, ssm_conv_b, ssm_dt_bias, ssm_a_log, ssm_d, ssm_norm_g, cmp_k_pos, cmp_k_w1, cmp_k_b1, cmp_k_w2, cmp_v_pos, cmp_v_w1, cmp_v_b1, cmp_v_w2, attn_norm_g, w_out, norm2_g, ffn_w_up, ffn_conv_w, ffn_conv_b, ffn_w_down, final_norm_g):
    batch, seq, d = x.shape
    assert w_in.shape[0] == 1, "single-layer problem"
    l = 0
    cos_t, sin_t = _rope_tables(seq)
    x2 = x.reshape(batch * seq, d)
    zs, xbc, qt, kc_r, vc_r, ks, vst, kw, vwt, tail, tailt = _in_proj(
        x2, norm1_g[l].reshape(1, d), _split_w_in(w_in.reshape(d, -1)), cos_t, sin_t, seq)
    y_ssm = _ssd(xbc, zs, tail, tailt, ssm_conv_w[l], ssm_conv_b[l].reshape(1, -1), ssm_dt_bias[l],
                 ssm_a_log[l], ssm_d[l], ssm_norm_g[l], batch, seq)
    kc, vct = _compress(kc_r, vc_r,
                        (cmp_k_pos[l], cmp_k_w1[l], cmp_k_b1[l], cmp_k_w2[l]),
                        (cmp_v_pos[l], cmp_v_w1[l], cmp_v_b1[l], cmp_v_w2[l]), batch, seq)
    o_nsa = _nsa(qt, kc, vct, ks, vst, kw, vwt, tailt, batch, seq)
    x1, h2 = _out_proj(y_ssm, o_nsa, x2, w_out.reshape(-1, d).astype(BF16), attn_norm_g[l].reshape(1, d),
                       norm2_g[l].reshape(1, d))
    out = _ffn(h2, x1, ffn_w_up.reshape(d, -1).astype(BF16), ffn_conv_w[l], ffn_conv_b[l].reshape(1, -1),
               ffn_w_down.reshape(-1, d).astype(BF16), final_norm_g.reshape(1, d), seq)
    return out.reshape(batch, seq, d)
```

```python
import functools
import math

import numpy as np
import jax
import jax.numpy as jnp
from jax import lax
from jax.experimental import pallas as pl
from jax.experimental.pallas import tpu as pltpu

F32 = jnp.float32
BF16 = jnp.bfloat16

D_MODEL = 1024
SSM_D_INNER = 1024
SSM_HEAD_DIM = 64
SSM_HEADS = 16
SSM_GROUPS = 2
SSM_STATE = 128
SSM_CONV = 4
SSM_CHUNK = 128
SSM_XBC = SSM_D_INNER + 2 * SSM_GROUPS * SSM_STATE
SSM_NORM_EPS = 1e-5
NSA_HEADS = 16
NSA_KV_HEADS = 4
NSA_REP = NSA_HEADS // NSA_KV_HEADS
NSA_HEAD_DIM = 64
NSA_Q = NSA_HEADS * NSA_HEAD_DIM
NSA_KV = NSA_KV_HEADS * NSA_HEAD_DIM
CMP_BLOCK = 32
CMP_STRIDE = 16
SLC_BLOCK = 64
SLC_TOP_N = 16
N_LOCAL_BLOCKS = 2
FORCED_SCORE = 1e4
WINDOW = 512
ROPE_THETA = 10000.0
D_FF = 2816
FFN_CONV = 3
NORM_EPS = 1e-6
NEG_BIG = -1e30

LANES = 128
BF16_SUBLANES = 16
TAIL = LANES
GATE_OFF = SSM_HEADS
GATE_SLOT = 16

TM_PROJ = 256
TQ = 256
TK = 256
ONES_ROWS = 16
KSEL = LANES
BIAS_ROWS = 16
MT_NONE, MT_FAR, MT_ALL, MT_CAUSAL = 0, TK, 2 * TK, 3 * TK
TM_FFN = 512
TN_FFN = 256
VMEM_LIMIT = 56 * 1024 * 1024

NT_DIMS = (((1,), (1,)), ((), ()))


def _cparams(sem):
    return pltpu.CompilerParams(dimension_semantics=sem, vmem_limit_bytes=VMEM_LIMIT)


def _rms(x, g, eps):
    return x * lax.rsqrt(jnp.mean(x * x, axis=-1, keepdims=True) + eps) * g


def _silu(x):
    return x * jax.nn.sigmoid(x)


def _inproj_kernel(x_ref, g_ref, wa_ref, wb_ref, wt_ref, cos_ref, sin_ref,
                   zs_ref, xbc_ref, qt_ref, kc_ref, vc_ref, ks_ref, vst_ref, kw_ref, vwt_ref,
                   tail_ref, tailt_ref, kbuf_ref, vbuf_ref, *, tiles_per_seq):
    tm = x_ref.shape[0]
    seq_tile = pl.program_id(0) % tiles_per_seq
    h = _rms(x_ref[...], g_ref[...], NORM_EPS).astype(BF16)

    def mm(w_ref, lo, hi):
        return jnp.dot(h, w_ref[:, lo:hi], preferred_element_type=F32)

    tail = mm(wt_ref, 0, TAIL)
    tail_ref[...] = tail
    tailt_ref[...] = tail.T

    cos = cos_ref[...]
    sin = sin_ref[...]
    lane = lax.broadcasted_iota(jnp.int32, (tm, LANES), 1)
    first_half = (lane % NSA_HEAD_DIM) < (NSA_HEAD_DIM // 2)

    def rope(xc):
        partner = jnp.where(first_half, pltpu.roll(xc, LANES - 32, 1), pltpu.roll(xc, 32, 1))
        return xc * cos + partner * sin

    q = mm(wb_ref, 0, NSA_Q)
    scale = NSA_HEAD_DIM ** -0.5 * math.log2(math.e)
    for c in range(NSA_Q // LANES):
        rt = (rope(q[:, c * LANES:(c + 1) * LANES]) * scale).T.astype(BF16)
        qt_ref[2 * c] = rt[:NSA_HEAD_DIM]
        qt_ref[2 * c + 1] = rt[NSA_HEAD_DIM:]

    kv = mm(wb_ref, NSA_Q, NSA_Q + 6 * NSA_KV)

    def seg(i):
        return kv[:, i * NSA_KV:(i + 1) * NSA_KV]

    def rope_seg(x):
        return jnp.concatenate([rope(x[:, :LANES]), rope(x[:, LANES:])], axis=1)

    def store_heads(ref, x):
        for g in range(NSA_KV_HEADS):
            ref[g] = x[:, g * NSA_HEAD_DIM:(g + 1) * NSA_HEAD_DIM].astype(BF16)

    def store_heads_t(ref, x):
        xt = x.T.astype(BF16)
        for g in range(NSA_KV_HEADS):
            ref[g] = xt[g * NSA_HEAD_DIM:(g + 1) * NSA_HEAD_DIM]

    kc_tile = rope_seg(seg(0))
    vc_tile = seg(1)
    for c in range(NSA_KV // LANES):
        kbuf_ref[c] = kc_tile[:, c * LANES:(c + 1) * LANES]
        vbuf_ref[c] = vc_tile[:, c * LANES:(c + 1) * LANES]
    gw = CMP_STRIDE * NSA_HEAD_DIM
    for l in range(CMP_STRIDE):
        for c in range(NSA_KV // LANES):
            k_rows = kbuf_ref[c, pl.ds(l, tm // CMP_STRIDE, stride=CMP_STRIDE), :]
            v_rows = vbuf_ref[c, pl.ds(l, tm // CMP_STRIDE, stride=CMP_STRIDE), :]
            for half in range(LANES // NSA_HEAD_DIM):
                g = c * (LANES // NSA_HEAD_DIM) + half
                src = slice(half * NSA_HEAD_DIM, (half + 1) * NSA_HEAD_DIM)
                dst = slice(g * gw + l * NSA_HEAD_DIM, g * gw + (l + 1) * NSA_HEAD_DIM)
                kc_ref[:, dst] = k_rows[:, src]
                vc_ref[:, dst] = v_rows[:, src]
    pos = seq_tile * tm + lax.broadcasted_iota(jnp.int32, (tm, NSA_HEAD_DIM), 0)
    block_in_chunk = (pos // SLC_BLOCK) % (TK // SLC_BLOCK)
    onehot = jnp.where(lax.broadcasted_iota(jnp.int32, (tm, NSA_HEAD_DIM), 1) == block_in_chunk, 1.0, 0.0)
    ks = rope_seg(seg(2))
    for g in range(NSA_KV_HEADS):
        kg = ks[:, g * NSA_HEAD_DIM:(g + 1) * NSA_HEAD_DIM]
        ks_ref[g] = jnp.concatenate([kg, onehot], axis=1).astype(BF16)
    store_heads_t(vst_ref, seg(3))
    store_heads(kw_ref, rope_seg(seg(4)))
    store_heads_t(vwt_ref, seg(5))

    zs_ref[...] = _silu(mm(wa_ref, 0, SSM_D_INNER))
    xbc_ref[...] = mm(wa_ref, SSM_D_INNER, SSM_D_INNER + SSM_XBC)


def _in_proj(x2, norm_g, w_parts, cos_t, sin_t, seq):
    t = x2.shape[0]
    tm = TM_PROJ
    nseq = seq // tm
    row = lambda i: (i, 0)
    const = lambda i: (0, 0)
    heads = lambda i: (0, i, 0)
    heads_t = lambda i: (0, 0, i)
    out_shape = (
        jax.ShapeDtypeStruct((t, SSM_D_INNER), F32),
        jax.ShapeDtypeStruct((t, SSM_XBC), F32),
        jax.ShapeDtypeStruct((NSA_HEADS, NSA_HEAD_DIM, t), BF16),
        jax.ShapeDtypeStruct((t // CMP_STRIDE, CMP_STRIDE * NSA_KV), F32),
        jax.ShapeDtypeStruct((t // CMP_STRIDE, CMP_STRIDE * NSA_KV), F32),
        jax.ShapeDtypeStruct((NSA_KV_HEADS, t, KSEL), BF16),
        jax.ShapeDtypeStruct((NSA_KV_HEADS, NSA_HEAD_DIM, t), BF16),
        jax.ShapeDtypeStruct((NSA_KV_HEADS, t, NSA_HEAD_DIM), BF16),
        jax.ShapeDtypeStruct((NSA_KV_HEADS, NSA_HEAD_DIM, t), BF16),
        jax.ShapeDtypeStruct((t, TAIL), F32),
        jax.ShapeDtypeStruct((TAIL, t), F32),
    )
    kvh = pl.BlockSpec((NSA_KV_HEADS, tm, NSA_HEAD_DIM), heads)
    kvh_t = pl.BlockSpec((NSA_KV_HEADS, NSA_HEAD_DIM, tm), heads_t)
    out_specs = (
        pl.BlockSpec((tm, SSM_D_INNER), row),
        pl.BlockSpec((tm, SSM_XBC), row),
        pl.BlockSpec((NSA_HEADS, NSA_HEAD_DIM, tm), heads_t),
        pl.BlockSpec((tm // CMP_STRIDE, CMP_STRIDE * NSA_KV), row),
        pl.BlockSpec((tm // CMP_STRIDE, CMP_STRIDE * NSA_KV), row),
        pl.BlockSpec((NSA_KV_HEADS, tm, KSEL), heads), kvh_t, kvh, kvh_t,
        pl.BlockSpec((tm, TAIL), row),
        pl.BlockSpec((TAIL, tm), lambda i: (0, i)),
    )
    in_specs = [
        pl.BlockSpec((tm, D_MODEL), row),
        pl.BlockSpec((1, D_MODEL), const),
        pl.BlockSpec((D_MODEL, SSM_D_INNER + SSM_XBC), const),
        pl.BlockSpec((D_MODEL, NSA_Q + 6 * NSA_KV), const),
        pl.BlockSpec((D_MODEL, TAIL), const),
        pl.BlockSpec((tm, LANES), lambda i: (i % nseq, 0)),
        pl.BlockSpec((tm, LANES), lambda i: (i % nseq, 0)),
    ]
    return pl.pallas_call(
        functools.partial(_inproj_kernel, tiles_per_seq=nseq), grid=(t // tm,), in_specs=in_specs,
        out_specs=out_specs,
        out_shape=out_shape,
        scratch_shapes=[pltpu.VMEM((NSA_KV // LANES, tm, LANES), F32),
                        pltpu.VMEM((NSA_KV // LANES, tm, LANES), F32)],
        compiler_params=_cparams(("arbitrary",)), name="in_proj",
    )(x2, norm_g, *w_parts, cos_t, sin_t)


def _ssd_kernel(xbc_ref, zs_ref, tail_ref, tailt_ref, cw_ref, cb_ref, dtb_ref, dtbt_ref,
                alog_ref, alogt_ref, dskip_ref, ng_ref,
                y_ref, xcat_ref, state_ref, ybuf_ref):
    L, P, N, H, G = SSM_CHUNK, SSM_HEAD_DIM, SSM_STATE, SSM_HEADS, SSM_GROUPS
    HG = H // G
    halo = 8

    @pl.when(pl.program_id(1) == 0)
    def _():
        xcat_ref[0:halo, :] = jnp.zeros((halo, SSM_XBC), F32)
        state_ref[...] = jnp.zeros_like(state_ref)

    xcat_ref[halo:halo + L, :] = xbc_ref[...]
    xcat = xcat_ref[...]
    conv = cb_ref[...] + cw_ref[SSM_CONV - 1:SSM_CONV, :] * xcat[halo:halo + L]
    for k in range(SSM_CONV - 1):
        shifted = pltpu.roll(xcat, SSM_CONV - 1 - k, 0)
        conv = conv + cw_ref[k:k + 1, :] * shifted[halo:halo + L]
    xcat_ref[0:halo, :] = xcat[L:L + halo]
    u = _silu(conv)
    xs = u[:, :SSM_D_INNER]
    bm = u[:, SSM_D_INNER:SSM_D_INNER + G * N]
    cm = u[:, SSM_D_INNER + G * N:]

    def softplus(v):
        return jnp.maximum(v, 0.0) + jnp.log1p(jnp.exp(-jnp.abs(v)))

    dt = softplus(tail_ref[:, 0:H] + dtb_ref[...])
    dtt = softplus(tailt_ref[0:H, :] + dtbt_ref[...])
    da = dt * (-jnp.exp(alog_ref[...]))
    dat = dtt * (-jnp.exp(alogt_ref[...]))
    ri = lax.broadcasted_iota(jnp.int32, (L, L), 0)
    ci = lax.broadcasted_iota(jnp.int32, (L, L), 1)
    tri = ci <= ri
    hi = lax.Precision.HIGHEST
    acs = jnp.dot(tri.astype(F32), da, precision=hi, preferred_element_type=F32)
    acst = jnp.dot(dat, (ri <= ci).astype(F32), precision=hi, preferred_element_type=F32)
    last = acs[L - 1:L, :]
    w_state = dt * jnp.exp(last - acs)
    eacs = jnp.exp(acs)
    cdec = jnp.exp(last)

    hrow = lax.broadcasted_iota(jnp.int32, (H, SSM_D_INNER), 0)
    hcol = lax.broadcasted_iota(jnp.int32, (H, SSM_D_INNER), 1)
    expand = jnp.where(hcol // P == hrow, 1.0, 0.0).astype(BF16)

    def split3(v):
        v_hi = v.astype(BF16)
        r1 = v - v_hi.astype(F32)
        v_mid = r1.astype(BF16)
        v_lo = (r1 - v_mid.astype(F32)).astype(BF16)
        return jnp.concatenate([v_hi, v_mid, v_lo], axis=1)

    small = jnp.concatenate([cdec, dskip_ref[...], jnp.zeros((6, H), F32)], axis=0)
    per_head = jnp.concatenate([dt, w_state, eacs, small], axis=0)
    spread = jnp.dot(split3(per_head), jnp.concatenate([expand] * 3, axis=0), preferred_element_type=F32)
    cdec_e = spread[3 * L:3 * L + 1, :]
    dskip_e = spread[3 * L + 1:3 * L + 2, :]
    xdt = (xs * spread[0:L]).astype(BF16)
    wst = (xs * spread[L:2 * L]).astype(BF16)
    eacs_e = spread[2 * L:3 * L]

    for g in range(G):
        bm_g = bm[:, g * N:(g + 1) * N]
        cm_g = cm[:, g * N:(g + 1) * N].astype(BF16)
        cb = lax.dot_general(cm_g, bm_g.astype(BF16), NT_DIMS, preferred_element_type=F32)
        cols = slice(g * HG * P, (g + 1) * HG * P)
        st = state_ref[:, cols]
        y_off = jnp.dot(cm_g, st.astype(BF16), preferred_element_type=F32) * eacs_e[:, cols]
        ybuf_ref[:, cols] = y_off
        bmt = bm_g.T.astype(BF16)
        state_ref[:, cols] = st * cdec_e[:, cols] + jnp.dot(bmt, wst[:, cols], preferred_element_type=F32)
        for r in range(HG):
            hh = g * HG + r
            diff = acs[:, hh:hh + 1] - acst[hh:hh + 1, :]
            seg = jnp.exp(jnp.where(tri, diff, -jnp.inf))
            lmat = (cb * seg).astype(BF16)
            hc = slice(hh * P, (hh + 1) * P)
            ybuf_ref[:, hc] = ybuf_ref[:, hc] + jnp.dot(lmat, xdt[:, hc], preferred_element_type=F32)

    y = (ybuf_ref[...] + xs * dskip_e) * zs_ref[...]
    gw = SSM_D_INNER // G
    parts = []
    for g in range(G):
        yg = y[:, g * gw:(g + 1) * gw]
        parts.append(yg * lax.rsqrt(jnp.mean(yg * yg, axis=-1, keepdims=True) + SSM_NORM_EPS))
    y_ref[...] = (jnp.concatenate(parts, axis=1) * ng_ref[...]).astype(y_ref.dtype)


def _ssd(xbc, zs, tail, tailt, conv_w, conv_b, dt_bias, a_log, d_skip, norm_g, batch, seq):
    t = xbc.shape[0]
    L = SSM_CHUNK
    nc = seq // L
    row = lambda b, c: (b * nc + c, 0)
    const = lambda b, c: (0, 0)
    H = SSM_HEADS
    in_specs = [
        pl.BlockSpec((L, SSM_XBC), row),
        pl.BlockSpec((L, SSM_D_INNER), row),
        pl.BlockSpec((L, TAIL), row),
        pl.BlockSpec((TAIL, L), lambda b, c: (0, b * nc + c)),
        pl.BlockSpec((SSM_CONV, SSM_XBC), const),
        pl.BlockSpec((1, SSM_XBC), const),
        pl.BlockSpec((1, H), const),
        pl.BlockSpec((H, 1), const),
        pl.BlockSpec((1, H), const),
        pl.BlockSpec((H, 1), const),
        pl.BlockSpec((1, H), const),
        pl.BlockSpec((1, SSM_D_INNER), const),
    ]
    return pl.pallas_call(
        _ssd_kernel, grid=(batch, nc), in_specs=in_specs,
        out_specs=pl.BlockSpec((L, SSM_D_INNER), row),
        out_shape=jax.ShapeDtypeStruct((t, SSM_D_INNER), BF16),
        scratch_shapes=[pltpu.VMEM((L + 8, SSM_XBC), F32),
                        pltpu.VMEM((SSM_STATE, SSM_D_INNER), F32),
                        pltpu.VMEM((L, SSM_D_INNER), F32)],
        compiler_params=_cparams(("arbitrary", "arbitrary")), name="ssd",
    )(xbc, zs, tail, tailt, conv_w, conv_b, dt_bias.reshape(1, H), dt_bias.reshape(H, 1),
      a_log.reshape(1, H), a_log.reshape(H, 1), d_skip.reshape(1, H), norm_g.reshape(1, -1))


def _compress_kernel(k_ref, v_ref, kpos_ref, vpos_ref, kw1_ref, vw1_ref, kb1_ref, vb1_ref,
                     kw2_ref, vw2_ref, kc_ref, vct_ref):
    D = NSA_HEAD_DIM
    gw = CMP_STRIDE * D

    def hidden(t_ref, pos_ref, w1_ref, b1_ref, g):
        t = t_ref[:, g * gw:(g + 1) * gw]
        n = t.shape[0]
        lo = jnp.dot((t + pos_ref[0:1, :]).astype(BF16), w1_ref[...], preferred_element_type=F32)
        hi = jnp.dot((t + pos_ref[1:2, :]).astype(BF16), w1_ref[...], preferred_element_type=F32)
        pre = lo + pltpu.roll(pltpu.roll(hi, n - 1, 0), D, 1)
        return _silu(pre[:, 0:D] + b1_ref[...]).astype(BF16)

    for g in range(NSA_KV_HEADS):
        kc = jnp.dot(hidden(k_ref, kpos_ref, kw1_ref, kb1_ref, g), kw2_ref[...], preferred_element_type=F32)
        kc_ref[g] = kc.astype(BF16)
        vct = lax.dot_general(vw2_ref[...], hidden(v_ref, vpos_ref, vw1_ref, vb1_ref, g), NT_DIMS,
                              preferred_element_type=F32)
        vct_ref[g] = vct.astype(BF16)


def _compress_weights(pos, w1, b1, w2):
    half = CMP_BLOCK // 2 * NSA_HEAD_DIM
    w1cat = jnp.concatenate([w1[:half], w1[half:]], axis=1).astype(BF16)
    return pos.reshape(2, half), w1cat, b1.reshape(1, -1), w2.astype(BF16)


def _compress(k16, v16, kparams, vparams, batch, seq):
    rows = seq // CMP_STRIDE
    width = CMP_STRIDE * NSA_KV
    D = NSA_HEAD_DIM
    gw = CMP_STRIDE * D
    kpos, kw1, kb1, kw2 = _compress_weights(*kparams)
    vpos, vw1, vb1, vw2 = _compress_weights(*vparams)
    vw2 = vw2.T
    c2 = lambda b: (0, 0)
    tok = pl.BlockSpec((rows, width), lambda b: (b, 0))
    in_specs = [tok, tok,
                pl.BlockSpec((2, gw), c2), pl.BlockSpec((2, gw), c2),
                pl.BlockSpec((gw, 2 * D), c2), pl.BlockSpec((gw, 2 * D), c2),
                pl.BlockSpec((1, D), c2), pl.BlockSpec((1, D), c2),
                pl.BlockSpec((D, D), c2), pl.BlockSpec((D, D), c2)]
    out = jax.ShapeDtypeStruct((NSA_KV_HEADS, batch * rows, D), BF16)
    out_t = jax.ShapeDtypeStruct((NSA_KV_HEADS, D, batch * rows), BF16)
    ospec = pl.BlockSpec((NSA_KV_HEADS, rows, D), lambda b: (0, b, 0))
    ospec_t = pl.BlockSpec((NSA_KV_HEADS, D, rows), lambda b: (0, 0, b))
    return pl.pallas_call(
        _compress_kernel, grid=(batch,), in_specs=in_specs, out_specs=(ospec, ospec_t),
        out_shape=(out, out_t), compiler_params=_cparams(("arbitrary",)), name="compress",
    )(k16, v16, kpos, vpos, kw1, vw1, kb1, vb1, kw2, vw2)


def _nsa_kernel(qt_ref, kc_ref, vct_ref, ks_ref, vst_ref, kw_ref, vwt_ref, tailt_ref, ovt_ref, mtab_ref, cmask_ref,
                o_ref, biasq_ref, m_ref, acc_ref, part_ref, sbuf0_ref, sbuf1_ref, mbuf0_ref, mbuf1_ref):
    R, D = NSA_REP, NSA_HEAD_DIM
    nl = R * TQ
    sbuf_refs = (sbuf0_ref, sbuf1_ref)
    mbuf_refs = (mbuf0_ref, mbuf1_ref)
    n_stages = (ks_ref.shape[1] // TK - 1) // 2
    g = pl.program_id(1)
    qi = pl.program_id(2)
    t0 = qi * TQ
    qt = jnp.concatenate([qt_ref[r] for r in range(R)], axis=1)
    tpos_row = t0 + lax.broadcasted_iota(jnp.int32, (1, nl), 1) % TQ

    def k_rows(k_ref, start):
        return k_ref[0, pl.ds(pl.multiple_of(start, TK), TK), :]

    def vt_cols(vt_ref, start, n):
        return vt_ref[0, :, pl.ds(pl.multiple_of(start, TK), n)]

    def pv(vt, p):
        vt1 = jnp.concatenate([vt, jnp.ones((ONES_ROWS, vt.shape[1]), BF16)], axis=0)
        return jnp.dot(vt1, p, preferred_element_type=F32)

    def mask_tile(off):
        return jnp.concatenate([mtab_ref[pl.ds(pl.multiple_of(off, TK), TK), :]] * R, axis=1)

    def col_max(s):
        return jnp.max(s, axis=0, keepdims=True)

    def online(state, s, m_chunk, vt):
        m_new = m_chunk if state is None else jnp.maximum(state[0], m_chunk)
        contrib = pv(vt, jnp.exp2(s - m_new).astype(BF16))
        if state is None:
            return m_new, contrib
        return m_new, jnp.exp2(state[0] - m_new) * state[1] + contrib

    causal = mask_tile(MT_CAUSAL)
    far0 = jnp.maximum(t0 - 2 * TK, 0)
    mid0 = jnp.maximum(t0 - TK, 0)
    s_dia = jnp.dot(k_rows(kw_ref, t0), qt, preferred_element_type=F32) + causal
    m_dia = col_max(s_dia)
    s_mid = (jnp.dot(k_rows(kw_ref, mid0), qt, preferred_element_type=F32)
             + mask_tile(jnp.where(qi >= 1, MT_ALL, MT_NONE)))
    m_mid = col_max(s_mid)
    s_far = (jnp.dot(k_rows(kw_ref, far0), qt, preferred_element_type=F32)
             + mask_tile(jnp.where(qi >= 2, MT_FAR, MT_NONE)))
    m_far = col_max(s_far)

    ncr = kc_ref.shape[1]
    cmask = cmask_ref[pl.ds(pl.multiple_of(qi * ncr, ncr), ncr), :]
    s_m = jnp.dot(kc_ref[0], qt, preferred_element_type=F32) + jnp.concatenate([cmask] * R, axis=1)
    mx = jnp.max(s_m, axis=0, keepdims=True)
    p = jnp.exp2(s_m - mx)
    den = jnp.sum(p, axis=0, keepdims=True)
    sees_any = tpos_row >= CMP_BLOCK - 1
    pc = p * jnp.where(sees_any, 1.0 / den, 0.0)
    o_c = jnp.dot(vct_ref[0], pc.astype(BF16), preferred_element_type=F32)

    psum = pc[:, 0:TQ]
    for r in range(1, R):
        psum = psum + pc[:, r * TQ:(r + 1) * TQ]
    n_slc = ovt_ref.shape[0]
    imp_t = jnp.dot(ovt_ref[...], psum, precision=lax.Precision.HIGHEST,
                    preferred_element_type=F32)
    jj = lax.broadcasted_iota(jnp.int32, (n_slc, TQ), 0)
    tt = t0 + lax.broadcasted_iota(jnp.int32, (n_slc, TQ), 1)
    lag = tt // SLC_BLOCK - jj
    forced = (jj == 0) | ((lag >= 0) & (lag < N_LOCAL_BLOCKS))
    valid = jj * SLC_BLOCK <= tt
    score = jnp.where(forced, FORCED_SCORE, jnp.where(valid, imp_t, -1.0))
    rows8 = 8
    j8 = lax.broadcasted_iota(jnp.int32, (rows8, TQ), 0)
    groups = [score[a:a + rows8] for a in range(0, n_slc, rows8)]
    ranks = [jnp.zeros((rows8, TQ), F32) for _ in groups]
    window_steps = {0: (s_dia, m_dia, t0), n_slc // 3: (s_mid, m_mid, mid0), 2 * n_slc // 3: (s_far, m_far, far0)}
    win = None
    for j2 in range(n_slc):
        if j2 in window_steps:
            s_w, m_w, start_w = window_steps[j2]
            win = online(win, s_w, m_w, vt_cols(vwt_ref, start_w, TK))
        sj = score[j2:j2 + 1, :]
        for gi, sg in enumerate(groups):
            lo = gi * rows8
            if lo + rows8 - 1 < j2:
                beats = sj > sg
            elif lo > j2:
                beats = sj >= sg
            else:
                beats = (sj > sg) | ((sj == sg) & (j8 + lo > j2))
            ranks[gi] = ranks[gi] + jnp.where(beats, 1.0, 0.0)
    rank = jnp.concatenate(ranks, axis=0)
    selected = (rank < float(min(SLC_TOP_N, n_slc))) & (score >= 0.0)
    bias_t = jnp.where(selected, 0.0, NEG_BIG)
    per = TK // SLC_BLOCK
    zrows = jnp.zeros((BIAS_ROWS - per, TQ), F32)
    for kb in range(n_slc // per):
        blk = jnp.concatenate([bias_t[kb * per:(kb + 1) * per], zrows], axis=0)
        biasq_ref[kb * BIAS_ROWS:(kb + 1) * BIAS_ROWS, :] = jnp.concatenate([blk] * R, axis=1).astype(BF16)
    qpad = jnp.zeros((KSEL - D - BIAS_ROWS, nl), BF16)

    def q_sel(kb):
        rows = biasq_ref[pl.ds(pl.multiple_of(kb * BIAS_ROWS, BIAS_ROWS), BIAS_ROWS), :]
        return jnp.concatenate([qt, rows, qpad], axis=0)

    def slc_scores(kb):
        return jnp.dot(k_rows(ks_ref, kb * TK), q_sel(kb), preferred_element_type=F32)

    def fill(buf_ref, mbuf_ref, chunks):
        for c, kb in enumerate(chunks):
            s = slc_scores(kb)
            buf_ref[c * TK:(c + 1) * TK, :] = s
            mbuf_ref[c:c + 1, :] = col_max(s)

    npairs = qi // 2
    s_sel = slc_scores(qi) + causal
    m_sel_chunk = col_max(s_sel)
    fill(sbuf_refs[0], mbuf_refs[0], (0, 1))

    accw = win[1]
    o_w = accw[0:D] / accw[D:D + 1]

    gate_row = pl.multiple_of(GATE_OFF + GATE_SLOT * g, 8)
    sig = jax.nn.sigmoid(tailt_ref[pl.ds(gate_row, GATE_SLOT), :])

    def gate(c):
        return jnp.concatenate([sig[3 * r + c:3 * r + c + 1, :] for r in range(R)], axis=1)

    part_ref[...] = gate(0) * o_c + gate(2) * o_w

    m_sel, acc_sel = online(None, s_sel, m_sel_chunk, vt_cols(vst_ref, t0, TK))
    m_ref[...] = m_sel
    acc_ref[...] = acc_sel

    def absorb(buf_ref, mbuf_ref, first, n):
        state = (m_ref[...], acc_ref[...])
        for c in range(n):
            state = online(state, buf_ref[c * TK:(c + 1) * TK, :], mbuf_ref[c:c + 1, :],
                           vt_cols(vst_ref, (first + c) * TK, TK))
        m_ref[...] = state[0]
        acc_ref[...] = state[1]

    for k in range(n_stages):
        @pl.when(npairs > k)
        def _(k=k):
            fill(sbuf_refs[(k + 1) % 2], mbuf_refs[(k + 1) % 2],
                 (jnp.minimum(2 * k + 2, qi - 1), jnp.minimum(2 * k + 3, qi - 1)))
            absorb(sbuf_refs[k % 2], mbuf_refs[k % 2], 2 * k, 2)

    for parity in range(2):
        @pl.when((qi % 2 == 1) & (npairs % 2 == parity))
        def _(parity=parity):
            absorb(sbuf_refs[parity], mbuf_refs[parity], qi - 1, 1)

    acc = acc_ref[...]
    ot = part_ref[...] + gate(1) * (acc[0:D] / acc[D:D + 1])
    stacked = jnp.concatenate([ot[:, r * TQ:(r + 1) * TQ] for r in range(R)], axis=0)
    o_ref[...] = stacked.T.astype(o_ref.dtype)


def _overlap_t(seq):
    n_cmp = (seq - CMP_BLOCK) // CMP_STRIDE + 1
    n_slc = seq // SLC_BLOCK
    cs = np.arange(n_cmp) * CMP_STRIDE
    ss = np.arange(n_slc) * SLC_BLOCK
    overlap = np.clip(np.minimum(cs[:, None] + CMP_BLOCK, ss[None, :] + SLC_BLOCK)
                      - np.maximum(cs[:, None], ss[None, :]), 0, None) / CMP_BLOCK
    ovt = np.zeros((n_slc, n_cmp + 1), np.float32)
    ovt[:, :n_cmp] = overlap.T
    return jnp.asarray(ovt)


def _mask_tiles():
    ki = np.arange(TK)[:, None]
    qi = np.arange(TQ)[None, :]
    neg = np.float32(NEG_BIG)
    none = np.full((TK, TQ), neg, np.float32)
    far = np.where(ki > qi, np.float32(0), neg)
    full = np.zeros((TK, TQ), np.float32)
    causal = np.where(ki <= qi, np.float32(0), neg)
    return jnp.asarray(np.concatenate([none, far, full, causal], axis=0))


def _cmp_masks(seq):
    ncr = seq // CMP_STRIDE
    n_cmp = (seq - CMP_BLOCK) // CMP_STRIDE + 1
    n = np.arange(ncr)[None, :, None]
    t = (np.arange(seq // TQ)[:, None, None] * TQ + np.arange(TQ)[None, None, :])
    visible = (n * CMP_STRIDE + CMP_BLOCK - 1 <= t) & (n < n_cmp)
    return jnp.asarray(np.where(visible, np.float32(0), np.float32(NEG_BIG)).reshape(-1, TQ))


def _nsa(qt, kc, vct, ks, vst, kw, vwt, tailt, batch, seq):
    t = batch * seq
    G, R, D = NSA_KV_HEADS, NSA_REP, NSA_HEAD_DIM
    nq = seq // TQ
    ncr = seq // CMP_STRIDE
    n_slc = seq // SLC_BLOCK
    assert TQ == TK and WINDOW == 2 * TK, "window branch visits exactly three key chunks"
    assert TK // SLC_BLOCK <= BIAS_ROWS and D + BIAS_ROWS <= KSEL
    const = lambda b, g, i: (0, 0)
    vtspec = pl.BlockSpec((1, D, seq), lambda b, g, i: (g, 0, b))
    in_specs = [
        pl.BlockSpec((R, D, TQ), lambda b, g, i: (g, 0, b * nq + i)),
        pl.BlockSpec((1, ncr, D), lambda b, g, i: (g, b, 0)),
        pl.BlockSpec((1, D, ncr), lambda b, g, i: (g, 0, b)),
        pl.BlockSpec((1, seq, KSEL), lambda b, g, i: (g, b, 0)),
        vtspec,
        pl.BlockSpec((1, seq, D), lambda b, g, i: (g, b, 0)),
        vtspec,
        pl.BlockSpec((TAIL, TQ), lambda b, g, i: (0, b * nq + i)),
        pl.BlockSpec((n_slc, ncr), const),
        pl.BlockSpec((4 * TK, TQ), const),
        pl.BlockSpec((nq * ncr, TQ), const),
    ]
    return pl.pallas_call(
        _nsa_kernel, grid=(batch, G, nq), in_specs=in_specs,
        out_specs=pl.BlockSpec((TQ, R * D), lambda b, g, i: (b * nq + i, g)),
        out_shape=jax.ShapeDtypeStruct((t, NSA_Q), BF16),
        scratch_shapes=[pltpu.VMEM((seq // TK * BIAS_ROWS, R * TQ), BF16),
                        pltpu.VMEM((1, R * TQ), F32),
                        pltpu.VMEM((D + ONES_ROWS, R * TQ), F32),
                        pltpu.VMEM((D, R * TQ), F32),
                        pltpu.VMEM((2 * TK, R * TQ), F32),
                        pltpu.VMEM((2 * TK, R * TQ), F32),
                        pltpu.VMEM((8, R * TQ), F32),
                        pltpu.VMEM((8, R * TQ), F32)],
        compiler_params=_cparams(("arbitrary", "arbitrary", "arbitrary")), name="nsa",
    )(qt, kc, vct, ks, vst, kw, vwt, tailt, _overlap_t(seq), _mask_tiles(), _cmp_masks(seq))


def _outproj_kernel(y_ref, o_ref, x_ref, w_ref, ag_ref, ng_ref, x1_ref, h2_ref):
    yn = _rms(o_ref[...].astype(F32), ag_ref[...], NORM_EPS).astype(BF16)
    x1 = (x_ref[...]
          + jnp.dot(y_ref[...], w_ref[0:SSM_D_INNER, :], preferred_element_type=F32)
          + jnp.dot(yn, w_ref[SSM_D_INNER:, :], preferred_element_type=F32))
    x1_ref[...] = x1
    h2_ref[...] = _rms(x1, ng_ref[...], NORM_EPS).astype(BF16)


def _out_proj(y_ssm, o_nsa, x2, w_out, attn_g, norm2_g):
    t = x2.shape[0]
    tm = TM_PROJ
    row = lambda i: (i, 0)
    const = lambda i: (0, 0)
    tok = pl.BlockSpec((tm, D_MODEL), row)
    vec = pl.BlockSpec((1, D_MODEL), const)
    return pl.pallas_call(
        _outproj_kernel, grid=(t // tm,),
        in_specs=[tok, tok, tok, pl.BlockSpec((SSM_D_INNER + NSA_Q, D_MODEL), const), vec, vec],
        out_specs=(tok, tok),
        out_shape=(jax.ShapeDtypeStruct((t, D_MODEL), F32), jax.ShapeDtypeStruct((t, D_MODEL), BF16)),
        compiler_params=_cparams(("arbitrary",)), name="out_proj",
    )(y_ssm, o_nsa, x2, w_out, attn_g, norm2_g)


def _ffn_kernel(h_ref, halo_ref, x1_ref, wup_ref, cw_ref, cb_ref, wd_ref, fg_ref, out_ref, act_ref,
                *, tiles_per_seq):
    i = pl.program_id(0)
    tm = h_ref.shape[0]
    pad = halo_ref.shape[0]
    tn = TN_FFN
    halo = halo_ref[...]
    halo = jnp.where(i % tiles_per_seq == 0, jnp.zeros_like(halo), halo)
    hc = jnp.concatenate([halo, h_ref[...]], axis=0)

    def branch(c0):
        cols = slice(c0, c0 + tn)
        u = jnp.dot(hc, wup_ref[:, cols], preferred_element_type=F32)
        out = cb_ref[:, cols] + cw_ref[FFN_CONV - 1:FFN_CONV, cols] * u[pad:pad + tm, :]
        for k in range(FFN_CONV - 1):
            shifted = pltpu.roll(u, FFN_CONV - 1 - k, 0)
            out = out + cw_ref[k:k + 1, cols] * shifted[pad:pad + tm, :]
        return out

    for j in range(D_FF // tn):
        act = _silu(branch(j * tn)) * branch(D_FF + j * tn)
        act_ref[:, j * tn:(j + 1) * tn] = act.astype(BF16)
    down = jnp.dot(act_ref[...], wd_ref[...], preferred_element_type=F32)
    out_ref[...] = _rms(x1_ref[...] + down, fg_ref[...], NORM_EPS)


def _ffn(h2, x1, w_up, conv_w, conv_b, w_down, final_g, seq):
    t = h2.shape[0]
    tm = TM_FFN
    pad = BF16_SUBLANES
    tok = pl.BlockSpec((tm, D_MODEL), lambda i: (i, 0))

    def resident(shape):
        return pl.BlockSpec(shape, lambda i: (0, 0), pipeline_mode=pl.Buffered(1))

    in_specs = [
        tok,
        pl.BlockSpec((pad, D_MODEL), lambda i: (jnp.maximum(i * (tm // pad) - 1, 0), 0)),
        tok,
        resident((D_MODEL, 2 * D_FF)),
        resident((FFN_CONV, 2 * D_FF)),
        resident((1, 2 * D_FF)),
        resident((D_FF, D_MODEL)),
        resident((1, D_MODEL)),
    ]
    return pl.pallas_call(
        functools.partial(_ffn_kernel, tiles_per_seq=seq // tm), grid=(t // tm,),
        in_specs=in_specs, out_specs=tok,
        out_shape=jax.ShapeDtypeStruct((t, D_MODEL), F32),
        scratch_shapes=[pltpu.VMEM((tm, D_FF), BF16)],
        compiler_params=_cparams(("arbitrary",)), name="ffn",
    )(h2, h2, x1, w_up, conv_w, conv_b, w_down, final_g)


def _rope_tables(seq):
    half = NSA_HEAD_DIM // 2
    inv_freq = 1.0 / (ROPE_THETA ** (jnp.arange(0, NSA_HEAD_DIM, 2, dtype=F32) / NSA_HEAD_DIM))
    ang = jnp.arange(seq).astype(F32)[:, None] * inv_freq[None, :]
    cos, sin = jnp.cos(ang), jnp.sin(ang)
    reps = LANES // NSA_HEAD_DIM
    cos_t = jnp.tile(jnp.concatenate([cos, cos], axis=1), (1, reps))
    sin_t = jnp.tile(jnp.concatenate([-sin, sin], axis=1), (1, reps))
    return cos_t, sin_t


def _split_w_in(w):
    o_z, o_xbc, o_dt, o_q, o_kv, o_g = np.cumsum([0, SSM_D_INNER, SSM_XBC, SSM_HEADS, NSA_Q, 6 * NSA_KV]).tolist()
    per = 3 * NSA_REP
    tail_cols = [w[:, o_dt:o_q]]
    for g in range(NSA_KV_HEADS):
        tail_cols += [w[:, o_g + g * per:o_g + (g + 1) * per], jnp.zeros((w.shape[0], GATE_SLOT - per), w.dtype)]
    tail_cols.append(jnp.zeros((w.shape[0], TAIL - GATE_OFF - NSA_KV_HEADS * GATE_SLOT), w.dtype))
    return (w[:, o_z:o_dt].astype(BF16), w[:, o_q:o_g].astype(BF16),
            jnp.concatenate(tail_cols, axis=1).astype(BF16))


def kernel(x, norm1_g, w_in, ssm_conv_w, ssm_conv_b, ssm_dt_bias, ssm_a_log, ssm_d, ssm_norm_g, cmp_k_pos, cmp_k_w1, cmp_k_b1, cmp_k_w2, cmp_v_pos, cmp_v_w1, cmp_v_b1, cmp_v_w2, attn_norm_g, w_out, norm2_g, ffn_w_up, ffn_conv_w, ffn_conv_b, ffn_w_down, final_norm_g):
    batch, seq, d = x.shape
    assert w_in.shape[0] == 1, "single-layer problem"
    l = 0
    cos_t, sin_t = _rope_tables(seq)
    x2 = x.reshape(batch * seq, d)
    zs, xbc, qt, kc_r, vc_r, ks, vst, kw, vwt, tail, tailt = _in_proj(
        x2, norm1_g[l].reshape(1, d), _split_w_in(w_in.reshape(d, -1)), cos_t, sin_t, seq)
    y_ssm = _ssd(xbc, zs, tail, tailt, ssm_conv_w[l], ssm_conv_b[l].reshape(1, -1), ssm_dt_bias[l],
                 ssm_a_log[l], ssm_d[l], ssm_norm_g[l], batch, seq)
    kc, vct = _compress(kc_r, vc_r,
                        (cmp_k_pos[l], cmp_k_w1[l], cmp_k_b1[l], cmp_k_w2[l]),
                        (cmp_v_pos[l], cmp_v_w1[l], cmp_v_b1[l], cmp_v_w2[l]), batch, seq)
    o_nsa = _nsa(qt, kc, vct, ks, vst, kw, vwt, tailt, batch, seq)
    x1, h2 = _out_proj(y_ssm, o_nsa, x2, w_out.reshape(-1, d).astype(BF16), attn_norm_g[l].reshape(1, d),
                       norm2_g[l].reshape(1, d))
    out = _ffn(h2, x1, ffn_w_up.reshape(d, -1).astype(BF16), ffn_conv_w[l], ffn_conv_b[l].reshape(1, -1),
               ffn_w_down.reshape(-1, d).astype(BF16), final_norm_g.reshape(1, d), seq)
    return out.reshape(batch, seq, d)
```

```python
import functools
import math

import numpy as np
import jax
import jax.numpy as jnp
from jax import lax
from jax.experimental import pallas as pl
from jax.experimental.pallas import tpu as pltpu

F32 = jnp.float32
BF16 = jnp.bfloat16

D_MODEL = 1024
SSM_D_INNER = 1024
SSM_HEAD_DIM = 64
SSM_HEADS = 16
SSM_GROUPS = 2
SSM_STATE = 128
SSM_CONV = 4
SSM_CHUNK = 128
SSM_XBC = SSM_D_INNER + 2 * SSM_GROUPS * SSM_STATE
SSM_NORM_EPS = 1e-5
NSA_HEADS = 16
NSA_KV_HEADS = 4
NSA_REP = NSA_HEADS // NSA_KV_HEADS
NSA_HEAD_DIM = 64
NSA_Q = NSA_HEADS * NSA_HEAD_DIM
NSA_KV = NSA_KV_HEADS * NSA_HEAD_DIM
CMP_BLOCK = 32
CMP_STRIDE = 16
SLC_BLOCK = 64
SLC_TOP_N = 16
N_LOCAL_BLOCKS = 2
FORCED_SCORE = 1e4
WINDOW = 512
ROPE_THETA = 10000.0
D_FF = 2816
FFN_CONV = 3
NORM_EPS = 1e-6
NEG_BIG = -1e30

LANES = 128
BF16_SUBLANES = 16
TAIL = LANES
GATE_OFF = SSM_HEADS
GATE_SLOT = 16

TM_PROJ = 256
TQ = 256
TK = 256
NSA_GROUPS_PER_STEP = 2
ONES_ROWS = 16
KSEL = LANES
BIAS_ROWS = 16
MT_NONE, MT_FAR, MT_ALL, MT_CAUSAL = 0, TK, 2 * TK, 3 * TK
TM_FFN = 512
TN_FFN = 256
VMEM_LIMIT = 56 * 1024 * 1024

NT_DIMS = (((1,), (1,)), ((), ()))


def _cparams(sem):
    return pltpu.CompilerParams(dimension_semantics=sem, vmem_limit_bytes=VMEM_LIMIT)


def _rms(x, g, eps):
    return x * lax.rsqrt(jnp.mean(x * x, axis=-1, keepdims=True) + eps) * g


def _silu(x):
    return x * jax.nn.sigmoid(x)


def _inproj_kernel(x_ref, g_ref, wa_ref, wb_ref, wt_ref, cos_ref, sin_ref,
                   zs_ref, xbc_ref, qt_ref, kc_ref, vc_ref, ks_ref, vst_ref, kw_ref, vwt_ref,
                   tail_ref, tailt_ref, kbuf_ref, vbuf_ref, *, tiles_per_seq):
    tm = x_ref.shape[0]
    seq_tile = pl.program_id(0) % tiles_per_seq
    h = _rms(x_ref[...], g_ref[...], NORM_EPS).astype(BF16)

    def mm(w_ref, lo, hi):
        return jnp.dot(h, w_ref[:, lo:hi], preferred_element_type=F32)

    tail = mm(wt_ref, 0, TAIL)
    tail_ref[...] = tail
    tailt_ref[...] = tail.T

    cos = cos_ref[...]
    sin = sin_ref[...]
    lane = lax.broadcasted_iota(jnp.int32, (tm, LANES), 1)
    first_half = (lane % NSA_HEAD_DIM) < (NSA_HEAD_DIM // 2)

    def rope(xc):
        partner = jnp.where(first_half, pltpu.roll(xc, LANES - 32, 1), pltpu.roll(xc, 32, 1))
        return xc * cos + partner * sin

    q = mm(wb_ref, 0, NSA_Q)
    scale = NSA_HEAD_DIM ** -0.5 * math.log2(math.e)
    for c in range(NSA_Q // LANES):
        rt = (rope(q[:, c * LANES:(c + 1) * LANES]) * scale).T.astype(BF16)
        qt_ref[2 * c] = rt[:NSA_HEAD_DIM]
        qt_ref[2 * c + 1] = rt[NSA_HEAD_DIM:]

    kv = mm(wb_ref, NSA_Q, NSA_Q + 6 * NSA_KV)

    def seg(i):
        return kv[:, i * NSA_KV:(i + 1) * NSA_KV]

    def rope_seg(x):
        return jnp.concatenate([rope(x[:, :LANES]), rope(x[:, LANES:])], axis=1)

    def store_heads(ref, x):
        for g in range(NSA_KV_HEADS):
            ref[g] = x[:, g * NSA_HEAD_DIM:(g + 1) * NSA_HEAD_DIM].astype(BF16)

    def store_heads_t(ref, x):
        xt = x.T.astype(BF16)
        for g in range(NSA_KV_HEADS):
            ref[g] = xt[g * NSA_HEAD_DIM:(g + 1) * NSA_HEAD_DIM]

    kc_tile = rope_seg(seg(0))
    vc_tile = seg(1)
    for c in range(NSA_KV // LANES):
        kbuf_ref[c] = kc_tile[:, c * LANES:(c + 1) * LANES]
        vbuf_ref[c] = vc_tile[:, c * LANES:(c + 1) * LANES]
    gw = CMP_STRIDE * NSA_HEAD_DIM
    for l in range(CMP_STRIDE):
        for c in range(NSA_KV // LANES):
            k_rows = kbuf_ref[c, pl.ds(l, tm // CMP_STRIDE, stride=CMP_STRIDE), :]
            v_rows = vbuf_ref[c, pl.ds(l, tm // CMP_STRIDE, stride=CMP_STRIDE), :]
            for half in range(LANES // NSA_HEAD_DIM):
                g = c * (LANES // NSA_HEAD_DIM) + half
                src = slice(half * NSA_HEAD_DIM, (half + 1) * NSA_HEAD_DIM)
                dst = slice(g * gw + l * NSA_HEAD_DIM, g * gw + (l + 1) * NSA_HEAD_DIM)
                kc_ref[:, dst] = k_rows[:, src]
                vc_ref[:, dst] = v_rows[:, src]
    pos = seq_tile * tm + lax.broadcasted_iota(jnp.int32, (tm, NSA_HEAD_DIM), 0)
    block_in_chunk = (pos // SLC_BLOCK) % (TK // SLC_BLOCK)
    onehot = jnp.where(lax.broadcasted_iota(jnp.int32, (tm, NSA_HEAD_DIM), 1) == block_in_chunk, 1.0, 0.0)
    ks = rope_seg(seg(2))
    for g in range(NSA_KV_HEADS):
        kg = ks[:, g * NSA_HEAD_DIM:(g + 1) * NSA_HEAD_DIM]
        ks_ref[g] = jnp.concatenate([kg, onehot], axis=1).astype(BF16)
    store_heads_t(vst_ref, seg(3))
    store_heads(kw_ref, rope_seg(seg(4)))
    store_heads_t(vwt_ref, seg(5))

    zs_ref[...] = _silu(mm(wa_ref, 0, SSM_D_INNER))
    xbc_ref[...] = mm(wa_ref, SSM_D_INNER, SSM_D_INNER + SSM_XBC)


def _in_proj(x2, norm_g, w_parts, cos_t, sin_t, seq):
    t = x2.shape[0]
    tm = TM_PROJ
    nseq = seq // tm
    row = lambda i: (i, 0)
    const = lambda i: (0, 0)
    heads = lambda i: (0, i, 0)
    heads_t = lambda i: (0, 0, i)
    out_shape = (
        jax.ShapeDtypeStruct((t, SSM_D_INNER), F32),
        jax.ShapeDtypeStruct((t, SSM_XBC), F32),
        jax.ShapeDtypeStruct((NSA_HEADS, NSA_HEAD_DIM, t), BF16),
        jax.ShapeDtypeStruct((t // CMP_STRIDE, CMP_STRIDE * NSA_KV), F32),
        jax.ShapeDtypeStruct((t // CMP_STRIDE, CMP_STRIDE * NSA_KV), F32),
        jax.ShapeDtypeStruct((NSA_KV_HEADS, t, KSEL), BF16),
        jax.ShapeDtypeStruct((NSA_KV_HEADS, NSA_HEAD_DIM, t), BF16),
        jax.ShapeDtypeStruct((NSA_KV_HEADS, t, NSA_HEAD_DIM), BF16),
        jax.ShapeDtypeStruct((NSA_KV_HEADS, NSA_HEAD_DIM, t), BF16),
        jax.ShapeDtypeStruct((t, TAIL), F32),
        jax.ShapeDtypeStruct((TAIL, t), F32),
    )
    kvh = pl.BlockSpec((NSA_KV_HEADS, tm, NSA_HEAD_DIM), heads)
    kvh_t = pl.BlockSpec((NSA_KV_HEADS, NSA_HEAD_DIM, tm), heads_t)
    out_specs = (
        pl.BlockSpec((tm, SSM_D_INNER), row),
        pl.BlockSpec((tm, SSM_XBC), row),
        pl.BlockSpec((NSA_HEADS, NSA_HEAD_DIM, tm), heads_t),
        pl.BlockSpec((tm // CMP_STRIDE, CMP_STRIDE * NSA_KV), row),
        pl.BlockSpec((tm // CMP_STRIDE, CMP_STRIDE * NSA_KV), row),
        pl.BlockSpec((NSA_KV_HEADS, tm, KSEL), heads), kvh_t, kvh, kvh_t,
        pl.BlockSpec((tm, TAIL), row),
        pl.BlockSpec((TAIL, tm), lambda i: (0, i)),
    )
    in_specs = [
        pl.BlockSpec((tm, D_MODEL), row),
        pl.BlockSpec((1, D_MODEL), const),
        pl.BlockSpec((D_MODEL, SSM_D_INNER + SSM_XBC), const),
        pl.BlockSpec((D_MODEL, NSA_Q + 6 * NSA_KV), const),
        pl.BlockSpec((D_MODEL, TAIL), const),
        pl.BlockSpec((tm, LANES), lambda i: (i % nseq, 0)),
        pl.BlockSpec((tm, LANES), lambda i: (i % nseq, 0)),
    ]
    return pl.pallas_call(
        functools.partial(_inproj_kernel, tiles_per_seq=nseq), grid=(t // tm,), in_specs=in_specs,
        out_specs=out_specs,
        out_shape=out_shape,
        scratch_shapes=[pltpu.VMEM((NSA_KV // LANES, tm, LANES), F32),
                        pltpu.VMEM((NSA_KV // LANES, tm, LANES), F32)],
        compiler_params=_cparams(("arbitrary",)), name="in_proj",
    )(x2, norm_g, *w_parts, cos_t, sin_t)


def _ssd_kernel(xbc_ref, zs_ref, tail_ref, tailt_ref, cw_ref, cb_ref, dtb_ref, dtbt_ref,
                alog_ref, alogt_ref, dskip_ref, ng_ref,
                y_ref, xcat_ref, state_ref, ybuf_ref):
    L, P, N, H, G = SSM_CHUNK, SSM_HEAD_DIM, SSM_STATE, SSM_HEADS, SSM_GROUPS
    HG = H // G
    halo = 8

    @pl.when(pl.program_id(1) == 0)
    def _():
        xcat_ref[0:halo, :] = jnp.zeros((halo, SSM_XBC), F32)
        state_ref[...] = jnp.zeros_like(state_ref)

    xcat_ref[halo:halo + L, :] = xbc_ref[...]
    xcat = xcat_ref[...]
    conv = cb_ref[...] + cw_ref[SSM_CONV - 1:SSM_CONV, :] * xcat[halo:halo + L]
    for k in range(SSM_CONV - 1):
        shifted = pltpu.roll(xcat, SSM_CONV - 1 - k, 0)
        conv = conv + cw_ref[k:k + 1, :] * shifted[halo:halo + L]
    xcat_ref[0:halo, :] = xcat[L:L + halo]
    u = _silu(conv)
    xs = u[:, :SSM_D_INNER]
    bm = u[:, SSM_D_INNER:SSM_D_INNER + G * N]
    cm = u[:, SSM_D_INNER + G * N:]

    def softplus(v):
        return jnp.maximum(v, 0.0) + jnp.log1p(jnp.exp(-jnp.abs(v)))

    dt = softplus(tail_ref[:, 0:H] + dtb_ref[...])
    dtt = softplus(tailt_ref[0:H, :] + dtbt_ref[...])
    da = dt * (-jnp.exp(alog_ref[...]))
    dat = dtt * (-jnp.exp(alogt_ref[...]))
    ri = lax.broadcasted_iota(jnp.int32, (L, L), 0)
    ci = lax.broadcasted_iota(jnp.int32, (L, L), 1)
    tri = ci <= ri
    hi = lax.Precision.HIGHEST
    acs = jnp.dot(tri.astype(F32), da, precision=hi, preferred_element_type=F32)
    acst = jnp.dot(dat, (ri <= ci).astype(F32), precision=hi, preferred_element_type=F32)
    last = acs[L - 1:L, :]
    w_state = dt * jnp.exp(last - acs)
    eacs = jnp.exp(acs)
    cdec = jnp.exp(last)

    hrow = lax.broadcasted_iota(jnp.int32, (H, SSM_D_INNER), 0)
    hcol = lax.broadcasted_iota(jnp.int32, (H, SSM_D_INNER), 1)
    expand = jnp.where(hcol // P == hrow, 1.0, 0.0).astype(BF16)

    def split3(v):
        v_hi = v.astype(BF16)
        r1 = v - v_hi.astype(F32)
        v_mid = r1.astype(BF16)
        v_lo = (r1 - v_mid.astype(F32)).astype(BF16)
        return jnp.concatenate([v_hi, v_mid, v_lo], axis=1)

    small = jnp.concatenate([cdec, dskip_ref[...], jnp.zeros((6, H), F32)], axis=0)
    per_head = jnp.concatenate([dt, w_state, eacs, small], axis=0)
    spread = jnp.dot(split3(per_head), jnp.concatenate([expand] * 3, axis=0), preferred_element_type=F32)
    cdec_e = spread[3 * L:3 * L + 1, :]
    dskip_e = spread[3 * L + 1:3 * L + 2, :]
    xdt = (xs * spread[0:L]).astype(BF16)
    wst = (xs * spread[L:2 * L]).astype(BF16)
    eacs_e = spread[2 * L:3 * L]

    for g in range(G):
        bm_g = bm[:, g * N:(g + 1) * N]
        cm_g = cm[:, g * N:(g + 1) * N].astype(BF16)
        cb = lax.dot_general(cm_g, bm_g.astype(BF16), NT_DIMS, preferred_element_type=F32)
        cols = slice(g * HG * P, (g + 1) * HG * P)
        st = state_ref[:, cols]
        y_off = jnp.dot(cm_g, st.astype(BF16), preferred_element_type=F32) * eacs_e[:, cols]
        ybuf_ref[:, cols] = y_off
        bmt = bm_g.T.astype(BF16)
        state_ref[:, cols] = st * cdec_e[:, cols] + jnp.dot(bmt, wst[:, cols], preferred_element_type=F32)
        for r in range(HG):
            hh = g * HG + r
            diff = acs[:, hh:hh + 1] - acst[hh:hh + 1, :]
            seg = jnp.exp(jnp.where(tri, diff, -jnp.inf))
            lmat = (cb * seg).astype(BF16)
            hc = slice(hh * P, (hh + 1) * P)
            ybuf_ref[:, hc] = ybuf_ref[:, hc] + jnp.dot(lmat, xdt[:, hc], preferred_element_type=F32)

    y = (ybuf_ref[...] + xs * dskip_e) * zs_ref[...]
    gw = SSM_D_INNER // G
    parts = []
    for g in range(G):
        yg = y[:, g * gw:(g + 1) * gw]
        parts.append(yg * lax.rsqrt(jnp.mean(yg * yg, axis=-1, keepdims=True) + SSM_NORM_EPS))
    y_ref[...] = (jnp.concatenate(parts, axis=1) * ng_ref[...]).astype(y_ref.dtype)


def _ssd(xbc, zs, tail, tailt, conv_w, conv_b, dt_bias, a_log, d_skip, norm_g, batch, seq):
    t = xbc.shape[0]
    L = SSM_CHUNK
    nc = seq // L
    row = lambda b, c: (b * nc + c, 0)
    const = lambda b, c: (0, 0)
    H = SSM_HEADS
    in_specs = [
        pl.BlockSpec((L, SSM_XBC), row),
        pl.BlockSpec((L, SSM_D_INNER), row),
        pl.BlockSpec((L, TAIL), row),
        pl.BlockSpec((TAIL, L), lambda b, c: (0, b * nc + c)),
        pl.BlockSpec((SSM_CONV, SSM_XBC), const),
        pl.BlockSpec((1, SSM_XBC), const),
        pl.BlockSpec((1, H), const),
        pl.BlockSpec((H, 1), const),
        pl.BlockSpec((1, H), const),
        pl.BlockSpec((H, 1), const),
        pl.BlockSpec((1, H), const),
        pl.BlockSpec((1, SSM_D_INNER), const),
    ]
    return pl.pallas_call(
        _ssd_kernel, grid=(batch, nc), in_specs=in_specs,
        out_specs=pl.BlockSpec((L, SSM_D_INNER), row),
        out_shape=jax.ShapeDtypeStruct((t, SSM_D_INNER), BF16),
        scratch_shapes=[pltpu.VMEM((L + 8, SSM_XBC), F32),
                        pltpu.VMEM((SSM_STATE, SSM_D_INNER), F32),
                        pltpu.VMEM((L, SSM_D_INNER), F32)],
        compiler_params=_cparams(("arbitrary", "arbitrary")), name="ssd",
    )(xbc, zs, tail, tailt, conv_w, conv_b, dt_bias.reshape(1, H), dt_bias.reshape(H, 1),
      a_log.reshape(1, H), a_log.reshape(H, 1), d_skip.reshape(1, H), norm_g.reshape(1, -1))


def _compress_kernel(k_ref, v_ref, kpos_ref, vpos_ref, kw1_ref, vw1_ref, kb1_ref, vb1_ref,
                     kw2_ref, vw2_ref, kc_ref, vct_ref):
    D = NSA_HEAD_DIM
    gw = CMP_STRIDE * D

    def hidden(t_ref, pos_ref, w1_ref, b1_ref, g):
        t = t_ref[:, g * gw:(g + 1) * gw]
        n = t.shape[0]
        lo = jnp.dot((t + pos_ref[0:1, :]).astype(BF16), w1_ref[...], preferred_element_type=F32)
        hi = jnp.dot((t + pos_ref[1:2, :]).astype(BF16), w1_ref[...], preferred_element_type=F32)
        pre = lo + pltpu.roll(pltpu.roll(hi, n - 1, 0), D, 1)
        return _silu(pre[:, 0:D] + b1_ref[...]).astype(BF16)

    for g in range(NSA_KV_HEADS):
        kc = jnp.dot(hidden(k_ref, kpos_ref, kw1_ref, kb1_ref, g), kw2_ref[...], preferred_element_type=F32)
        kc_ref[g] = kc.astype(BF16)
        vct = lax.dot_general(vw2_ref[...], hidden(v_ref, vpos_ref, vw1_ref, vb1_ref, g), NT_DIMS,
                              preferred_element_type=F32)
        vct_ref[g] = vct.astype(BF16)


def _compress_weights(pos, w1, b1, w2):
    half = CMP_BLOCK // 2 * NSA_HEAD_DIM
    w1cat = jnp.concatenate([w1[:half], w1[half:]], axis=1).astype(BF16)
    return pos.reshape(2, half), w1cat, b1.reshape(1, -1), w2.astype(BF16)


def _compress(k16, v16, kparams, vparams, batch, seq):
    rows = seq // CMP_STRIDE
    width = CMP_STRIDE * NSA_KV
    D = NSA_HEAD_DIM
    gw = CMP_STRIDE * D
    kpos, kw1, kb1, kw2 = _compress_weights(*kparams)
    vpos, vw1, vb1, vw2 = _compress_weights(*vparams)
    vw2 = vw2.T
    c2 = lambda b: (0, 0)
    tok = pl.BlockSpec((rows, width), lambda b: (b, 0))
    in_specs = [tok, tok,
                pl.BlockSpec((2, gw), c2), pl.BlockSpec((2, gw), c2),
                pl.BlockSpec((gw, 2 * D), c2), pl.BlockSpec((gw, 2 * D), c2),
                pl.BlockSpec((1, D), c2), pl.BlockSpec((1, D), c2),
                pl.BlockSpec((D, D), c2), pl.BlockSpec((D, D), c2)]
    out = jax.ShapeDtypeStruct((NSA_KV_HEADS, batch * rows, D), BF16)
    out_t = jax.ShapeDtypeStruct((NSA_KV_HEADS, D, batch * rows), BF16)
    ospec = pl.BlockSpec((NSA_KV_HEADS, rows, D), lambda b: (0, b, 0))
    ospec_t = pl.BlockSpec((NSA_KV_HEADS, D, rows), lambda b: (0, 0, b))
    return pl.pallas_call(
        _compress_kernel, grid=(batch,), in_specs=in_specs, out_specs=(ospec, ospec_t),
        out_shape=(out, out_t), compiler_params=_cparams(("arbitrary",)), name="compress",
    )(k16, v16, kpos, vpos, kw1, vw1, kb1, vb1, kw2, vw2)


def _nsa_kernel(qt_ref, kc_ref, vct_ref, ks_ref, vst_ref, kw_ref, vwt_ref, tailt_ref, ovt_ref, mtab_ref, cmask_ref,
                o_ref, biasq_ref, m_ref, acc_ref, part_ref, sbuf0_ref, sbuf1_ref, mbuf0_ref, mbuf1_ref):
    R, D = NSA_REP, NSA_HEAD_DIM
    nl = R * TQ
    grp = range(kc_ref.shape[0])
    sbuf_refs = (sbuf0_ref, sbuf1_ref)
    mbuf_refs = (mbuf0_ref, mbuf1_ref)
    n_stages = (ks_ref.shape[1] // TK - 1) // 2
    gp = pl.program_id(1)
    qi = pl.program_id(2)
    t0 = qi * TQ
    qt = [jnp.concatenate([qt_ref[gi * R + r] for r in range(R)], axis=1) for gi in grp]
    tpos_row = t0 + lax.broadcasted_iota(jnp.int32, (1, nl), 1) % TQ

    def k_rows(k_ref, gi, start):
        return k_ref[gi, pl.ds(pl.multiple_of(start, TK), TK), :]

    def vt_cols(vt_ref, gi, start, n):
        return vt_ref[gi, :, pl.ds(pl.multiple_of(start, TK), n)]

    def pv(vt, p):
        vt1 = jnp.concatenate([vt, jnp.ones((ONES_ROWS, vt.shape[1]), BF16)], axis=0)
        return jnp.dot(vt1, p, preferred_element_type=F32)

    def mask_tile(off):
        return jnp.concatenate([mtab_ref[pl.ds(pl.multiple_of(off, TK), TK), :]] * R, axis=1)

    def col_max(s):
        return jnp.max(s, axis=0, keepdims=True)

    def online(state, s, m_chunk, vt):
        m_new = m_chunk if state is None else jnp.maximum(state[0], m_chunk)
        contrib = pv(vt, jnp.exp2(s - m_new).astype(BF16))
        if state is None:
            return m_new, contrib
        return m_new, jnp.exp2(state[0] - m_new) * state[1] + contrib

    causal = mask_tile(MT_CAUSAL)
    far0 = jnp.maximum(t0 - 2 * TK, 0)
    mid0 = jnp.maximum(t0 - TK, 0)
    window = []
    for gi in grp:
        s_dia = jnp.dot(k_rows(kw_ref, gi, t0), qt[gi], preferred_element_type=F32) + causal
        m_dia = col_max(s_dia)
        s_mid = (jnp.dot(k_rows(kw_ref, gi, mid0), qt[gi], preferred_element_type=F32)
                 + mask_tile(jnp.where(qi >= 1, MT_ALL, MT_NONE)))
        m_mid = col_max(s_mid)
        s_far = (jnp.dot(k_rows(kw_ref, gi, far0), qt[gi], preferred_element_type=F32)
                 + mask_tile(jnp.where(qi >= 2, MT_FAR, MT_NONE)))
        m_far = col_max(s_far)
        window.append(((s_dia, m_dia, t0), (s_mid, m_mid, mid0), (s_far, m_far, far0)))

    ncr = kc_ref.shape[1]
    n_slc = ovt_ref.shape[0]
    cmask = cmask_ref[pl.ds(pl.multiple_of(qi * ncr, ncr), ncr), :]
    cmask = jnp.concatenate([cmask] * R, axis=1)
    sees_any = tpos_row >= CMP_BLOCK - 1
    jj = lax.broadcasted_iota(jnp.int32, (n_slc, TQ), 0)
    tt = t0 + lax.broadcasted_iota(jnp.int32, (n_slc, TQ), 1)
    lag = tt // SLC_BLOCK - jj
    forced = (jj == 0) | ((lag >= 0) & (lag < N_LOCAL_BLOCKS))
    valid = jj * SLC_BLOCK <= tt
    rows8 = 8
    j8 = lax.broadcasted_iota(jnp.int32, (rows8, TQ), 0)
    per = TK // SLC_BLOCK
    zrows = jnp.zeros((BIAS_ROWS - per, TQ), F32)
    o_c, o_w = [], []
    for gi in grp:
        s_m = jnp.dot(kc_ref[gi], qt[gi], preferred_element_type=F32) + cmask
        mx = jnp.max(s_m, axis=0, keepdims=True)
        p = jnp.exp2(s_m - mx)
        den = jnp.sum(p, axis=0, keepdims=True)
        pc = p * jnp.where(sees_any, 1.0 / den, 0.0)
        o_c.append(jnp.dot(vct_ref[gi], pc.astype(BF16), preferred_element_type=F32))

        psum = pc[:, 0:TQ]
        for r in range(1, R):
            psum = psum + pc[:, r * TQ:(r + 1) * TQ]
        imp_t = jnp.dot(ovt_ref[...], psum, precision=lax.Precision.HIGHEST,
                        preferred_element_type=F32)
        score = jnp.where(forced, FORCED_SCORE, jnp.where(valid, imp_t, -1.0))
        groups = [score[a:a + rows8] for a in range(0, n_slc, rows8)]
        ranks = [jnp.zeros((rows8, TQ), F32) for _ in groups]
        window_steps = dict(zip((0, n_slc // 3, 2 * n_slc // 3), window[gi]))
        win = None
        for j2 in range(n_slc):
            if j2 in window_steps:
                s_w, m_w, start_w = window_steps[j2]
                win = online(win, s_w, m_w, vt_cols(vwt_ref, gi, start_w, TK))
            sj = score[j2:j2 + 1, :]
            for a, sg in enumerate(groups):
                lo = a * rows8
                if lo + rows8 - 1 < j2:
                    beats = sj > sg
                elif lo > j2:
                    beats = sj >= sg
                else:
                    beats = (sj > sg) | ((sj == sg) & (j8 + lo > j2))
                ranks[a] = ranks[a] + jnp.where(beats, 1.0, 0.0)
        rank = jnp.concatenate(ranks, axis=0)
        selected = (rank < float(min(SLC_TOP_N, n_slc))) & (score >= 0.0)
        bias_t = jnp.where(selected, 0.0, NEG_BIG)
        for kb in range(n_slc // per):
            blk = jnp.concatenate([bias_t[kb * per:(kb + 1) * per], zrows], axis=0)
            biasq_ref[gi, kb * BIAS_ROWS:(kb + 1) * BIAS_ROWS, :] = (
                jnp.concatenate([blk] * R, axis=1).astype(BF16))
        accw = win[1]
        o_w.append(accw[0:D] / accw[D:D + 1])
    qpad = jnp.zeros((KSEL - D - BIAS_ROWS, nl), BF16)

    def q_sel(gi, kb):
        rows = biasq_ref[gi, pl.ds(pl.multiple_of(kb * BIAS_ROWS, BIAS_ROWS), BIAS_ROWS), :]
        return jnp.concatenate([qt[gi], rows, qpad], axis=0)

    def slc_scores(gi, kb):
        return jnp.dot(k_rows(ks_ref, gi, kb * TK), q_sel(gi, kb), preferred_element_type=F32)

    def fill(gi, slot, chunks):
        for c, kb in enumerate(chunks):
            s = slc_scores(gi, kb)
            sbuf_refs[slot][gi, c * TK:(c + 1) * TK, :] = s
            mbuf_refs[slot][gi, c:c + 1, :] = col_max(s)

    npairs = qi // 2
    diag = []
    for gi in grp:
        s_sel = slc_scores(gi, qi) + causal
        diag.append((s_sel, col_max(s_sel)))
        fill(gi, 0, (0, 1))

    def gates(gi):
        gate_row = pl.multiple_of(GATE_OFF + GATE_SLOT * (gp * len(grp) + gi), 8)
        sig = jax.nn.sigmoid(tailt_ref[pl.ds(gate_row, GATE_SLOT), :])
        return [jnp.concatenate([sig[3 * r + c:3 * r + c + 1, :] for r in range(R)], axis=1)
                for c in range(3)]

    for gi in grp:
        gate = gates(gi)
        part_ref[gi] = gate[0] * o_c[gi] + gate[2] * o_w[gi]

    for gi in grp:
        m_sel, acc_sel = online(None, diag[gi][0], diag[gi][1], vt_cols(vst_ref, gi, t0, TK))
        m_ref[gi] = m_sel
        acc_ref[gi] = acc_sel

    def absorb(gi, slot, first, n):
        state = (m_ref[gi], acc_ref[gi])
        for c in range(n):
            state = online(state, sbuf_refs[slot][gi, c * TK:(c + 1) * TK, :], mbuf_refs[slot][gi, c:c + 1, :],
                           vt_cols(vst_ref, gi, (first + c) * TK, TK))
        m_ref[gi] = state[0]
        acc_ref[gi] = state[1]

    for k in range(n_stages):
        @pl.when(npairs > k)
        def _(k=k):
            for gi in grp:
                fill(gi, (k + 1) % 2, (jnp.minimum(2 * k + 2, qi - 1), jnp.minimum(2 * k + 3, qi - 1)))
            for gi in grp:
                absorb(gi, k % 2, 2 * k, 2)

    for parity in range(2):
        @pl.when((qi % 2 == 1) & (npairs % 2 == parity))
        def _(parity=parity):
            for gi in grp:
                absorb(gi, parity, qi - 1, 1)

    for gi in grp:
        acc = acc_ref[gi]
        ot = part_ref[gi] + gates(gi)[1] * (acc[0:D] / acc[D:D + 1])
        stacked = jnp.concatenate([ot[:, r * TQ:(r + 1) * TQ] for r in range(R)], axis=0)
        o_ref[:, gi * R * D:(gi + 1) * R * D] = stacked.T.astype(o_ref.dtype)


def _overlap_t(seq):
    n_cmp = (seq - CMP_BLOCK) // CMP_STRIDE + 1
    n_slc = seq // SLC_BLOCK
    cs = np.arange(n_cmp) * CMP_STRIDE
    ss = np.arange(n_slc) * SLC_BLOCK
    overlap = np.clip(np.minimum(cs[:, None] + CMP_BLOCK, ss[None, :] + SLC_BLOCK)
                      - np.maximum(cs[:, None], ss[None, :]), 0, None) / CMP_BLOCK
    ovt = np.zeros((n_slc, n_cmp + 1), np.float32)
    ovt[:, :n_cmp] = overlap.T
    return jnp.asarray(ovt)


def _mask_tiles():
    ki = np.arange(TK)[:, None]
    qi = np.arange(TQ)[None, :]
    neg = np.float32(NEG_BIG)
    none = np.full((TK, TQ), neg, np.float32)
    far = np.where(ki > qi, np.float32(0), neg)
    full = np.zeros((TK, TQ), np.float32)
    causal = np.where(ki <= qi, np.float32(0), neg)
    return jnp.asarray(np.concatenate([none, far, full, causal], axis=0))


def _cmp_masks(seq):
    ncr = seq // CMP_STRIDE
    n_cmp = (seq - CMP_BLOCK) // CMP_STRIDE + 1
    n = np.arange(ncr)[None, :, None]
    t = (np.arange(seq // TQ)[:, None, None] * TQ + np.arange(TQ)[None, None, :])
    visible = (n * CMP_STRIDE + CMP_BLOCK - 1 <= t) & (n < n_cmp)
    return jnp.asarray(np.where(visible, np.float32(0), np.float32(NEG_BIG)).reshape(-1, TQ))


def _nsa(qt, kc, vct, ks, vst, kw, vwt, tailt, batch, seq):
    t = batch * seq
    G, R, D = NSA_KV_HEADS, NSA_REP, NSA_HEAD_DIM
    nq = seq // TQ
    ncr = seq // CMP_STRIDE
    n_slc = seq // SLC_BLOCK
    assert TQ == TK and WINDOW == 2 * TK, "window branch visits exactly three key chunks"
    assert TK // SLC_BLOCK <= BIAS_ROWS and D + BIAS_ROWS <= KSEL
    P = NSA_GROUPS_PER_STEP
    const = lambda b, g, i: (0, 0)
    vtspec = pl.BlockSpec((P, D, seq), lambda b, g, i: (g, 0, b))
    in_specs = [
        pl.BlockSpec((P * R, D, TQ), lambda b, g, i: (g, 0, b * nq + i)),
        pl.BlockSpec((P, ncr, D), lambda b, g, i: (g, b, 0)),
        pl.BlockSpec((P, D, ncr), lambda b, g, i: (g, 0, b)),
        pl.BlockSpec((P, seq, KSEL), lambda b, g, i: (g, b, 0)),
        vtspec,
        pl.BlockSpec((P, seq, D), lambda b, g, i: (g, b, 0)),
        vtspec,
        pl.BlockSpec((TAIL, TQ), lambda b, g, i: (0, b * nq + i)),
        pl.BlockSpec((n_slc, ncr), const),
        pl.BlockSpec((4 * TK, TQ), const),
        pl.BlockSpec((nq * ncr, TQ), const),
    ]
    return pl.pallas_call(
        _nsa_kernel, grid=(batch, G // P, nq), in_specs=in_specs,
        out_specs=pl.BlockSpec((TQ, P * R * D), lambda b, g, i: (b * nq + i, g)),
        out_shape=jax.ShapeDtypeStruct((t, NSA_Q), BF16),
        scratch_shapes=[pltpu.VMEM((P, seq // TK * BIAS_ROWS, R * TQ), BF16),
                        pltpu.VMEM((P, 1, R * TQ), F32),
                        pltpu.VMEM((P, D + ONES_ROWS, R * TQ), F32),
                        pltpu.VMEM((P, D, R * TQ), F32),
                        pltpu.VMEM((P, 2 * TK, R * TQ), F32),
                        pltpu.VMEM((P, 2 * TK, R * TQ), F32),
                        pltpu.VMEM((P, 8, R * TQ), F32),
                        pltpu.VMEM((P, 8, R * TQ), F32)],
        compiler_params=_cparams(("arbitrary", "arbitrary", "arbitrary")), name="nsa",
    )(qt, kc, vct, ks, vst, kw, vwt, tailt, _overlap_t(seq), _mask_tiles(), _cmp_masks(seq))


def _outproj_kernel(y_ref, o_ref, x_ref, w_ref, ag_ref, ng_ref, x1_ref, h2_ref):
    yn = _rms(o_ref[...].astype(F32), ag_ref[...], NORM_EPS).astype(BF16)
    x1 = (x_ref[...]
          + jnp.dot(y_ref[...], w_ref[0:SSM_D_INNER, :], preferred_element_type=F32)
          + jnp.dot(yn, w_ref[SSM_D_INNER:, :], preferred_element_type=F32))
    x1_ref[...] = x1
    h2_ref[...] = _rms(x1, ng_ref[...], NORM_EPS).astype(BF16)


def _out_proj(y_ssm, o_nsa, x2, w_out, attn_g, norm2_g):
    t = x2.shape[0]
    tm = TM_PROJ
    row = lambda i: (i, 0)
    const = lambda i: (0, 0)
    tok = pl.BlockSpec((tm, D_MODEL), row)
    vec = pl.BlockSpec((1, D_MODEL), const)
    return pl.pallas_call(
        _outproj_kernel, grid=(t // tm,),
        in_specs=[tok, tok, tok, pl.BlockSpec((SSM_D_INNER + NSA_Q, D_MODEL), const), vec, vec],
        out_specs=(tok, tok),
        out_shape=(jax.ShapeDtypeStruct((t, D_MODEL), F32), jax.ShapeDtypeStruct((t, D_MODEL), BF16)),
        compiler_params=_cparams(("arbitrary",)), name="out_proj",
    )(y_ssm, o_nsa, x2, w_out, attn_g, norm2_g)


def _ffn_kernel(h_ref, halo_ref, x1_ref, wup_ref, cw_ref, cb_ref, wd_ref, fg_ref, out_ref, act_ref,
                *, tiles_per_seq):
    i = pl.program_id(0)
    tm = h_ref.shape[0]
    pad = halo_ref.shape[0]
    tn = TN_FFN
    halo = halo_ref[...]
    halo = jnp.where(i % tiles_per_seq == 0, jnp.zeros_like(halo), halo)
    hc = jnp.concatenate([halo, h_ref[...]], axis=0)

    def branch(c0):
        cols = slice(c0, c0 + tn)
        u = jnp.dot(hc, wup_ref[:, cols], preferred_element_type=F32)
        out = cb_ref[:, cols] + cw_ref[FFN_CONV - 1:FFN_CONV, cols] * u[pad:pad + tm, :]
        for k in range(FFN_CONV - 1):
            shifted = pltpu.roll(u, FFN_CONV - 1 - k, 0)
            out = out + cw_ref[k:k + 1, cols] * shifted[pad:pad + tm, :]
        return out

    for j in range(D_FF // tn):
        act = _silu(branch(j * tn)) * branch(D_FF + j * tn)
        act_ref[:, j * tn:(j + 1) * tn] = act.astype(BF16)
    down = jnp.dot(act_ref[...], wd_ref[...], preferred_element_type=F32)
    out_ref[...] = _rms(x1_ref[...] + down, fg_ref[...], NORM_EPS)


def _ffn(h2, x1, w_up, conv_w, conv_b, w_down, final_g, seq):
    t = h2.shape[0]
    tm = TM_FFN
    pad = BF16_SUBLANES
    tok = pl.BlockSpec((tm, D_MODEL), lambda i: (i, 0))

    def resident(shape):
        return pl.BlockSpec(shape, lambda i: (0, 0), pipeline_mode=pl.Buffered(1))

    in_specs = [
        tok,
        pl.BlockSpec((pad, D_MODEL), lambda i: (jnp.maximum(i * (tm // pad) - 1, 0), 0)),
        tok,
        resident((D_MODEL, 2 * D_FF)),
        resident((FFN_CONV, 2 * D_FF)),
        resident((1, 2 * D_FF)),
        resident((D_FF, D_MODEL)),
        resident((1, D_MODEL)),
    ]
    return pl.pallas_call(
        functools.partial(_ffn_kernel, tiles_per_seq=seq // tm), grid=(t // tm,),
        in_specs=in_specs, out_specs=tok,
        out_shape=jax.ShapeDtypeStruct((t, D_MODEL), F32),
        scratch_shapes=[pltpu.VMEM((tm, D_FF), BF16)],
        compiler_params=_cparams(("arbitrary",)), name="ffn",
    )(h2, h2, x1, w_up, conv_w, conv_b, w_down, final_g)


def _rope_tables(seq):
    half = NSA_HEAD_DIM // 2
    inv_freq = 1.0 / (ROPE_THETA ** (jnp.arange(0, NSA_HEAD_DIM, 2, dtype=F32) / NSA_HEAD_DIM))
    ang = jnp.arange(seq).astype(F32)[:, None] * inv_freq[None, :]
    cos, sin = jnp.cos(ang), jnp.sin(ang)
    reps = LANES // NSA_HEAD_DIM
    cos_t = jnp.tile(jnp.concatenate([cos, cos], axis=1), (1, reps))
    sin_t = jnp.tile(jnp.concatenate([-sin, sin], axis=1), (1, reps))
    return cos_t, sin_t


def _split_w_in(w):
    o_z, o_xbc, o_dt, o_q, o_kv, o_g = np.cumsum([0, SSM_D_INNER, SSM_XBC, SSM_HEADS, NSA_Q, 6 * NSA_KV]).tolist()
    per = 3 * NSA_REP
    tail_cols = [w[:, o_dt:o_q]]
    for g in range(NSA_KV_HEADS):
        tail_cols += [w[:, o_g + g * per:o_g + (g + 1) * per], jnp.zeros((w.shape[0], GATE_SLOT - per), w.dtype)]
    tail_cols.append(jnp.zeros((w.shape[0], TAIL - GATE_OFF - NSA_KV_HEADS * GATE_SLOT), w.dtype))
    return (w[:, o_z:o_dt].astype(BF16), w[:, o_q:o_g].astype(BF16),
            jnp.concatenate(tail_cols, axis=1).astype(BF16))


def kernel(x, norm1_g, w_in, ssm_conv_w, ssm_conv_b, ssm_dt_bias, ssm_a_log, ssm_d, ssm_norm_g, cmp_k_pos, cmp_k_w1, cmp_k_b1, cmp_k_w2, cmp_v_pos, cmp_v_w1, cmp_v_b1, cmp_v_w2, attn_norm_g, w_out, norm2_g, ffn_w_up, ffn_conv_w, ffn_conv_b, ffn_w_down, final_norm_g):
    batch, seq, d = x.shape
    assert w_in.shape[0] == 1, "single-layer problem"
    l = 0
    cos_t, sin_t = _rope_tables(seq)
    x2 = x.reshape(batch * seq, d)
    zs, xbc, qt, kc_r, vc_r, ks, vst, kw, vwt, tail, tailt = _in_proj(
        x2, norm1_g[l].reshape(1, d), _split_w_in(w_in.reshape(d, -1)), cos_t, sin_t, seq)
    y_ssm = _ssd(xbc, zs, tail, tailt, ssm_conv_w[l], ssm_conv_b[l].reshape(1, -1), ssm_dt_bias[l],
                 ssm_a_log[l], ssm_d[l], ssm_norm_g[l], batch, seq)
    kc, vct = _compress(kc_r, vc_r,
                        (cmp_k_pos[l], cmp_k_w1[l], cmp_k_b1[l], cmp_k_w2[l]),
                        (cmp_v_pos[l], cmp_v_w1[l], cmp_v_b1[l], cmp_v_w2[l]), batch, seq)
    o_nsa = _nsa(qt, kc, vct, ks, vst, kw, vwt, tailt, batch, seq)
    x1, h2 = _out_proj(y_ssm, o_nsa, x2, w_out.reshape(-1, d).astype(BF16), attn_norm_g[l].reshape(1, d),
                       norm2_g[l].reshape(1, d))
    out = _ffn(h2, x1, ffn_w_up.reshape(d, -1).astype(BF16), ffn_conv_w[l], ffn_conv_b[l].reshape(1, -1),
               ffn_w_down.reshape(-1, d).astype(BF16), final_norm_g.reshape(1, d), seq)
    return out.reshape(batch, seq, d)
```

```python
import functools
import math

import numpy as np
import jax
import jax.numpy as jnp
from jax import lax
from jax.experimental import pallas as pl
from jax.experimental.pallas import tpu as pltpu

F32 = jnp.float32
BF16 = jnp.bfloat16

D_MODEL = 1024
SSM_D_INNER = 1024
SSM_HEAD_DIM = 64
SSM_HEADS = 16
SSM_GROUPS = 2
SSM_STATE = 128
SSM_CONV = 4
SSM_CHUNK = 128
SSM_XBC = SSM_D_INNER + 2 * SSM_GROUPS * SSM_STATE
SSM_NORM_EPS = 1e-5
NSA_HEADS = 16
NSA_KV_HEADS = 4
NSA_REP = NSA_HEADS // NSA_KV_HEADS
NSA_HEAD_DIM = 64
NSA_Q = NSA_HEADS * NSA_HEAD_DIM
NSA_KV = NSA_KV_HEADS * NSA_HEAD_DIM
CMP_BLOCK = 32
CMP_STRIDE = 16
SLC_BLOCK = 64
SLC_TOP_N = 16
N_LOCAL_BLOCKS = 2
FORCED_SCORE = 1e4
WINDOW = 512
ROPE_THETA = 10000.0
D_FF = 2816
FFN_CONV = 3
NORM_EPS = 1e-6
NEG_BIG = -1e30

LANES = 128
BF16_SUBLANES = 16
TAIL = LANES
GATE_OFF = SSM_HEADS
GATE_SLOT = 16

TM_PROJ = 256
TQ = 256
TK = 256
NSA_GROUPS_PER_STEP = 4
ONES_ROWS = 16
KSEL = LANES
BIAS_ROWS = 16
MT_NONE, MT_FAR, MT_ALL, MT_CAUSAL = 0, TK, 2 * TK, 3 * TK
TM_FFN = 512
TN_FFN = 256
VMEM_LIMIT = 56 * 1024 * 1024

NT_DIMS = (((1,), (1,)), ((), ()))


def _cparams(sem):
    return pltpu.CompilerParams(dimension_semantics=sem, vmem_limit_bytes=VMEM_LIMIT)


def _rms(x, g, eps):
    return x * lax.rsqrt(jnp.mean(x * x, axis=-1, keepdims=True) + eps) * g


def _silu(x):
    return x * jax.nn.sigmoid(x)


def _inproj_kernel(x_ref, g_ref, wa_ref, wb_ref, wt_ref, cos_ref, sin_ref,
                   zs_ref, xbc_ref, qt_ref, kc_ref, vc_ref, ks_ref, vst_ref, kw_ref, vwt_ref,
                   tail_ref, tailt_ref, kbuf_ref, vbuf_ref, *, tiles_per_seq):
    tm = x_ref.shape[0]
    seq_tile = pl.program_id(0) % tiles_per_seq
    h = _rms(x_ref[...], g_ref[...], NORM_EPS).astype(BF16)

    def mm(w_ref, lo, hi):
        return jnp.dot(h, w_ref[:, lo:hi], preferred_element_type=F32)

    tail = mm(wt_ref, 0, TAIL)
    tail_ref[...] = tail
    tailt_ref[...] = tail.T

    cos = cos_ref[...]
    sin = sin_ref[...]
    lane = lax.broadcasted_iota(jnp.int32, (tm, LANES), 1)
    first_half = (lane % NSA_HEAD_DIM) < (NSA_HEAD_DIM // 2)

    def rope(xc):
        partner = jnp.where(first_half, pltpu.roll(xc, LANES - 32, 1), pltpu.roll(xc, 32, 1))
        return xc * cos + partner * sin

    q = mm(wb_ref, 0, NSA_Q)
    scale = NSA_HEAD_DIM ** -0.5 * math.log2(math.e)
    for c in range(NSA_Q // LANES):
        rt = (rope(q[:, c * LANES:(c + 1) * LANES]) * scale).T.astype(BF16)
        qt_ref[2 * c] = rt[:NSA_HEAD_DIM]
        qt_ref[2 * c + 1] = rt[NSA_HEAD_DIM:]

    kv = mm(wb_ref, NSA_Q, NSA_Q + 6 * NSA_KV)

    def seg(i):
        return kv[:, i * NSA_KV:(i + 1) * NSA_KV]

    def rope_seg(x):
        return jnp.concatenate([rope(x[:, :LANES]), rope(x[:, LANES:])], axis=1)

    def store_heads(ref, x):
        for g in range(NSA_KV_HEADS):
            ref[g] = x[:, g * NSA_HEAD_DIM:(g + 1) * NSA_HEAD_DIM].astype(BF16)

    def store_heads_t(ref, x):
        xt = x.T.astype(BF16)
        for g in range(NSA_KV_HEADS):
            ref[g] = xt[g * NSA_HEAD_DIM:(g + 1) * NSA_HEAD_DIM]

    kc_tile = rope_seg(seg(0))
    vc_tile = seg(1)
    for c in range(NSA_KV // LANES):
        kbuf_ref[c] = kc_tile[:, c * LANES:(c + 1) * LANES]
        vbuf_ref[c] = vc_tile[:, c * LANES:(c + 1) * LANES]
    gw = CMP_STRIDE * NSA_HEAD_DIM
    for l in range(CMP_STRIDE):
        for c in range(NSA_KV // LANES):
            k_rows = kbuf_ref[c, pl.ds(l, tm // CMP_STRIDE, stride=CMP_STRIDE), :]
            v_rows = vbuf_ref[c, pl.ds(l, tm // CMP_STRIDE, stride=CMP_STRIDE), :]
            for half in range(LANES // NSA_HEAD_DIM):
                g = c * (LANES // NSA_HEAD_DIM) + half
                src = slice(half * NSA_HEAD_DIM, (half + 1) * NSA_HEAD_DIM)
                dst = slice(g * gw + l * NSA_HEAD_DIM, g * gw + (l + 1) * NSA_HEAD_DIM)
                kc_ref[:, dst] = k_rows[:, src]
                vc_ref[:, dst] = v_rows[:, src]
    pos = seq_tile * tm + lax.broadcasted_iota(jnp.int32, (tm, NSA_HEAD_DIM), 0)
    block_in_chunk = (pos // SLC_BLOCK) % (TK // SLC_BLOCK)
    onehot = jnp.where(lax.broadcasted_iota(jnp.int32, (tm, NSA_HEAD_DIM), 1) == block_in_chunk, 1.0, 0.0)
    ks = rope_seg(seg(2))
    for g in range(NSA_KV_HEADS):
        kg = ks[:, g * NSA_HEAD_DIM:(g + 1) * NSA_HEAD_DIM]
        ks_ref[g] = jnp.concatenate([kg, onehot], axis=1).astype(BF16)
    store_heads_t(vst_ref, seg(3))
    store_heads(kw_ref, rope_seg(seg(4)))
    store_heads_t(vwt_ref, seg(5))

    zs_ref[...] = _silu(mm(wa_ref, 0, SSM_D_INNER))
    xbc_ref[...] = mm(wa_ref, SSM_D_INNER, SSM_D_INNER + SSM_XBC)


def _in_proj(x2, norm_g, w_parts, cos_t, sin_t, seq):
    t = x2.shape[0]
    tm = TM_PROJ
    nseq = seq // tm
    row = lambda i: (i, 0)
    const = lambda i: (0, 0)
    heads = lambda i: (0, i, 0)
    heads_t = lambda i: (0, 0, i)
    out_shape = (
        jax.ShapeDtypeStruct((t, SSM_D_INNER), F32),
        jax.ShapeDtypeStruct((t, SSM_XBC), F32),
        jax.ShapeDtypeStruct((NSA_HEADS, NSA_HEAD_DIM, t), BF16),
        jax.ShapeDtypeStruct((t // CMP_STRIDE, CMP_STRIDE * NSA_KV), F32),
        jax.ShapeDtypeStruct((t // CMP_STRIDE, CMP_STRIDE * NSA_KV), F32),
        jax.ShapeDtypeStruct((NSA_KV_HEADS, t, KSEL), BF16),
        jax.ShapeDtypeStruct((NSA_KV_HEADS, NSA_HEAD_DIM, t), BF16),
        jax.ShapeDtypeStruct((NSA_KV_HEADS, t, NSA_HEAD_DIM), BF16),
        jax.ShapeDtypeStruct((NSA_KV_HEADS, NSA_HEAD_DIM, t), BF16),
        jax.ShapeDtypeStruct((t, TAIL), F32),
        jax.ShapeDtypeStruct((TAIL, t), F32),
    )
    kvh = pl.BlockSpec((NSA_KV_HEADS, tm, NSA_HEAD_DIM), heads)
    kvh_t = pl.BlockSpec((NSA_KV_HEADS, NSA_HEAD_DIM, tm), heads_t)
    out_specs = (
        pl.BlockSpec((tm, SSM_D_INNER), row),
        pl.BlockSpec((tm, SSM_XBC), row),
        pl.BlockSpec((NSA_HEADS, NSA_HEAD_DIM, tm), heads_t),
        pl.BlockSpec((tm // CMP_STRIDE, CMP_STRIDE * NSA_KV), row),
        pl.BlockSpec((tm // CMP_STRIDE, CMP_STRIDE * NSA_KV), row),
        pl.BlockSpec((NSA_KV_HEADS, tm, KSEL), heads), kvh_t, kvh, kvh_t,
        pl.BlockSpec((tm, TAIL), row),
        pl.BlockSpec((TAIL, tm), lambda i: (0, i)),
    )
    in_specs = [
        pl.BlockSpec((tm, D_MODEL), row),
        pl.BlockSpec((1, D_MODEL), const),
        pl.BlockSpec((D_MODEL, SSM_D_INNER + SSM_XBC), const),
        pl.BlockSpec((D_MODEL, NSA_Q + 6 * NSA_KV), const),
        pl.BlockSpec((D_MODEL, TAIL), const),
        pl.BlockSpec((tm, LANES), lambda i: (i % nseq, 0)),
        pl.BlockSpec((tm, LANES), lambda i: (i % nseq, 0)),
    ]
    return pl.pallas_call(
        functools.partial(_inproj_kernel, tiles_per_seq=nseq), grid=(t // tm,), in_specs=in_specs,
        out_specs=out_specs,
        out_shape=out_shape,
        scratch_shapes=[pltpu.VMEM((NSA_KV // LANES, tm, LANES), F32),
                        pltpu.VMEM((NSA_KV // LANES, tm, LANES), F32)],
        compiler_params=_cparams(("arbitrary",)), name="in_proj",
    )(x2, norm_g, *w_parts, cos_t, sin_t)


def _ssd_kernel(xbc_ref, zs_ref, tail_ref, tailt_ref, cw_ref, cb_ref, dtb_ref, dtbt_ref,
                alog_ref, alogt_ref, dskip_ref, ng_ref,
                y_ref, xcat_ref, state_ref, ybuf_ref):
    L, P, N, H, G = SSM_CHUNK, SSM_HEAD_DIM, SSM_STATE, SSM_HEADS, SSM_GROUPS
    HG = H // G
    halo = 8

    @pl.when(pl.program_id(1) == 0)
    def _():
        xcat_ref[0:halo, :] = jnp.zeros((halo, SSM_XBC), F32)
        state_ref[...] = jnp.zeros_like(state_ref)

    xcat_ref[halo:halo + L, :] = xbc_ref[...]
    xcat = xcat_ref[...]
    conv = cb_ref[...] + cw_ref[SSM_CONV - 1:SSM_CONV, :] * xcat[halo:halo + L]
    for k in range(SSM_CONV - 1):
        shifted = pltpu.roll(xcat, SSM_CONV - 1 - k, 0)
        conv = conv + cw_ref[k:k + 1, :] * shifted[halo:halo + L]
    xcat_ref[0:halo, :] = xcat[L:L + halo]
    u = _silu(conv)
    xs = u[:, :SSM_D_INNER]
    bm = u[:, SSM_D_INNER:SSM_D_INNER + G * N]
    cm = u[:, SSM_D_INNER + G * N:]

    def softplus(v):
        return jnp.maximum(v, 0.0) + jnp.log1p(jnp.exp(-jnp.abs(v)))

    dt = softplus(tail_ref[:, 0:H] + dtb_ref[...])
    dtt = softplus(tailt_ref[0:H, :] + dtbt_ref[...])
    da = dt * (-jnp.exp(alog_ref[...]))
    dat = dtt * (-jnp.exp(alogt_ref[...]))
    ri = lax.broadcasted_iota(jnp.int32, (L, L), 0)
    ci = lax.broadcasted_iota(jnp.int32, (L, L), 1)
    tri = ci <= ri
    hi = lax.Precision.HIGHEST
    acs = jnp.dot(tri.astype(F32), da, precision=hi, preferred_element_type=F32)
    acst = jnp.dot(dat, (ri <= ci).astype(F32), precision=hi, preferred_element_type=F32)
    last = acs[L - 1:L, :]
    w_state = dt * jnp.exp(last - acs)
    eacs = jnp.exp(acs)
    cdec = jnp.exp(last)

    hrow = lax.broadcasted_iota(jnp.int32, (H, SSM_D_INNER), 0)
    hcol = lax.broadcasted_iota(jnp.int32, (H, SSM_D_INNER), 1)
    expand = jnp.where(hcol // P == hrow, 1.0, 0.0).astype(BF16)

    def split3(v):
        v_hi = v.astype(BF16)
        r1 = v - v_hi.astype(F32)
        v_mid = r1.astype(BF16)
        v_lo = (r1 - v_mid.astype(F32)).astype(BF16)
        return jnp.concatenate([v_hi, v_mid, v_lo], axis=1)

    small = jnp.concatenate([cdec, dskip_ref[...], jnp.zeros((6, H), F32)], axis=0)
    per_head = jnp.concatenate([dt, w_state, eacs, small], axis=0)
    spread = jnp.dot(split3(per_head), jnp.concatenate([expand] * 3, axis=0), preferred_element_type=F32)
    cdec_e = spread[3 * L:3 * L + 1, :]
    dskip_e = spread[3 * L + 1:3 * L + 2, :]
    xdt = (xs * spread[0:L]).astype(BF16)
    wst = (xs * spread[L:2 * L]).astype(BF16)
    eacs_e = spread[2 * L:3 * L]

    for g in range(G):
        bm_g = bm[:, g * N:(g + 1) * N]
        cm_g = cm[:, g * N:(g + 1) * N].astype(BF16)
        cb = lax.dot_general(cm_g, bm_g.astype(BF16), NT_DIMS, preferred_element_type=F32)
        cols = slice(g * HG * P, (g + 1) * HG * P)
        st = state_ref[:, cols]
        y_off = jnp.dot(cm_g, st.astype(BF16), preferred_element_type=F32) * eacs_e[:, cols]
        ybuf_ref[:, cols] = y_off
        bmt = bm_g.T.astype(BF16)
        state_ref[:, cols] = st * cdec_e[:, cols] + jnp.dot(bmt, wst[:, cols], preferred_element_type=F32)
        for r in range(HG):
            hh = g * HG + r
            diff = acs[:, hh:hh + 1] - acst[hh:hh + 1, :]
            seg = jnp.exp(jnp.where(tri, diff, -jnp.inf))
            lmat = (cb * seg).astype(BF16)
            hc = slice(hh * P, (hh + 1) * P)
            ybuf_ref[:, hc] = ybuf_ref[:, hc] + jnp.dot(lmat, xdt[:, hc], preferred_element_type=F32)

    y = (ybuf_ref[...] + xs * dskip_e) * zs_ref[...]
    gw = SSM_D_INNER // G
    parts = []
    for g in range(G):
        yg = y[:, g * gw:(g + 1) * gw]
        parts.append(yg * lax.rsqrt(jnp.mean(yg * yg, axis=-1, keepdims=True) + SSM_NORM_EPS))
    y_ref[...] = (jnp.concatenate(parts, axis=1) * ng_ref[...]).astype(y_ref.dtype)


def _ssd(xbc, zs, tail, tailt, conv_w, conv_b, dt_bias, a_log, d_skip, norm_g, batch, seq):
    t = xbc.shape[0]
    L = SSM_CHUNK
    nc = seq // L
    row = lambda b, c: (b * nc + c, 0)
    const = lambda b, c: (0, 0)
    H = SSM_HEADS
    in_specs = [
        pl.BlockSpec((L, SSM_XBC), row),
        pl.BlockSpec((L, SSM_D_INNER), row),
        pl.BlockSpec((L, TAIL), row),
        pl.BlockSpec((TAIL, L), lambda b, c: (0, b * nc + c)),
        pl.BlockSpec((SSM_CONV, SSM_XBC), const),
        pl.BlockSpec((1, SSM_XBC), const),
        pl.BlockSpec((1, H), const),
        pl.BlockSpec((H, 1), const),
        pl.BlockSpec((1, H), const),
        pl.BlockSpec((H, 1), const),
        pl.BlockSpec((1, H), const),
        pl.BlockSpec((1, SSM_D_INNER), const),
    ]
    return pl.pallas_call(
        _ssd_kernel, grid=(batch, nc), in_specs=in_specs,
        out_specs=pl.BlockSpec((L, SSM_D_INNER), row),
        out_shape=jax.ShapeDtypeStruct((t, SSM_D_INNER), BF16),
        scratch_shapes=[pltpu.VMEM((L + 8, SSM_XBC), F32),
                        pltpu.VMEM((SSM_STATE, SSM_D_INNER), F32),
                        pltpu.VMEM((L, SSM_D_INNER), F32)],
        compiler_params=_cparams(("arbitrary", "arbitrary")), name="ssd",
    )(xbc, zs, tail, tailt, conv_w, conv_b, dt_bias.reshape(1, H), dt_bias.reshape(H, 1),
      a_log.reshape(1, H), a_log.reshape(H, 1), d_skip.reshape(1, H), norm_g.reshape(1, -1))


def _compress_kernel(k_ref, v_ref, kpos_ref, vpos_ref, kw1_ref, vw1_ref, kb1_ref, vb1_ref,
                     kw2_ref, vw2_ref, kc_ref, vct_ref):
    D = NSA_HEAD_DIM
    gw = CMP_STRIDE * D

    def hidden(t_ref, pos_ref, w1_ref, b1_ref, g):
        t = t_ref[:, g * gw:(g + 1) * gw]
        n = t.shape[0]
        lo = jnp.dot((t + pos_ref[0:1, :]).astype(BF16), w1_ref[...], preferred_element_type=F32)
        hi = jnp.dot((t + pos_ref[1:2, :]).astype(BF16), w1_ref[...], preferred_element_type=F32)
        pre = lo + pltpu.roll(pltpu.roll(hi, n - 1, 0), D, 1)
        return _silu(pre[:, 0:D] + b1_ref[...]).astype(BF16)

    for g in range(NSA_KV_HEADS):
        kc = jnp.dot(hidden(k_ref, kpos_ref, kw1_ref, kb1_ref, g), kw2_ref[...], preferred_element_type=F32)
        kc_ref[g] = kc.astype(BF16)
        vct = lax.dot_general(vw2_ref[...], hidden(v_ref, vpos_ref, vw1_ref, vb1_ref, g), NT_DIMS,
                              preferred_element_type=F32)
        vct_ref[g] = vct.astype(BF16)


def _compress_weights(pos, w1, b1, w2):
    half = CMP_BLOCK // 2 * NSA_HEAD_DIM
    w1cat = jnp.concatenate([w1[:half], w1[half:]], axis=1).astype(BF16)
    return pos.reshape(2, half), w1cat, b1.reshape(1, -1), w2.astype(BF16)


def _compress(k16, v16, kparams, vparams, batch, seq):
    rows = seq // CMP_STRIDE
    width = CMP_STRIDE * NSA_KV
    D = NSA_HEAD_DIM
    gw = CMP_STRIDE * D
    kpos, kw1, kb1, kw2 = _compress_weights(*kparams)
    vpos, vw1, vb1, vw2 = _compress_weights(*vparams)
    vw2 = vw2.T
    c2 = lambda b: (0, 0)
    tok = pl.BlockSpec((rows, width), lambda b: (b, 0))
    in_specs = [tok, tok,
                pl.BlockSpec((2, gw), c2), pl.BlockSpec((2, gw), c2),
                pl.BlockSpec((gw, 2 * D), c2), pl.BlockSpec((gw, 2 * D), c2),
                pl.BlockSpec((1, D), c2), pl.BlockSpec((1, D), c2),
                pl.BlockSpec((D, D), c2), pl.BlockSpec((D, D), c2)]
    out = jax.ShapeDtypeStruct((NSA_KV_HEADS, batch * rows, D), BF16)
    out_t = jax.ShapeDtypeStruct((NSA_KV_HEADS, D, batch * rows), BF16)
    ospec = pl.BlockSpec((NSA_KV_HEADS, rows, D), lambda b: (0, b, 0))
    ospec_t = pl.BlockSpec((NSA_KV_HEADS, D, rows), lambda b: (0, 0, b))
    return pl.pallas_call(
        _compress_kernel, grid=(batch,), in_specs=in_specs, out_specs=(ospec, ospec_t),
        out_shape=(out, out_t), compiler_params=_cparams(("arbitrary",)), name="compress",
    )(k16, v16, kpos, vpos, kw1, vw1, kb1, vb1, kw2, vw2)


def _nsa_kernel(qt_ref, kc_ref, vct_ref, ks_ref, vst_ref, kw_ref, vwt_ref, tailt_ref, ovt_ref, mtab_ref, cmask_ref,
                o_ref, biasq_ref, m_ref, acc_ref, part_ref, sbuf0_ref, sbuf1_ref, mbuf0_ref, mbuf1_ref):
    R, D = NSA_REP, NSA_HEAD_DIM
    nl = R * TQ
    grp = range(kc_ref.shape[0])
    sbuf_refs = (sbuf0_ref, sbuf1_ref)
    mbuf_refs = (mbuf0_ref, mbuf1_ref)
    n_stages = (ks_ref.shape[1] // TK - 1) // 2
    gp = pl.program_id(1)
    qi = pl.program_id(2)
    t0 = qi * TQ
    qt = [jnp.concatenate([qt_ref[gi * R + r] for r in range(R)], axis=1) for gi in grp]
    tpos_row = t0 + lax.broadcasted_iota(jnp.int32, (1, nl), 1) % TQ

    def k_rows(k_ref, gi, start):
        return k_ref[gi, pl.ds(pl.multiple_of(start, TK), TK), :]

    def vt_cols(vt_ref, gi, start, n):
        return vt_ref[gi, :, pl.ds(pl.multiple_of(start, TK), n)]

    def pv(vt, p):
        vt1 = jnp.concatenate([vt, jnp.ones((ONES_ROWS, vt.shape[1]), BF16)], axis=0)
        return jnp.dot(vt1, p, preferred_element_type=F32)

    def mask_tile(off):
        return jnp.concatenate([mtab_ref[pl.ds(pl.multiple_of(off, TK), TK), :]] * R, axis=1)

    def col_max(s):
        return jnp.max(s, axis=0, keepdims=True)

    def online(state, s, m_chunk, vt):
        m_new = m_chunk if state is None else jnp.maximum(state[0], m_chunk)
        contrib = pv(vt, jnp.exp2(s - m_new).astype(BF16))
        if state is None:
            return m_new, contrib
        return m_new, jnp.exp2(state[0] - m_new) * state[1] + contrib

    causal = mask_tile(MT_CAUSAL)
    far0 = jnp.maximum(t0 - 2 * TK, 0)
    mid0 = jnp.maximum(t0 - TK, 0)
    window = []
    for gi in grp:
        s_dia = jnp.dot(k_rows(kw_ref, gi, t0), qt[gi], preferred_element_type=F32) + causal
        m_dia = col_max(s_dia)
        s_mid = (jnp.dot(k_rows(kw_ref, gi, mid0), qt[gi], preferred_element_type=F32)
                 + mask_tile(jnp.where(qi >= 1, MT_ALL, MT_NONE)))
        m_mid = col_max(s_mid)
        s_far = (jnp.dot(k_rows(kw_ref, gi, far0), qt[gi], preferred_element_type=F32)
                 + mask_tile(jnp.where(qi >= 2, MT_FAR, MT_NONE)))
        m_far = col_max(s_far)
        window.append(((s_dia, m_dia, t0), (s_mid, m_mid, mid0), (s_far, m_far, far0)))

    ncr = kc_ref.shape[1]
    n_slc = ovt_ref.shape[0]
    cmask = cmask_ref[pl.ds(pl.multiple_of(qi * ncr, ncr), ncr), :]
    cmask = jnp.concatenate([cmask] * R, axis=1)
    sees_any = tpos_row >= CMP_BLOCK - 1
    jj = lax.broadcasted_iota(jnp.int32, (n_slc, TQ), 0)
    tt = t0 + lax.broadcasted_iota(jnp.int32, (n_slc, TQ), 1)
    lag = tt // SLC_BLOCK - jj
    forced = (jj == 0) | ((lag >= 0) & (lag < N_LOCAL_BLOCKS))
    valid = jj * SLC_BLOCK <= tt
    rows8 = 8
    j8 = lax.broadcasted_iota(jnp.int32, (rows8, TQ), 0)
    per = TK // SLC_BLOCK
    zrows = jnp.zeros((BIAS_ROWS - per, TQ), F32)
    o_c, o_w = [], []
    for gi in grp:
        s_m = jnp.dot(kc_ref[gi], qt[gi], preferred_element_type=F32) + cmask
        mx = jnp.max(s_m, axis=0, keepdims=True)
        p = jnp.exp2(s_m - mx)
        den = jnp.sum(p, axis=0, keepdims=True)
        pc = p * jnp.where(sees_any, 1.0 / den, 0.0)
        o_c.append(jnp.dot(vct_ref[gi], pc.astype(BF16), preferred_element_type=F32))

        psum = pc[:, 0:TQ]
        for r in range(1, R):
            psum = psum + pc[:, r * TQ:(r + 1) * TQ]
        imp_t = jnp.dot(ovt_ref[...], psum, precision=lax.Precision.HIGHEST,
                        preferred_element_type=F32)
        score = jnp.where(forced, FORCED_SCORE, jnp.where(valid, imp_t, -1.0))
        groups = [score[a:a + rows8] for a in range(0, n_slc, rows8)]
        ranks = [jnp.zeros((rows8, TQ), F32) for _ in groups]
        window_steps = dict(zip((0, n_slc // 3, 2 * n_slc // 3), window[gi]))
        win = None
        for j2 in range(n_slc):
            if j2 in window_steps:
                s_w, m_w, start_w = window_steps[j2]
                win = online(win, s_w, m_w, vt_cols(vwt_ref, gi, start_w, TK))
            sj = score[j2:j2 + 1, :]
            for a, sg in enumerate(groups):
                lo = a * rows8
                if lo + rows8 - 1 < j2:
                    beats = sj > sg
                elif lo > j2:
                    beats = sj >= sg
                else:
                    beats = (sj > sg) | ((sj == sg) & (j8 + lo > j2))
                ranks[a] = ranks[a] + jnp.where(beats, 1.0, 0.0)
        rank = jnp.concatenate(ranks, axis=0)
        selected = (rank < float(min(SLC_TOP_N, n_slc))) & (score >= 0.0)
        bias_t = jnp.where(selected, 0.0, NEG_BIG)
        for kb in range(n_slc // per):
            blk = jnp.concatenate([bias_t[kb * per:(kb + 1) * per], zrows], axis=0)
            biasq_ref[gi, kb * BIAS_ROWS:(kb + 1) * BIAS_ROWS, :] = (
                jnp.concatenate([blk] * R, axis=1).astype(BF16))
        accw = win[1]
        o_w.append(accw[0:D] / accw[D:D + 1])
    qpad = jnp.zeros((KSEL - D - BIAS_ROWS, nl), BF16)

    def q_sel(gi, kb):
        rows = biasq_ref[gi, pl.ds(pl.multiple_of(kb * BIAS_ROWS, BIAS_ROWS), BIAS_ROWS), :]
        return jnp.concatenate([qt[gi], rows, qpad], axis=0)

    def slc_scores(gi, kb):
        return jnp.dot(k_rows(ks_ref, gi, kb * TK), q_sel(gi, kb), preferred_element_type=F32)

    def fill(gi, slot, chunks):
        for c, kb in enumerate(chunks):
            s = slc_scores(gi, kb)
            sbuf_refs[slot][gi, c * TK:(c + 1) * TK, :] = s
            mbuf_refs[slot][gi, c:c + 1, :] = col_max(s)

    npairs = qi // 2
    diag = []
    for gi in grp:
        s_sel = slc_scores(gi, qi) + causal
        diag.append((s_sel, col_max(s_sel)))
        fill(gi, 0, (0, 1))

    def gates(gi):
        gate_row = pl.multiple_of(GATE_OFF + GATE_SLOT * (gp * len(grp) + gi), 8)
        sig = jax.nn.sigmoid(tailt_ref[pl.ds(gate_row, GATE_SLOT), :])
        return [jnp.concatenate([sig[3 * r + c:3 * r + c + 1, :] for r in range(R)], axis=1)
                for c in range(3)]

    for gi in grp:
        gate = gates(gi)
        part_ref[gi] = gate[0] * o_c[gi] + gate[2] * o_w[gi]

    for gi in grp:
        m_sel, acc_sel = online(None, diag[gi][0], diag[gi][1], vt_cols(vst_ref, gi, t0, TK))
        m_ref[gi] = m_sel
        acc_ref[gi] = acc_sel

    def absorb(gi, slot, first, n):
        state = (m_ref[gi], acc_ref[gi])
        for c in range(n):
            state = online(state, sbuf_refs[slot][gi, c * TK:(c + 1) * TK, :], mbuf_refs[slot][gi, c:c + 1, :],
                           vt_cols(vst_ref, gi, (first + c) * TK, TK))
        m_ref[gi] = state[0]
        acc_ref[gi] = state[1]

    for k in range(n_stages):
        @pl.when(npairs > k)
        def _(k=k):
            for gi in grp:
                fill(gi, (k + 1) % 2, (jnp.minimum(2 * k + 2, qi - 1), jnp.minimum(2 * k + 3, qi - 1)))
            for gi in grp:
                absorb(gi, k % 2, 2 * k, 2)

    for parity in range(2):
        @pl.when((qi % 2 == 1) & (npairs % 2 == parity))
        def _(parity=parity):
            for gi in grp:
                absorb(gi, parity, qi - 1, 1)

    for gi in grp:
        acc = acc_ref[gi]
        ot = part_ref[gi] + gates(gi)[1] * (acc[0:D] / acc[D:D + 1])
        stacked = jnp.concatenate([ot[:, r * TQ:(r + 1) * TQ] for r in range(R)], axis=0)
        o_ref[:, gi * R * D:(gi + 1) * R * D] = stacked.T.astype(o_ref.dtype)


def _overlap_t(seq):
    n_cmp = (seq - CMP_BLOCK) // CMP_STRIDE + 1
    n_slc = seq // SLC_BLOCK
    cs = np.arange(n_cmp) * CMP_STRIDE
    ss = np.arange(n_slc) * SLC_BLOCK
    overlap = np.clip(np.minimum(cs[:, None] + CMP_BLOCK, ss[None, :] + SLC_BLOCK)
                      - np.maximum(cs[:, None], ss[None, :]), 0, None) / CMP_BLOCK
    ovt = np.zeros((n_slc, n_cmp + 1), np.float32)
    ovt[:, :n_cmp] = overlap.T
    return jnp.asarray(ovt)


def _mask_tiles():
    ki = np.arange(TK)[:, None]
    qi = np.arange(TQ)[None, :]
    neg = np.float32(NEG_BIG)
    none = np.full((TK, TQ), neg, np.float32)
    far = np.where(ki > qi, np.float32(0), neg)
    full = np.zeros((TK, TQ), np.float32)
    causal = np.where(ki <= qi, np.float32(0), neg)
    return jnp.asarray(np.concatenate([none, far, full, causal], axis=0))


def _cmp_masks(seq):
    ncr = seq // CMP_STRIDE
    n_cmp = (seq - CMP_BLOCK) // CMP_STRIDE + 1
    n = np.arange(ncr)[None, :, None]
    t = (np.arange(seq // TQ)[:, None, None] * TQ + np.arange(TQ)[None, None, :])
    visible = (n * CMP_STRIDE + CMP_BLOCK - 1 <= t) & (n < n_cmp)
    return jnp.asarray(np.where(visible, np.float32(0), np.float32(NEG_BIG)).reshape(-1, TQ))


def _nsa(qt, kc, vct, ks, vst, kw, vwt, tailt, batch, seq):
    t = batch * seq
    G, R, D = NSA_KV_HEADS, NSA_REP, NSA_HEAD_DIM
    nq = seq // TQ
    ncr = seq // CMP_STRIDE
    n_slc = seq // SLC_BLOCK
    assert TQ == TK and WINDOW == 2 * TK, "window branch visits exactly three key chunks"
    assert TK // SLC_BLOCK <= BIAS_ROWS and D + BIAS_ROWS <= KSEL
    P = NSA_GROUPS_PER_STEP
    const = lambda b, g, i: (0, 0)
    vtspec = pl.BlockSpec((P, D, seq), lambda b, g, i: (g, 0, b))
    in_specs = [
        pl.BlockSpec((P * R, D, TQ), lambda b, g, i: (g, 0, b * nq + i)),
        pl.BlockSpec((P, ncr, D), lambda b, g, i: (g, b, 0)),
        pl.BlockSpec((P, D, ncr), lambda b, g, i: (g, 0, b)),
        pl.BlockSpec((P, seq, KSEL), lambda b, g, i: (g, b, 0)),
        vtspec,
        pl.BlockSpec((P, seq, D), lambda b, g, i: (g, b, 0)),
        vtspec,
        pl.BlockSpec((TAIL, TQ), lambda b, g, i: (0, b * nq + i)),
        pl.BlockSpec((n_slc, ncr), const),
        pl.BlockSpec((4 * TK, TQ), const),
        pl.BlockSpec((nq * ncr, TQ), const),
    ]
    return pl.pallas_call(
        _nsa_kernel, grid=(batch, G // P, nq), in_specs=in_specs,
        out_specs=pl.BlockSpec((TQ, P * R * D), lambda b, g, i: (b * nq + i, g)),
        out_shape=jax.ShapeDtypeStruct((t, NSA_Q), BF16),
        scratch_shapes=[pltpu.VMEM((P, seq // TK * BIAS_ROWS, R * TQ), BF16),
                        pltpu.VMEM((P, 1, R * TQ), F32),
                        pltpu.VMEM((P, D + ONES_ROWS, R * TQ), F32),
                        pltpu.VMEM((P, D, R * TQ), F32),
                        pltpu.VMEM((P, 2 * TK, R * TQ), F32),
                        pltpu.VMEM((P, 2 * TK, R * TQ), F32),
                        pltpu.VMEM((P, 8, R * TQ), F32),
                        pltpu.VMEM((P, 8, R * TQ), F32)],
        compiler_params=_cparams(("arbitrary", "arbitrary", "arbitrary")), name="nsa",
    )(qt, kc, vct, ks, vst, kw, vwt, tailt, _overlap_t(seq), _mask_tiles(), _cmp_masks(seq))


def _outproj_kernel(y_ref, o_ref, x_ref, w_ref, ag_ref, ng_ref, x1_ref, h2_ref):
    yn = _rms(o_ref[...].astype(F32), ag_ref[...], NORM_EPS).astype(BF16)
    x1 = (x_ref[...]
          + jnp.dot(y_ref[...], w_ref[0:SSM_D_INNER, :], preferred_element_type=F32)
          + jnp.dot(yn, w_ref[SSM_D_INNER:, :], preferred_element_type=F32))
    x1_ref[...] = x1
    h2_ref[...] = _rms(x1, ng_ref[...], NORM_EPS).astype(BF16)


def _out_proj(y_ssm, o_nsa, x2, w_out, attn_g, norm2_g):
    t = x2.shape[0]
    tm = TM_PROJ
    row = lambda i: (i, 0)
    const = lambda i: (0, 0)
    tok = pl.BlockSpec((tm, D_MODEL), row)
    vec = pl.BlockSpec((1, D_MODEL), const)
    return pl.pallas_call(
        _outproj_kernel, grid=(t // tm,),
        in_specs=[tok, tok, tok, pl.BlockSpec((SSM_D_INNER + NSA_Q, D_MODEL), const), vec, vec],
        out_specs=(tok, tok),
        out_shape=(jax.ShapeDtypeStruct((t, D_MODEL), F32), jax.ShapeDtypeStruct((t, D_MODEL), BF16)),
        compiler_params=_cparams(("arbitrary",)), name="out_proj",
    )(y_ssm, o_nsa, x2, w_out, attn_g, norm2_g)


def _ffn_kernel(h_ref, halo_ref, x1_ref, wup_ref, cw_ref, cb_ref, wd_ref, fg_ref, out_ref, act_ref,
                *, tiles_per_seq):
    i = pl.program_id(0)
    tm = h_ref.shape[0]
    pad = halo_ref.shape[0]
    tn = TN_FFN
    halo = halo_ref[...]
    halo = jnp.where(i % tiles_per_seq == 0, jnp.zeros_like(halo), halo)
    hc = jnp.concatenate([halo, h_ref[...]], axis=0)

    def branch(c0):
        cols = slice(c0, c0 + tn)
        u = jnp.dot(hc, wup_ref[:, cols], preferred_element_type=F32)
        out = cb_ref[:, cols] + cw_ref[FFN_CONV - 1:FFN_CONV, cols] * u[pad:pad + tm, :]
        for k in range(FFN_CONV - 1):
            shifted = pltpu.roll(u, FFN_CONV - 1 - k, 0)
            out = out + cw_ref[k:k + 1, cols] * shifted[pad:pad + tm, :]
        return out

    for j in range(D_FF // tn):
        act = _silu(branch(j * tn)) * branch(D_FF + j * tn)
        act_ref[:, j * tn:(j + 1) * tn] = act.astype(BF16)
    down = jnp.dot(act_ref[...], wd_ref[...], preferred_element_type=F32)
    out_ref[...] = _rms(x1_ref[...] + down, fg_ref[...], NORM_EPS)


def _ffn(h2, x1, w_up, conv_w, conv_b, w_down, final_g, seq):
    t = h2.shape[0]
    tm = TM_FFN
    pad = BF16_SUBLANES
    tok = pl.BlockSpec((tm, D_MODEL), lambda i: (i, 0))

    def resident(shape):
        return pl.BlockSpec(shape, lambda i: (0, 0), pipeline_mode=pl.Buffered(1))

    in_specs = [
        tok,
        pl.BlockSpec((pad, D_MODEL), lambda i: (jnp.maximum(i * (tm // pad) - 1, 0), 0)),
        tok,
        resident((D_MODEL, 2 * D_FF)),
        resident((FFN_CONV, 2 * D_FF)),
        resident((1, 2 * D_FF)),
        resident((D_FF, D_MODEL)),
        resident((1, D_MODEL)),
    ]
    return pl.pallas_call(
        functools.partial(_ffn_kernel, tiles_per_seq=seq // tm), grid=(t // tm,),
        in_specs=in_specs, out_specs=tok,
        out_shape=jax.ShapeDtypeStruct((t, D_MODEL), F32),
        scratch_shapes=[pltpu.VMEM((tm, D_FF), BF16)],
        compiler_params=_cparams(("arbitrary",)), name="ffn",
    )(h2, h2, x1, w_up, conv_w, conv_b, w_down, final_g)


def _rope_tables(seq):
    half = NSA_HEAD_DIM // 2
    inv_freq = 1.0 / (ROPE_THETA ** (jnp.arange(0, NSA_HEAD_DIM, 2, dtype=F32) / NSA_HEAD_DIM))
    ang = jnp.arange(seq).astype(F32)[:, None] * inv_freq[None, :]
    cos, sin = jnp.cos(ang), jnp.sin(ang)
    reps = LANES // NSA_HEAD_DIM
    cos_t = jnp.tile(jnp.concatenate([cos, cos], axis=1), (1, reps))
    sin_t = jnp.tile(jnp.concatenate([-sin, sin], axis=1), (1, reps))
    return cos_t, sin_t


def _split_w_in(w):
    o_z, o_xbc, o_dt, o_q, o_kv, o_g = np.cumsum([0, SSM_D_INNER, SSM_XBC, SSM_HEADS, NSA_Q, 6 * NSA_KV]).tolist()
    per = 3 * NSA_REP
    tail_cols = [w[:, o_dt:o_q]]
    for g in range(NSA_KV_HEADS):
        tail_cols += [w[:, o_g + g * per:o_g + (g + 1) * per], jnp.zeros((w.shape[0], GATE_SLOT - per), w.dtype)]
    tail_cols.append(jnp.zeros((w.shape[0], TAIL - GATE_OFF - NSA_KV_HEADS * GATE_SLOT), w.dtype))
    return (w[:, o_z:o_dt].astype(BF16), w[:, o_q:o_g].astype(BF16),
            jnp.concatenate(tail_cols, axis=1).astype(BF16))


def kernel(x, norm1_g, w_in, ssm_conv_w, ssm_conv_b, ssm_dt_bias, ssm_a_log, ssm_d, ssm_norm_g, cmp_k_pos, cmp_k_w1, cmp_k_b1, cmp_k_w2, cmp_v_pos, cmp_v_w1, cmp_v_b1, cmp_v_w2, attn_norm_g, w_out, norm2_g, ffn_w_up, ffn_conv_w, ffn_conv_b, ffn_w_down, final_norm_g):
    batch, seq, d = x.shape
    assert w_in.shape[0] == 1, "single-layer problem"
    l = 0
    cos_t, sin_t = _rope_tables(seq)
    x2 = x.reshape(batch * seq, d)
    zs, xbc, qt, kc_r, vc_r, ks, vst, kw, vwt, tail, tailt = _in_proj(
        x2, norm1_g[l].reshape(1, d), _split_w_in(w_in.reshape(d, -1)), cos_t, sin_t, seq)
    y_ssm = _ssd(xbc, zs, tail, tailt, ssm_conv_w[l], ssm_conv_b[l].reshape(1, -1), ssm_dt_bias[l],
                 ssm_a_log[l], ssm_d[l], ssm_norm_g[l], batch, seq)
    kc, vct = _compress(kc_r, vc_r,
                        (cmp_k_pos[l], cmp_k_w1[l], cmp_k_b1[l], cmp_k_w2[l]),
                        (cmp_v_pos[l], cmp_v_w1[l], cmp_v_b1[l], cmp_v_w2[l]), batch, seq)
    o_nsa = _nsa(qt, kc, vct, ks, vst, kw, vwt, tailt, batch, seq)
    x1, h2 = _out_proj(y_ssm, o_nsa, x2, w_out.reshape(-1, d).astype(BF16), attn_norm_g[l].reshape(1, d),
                       norm2_g[l].reshape(1, d))
    out = _ffn(h2, x1, ffn_w_up.reshape(d, -1).astype(BF16), ffn_conv_w[l], ffn_conv_b[l].reshape(1, -1),
               ffn_w_down.reshape(-1, d).astype(BF16), final_norm_g.reshape(1, d), seq)
    return out.reshape(batch, seq, d)
```

```python
import functools
import math

import numpy as np
import jax
import jax.numpy as jnp
from jax import lax
from jax.experimental import pallas as pl
from jax.experimental.pallas import tpu as pltpu

F32 = jnp.float32
BF16 = jnp.bfloat16

D_MODEL = 1024
SSM_D_INNER = 1024
SSM_HEAD_DIM = 64
SSM_HEADS = 16
SSM_GROUPS = 2
SSM_STATE = 128
SSM_CONV = 4
SSM_CHUNK = 128
SSM_XBC = SSM_D_INNER + 2 * SSM_GROUPS * SSM_STATE
SSM_NORM_EPS = 1e-5
NSA_HEADS = 16
NSA_KV_HEADS = 4
NSA_REP = NSA_HEADS // NSA_KV_HEADS
NSA_HEAD_DIM = 64
NSA_Q = NSA_HEADS * NSA_HEAD_DIM
NSA_KV = NSA_KV_HEADS * NSA_HEAD_DIM
CMP_BLOCK = 32
CMP_STRIDE = 16
SLC_BLOCK = 64
SLC_TOP_N = 16
N_LOCAL_BLOCKS = 2
FORCED_SCORE = 1e4
WINDOW = 512
ROPE_THETA = 10000.0
D_FF = 2816
FFN_CONV = 3
NORM_EPS = 1e-6
NEG_BIG = -1e30

LANES = 128
BF16_SUBLANES = 16
TAIL = LANES
GATE_OFF = SSM_HEADS
GATE_SLOT = 16

TM_PROJ = 512
TQ = 256
TK = 256
SSD_SEQS_PER_STEP = 4
NSA_GROUPS_PER_STEP = 4
ONES_ROWS = 16
KSEL = LANES
BIAS_ROWS = 16
MT_NONE, MT_FAR, MT_ALL, MT_CAUSAL = 0, TK, 2 * TK, 3 * TK
TM_FFN = 512
TN_FFN = 256
VMEM_LIMIT = 56 * 1024 * 1024

NT_DIMS = (((1,), (1,)), ((), ()))


def _cparams(sem):
    return pltpu.CompilerParams(dimension_semantics=sem, vmem_limit_bytes=VMEM_LIMIT)


def _rms(x, g, eps):
    return x * lax.rsqrt(jnp.mean(x * x, axis=-1, keepdims=True) + eps) * g


def _silu(x):
    return x * jax.nn.sigmoid(x)


def _inproj_kernel(x_ref, g_ref, wa_ref, wb_ref, wt_ref, cos_ref, sin_ref,
                   zs_ref, xbc_ref, qt_ref, kc_ref, vc_ref, ks_ref, vst_ref, kw_ref, vwt_ref,
                   tail_ref, tailt_ref, kbuf_ref, vbuf_ref, *, tiles_per_seq):
    tm = x_ref.shape[0]
    seq_tile = pl.program_id(0) % tiles_per_seq
    h = _rms(x_ref[...], g_ref[...], NORM_EPS).astype(BF16)

    def mm(w_ref, lo, hi):
        return jnp.dot(h, w_ref[:, lo:hi], preferred_element_type=F32)

    tail = mm(wt_ref, 0, TAIL)
    tail_ref[...] = tail
    tailt_ref[...] = tail.T

    cos = cos_ref[...]
    sin = sin_ref[...]
    lane = lax.broadcasted_iota(jnp.int32, (tm, LANES), 1)
    first_half = (lane % NSA_HEAD_DIM) < (NSA_HEAD_DIM // 2)

    def rope(xc):
        partner = jnp.where(first_half, pltpu.roll(xc, LANES - 32, 1), pltpu.roll(xc, 32, 1))
        return xc * cos + partner * sin

    q = mm(wb_ref, 0, NSA_Q)
    scale = NSA_HEAD_DIM ** -0.5 * math.log2(math.e)
    for c in range(NSA_Q // LANES):
        rt = (rope(q[:, c * LANES:(c + 1) * LANES]) * scale).T.astype(BF16)
        qt_ref[2 * c] = rt[:NSA_HEAD_DIM]
        qt_ref[2 * c + 1] = rt[NSA_HEAD_DIM:]

    kv = mm(wb_ref, NSA_Q, NSA_Q + 6 * NSA_KV)

    def seg(i):
        return kv[:, i * NSA_KV:(i + 1) * NSA_KV]

    def rope_seg(x):
        return jnp.concatenate([rope(x[:, :LANES]), rope(x[:, LANES:])], axis=1)

    def store_heads(ref, x):
        for g in range(NSA_KV_HEADS):
            ref[g] = x[:, g * NSA_HEAD_DIM:(g + 1) * NSA_HEAD_DIM].astype(BF16)

    def store_heads_t(ref, x):
        xt = x.T.astype(BF16)
        for g in range(NSA_KV_HEADS):
            ref[g] = xt[g * NSA_HEAD_DIM:(g + 1) * NSA_HEAD_DIM]

    kc_tile = rope_seg(seg(0))
    vc_tile = seg(1)
    for c in range(NSA_KV // LANES):
        kbuf_ref[c] = kc_tile[:, c * LANES:(c + 1) * LANES]
        vbuf_ref[c] = vc_tile[:, c * LANES:(c + 1) * LANES]
    gw = CMP_STRIDE * NSA_HEAD_DIM
    for l in range(CMP_STRIDE):
        for c in range(NSA_KV // LANES):
            k_rows = kbuf_ref[c, pl.ds(l, tm // CMP_STRIDE, stride=CMP_STRIDE), :]
            v_rows = vbuf_ref[c, pl.ds(l, tm // CMP_STRIDE, stride=CMP_STRIDE), :]
            for half in range(LANES // NSA_HEAD_DIM):
                g = c * (LANES // NSA_HEAD_DIM) + half
                src = slice(half * NSA_HEAD_DIM, (half + 1) * NSA_HEAD_DIM)
                dst = slice(g * gw + l * NSA_HEAD_DIM, g * gw + (l + 1) * NSA_HEAD_DIM)
                kc_ref[:, dst] = k_rows[:, src]
                vc_ref[:, dst] = v_rows[:, src]
    pos = seq_tile * tm + lax.broadcasted_iota(jnp.int32, (tm, NSA_HEAD_DIM), 0)
    block_in_chunk = (pos // SLC_BLOCK) % (TK // SLC_BLOCK)
    onehot = jnp.where(lax.broadcasted_iota(jnp.int32, (tm, NSA_HEAD_DIM), 1) == block_in_chunk, 1.0, 0.0)
    ks = rope_seg(seg(2))
    for g in range(NSA_KV_HEADS):
        kg = ks[:, g * NSA_HEAD_DIM:(g + 1) * NSA_HEAD_DIM]
        ks_ref[g] = jnp.concatenate([kg, onehot], axis=1).astype(BF16)
    store_heads_t(vst_ref, seg(3))
    store_heads(kw_ref, rope_seg(seg(4)))
    store_heads_t(vwt_ref, seg(5))

    zs_ref[...] = _silu(mm(wa_ref, 0, SSM_D_INNER))
    xbc_ref[...] = mm(wa_ref, SSM_D_INNER, SSM_D_INNER + SSM_XBC)


def _in_proj(x2, norm_g, w_parts, cos_t, sin_t, seq):
    t = x2.shape[0]
    tm = TM_PROJ
    nseq = seq // tm
    row = lambda i: (i, 0)
    const = lambda i: (0, 0)
    heads = lambda i: (0, i, 0)
    heads_t = lambda i: (0, 0, i)
    out_shape = (
        jax.ShapeDtypeStruct((t, SSM_D_INNER), F32),
        jax.ShapeDtypeStruct((t, SSM_XBC), F32),
        jax.ShapeDtypeStruct((NSA_HEADS, NSA_HEAD_DIM, t), BF16),
        jax.ShapeDtypeStruct((t // CMP_STRIDE, CMP_STRIDE * NSA_KV), F32),
        jax.ShapeDtypeStruct((t // CMP_STRIDE, CMP_STRIDE * NSA_KV), F32),
        jax.ShapeDtypeStruct((NSA_KV_HEADS, t, KSEL), BF16),
        jax.ShapeDtypeStruct((NSA_KV_HEADS, NSA_HEAD_DIM, t), BF16),
        jax.ShapeDtypeStruct((NSA_KV_HEADS, t, NSA_HEAD_DIM), BF16),
        jax.ShapeDtypeStruct((NSA_KV_HEADS, NSA_HEAD_DIM, t), BF16),
        jax.ShapeDtypeStruct((t, TAIL), F32),
        jax.ShapeDtypeStruct((TAIL, t), F32),
    )
    kvh = pl.BlockSpec((NSA_KV_HEADS, tm, NSA_HEAD_DIM), heads)
    kvh_t = pl.BlockSpec((NSA_KV_HEADS, NSA_HEAD_DIM, tm), heads_t)
    out_specs = (
        pl.BlockSpec((tm, SSM_D_INNER), row),
        pl.BlockSpec((tm, SSM_XBC), row),
        pl.BlockSpec((NSA_HEADS, NSA_HEAD_DIM, tm), heads_t),
        pl.BlockSpec((tm // CMP_STRIDE, CMP_STRIDE * NSA_KV), row),
        pl.BlockSpec((tm // CMP_STRIDE, CMP_STRIDE * NSA_KV), row),
        pl.BlockSpec((NSA_KV_HEADS, tm, KSEL), heads), kvh_t, kvh, kvh_t,
        pl.BlockSpec((tm, TAIL), row),
        pl.BlockSpec((TAIL, tm), lambda i: (0, i)),
    )
    in_specs = [
        pl.BlockSpec((tm, D_MODEL), row),
        pl.BlockSpec((1, D_MODEL), const),
        pl.BlockSpec((D_MODEL, SSM_D_INNER + SSM_XBC), const, pipeline_mode=pl.Buffered(1)),
        pl.BlockSpec((D_MODEL, NSA_Q + 6 * NSA_KV), const, pipeline_mode=pl.Buffered(1)),
        pl.BlockSpec((D_MODEL, TAIL), const, pipeline_mode=pl.Buffered(1)),
        pl.BlockSpec((tm, LANES), lambda i: (i % nseq, 0)),
        pl.BlockSpec((tm, LANES), lambda i: (i % nseq, 0)),
    ]
    return pl.pallas_call(
        functools.partial(_inproj_kernel, tiles_per_seq=nseq), grid=(t // tm,), in_specs=in_specs,
        out_specs=out_specs,
        out_shape=out_shape,
        scratch_shapes=[pltpu.VMEM((NSA_KV // LANES, tm, LANES), F32),
                        pltpu.VMEM((NSA_KV // LANES, tm, LANES), F32)],
        compiler_params=_cparams(("arbitrary",)), name="in_proj",
    )(x2, norm_g, *w_parts, cos_t, sin_t)


def _ssd_kernel(xbc_ref, zs_ref, tail_ref, *rest):
    tailt_refs, rest = rest[:SSD_SEQS_PER_STEP], rest[SSD_SEQS_PER_STEP:]
    params, (y_ref, xcat_ref, state_ref, ybuf_ref) = rest[:-4], rest[-4:]
    for bi, tailt_ref in enumerate(tailt_refs):
        _ssd_chunk(xbc_ref.at[bi], zs_ref.at[bi], tail_ref.at[bi], tailt_ref, *params,
                   y_ref.at[bi], xcat_ref.at[bi], state_ref.at[bi], ybuf_ref.at[bi])


def _ssd_chunk(xbc_ref, zs_ref, tail_ref, tailt_ref, cw_ref, cb_ref, dtb_ref, dtbt_ref,
               alog_ref, alogt_ref, dskip_ref, ng_ref,
               y_ref, xcat_ref, state_ref, ybuf_ref):
    L, P, N, H, G = SSM_CHUNK, SSM_HEAD_DIM, SSM_STATE, SSM_HEADS, SSM_GROUPS
    HG = H // G
    halo = 8

    @pl.when(pl.program_id(1) == 0)
    def _():
        xcat_ref[0:halo, :] = jnp.zeros((halo, SSM_XBC), F32)
        state_ref[...] = jnp.zeros_like(state_ref)

    xcat_ref[halo:halo + L, :] = xbc_ref[...]
    xcat = xcat_ref[...]
    conv = cb_ref[...] + cw_ref[SSM_CONV - 1:SSM_CONV, :] * xcat[halo:halo + L]
    for k in range(SSM_CONV - 1):
        shifted = pltpu.roll(xcat, SSM_CONV - 1 - k, 0)
        conv = conv + cw_ref[k:k + 1, :] * shifted[halo:halo + L]
    xcat_ref[0:halo, :] = xcat[L:L + halo]
    u = _silu(conv)
    xs = u[:, :SSM_D_INNER]
    bm = u[:, SSM_D_INNER:SSM_D_INNER + G * N]
    cm = u[:, SSM_D_INNER + G * N:]

    def softplus(v):
        return jnp.maximum(v, 0.0) + jnp.log1p(jnp.exp(-jnp.abs(v)))

    dt = softplus(tail_ref[:, 0:H] + dtb_ref[...])
    dtt = softplus(tailt_ref[0:H, :] + dtbt_ref[...])
    da = dt * (-jnp.exp(alog_ref[...]))
    dat = dtt * (-jnp.exp(alogt_ref[...]))
    ri = lax.broadcasted_iota(jnp.int32, (L, L), 0)
    ci = lax.broadcasted_iota(jnp.int32, (L, L), 1)
    tri = ci <= ri
    hi = lax.Precision.HIGHEST
    acs = jnp.dot(tri.astype(F32), da, precision=hi, preferred_element_type=F32)
    acst = jnp.dot(dat, (ri <= ci).astype(F32), precision=hi, preferred_element_type=F32)
    last = acs[L - 1:L, :]
    w_state = dt * jnp.exp(last - acs)
    eacs = jnp.exp(acs)
    cdec = jnp.exp(last)

    hrow = lax.broadcasted_iota(jnp.int32, (H, SSM_D_INNER), 0)
    hcol = lax.broadcasted_iota(jnp.int32, (H, SSM_D_INNER), 1)
    expand = jnp.where(hcol // P == hrow, 1.0, 0.0).astype(BF16)

    def split3(v):
        v_hi = v.astype(BF16)
        r1 = v - v_hi.astype(F32)
        v_mid = r1.astype(BF16)
        v_lo = (r1 - v_mid.astype(F32)).astype(BF16)
        return jnp.concatenate([v_hi, v_mid, v_lo], axis=1)

    small = jnp.concatenate([cdec, dskip_ref[...], jnp.zeros((6, H), F32)], axis=0)
    per_head = jnp.concatenate([dt, w_state, eacs, small], axis=0)
    spread = jnp.dot(split3(per_head), jnp.concatenate([expand] * 3, axis=0), preferred_element_type=F32)
    cdec_e = spread[3 * L:3 * L + 1, :]
    dskip_e = spread[3 * L + 1:3 * L + 2, :]
    xdt = (xs * spread[0:L]).astype(BF16)
    wst = (xs * spread[L:2 * L]).astype(BF16)
    eacs_e = spread[2 * L:3 * L]

    for g in range(G):
        bm_g = bm[:, g * N:(g + 1) * N]
        cm_g = cm[:, g * N:(g + 1) * N].astype(BF16)
        cb = lax.dot_general(cm_g, bm_g.astype(BF16), NT_DIMS, preferred_element_type=F32)
        cols = slice(g * HG * P, (g + 1) * HG * P)
        st = state_ref[:, cols]
        y_off = jnp.dot(cm_g, st.astype(BF16), preferred_element_type=F32) * eacs_e[:, cols]
        ybuf_ref[:, cols] = y_off
        bmt = bm_g.T.astype(BF16)
        state_ref[:, cols] = st * cdec_e[:, cols] + jnp.dot(bmt, wst[:, cols], preferred_element_type=F32)
        for r in range(HG):
            hh = g * HG + r
            diff = acs[:, hh:hh + 1] - acst[hh:hh + 1, :]
            seg = jnp.exp(jnp.where(tri, diff, -jnp.inf))
            lmat = (cb * seg).astype(BF16)
            hc = slice(hh * P, (hh + 1) * P)
            ybuf_ref[:, hc] = ybuf_ref[:, hc] + jnp.dot(lmat, xdt[:, hc], preferred_element_type=F32)

    y = (ybuf_ref[...] + xs * dskip_e) * zs_ref[...]
    gw = SSM_D_INNER // G
    parts = []
    for g in range(G):
        yg = y[:, g * gw:(g + 1) * gw]
        parts.append(yg * lax.rsqrt(jnp.mean(yg * yg, axis=-1, keepdims=True) + SSM_NORM_EPS))
    y_ref[...] = (jnp.concatenate(parts, axis=1) * ng_ref[...]).astype(y_ref.dtype)


def _ssd(xbc, zs, tail, tailt, conv_w, conv_b, dt_bias, a_log, d_skip, norm_g, batch, seq):
    L = SSM_CHUNK
    nc = seq // L
    nb = SSD_SEQS_PER_STEP
    row = lambda b, c: (b, c, 0)
    const = lambda b, c: (0, 0)
    H = SSM_HEADS
    per_seq = lambda a: a.reshape(batch, seq, a.shape[-1])
    in_specs = [
        pl.BlockSpec((nb, L, SSM_XBC), row),
        pl.BlockSpec((nb, L, SSM_D_INNER), row),
        pl.BlockSpec((nb, L, TAIL), row),
    ] + [
        pl.BlockSpec((TAIL, L), lambda b, c, j=j: (0, (b * nb + j) * nc + c)) for j in range(nb)
    ] + [
        pl.BlockSpec((SSM_CONV, SSM_XBC), const),
        pl.BlockSpec((1, SSM_XBC), const),
        pl.BlockSpec((1, H), const),
        pl.BlockSpec((H, 1), const),
        pl.BlockSpec((1, H), const),
        pl.BlockSpec((H, 1), const),
        pl.BlockSpec((1, H), const),
        pl.BlockSpec((1, SSM_D_INNER), const),
    ]
    y = pl.pallas_call(
        _ssd_kernel, grid=(batch // nb, nc), in_specs=in_specs,
        out_specs=pl.BlockSpec((nb, L, SSM_D_INNER), row),
        out_shape=jax.ShapeDtypeStruct((batch, seq, SSM_D_INNER), BF16),
        scratch_shapes=[pltpu.VMEM((nb, L + 8, SSM_XBC), F32),
                        pltpu.VMEM((nb, SSM_STATE, SSM_D_INNER), F32),
                        pltpu.VMEM((nb, L, SSM_D_INNER), F32)],
        compiler_params=_cparams(("arbitrary", "arbitrary")), name="ssd",
    )(per_seq(xbc), per_seq(zs), per_seq(tail), *([tailt] * nb), conv_w, conv_b,
      dt_bias.reshape(1, H), dt_bias.reshape(H, 1),
      a_log.reshape(1, H), a_log.reshape(H, 1), d_skip.reshape(1, H), norm_g.reshape(1, -1))
    return y.reshape(batch * seq, SSM_D_INNER)


def _compress_kernel(k_ref, v_ref, kpos_ref, vpos_ref, kw1_ref, vw1_ref, kb1_ref, vb1_ref,
                     kw2_ref, vw2_ref, kc_ref, vct_ref):
    D = NSA_HEAD_DIM
    gw = CMP_STRIDE * D

    def hidden(t_ref, pos_ref, w1_ref, b1_ref, g):
        t = t_ref[:, g * gw:(g + 1) * gw]
        n = t.shape[0]
        lo = jnp.dot((t + pos_ref[0:1, :]).astype(BF16), w1_ref[...], preferred_element_type=F32)
        hi = jnp.dot((t + pos_ref[1:2, :]).astype(BF16), w1_ref[...], preferred_element_type=F32)
        pre = lo + pltpu.roll(pltpu.roll(hi, n - 1, 0), D, 1)
        return _silu(pre[:, 0:D] + b1_ref[...]).astype(BF16)

    for g in range(NSA_KV_HEADS):
        kc = jnp.dot(hidden(k_ref, kpos_ref, kw1_ref, kb1_ref, g), kw2_ref[...], preferred_element_type=F32)
        kc_ref[g] = kc.astype(BF16)
        vct = lax.dot_general(vw2_ref[...], hidden(v_ref, vpos_ref, vw1_ref, vb1_ref, g), NT_DIMS,
                              preferred_element_type=F32)
        vct_ref[g] = vct.astype(BF16)


def _compress_weights(pos, w1, b1, w2):
    half = CMP_BLOCK // 2 * NSA_HEAD_DIM
    w1cat = jnp.concatenate([w1[:half], w1[half:]], axis=1).astype(BF16)
    return pos.reshape(2, half), w1cat, b1.reshape(1, -1), w2.astype(BF16)


def _compress(k16, v16, kparams, vparams, batch, seq):
    rows = seq // CMP_STRIDE
    width = CMP_STRIDE * NSA_KV
    D = NSA_HEAD_DIM
    gw = CMP_STRIDE * D
    kpos, kw1, kb1, kw2 = _compress_weights(*kparams)
    vpos, vw1, vb1, vw2 = _compress_weights(*vparams)
    vw2 = vw2.T
    c2 = lambda b: (0, 0)
    tok = pl.BlockSpec((rows, width), lambda b: (b, 0))
    in_specs = [tok, tok,
                pl.BlockSpec((2, gw), c2), pl.BlockSpec((2, gw), c2),
                pl.BlockSpec((gw, 2 * D), c2), pl.BlockSpec((gw, 2 * D), c2),
                pl.BlockSpec((1, D), c2), pl.BlockSpec((1, D), c2),
                pl.BlockSpec((D, D), c2), pl.BlockSpec((D, D), c2)]
    out = jax.ShapeDtypeStruct((NSA_KV_HEADS, batch * rows, D), BF16)
    out_t = jax.ShapeDtypeStruct((NSA_KV_HEADS, D, batch * rows), BF16)
    ospec = pl.BlockSpec((NSA_KV_HEADS, rows, D), lambda b: (0, b, 0))
    ospec_t = pl.BlockSpec((NSA_KV_HEADS, D, rows), lambda b: (0, 0, b))
    return pl.pallas_call(
        _compress_kernel, grid=(batch,), in_specs=in_specs, out_specs=(ospec, ospec_t),
        out_shape=(out, out_t), compiler_params=_cparams(("arbitrary",)), name="compress",
    )(k16, v16, kpos, vpos, kw1, vw1, kb1, vb1, kw2, vw2)


def _nsa_kernel(qt_ref, kc_ref, vct_ref, ks_ref, vst_ref, kw_ref, vwt_ref, tailt_ref, ovt_ref, mtab_ref, cmask_ref,
                o_ref, biasq_ref, m_ref, acc_ref, part_ref, sbuf0_ref, sbuf1_ref, mbuf0_ref, mbuf1_ref):
    R, D = NSA_REP, NSA_HEAD_DIM
    nl = R * TQ
    grp = range(kc_ref.shape[0])
    sbuf_refs = (sbuf0_ref, sbuf1_ref)
    mbuf_refs = (mbuf0_ref, mbuf1_ref)
    n_stages = (ks_ref.shape[1] // TK - 1) // 2
    gp = pl.program_id(1)
    qi = pl.program_id(2)
    t0 = qi * TQ
    qt = [jnp.concatenate([qt_ref[gi * R + r] for r in range(R)], axis=1) for gi in grp]
    tpos_row = t0 + lax.broadcasted_iota(jnp.int32, (1, nl), 1) % TQ

    def k_rows(k_ref, gi, start):
        return k_ref[gi, pl.ds(pl.multiple_of(start, TK), TK), :]

    def vt_cols(vt_ref, gi, start, n):
        return vt_ref[gi, :, pl.ds(pl.multiple_of(start, TK), n)]

    def pv(vt, p):
        vt1 = jnp.concatenate([vt, jnp.ones((ONES_ROWS, vt.shape[1]), BF16)], axis=0)
        return jnp.dot(vt1, p, preferred_element_type=F32)

    def mask_tile(off):
        return jnp.concatenate([mtab_ref[pl.ds(pl.multiple_of(off, TK), TK), :]] * R, axis=1)

    def col_max(s):
        return jnp.max(s, axis=0, keepdims=True)

    def online(state, s, m_chunk, vt):
        m_new = m_chunk if state is None else jnp.maximum(state[0], m_chunk)
        contrib = pv(vt, jnp.exp2(s - m_new).astype(BF16))
        if state is None:
            return m_new, contrib
        return m_new, jnp.exp2(state[0] - m_new) * state[1] + contrib

    causal = mask_tile(MT_CAUSAL)
    far0 = jnp.maximum(t0 - 2 * TK, 0)
    mid0 = jnp.maximum(t0 - TK, 0)
    window = []
    for gi in grp:
        s_dia = jnp.dot(k_rows(kw_ref, gi, t0), qt[gi], preferred_element_type=F32) + causal
        m_dia = col_max(s_dia)
        s_mid = (jnp.dot(k_rows(kw_ref, gi, mid0), qt[gi], preferred_element_type=F32)
                 + mask_tile(jnp.where(qi >= 1, MT_ALL, MT_NONE)))
        m_mid = col_max(s_mid)
        s_far = (jnp.dot(k_rows(kw_ref, gi, far0), qt[gi], preferred_element_type=F32)
                 + mask_tile(jnp.where(qi >= 2, MT_FAR, MT_NONE)))
        m_far = col_max(s_far)
        window.append(((s_dia, m_dia, t0), (s_mid, m_mid, mid0), (s_far, m_far, far0)))

    ncr = kc_ref.shape[1]
    n_slc = ovt_ref.shape[0]
    cmask = cmask_ref[pl.ds(pl.multiple_of(qi * ncr, ncr), ncr), :]
    cmask = jnp.concatenate([cmask] * R, axis=1)
    sees_any = tpos_row >= CMP_BLOCK - 1
    jj = lax.broadcasted_iota(jnp.int32, (n_slc, TQ), 0)
    tt = t0 + lax.broadcasted_iota(jnp.int32, (n_slc, TQ), 1)
    lag = tt // SLC_BLOCK - jj
    forced = (jj == 0) | ((lag >= 0) & (lag < N_LOCAL_BLOCKS))
    valid = jj * SLC_BLOCK <= tt
    rows8 = 8
    j8 = lax.broadcasted_iota(jnp.int32, (rows8, TQ), 0)
    per = TK // SLC_BLOCK
    zrows = jnp.zeros((BIAS_ROWS - per, TQ), F32)
    o_c, o_w = [], []
    for gi in grp:
        s_m = jnp.dot(kc_ref[gi], qt[gi], preferred_element_type=F32) + cmask
        mx = jnp.max(s_m, axis=0, keepdims=True)
        p = jnp.exp2(s_m - mx)
        den = jnp.sum(p, axis=0, keepdims=True)
        pc = p * jnp.where(sees_any, 1.0 / den, 0.0)
        o_c.append(jnp.dot(vct_ref[gi], pc.astype(BF16), preferred_element_type=F32))

        psum = pc[:, 0:TQ]
        for r in range(1, R):
            psum = psum + pc[:, r * TQ:(r + 1) * TQ]
        imp_t = jnp.dot(ovt_ref[...], psum, precision=lax.Precision.HIGHEST,
                        preferred_element_type=F32)
        score = jnp.where(forced, FORCED_SCORE, jnp.where(valid, imp_t, -1.0))
        groups = [score[a:a + rows8] for a in range(0, n_slc, rows8)]
        ranks = [jnp.zeros((rows8, TQ), F32) for _ in groups]
        window_steps = dict(zip((0, n_slc // 3, 2 * n_slc // 3), window[gi]))
        win = None
        for j2 in range(n_slc):
            if j2 in window_steps:
                s_w, m_w, start_w = window_steps[j2]
                win = online(win, s_w, m_w, vt_cols(vwt_ref, gi, start_w, TK))
            sj = score[j2:j2 + 1, :]
            for a, sg in enumerate(groups):
                lo = a * rows8
                if lo + rows8 - 1 < j2:
                    beats = sj > sg
                elif lo > j2:
                    beats = sj >= sg
                else:
                    beats = (sj > sg) | ((sj == sg) & (j8 + lo > j2))
                ranks[a] = ranks[a] + jnp.where(beats, 1.0, 0.0)
        rank = jnp.concatenate(ranks, axis=0)
        selected = (rank < float(min(SLC_TOP_N, n_slc))) & (score >= 0.0)
        bias_t = jnp.where(selected, 0.0, NEG_BIG)
        for kb in range(n_slc // per):
            blk = jnp.concatenate([bias_t[kb * per:(kb + 1) * per], zrows], axis=0)
            biasq_ref[gi, kb * BIAS_ROWS:(kb + 1) * BIAS_ROWS, :] = (
                jnp.concatenate([blk] * R, axis=1).astype(BF16))
        accw = win[1]
        o_w.append(accw[0:D] / accw[D:D + 1])
    qpad = jnp.zeros((KSEL - D - BIAS_ROWS, nl), BF16)

    def q_sel(gi, kb):
        rows = biasq_ref[gi, pl.ds(pl.multiple_of(kb * BIAS_ROWS, BIAS_ROWS), BIAS_ROWS), :]
        return jnp.concatenate([qt[gi], rows, qpad], axis=0)

    def slc_scores(gi, kb):
        return jnp.dot(k_rows(ks_ref, gi, kb * TK), q_sel(gi, kb), preferred_element_type=F32)

    def fill(gi, slot, chunks):
        for c, kb in enumerate(chunks):
            s = slc_scores(gi, kb)
            sbuf_refs[slot][gi, c * TK:(c + 1) * TK, :] = s
            mbuf_refs[slot][gi, c:c + 1, :] = col_max(s)

    npairs = qi // 2
    diag = []
    for gi in grp:
        s_sel = slc_scores(gi, qi) + causal
        diag.append((s_sel, col_max(s_sel)))
        fill(gi, 0, (0, 1))

    def gates(gi):
        gate_row = pl.multiple_of(GATE_OFF + GATE_SLOT * (gp * len(grp) + gi), 8)
        sig = jax.nn.sigmoid(tailt_ref[pl.ds(gate_row, GATE_SLOT), :])
        return [jnp.concatenate([sig[3 * r + c:3 * r + c + 1, :] for r in range(R)], axis=1)
                for c in range(3)]

    for gi in grp:
        gate = gates(gi)
        part_ref[gi] = gate[0] * o_c[gi] + gate[2] * o_w[gi]

    for gi in grp:
        m_sel, acc_sel = online(None, diag[gi][0], diag[gi][1], vt_cols(vst_ref, gi, t0, TK))
        m_ref[gi] = m_sel
        acc_ref[gi] = acc_sel

    def absorb(gi, slot, first, n):
        state = (m_ref[gi], acc_ref[gi])
        for c in range(n):
            state = online(state, sbuf_refs[slot][gi, c * TK:(c + 1) * TK, :], mbuf_refs[slot][gi, c:c + 1, :],
                           vt_cols(vst_ref, gi, (first + c) * TK, TK))
        m_ref[gi] = state[0]
        acc_ref[gi] = state[1]

    for k in range(n_stages):
        @pl.when(npairs > k)
        def _(k=k):
            for gi in grp:
                fill(gi, (k + 1) % 2, (jnp.minimum(2 * k + 2, qi - 1), jnp.minimum(2 * k + 3, qi - 1)))
            for gi in grp:
                absorb(gi, k % 2, 2 * k, 2)

    for parity in range(2):
        @pl.when((qi % 2 == 1) & (npairs % 2 == parity))
        def _(parity=parity):
            for gi in grp:
                absorb(gi, parity, qi - 1, 1)

    for gi in grp:
        acc = acc_ref[gi]
        ot = part_ref[gi] + gates(gi)[1] * (acc[0:D] / acc[D:D + 1])
        stacked = jnp.concatenate([ot[:, r * TQ:(r + 1) * TQ] for r in range(R)], axis=0)
        o_ref[:, gi * R * D:(gi + 1) * R * D] = stacked.T.astype(o_ref.dtype)


def _overlap_t(seq):
    n_cmp = (seq - CMP_BLOCK) // CMP_STRIDE + 1
    n_slc = seq // SLC_BLOCK
    cs = np.arange(n_cmp) * CMP_STRIDE
    ss = np.arange(n_slc) * SLC_BLOCK
    overlap = np.clip(np.minimum(cs[:, None] + CMP_BLOCK, ss[None, :] + SLC_BLOCK)
                      - np.maximum(cs[:, None], ss[None, :]), 0, None) / CMP_BLOCK
    ovt = np.zeros((n_slc, n_cmp + 1), np.float32)
    ovt[:, :n_cmp] = overlap.T
    return jnp.asarray(ovt)


def _mask_tiles():
    ki = np.arange(TK)[:, None]
    qi = np.arange(TQ)[None, :]
    neg = np.float32(NEG_BIG)
    none = np.full((TK, TQ), neg, np.float32)
    far = np.where(ki > qi, np.float32(0), neg)
    full = np.zeros((TK, TQ), np.float32)
    causal = np.where(ki <= qi, np.float32(0), neg)
    return jnp.asarray(np.concatenate([none, far, full, causal], axis=0))


def _cmp_masks(seq):
    ncr = seq // CMP_STRIDE
    n_cmp = (seq - CMP_BLOCK) // CMP_STRIDE + 1
    n = np.arange(ncr)[None, :, None]
    t = (np.arange(seq // TQ)[:, None, None] * TQ + np.arange(TQ)[None, None, :])
    visible = (n * CMP_STRIDE + CMP_BLOCK - 1 <= t) & (n < n_cmp)
    return jnp.asarray(np.where(visible, np.float32(0), np.float32(NEG_BIG)).reshape(-1, TQ))


def _nsa(qt, kc, vct, ks, vst, kw, vwt, tailt, batch, seq):
    t = batch * seq
    G, R, D = NSA_KV_HEADS, NSA_REP, NSA_HEAD_DIM
    nq = seq // TQ
    ncr = seq // CMP_STRIDE
    n_slc = seq // SLC_BLOCK
    assert TQ == TK and WINDOW == 2 * TK, "window branch visits exactly three key chunks"
    assert TK // SLC_BLOCK <= BIAS_ROWS and D + BIAS_ROWS <= KSEL
    P = NSA_GROUPS_PER_STEP
    const = lambda b, g, i: (0, 0)
    vtspec = pl.BlockSpec((P, D, seq), lambda b, g, i: (g, 0, b))
    in_specs = [
        pl.BlockSpec((P * R, D, TQ), lambda b, g, i: (g, 0, b * nq + i)),
        pl.BlockSpec((P, ncr, D), lambda b, g, i: (g, b, 0)),
        pl.BlockSpec((P, D, ncr), lambda b, g, i: (g, 0, b)),
        pl.BlockSpec((P, seq, KSEL), lambda b, g, i: (g, b, 0)),
        vtspec,
        pl.BlockSpec((P, seq, D), lambda b, g, i: (g, b, 0)),
        vtspec,
        pl.BlockSpec((TAIL, TQ), lambda b, g, i: (0, b * nq + i)),
        pl.BlockSpec((n_slc, ncr), const),
        pl.BlockSpec((4 * TK, TQ), const),
        pl.BlockSpec((nq * ncr, TQ), const),
    ]
    return pl.pallas_call(
        _nsa_kernel, grid=(batch, G // P, nq), in_specs=in_specs,
        out_specs=pl.BlockSpec((TQ, P * R * D), lambda b, g, i: (b * nq + i, g)),
        out_shape=jax.ShapeDtypeStruct((t, NSA_Q), BF16),
        scratch_shapes=[pltpu.VMEM((P, seq // TK * BIAS_ROWS, R * TQ), BF16),
                        pltpu.VMEM((P, 1, R * TQ), F32),
                        pltpu.VMEM((P, D + ONES_ROWS, R * TQ), F32),
                        pltpu.VMEM((P, D, R * TQ), F32),
                        pltpu.VMEM((P, 2 * TK, R * TQ), F32),
                        pltpu.VMEM((P, 2 * TK, R * TQ), F32),
                        pltpu.VMEM((P, 8, R * TQ), F32),
                        pltpu.VMEM((P, 8, R * TQ), F32)],
        compiler_params=_cparams(("arbitrary", "arbitrary", "arbitrary")), name="nsa",
    )(qt, kc, vct, ks, vst, kw, vwt, tailt, _overlap_t(seq), _mask_tiles(), _cmp_masks(seq))


def _outproj_kernel(y_ref, o_ref, x_ref, w_ref, ag_ref, ng_ref, x1_ref, h2_ref):
    yn = _rms(o_ref[...].astype(F32), ag_ref[...], NORM_EPS).astype(BF16)
    x1 = (x_ref[...]
          + jnp.dot(y_ref[...], w_ref[0:SSM_D_INNER, :], preferred_element_type=F32)
          + jnp.dot(yn, w_ref[SSM_D_INNER:, :], preferred_element_type=F32))
    x1_ref[...] = x1
    h2_ref[...] = _rms(x1, ng_ref[...], NORM_EPS).astype(BF16)


def _out_proj(y_ssm, o_nsa, x2, w_out, attn_g, norm2_g):
    t = x2.shape[0]
    tm = TM_PROJ
    row = lambda i: (i, 0)
    const = lambda i: (0, 0)
    tok = pl.BlockSpec((tm, D_MODEL), row)
    vec = pl.BlockSpec((1, D_MODEL), const)
    return pl.pallas_call(
        _outproj_kernel, grid=(t // tm,),
        in_specs=[tok, tok, tok, pl.BlockSpec((SSM_D_INNER + NSA_Q, D_MODEL), const), vec, vec],
        out_specs=(tok, tok),
        out_shape=(jax.ShapeDtypeStruct((t, D_MODEL), F32), jax.ShapeDtypeStruct((t, D_MODEL), BF16)),
        compiler_params=_cparams(("arbitrary",)), name="out_proj",
    )(y_ssm, o_nsa, x2, w_out, attn_g, norm2_g)


def _ffn_kernel(h_ref, halo_ref, x1_ref, wup_ref, cw_ref, cb_ref, wd_ref, fg_ref, out_ref, act_ref,
                *, tiles_per_seq):
    i = pl.program_id(0)
    tm = h_ref.shape[0]
    pad = halo_ref.shape[0]
    tn = TN_FFN
    halo = halo_ref[...]
    halo = jnp.where(i % tiles_per_seq == 0, jnp.zeros_like(halo), halo)
    hc = jnp.concatenate([halo, h_ref[...]], axis=0)

    def branch(c0):
        cols = slice(c0, c0 + tn)
        u = jnp.dot(hc, wup_ref[:, cols], preferred_element_type=F32)
        out = cb_ref[:, cols] + cw_ref[FFN_CONV - 1:FFN_CONV, cols] * u[pad:pad + tm, :]
        for k in range(FFN_CONV - 1):
            shifted = pltpu.roll(u, FFN_CONV - 1 - k, 0)
            out = out + cw_ref[k:k + 1, cols] * shifted[pad:pad + tm, :]
        return out

    for j in range(D_FF // tn):
        act = _silu(branch(j * tn)) * branch(D_FF + j * tn)
        act_ref[:, j * tn:(j + 1) * tn] = act.astype(BF16)
    down = jnp.dot(act_ref[...], wd_ref[...], preferred_element_type=F32)
    out_ref[...] = _rms(x1_ref[...] + down, fg_ref[...], NORM_EPS)


def _ffn(h2, x1, w_up, conv_w, conv_b, w_down, final_g, seq):
    t = h2.shape[0]
    tm = TM_FFN
    pad = BF16_SUBLANES
    tok = pl.BlockSpec((tm, D_MODEL), lambda i: (i, 0))

    def resident(shape):
        return pl.BlockSpec(shape, lambda i: (0, 0), pipeline_mode=pl.Buffered(1))

    in_specs = [
        tok,
        pl.BlockSpec((pad, D_MODEL), lambda i: (jnp.maximum(i * (tm // pad) - 1, 0), 0)),
        tok,
        resident((D_MODEL, 2 * D_FF)),
        resident((FFN_CONV, 2 * D_FF)),
        resident((1, 2 * D_FF)),
        resident((D_FF, D_MODEL)),
        resident((1, D_MODEL)),
    ]
    return pl.pallas_call(
        functools.partial(_ffn_kernel, tiles_per_seq=seq // tm), grid=(t // tm,),
        in_specs=in_specs, out_specs=tok,
        out_shape=jax.ShapeDtypeStruct((t, D_MODEL), F32),
        scratch_shapes=[pltpu.VMEM((tm, D_FF), BF16)],
        compiler_params=_cparams(("arbitrary",)), name="ffn",
    )(h2, h2, x1, w_up, conv_w, conv_b, w_down, final_g)


def _rope_tables(seq):
    half = NSA_HEAD_DIM // 2
    inv_freq = 1.0 / (ROPE_THETA ** (jnp.arange(0, NSA_HEAD_DIM, 2, dtype=F32) / NSA_HEAD_DIM))
    ang = jnp.arange(seq).astype(F32)[:, None] * inv_freq[None, :]
    cos, sin = jnp.cos(ang), jnp.sin(ang)
    reps = LANES // NSA_HEAD_DIM
    cos_t = jnp.tile(jnp.concatenate([cos, cos], axis=1), (1, reps))
    sin_t = jnp.tile(jnp.concatenate([-sin, sin], axis=1), (1, reps))
    return cos_t, sin_t


def _split_w_in(w):
    o_z, o_xbc, o_dt, o_q, o_kv, o_g = np.cumsum([0, SSM_D_INNER, SSM_XBC, SSM_HEADS, NSA_Q, 6 * NSA_KV]).tolist()
    per = 3 * NSA_REP
    tail_cols = [w[:, o_dt:o_q]]
    for g in range(NSA_KV_HEADS):
        tail_cols += [w[:, o_g + g * per:o_g + (g + 1) * per], jnp.zeros((w.shape[0], GATE_SLOT - per), w.dtype)]
    tail_cols.append(jnp.zeros((w.shape[0], TAIL - GATE_OFF - NSA_KV_HEADS * GATE_SLOT), w.dtype))
    return (w[:, o_z:o_dt].astype(BF16), w[:, o_q:o_g].astype(BF16),
            jnp.concatenate(tail_cols, axis=1).astype(BF16))


def kernel(x, norm1_g, w_in, ssm_conv_w, ssm_conv_b, ssm_dt_bias, ssm_a_log, ssm_d, ssm_norm_g, cmp_k_pos, cmp_k_w1, cmp_k_b1, cmp_k_w2, cmp_v_pos, cmp_v_w1, cmp_v_b1, cmp_v_w2, attn_norm_g, w_out, norm2_g, ffn_w_up, ffn_conv_w, ffn_conv_b, ffn_w_down, final_norm_g):
    batch, seq, d = x.shape
    assert w_in.shape[0] == 1, "single-layer problem"
    l = 0
    cos_t, sin_t = _rope_tables(seq)
    x2 = x.reshape(batch * seq, d)
    zs, xbc, qt, kc_r, vc_r, ks, vst, kw, vwt, tail, tailt = _in_proj(
        x2, norm1_g[l].reshape(1, d), _split_w_in(w_in.reshape(d, -1)), cos_t, sin_t, seq)
    y_ssm = _ssd(xbc, zs, tail, tailt, ssm_conv_w[l], ssm_conv_b[l].reshape(1, -1), ssm_dt_bias[l],
                 ssm_a_log[l], ssm_d[l], ssm_norm_g[l], batch, seq)
    kc, vct = _compress(kc_r, vc_r,
                        (cmp_k_pos[l], cmp_k_w1[l], cmp_k_b1[l], cmp_k_w2[l]),
                        (cmp_v_pos[l], cmp_v_w1[l], cmp_v_b1[l], cmp_v_w2[l]), batch, seq)
    o_nsa = _nsa(qt, kc, vct, ks, vst, kw, vwt, tailt, batch, seq)
    x1, h2 = _out_proj(y_ssm, o_nsa, x2, w_out.reshape(-1, d).astype(BF16), attn_norm_g[l].reshape(1, d),
                       norm2_g[l].reshape(1, d))
    out = _ffn(h2, x1, ffn_w_up.reshape(d, -1).astype(BF16), ffn_conv_w[l], ffn_conv_b[l].reshape(1, -1),
               ffn_w_down.reshape(-1, d).astype(BF16), final_norm_g.reshape(1, d), seq)
    return out.reshape(batch, seq, d)
```

```python
import functools
import math

import numpy as np
import jax
import jax.numpy as jnp
from jax import lax
from jax.experimental import pallas as pl
from jax.experimental.pallas import tpu as pltpu

F32 = jnp.float32
BF16 = jnp.bfloat16

D_MODEL = 1024
SSM_D_INNER = 1024
SSM_HEAD_DIM = 64
SSM_HEADS = 16
SSM_GROUPS = 2
SSM_STATE = 128
SSM_CONV = 4
SSM_CHUNK = 128
SSM_XBC = SSM_D_INNER + 2 * SSM_GROUPS * SSM_STATE
SSM_NORM_EPS = 1e-5
NSA_HEADS = 16
NSA_KV_HEADS = 4
NSA_REP = NSA_HEADS // NSA_KV_HEADS
NSA_HEAD_DIM = 64
NSA_Q = NSA_HEADS * NSA_HEAD_DIM
NSA_KV = NSA_KV_HEADS * NSA_HEAD_DIM
CMP_BLOCK = 32
CMP_STRIDE = 16
SLC_BLOCK = 64
SLC_TOP_N = 16
N_LOCAL_BLOCKS = 2
FORCED_SCORE = 1e4
WINDOW = 512
ROPE_THETA = 10000.0
D_FF = 2816
FFN_CONV = 3
NORM_EPS = 1e-6
NEG_BIG = -1e30

LANES = 128
BF16_SUBLANES = 16
TAIL = LANES
GATE_OFF = SSM_HEADS
GATE_SLOT = 16

TM_IN_PROJ = 256
TM_OUT_PROJ = 512
TQ = 256
TK = 256
NSA_GROUPS_PER_STEP = 4
ONES_ROWS = 16
KSEL = LANES
BIAS_ROWS = 16
MT_NONE, MT_FAR, MT_ALL, MT_CAUSAL = 0, TK, 2 * TK, 3 * TK
TM_FFN = 512
TN_FFN = 256
VMEM_LIMIT = 56 * 1024 * 1024

NT_DIMS = (((1,), (1,)), ((), ()))


def _cparams(sem):
    return pltpu.CompilerParams(dimension_semantics=sem, vmem_limit_bytes=VMEM_LIMIT)


def _rms(x, g, eps):
    return x * lax.rsqrt(jnp.mean(x * x, axis=-1, keepdims=True) + eps) * g


def _silu(x):
    return x * jax.nn.sigmoid(x)


def _inproj_kernel(x_ref, g_ref, wa_ref, wb_ref, wt_ref, cos_ref, sin_ref,
                   zs_ref, xbc_ref, qt_ref, kc_ref, vc_ref, ks_ref, vst_ref, kw_ref, vwt_ref,
                   tail_ref, tailt_ref, kbuf_ref, vbuf_ref, *, tiles_per_seq):
    tm = x_ref.shape[0]
    seq_tile = pl.program_id(0) % tiles_per_seq
    h = _rms(x_ref[...], g_ref[...], NORM_EPS).astype(BF16)

    def mm(w_ref, lo, hi):
        return jnp.dot(h, w_ref[:, lo:hi], preferred_element_type=F32)

    tail = mm(wt_ref, 0, TAIL)
    tail_ref[...] = tail
    tailt_ref[...] = tail.T

    cos = cos_ref[...]
    sin = sin_ref[...]
    lane = lax.broadcasted_iota(jnp.int32, (tm, LANES), 1)
    first_half = (lane % NSA_HEAD_DIM) < (NSA_HEAD_DIM // 2)

    def rope(xc):
        partner = jnp.where(first_half, pltpu.roll(xc, LANES - 32, 1), pltpu.roll(xc, 32, 1))
        return xc * cos + partner * sin

    q = mm(wb_ref, 0, NSA_Q)
    scale = NSA_HEAD_DIM ** -0.5 * math.log2(math.e)
    for c in range(NSA_Q // LANES):
        rt = (rope(q[:, c * LANES:(c + 1) * LANES]) * scale).T.astype(BF16)
        qt_ref[2 * c] = rt[:NSA_HEAD_DIM]
        qt_ref[2 * c + 1] = rt[NSA_HEAD_DIM:]

    kv = mm(wb_ref, NSA_Q, NSA_Q + 6 * NSA_KV)

    def seg(i):
        return kv[:, i * NSA_KV:(i + 1) * NSA_KV]

    def rope_seg(x):
        return jnp.concatenate([rope(x[:, :LANES]), rope(x[:, LANES:])], axis=1)

    def store_heads(ref, x):
        for g in range(NSA_KV_HEADS):
            ref[g] = x[:, g * NSA_HEAD_DIM:(g + 1) * NSA_HEAD_DIM].astype(BF16)

    def store_heads_t(ref, x):
        xt = x.T.astype(BF16)
        for g in range(NSA_KV_HEADS):
            ref[g] = xt[g * NSA_HEAD_DIM:(g + 1) * NSA_HEAD_DIM]

    kc_tile = rope_seg(seg(0))
    vc_tile = seg(1)
    for c in range(NSA_KV // LANES):
        kbuf_ref[c] = kc_tile[:, c * LANES:(c + 1) * LANES]
        vbuf_ref[c] = vc_tile[:, c * LANES:(c + 1) * LANES]
    gw = CMP_STRIDE * NSA_HEAD_DIM
    for l in range(CMP_STRIDE):
        for c in range(NSA_KV // LANES):
            k_rows = kbuf_ref[c, pl.ds(l, tm // CMP_STRIDE, stride=CMP_STRIDE), :]
            v_rows = vbuf_ref[c, pl.ds(l, tm // CMP_STRIDE, stride=CMP_STRIDE), :]
            for half in range(LANES // NSA_HEAD_DIM):
                g = c * (LANES // NSA_HEAD_DIM) + half
                src = slice(half * NSA_HEAD_DIM, (half + 1) * NSA_HEAD_DIM)
                dst = slice(g * gw + l * NSA_HEAD_DIM, g * gw + (l + 1) * NSA_HEAD_DIM)
                kc_ref[:, dst] = k_rows[:, src]
                vc_ref[:, dst] = v_rows[:, src]
    pos = seq_tile * tm + lax.broadcasted_iota(jnp.int32, (tm, NSA_HEAD_DIM), 0)
    block_in_chunk = (pos // SLC_BLOCK) % (TK // SLC_BLOCK)
    onehot = jnp.where(lax.broadcasted_iota(jnp.int32, (tm, NSA_HEAD_DIM), 1) == block_in_chunk, 1.0, 0.0)
    ks = rope_seg(seg(2))
    for g in range(NSA_KV_HEADS):
        kg = ks[:, g * NSA_HEAD_DIM:(g + 1) * NSA_HEAD_DIM]
        ks_ref[g] = jnp.concatenate([kg, onehot], axis=1).astype(BF16)
    store_heads_t(vst_ref, seg(3))
    store_heads(kw_ref, rope_seg(seg(4)))
    store_heads_t(vwt_ref, seg(5))

    zs_ref[...] = _silu(mm(wa_ref, 0, SSM_D_INNER))
    xbc_ref[...] = mm(wa_ref, SSM_D_INNER, SSM_D_INNER + SSM_XBC)


def _in_proj(x2, norm_g, w_parts, cos_t, sin_t, seq):
    t = x2.shape[0]
    tm = TM_IN_PROJ
    nseq = seq // tm
    row = lambda i: (i, 0)
    const = lambda i: (0, 0)
    heads = lambda i: (0, i, 0)
    heads_t = lambda i: (0, 0, i)
    out_shape = (
        jax.ShapeDtypeStruct((t, SSM_D_INNER), F32),
        jax.ShapeDtypeStruct((t, SSM_XBC), F32),
        jax.ShapeDtypeStruct((NSA_HEADS, NSA_HEAD_DIM, t), BF16),
        jax.ShapeDtypeStruct((t // CMP_STRIDE, CMP_STRIDE * NSA_KV), F32),
        jax.ShapeDtypeStruct((t // CMP_STRIDE, CMP_STRIDE * NSA_KV), F32),
        jax.ShapeDtypeStruct((NSA_KV_HEADS, t, KSEL), BF16),
        jax.ShapeDtypeStruct((NSA_KV_HEADS, NSA_HEAD_DIM, t), BF16),
        jax.ShapeDtypeStruct((NSA_KV_HEADS, t, NSA_HEAD_DIM), BF16),
        jax.ShapeDtypeStruct((NSA_KV_HEADS, NSA_HEAD_DIM, t), BF16),
        jax.ShapeDtypeStruct((t, TAIL), F32),
        jax.ShapeDtypeStruct((TAIL, t), F32),
    )
    kvh = pl.BlockSpec((NSA_KV_HEADS, tm, NSA_HEAD_DIM), heads)
    kvh_t = pl.BlockSpec((NSA_KV_HEADS, NSA_HEAD_DIM, tm), heads_t)
    out_specs = (
        pl.BlockSpec((tm, SSM_D_INNER), row),
        pl.BlockSpec((tm, SSM_XBC), row),
        pl.BlockSpec((NSA_HEADS, NSA_HEAD_DIM, tm), heads_t),
        pl.BlockSpec((tm // CMP_STRIDE, CMP_STRIDE * NSA_KV), row),
        pl.BlockSpec((tm // CMP_STRIDE, CMP_STRIDE * NSA_KV), row),
        pl.BlockSpec((NSA_KV_HEADS, tm, KSEL), heads), kvh_t, kvh, kvh_t,
        pl.BlockSpec((tm, TAIL), row),
        pl.BlockSpec((TAIL, tm), lambda i: (0, i)),
    )
    in_specs = [
        pl.BlockSpec((tm, D_MODEL), row),
        pl.BlockSpec((1, D_MODEL), const),
        pl.BlockSpec((D_MODEL, SSM_D_INNER + SSM_XBC), const),
        pl.BlockSpec((D_MODEL, NSA_Q + 6 * NSA_KV), const),
        pl.BlockSpec((D_MODEL, TAIL), const),
        pl.BlockSpec((tm, LANES), lambda i: (i % nseq, 0)),
        pl.BlockSpec((tm, LANES), lambda i: (i % nseq, 0)),
    ]
    return pl.pallas_call(
        functools.partial(_inproj_kernel, tiles_per_seq=nseq), grid=(t // tm,), in_specs=in_specs,
        out_specs=out_specs,
        out_shape=out_shape,
        scratch_shapes=[pltpu.VMEM((NSA_KV // LANES, tm, LANES), F32),
                        pltpu.VMEM((NSA_KV // LANES, tm, LANES), F32)],
        compiler_params=_cparams(("arbitrary",)), name="in_proj",
    )(x2, norm_g, *w_parts, cos_t, sin_t)


def _ssd_kernel(xbc_ref, zs_ref, tail_ref, tailt_ref, cw_ref, cb_ref, dtb_ref, dtbt_ref,
                alog_ref, alogt_ref, dskip_ref, ng_ref,
                y_ref, xcat_ref, state_ref, ybuf_ref):
    L, P, N, H, G = SSM_CHUNK, SSM_HEAD_DIM, SSM_STATE, SSM_HEADS, SSM_GROUPS
    HG = H // G
    halo = 8

    @pl.when(pl.program_id(1) == 0)
    def _():
        xcat_ref[0:halo, :] = jnp.zeros((halo, SSM_XBC), F32)
        state_ref[...] = jnp.zeros_like(state_ref)

    xcat_ref[halo:halo + L, :] = xbc_ref[...]
    xcat = xcat_ref[...]
    conv = cb_ref[...] + cw_ref[SSM_CONV - 1:SSM_CONV, :] * xcat[halo:halo + L]
    for k in range(SSM_CONV - 1):
        shifted = pltpu.roll(xcat, SSM_CONV - 1 - k, 0)
        conv = conv + cw_ref[k:k + 1, :] * shifted[halo:halo + L]
    xcat_ref[0:halo, :] = xcat[L:L + halo]
    u = _silu(conv)
    xs = u[:, :SSM_D_INNER]
    bm = u[:, SSM_D_INNER:SSM_D_INNER + G * N]
    cm = u[:, SSM_D_INNER + G * N:]

    def softplus(v):
        return jnp.maximum(v, 0.0) + jnp.log1p(jnp.exp(-jnp.abs(v)))

    dt = softplus(tail_ref[:, 0:H] + dtb_ref[...])
    dtt = softplus(tailt_ref[0:H, :] + dtbt_ref[...])
    da = dt * (-jnp.exp(alog_ref[...]))
    dat = dtt * (-jnp.exp(alogt_ref[...]))
    ri = lax.broadcasted_iota(jnp.int32, (L, L), 0)
    ci = lax.broadcasted_iota(jnp.int32, (L, L), 1)
    tri = ci <= ri
    hi = lax.Precision.HIGHEST
    acs = jnp.dot(tri.astype(F32), da, precision=hi, preferred_element_type=F32)
    acst = jnp.dot(dat, (ri <= ci).astype(F32), precision=hi, preferred_element_type=F32)
    last = acs[L - 1:L, :]
    w_state = dt * jnp.exp(last - acs)
    eacs = jnp.exp(acs)
    cdec = jnp.exp(last)

    hrow = lax.broadcasted_iota(jnp.int32, (H, SSM_D_INNER), 0)
    hcol = lax.broadcasted_iota(jnp.int32, (H, SSM_D_INNER), 1)
    expand = jnp.where(hcol // P == hrow, 1.0, 0.0).astype(BF16)

    def split3(v):
        v_hi = v.astype(BF16)
        r1 = v - v_hi.astype(F32)
        v_mid = r1.astype(BF16)
        v_lo = (r1 - v_mid.astype(F32)).astype(BF16)
        return jnp.concatenate([v_hi, v_mid, v_lo], axis=1)

    small = jnp.concatenate([cdec, dskip_ref[...], jnp.zeros((6, H), F32)], axis=0)
    per_head = jnp.concatenate([dt, w_state, eacs, small], axis=0)
    spread = jnp.dot(split3(per_head), jnp.concatenate([expand] * 3, axis=0), preferred_element_type=F32)
    cdec_e = spread[3 * L:3 * L + 1, :]
    dskip_e = spread[3 * L + 1:3 * L + 2, :]
    xdt = (xs * spread[0:L]).astype(BF16)
    wst = (xs * spread[L:2 * L]).astype(BF16)
    eacs_e = spread[2 * L:3 * L]

    for g in range(G):
        bm_g = bm[:, g * N:(g + 1) * N]
        cm_g = cm[:, g * N:(g + 1) * N].astype(BF16)
        cb = lax.dot_general(cm_g, bm_g.astype(BF16), NT_DIMS, preferred_element_type=F32)
        cols = slice(g * HG * P, (g + 1) * HG * P)
        st = state_ref[:, cols]
        y_off = jnp.dot(cm_g, st.astype(BF16), preferred_element_type=F32) * eacs_e[:, cols]
        ybuf_ref[:, cols] = y_off
        bmt = bm_g.T.astype(BF16)
        state_ref[:, cols] = st * cdec_e[:, cols] + jnp.dot(bmt, wst[:, cols], preferred_element_type=F32)
        for r in range(HG):
            hh = g * HG + r
            diff = acs[:, hh:hh + 1] - acst[hh:hh + 1, :]
            seg = jnp.exp(jnp.where(tri, diff, -jnp.inf))
            lmat = (cb * seg).astype(BF16)
            hc = slice(hh * P, (hh + 1) * P)
            ybuf_ref[:, hc] = ybuf_ref[:, hc] + jnp.dot(lmat, xdt[:, hc], preferred_element_type=F32)

    y = (ybuf_ref[...] + xs * dskip_e) * zs_ref[...]
    gw = SSM_D_INNER // G
    parts = []
    for g in range(G):
        yg = y[:, g * gw:(g + 1) * gw]
        parts.append(yg * lax.rsqrt(jnp.mean(yg * yg, axis=-1, keepdims=True) + SSM_NORM_EPS))
    y_ref[...] = (jnp.concatenate(parts, axis=1) * ng_ref[...]).astype(y_ref.dtype)


def _ssd(xbc, zs, tail, tailt, conv_w, conv_b, dt_bias, a_log, d_skip, norm_g, batch, seq):
    t = xbc.shape[0]
    L = SSM_CHUNK
    nc = seq // L
    row = lambda b, c: (b * nc + c, 0)
    const = lambda b, c: (0, 0)
    H = SSM_HEADS
    in_specs = [
        pl.BlockSpec((L, SSM_XBC), row),
        pl.BlockSpec((L, SSM_D_INNER), row),
        pl.BlockSpec((L, TAIL), row),
        pl.BlockSpec((TAIL, L), lambda b, c: (0, b * nc + c)),
        pl.BlockSpec((SSM_CONV, SSM_XBC), const),
        pl.BlockSpec((1, SSM_XBC), const),
        pl.BlockSpec((1, H), const),
        pl.BlockSpec((H, 1), const),
        pl.BlockSpec((1, H), const),
        pl.BlockSpec((H, 1), const),
        pl.BlockSpec((1, H), const),
        pl.BlockSpec((1, SSM_D_INNER), const),
    ]
    return pl.pallas_call(
        _ssd_kernel, grid=(batch, nc), in_specs=in_specs,
        out_specs=pl.BlockSpec((L, SSM_D_INNER), row),
        out_shape=jax.ShapeDtypeStruct((t, SSM_D_INNER), BF16),
        scratch_shapes=[pltpu.VMEM((L + 8, SSM_XBC), F32),
                        pltpu.VMEM((SSM_STATE, SSM_D_INNER), F32),
                        pltpu.VMEM((L, SSM_D_INNER), F32)],
        compiler_params=_cparams(("arbitrary", "arbitrary")), name="ssd",
    )(xbc, zs, tail, tailt, conv_w, conv_b, dt_bias.reshape(1, H), dt_bias.reshape(H, 1),
      a_log.reshape(1, H), a_log.reshape(H, 1), d_skip.reshape(1, H), norm_g.reshape(1, -1))


def _compress_kernel(k_ref, v_ref, kpos_ref, vpos_ref, kw1_ref, vw1_ref, kb1_ref, vb1_ref,
                     kw2_ref, vw2_ref, kc_ref, vct_ref):
    D = NSA_HEAD_DIM
    gw = CMP_STRIDE * D

    def hidden(t_ref, pos_ref, w1_ref, b1_ref, g):
        t = t_ref[:, g * gw:(g + 1) * gw]
        n = t.shape[0]
        lo = jnp.dot((t + pos_ref[0:1, :]).astype(BF16), w1_ref[...], preferred_element_type=F32)
        hi = jnp.dot((t + pos_ref[1:2, :]).astype(BF16), w1_ref[...], preferred_element_type=F32)
        pre = lo + pltpu.roll(pltpu.roll(hi, n - 1, 0), D, 1)
        return _silu(pre[:, 0:D] + b1_ref[...]).astype(BF16)

    for g in range(NSA_KV_HEADS):
        kc = jnp.dot(hidden(k_ref, kpos_ref, kw1_ref, kb1_ref, g), kw2_ref[...], preferred_element_type=F32)
        kc_ref[g] = kc.astype(BF16)
        vct = lax.dot_general(vw2_ref[...], hidden(v_ref, vpos_ref, vw1_ref, vb1_ref, g), NT_DIMS,
                              preferred_element_type=F32)
        vct_ref[g] = vct.astype(BF16)


def _compress_weights(pos, w1, b1, w2):
    half = CMP_BLOCK // 2 * NSA_HEAD_DIM
    w1cat = jnp.concatenate([w1[:half], w1[half:]], axis=1).astype(BF16)
    return pos.reshape(2, half), w1cat, b1.reshape(1, -1), w2.astype(BF16)


def _compress(k16, v16, kparams, vparams, batch, seq):
    rows = seq // CMP_STRIDE
    width = CMP_STRIDE * NSA_KV
    D = NSA_HEAD_DIM
    gw = CMP_STRIDE * D
    kpos, kw1, kb1, kw2 = _compress_weights(*kparams)
    vpos, vw1, vb1, vw2 = _compress_weights(*vparams)
    vw2 = vw2.T
    c2 = lambda b: (0, 0)
    tok = pl.BlockSpec((rows, width), lambda b: (b, 0))
    in_specs = [tok, tok,
                pl.BlockSpec((2, gw), c2), pl.BlockSpec((2, gw), c2),
                pl.BlockSpec((gw, 2 * D), c2), pl.BlockSpec((gw, 2 * D), c2),
                pl.BlockSpec((1, D), c2), pl.BlockSpec((1, D), c2),
                pl.BlockSpec((D, D), c2), pl.BlockSpec((D, D), c2)]
    out = jax.ShapeDtypeStruct((NSA_KV_HEADS, batch * rows, D), BF16)
    out_t = jax.ShapeDtypeStruct((NSA_KV_HEADS, D, batch * rows), BF16)
    ospec = pl.BlockSpec((NSA_KV_HEADS, rows, D), lambda b: (0, b, 0))
    ospec_t = pl.BlockSpec((NSA_KV_HEADS, D, rows), lambda b: (0, 0, b))
    return pl.pallas_call(
        _compress_kernel, grid=(batch,), in_specs=in_specs, out_specs=(ospec, ospec_t),
        out_shape=(out, out_t), compiler_params=_cparams(("arbitrary",)), name="compress",
    )(k16, v16, kpos, vpos, kw1, vw1, kb1, vb1, kw2, vw2)


def _nsa_kernel(qt_ref, kc_ref, vct_ref, ks_ref, vst_ref, kw_ref, vwt_ref, tailt_ref, ovt_ref, mtab_ref, cmask_ref,
                o_ref, biasq_ref, m_ref, acc_ref, part_ref, sbuf0_ref, sbuf1_ref, mbuf0_ref, mbuf1_ref):
    R, D = NSA_REP, NSA_HEAD_DIM
    nl = R * TQ
    grp = range(kc_ref.shape[0])
    sbuf_refs = (sbuf0_ref, sbuf1_ref)
    mbuf_refs = (mbuf0_ref, mbuf1_ref)
    n_stages = (ks_ref.shape[1] // TK - 1) // 2
    gp = pl.program_id(1)
    qi = pl.program_id(2)
    t0 = qi * TQ
    qt = [jnp.concatenate([qt_ref[gi * R + r] for r in range(R)], axis=1) for gi in grp]
    tpos_row = t0 + lax.broadcasted_iota(jnp.int32, (1, nl), 1) % TQ

    def k_rows(k_ref, gi, start):
        return k_ref[gi, pl.ds(pl.multiple_of(start, TK), TK), :]

    def vt_cols(vt_ref, gi, start, n):
        return vt_ref[gi, :, pl.ds(pl.multiple_of(start, TK), n)]

    def pv(vt, p):
        vt1 = jnp.concatenate([vt, jnp.ones((ONES_ROWS, vt.shape[1]), BF16)], axis=0)
        return jnp.dot(vt1, p, preferred_element_type=F32)

    def mask_tile(off):
        return jnp.concatenate([mtab_ref[pl.ds(pl.multiple_of(off, TK), TK), :]] * R, axis=1)

    def col_max(s):
        return jnp.max(s, axis=0, keepdims=True)

    def online(state, s, m_chunk, vt):
        m_new = m_chunk if state is None else jnp.maximum(state[0], m_chunk)
        contrib = pv(vt, jnp.exp2(s - m_new).astype(BF16))
        if state is None:
            return m_new, contrib
        return m_new, jnp.exp2(state[0] - m_new) * state[1] + contrib

    causal = mask_tile(MT_CAUSAL)
    far0 = jnp.maximum(t0 - 2 * TK, 0)
    mid0 = jnp.maximum(t0 - TK, 0)
    window, o_c, o_w, diag = {}, {}, {}, {}

    def phase_window(gi):
        s_dia = jnp.dot(k_rows(kw_ref, gi, t0), qt[gi], preferred_element_type=F32) + causal
        m_dia = col_max(s_dia)
        s_mid = (jnp.dot(k_rows(kw_ref, gi, mid0), qt[gi], preferred_element_type=F32)
                 + mask_tile(jnp.where(qi >= 1, MT_ALL, MT_NONE)))
        m_mid = col_max(s_mid)
        s_far = (jnp.dot(k_rows(kw_ref, gi, far0), qt[gi], preferred_element_type=F32)
                 + mask_tile(jnp.where(qi >= 2, MT_FAR, MT_NONE)))
        m_far = col_max(s_far)
        window[gi] = ((s_dia, m_dia, t0), (s_mid, m_mid, mid0), (s_far, m_far, far0))

    ncr = kc_ref.shape[1]
    n_slc = ovt_ref.shape[0]
    cmask = cmask_ref[pl.ds(pl.multiple_of(qi * ncr, ncr), ncr), :]
    cmask = jnp.concatenate([cmask] * R, axis=1)
    sees_any = tpos_row >= CMP_BLOCK - 1
    jj = lax.broadcasted_iota(jnp.int32, (n_slc, TQ), 0)
    tt = t0 + lax.broadcasted_iota(jnp.int32, (n_slc, TQ), 1)
    lag = tt // SLC_BLOCK - jj
    forced = (jj == 0) | ((lag >= 0) & (lag < N_LOCAL_BLOCKS))
    valid = jj * SLC_BLOCK <= tt
    rows8 = 8
    j8 = lax.broadcasted_iota(jnp.int32, (rows8, TQ), 0)
    per = TK // SLC_BLOCK
    zrows = jnp.zeros((BIAS_ROWS - per, TQ), F32)

    def phase_rank(gi):
        s_m = jnp.dot(kc_ref[gi], qt[gi], preferred_element_type=F32) + cmask
        mx = jnp.max(s_m, axis=0, keepdims=True)
        p = jnp.exp2(s_m - mx)
        den = jnp.sum(p, axis=0, keepdims=True)
        pc = p * jnp.where(sees_any, 1.0 / den, 0.0)
        o_c[gi] = jnp.dot(vct_ref[gi], pc.astype(BF16), preferred_element_type=F32)

        psum = pc[:, 0:TQ]
        for r in range(1, R):
            psum = psum + pc[:, r * TQ:(r + 1) * TQ]
        imp_t = jnp.dot(ovt_ref[...], psum, precision=lax.Precision.HIGHEST,
                        preferred_element_type=F32)
        score = jnp.where(forced, FORCED_SCORE, jnp.where(valid, imp_t, -1.0))
        groups = [score[a:a + rows8] for a in range(0, n_slc, rows8)]
        ranks = [jnp.zeros((rows8, TQ), F32) for _ in groups]
        window_steps = dict(zip((0, n_slc // 3, 2 * n_slc // 3), window[gi]))
        win = None
        for j2 in range(n_slc):
            if j2 in window_steps:
                s_w, m_w, start_w = window_steps[j2]
                win = online(win, s_w, m_w, vt_cols(vwt_ref, gi, start_w, TK))
            sj = score[j2:j2 + 1, :]
            for a, sg in enumerate(groups):
                lo = a * rows8
                if lo + rows8 - 1 < j2:
                    beats = sj > sg
                elif lo > j2:
                    beats = sj >= sg
                else:
                    beats = (sj > sg) | ((sj == sg) & (j8 + lo > j2))
                ranks[a] = ranks[a] + jnp.where(beats, 1.0, 0.0)
        rank = jnp.concatenate(ranks, axis=0)
        selected = (rank < float(min(SLC_TOP_N, n_slc))) & (score >= 0.0)
        bias_t = jnp.where(selected, 0.0, NEG_BIG)
        for kb in range(n_slc // per):
            blk = jnp.concatenate([bias_t[kb * per:(kb + 1) * per], zrows], axis=0)
            biasq_ref[gi, kb * BIAS_ROWS:(kb + 1) * BIAS_ROWS, :] = (
                jnp.concatenate([blk] * R, axis=1).astype(BF16))
        accw = win[1]
        o_w[gi] = accw[0:D] / accw[D:D + 1]

    qpad = jnp.zeros((KSEL - D - BIAS_ROWS, nl), BF16)

    def q_sel(gi, kb):
        rows = biasq_ref[gi, pl.ds(pl.multiple_of(kb * BIAS_ROWS, BIAS_ROWS), BIAS_ROWS), :]
        return jnp.concatenate([qt[gi], rows, qpad], axis=0)

    def slc_scores(gi, kb):
        return jnp.dot(k_rows(ks_ref, gi, kb * TK), q_sel(gi, kb), preferred_element_type=F32)

    def fill(gi, slot, chunks):
        for c, kb in enumerate(chunks):
            s = slc_scores(gi, kb)
            sbuf_refs[slot][gi, c * TK:(c + 1) * TK, :] = s
            mbuf_refs[slot][gi, c:c + 1, :] = col_max(s)

    npairs = qi // 2

    def phase_selq(gi):
        s_sel = slc_scores(gi, qi) + causal
        diag[gi] = (s_sel, col_max(s_sel))
        fill(gi, 0, (0, 1))

    def gates(gi):
        gate_row = pl.multiple_of(GATE_OFF + GATE_SLOT * (gp * len(grp) + gi), 8)
        sig = jax.nn.sigmoid(tailt_ref[pl.ds(gate_row, GATE_SLOT), :])
        return [jnp.concatenate([sig[3 * r + c:3 * r + c + 1, :] for r in range(R)], axis=1)
                for c in range(3)]

    def phase_part(gi):
        gate = gates(gi)
        part_ref[gi] = gate[0] * o_c[gi] + gate[2] * o_w[gi]

    def phase_seldiag(gi):
        m_sel, acc_sel = online(None, diag[gi][0], diag[gi][1], vt_cols(vst_ref, gi, t0, TK))
        m_ref[gi] = m_sel
        acc_ref[gi] = acc_sel

    n_grp = len(grp)
    phase_window(0)
    for gi in range(n_grp + 2):
        if gi + 1 < n_grp:
            phase_window(gi + 1)
        if gi < n_grp:
            phase_rank(gi)
        if 0 <= gi - 1 < n_grp:
            phase_selq(gi - 1)
            phase_part(gi - 1)
        if 0 <= gi - 2 < n_grp:
            phase_seldiag(gi - 2)

    def absorb(gi, slot, first, n):
        state = (m_ref[gi], acc_ref[gi])
        for c in range(n):
            state = online(state, sbuf_refs[slot][gi, c * TK:(c + 1) * TK, :], mbuf_refs[slot][gi, c:c + 1, :],
                           vt_cols(vst_ref, gi, (first + c) * TK, TK))
        m_ref[gi] = state[0]
        acc_ref[gi] = state[1]

    for k in range(n_stages):
        @pl.when(npairs > k)
        def _(k=k):
            for gi in grp:
                fill(gi, (k + 1) % 2, (jnp.minimum(2 * k + 2, qi - 1), jnp.minimum(2 * k + 3, qi - 1)))
            for gi in grp:
                absorb(gi, k % 2, 2 * k, 2)

    for parity in range(2):
        @pl.when((qi % 2 == 1) & (npairs % 2 == parity))
        def _(parity=parity):
            for gi in grp:
                absorb(gi, parity, qi - 1, 1)

    for gi in grp:
        acc = acc_ref[gi]
        ot = part_ref[gi] + gates(gi)[1] * (acc[0:D] / acc[D:D + 1])
        stacked = jnp.concatenate([ot[:, r * TQ:(r + 1) * TQ] for r in range(R)], axis=0)
        o_ref[:, gi * R * D:(gi + 1) * R * D] = stacked.T.astype(o_ref.dtype)


def _overlap_t(seq):
    n_cmp = (seq - CMP_BLOCK) // CMP_STRIDE + 1
    n_slc = seq // SLC_BLOCK
    cs = np.arange(n_cmp) * CMP_STRIDE
    ss = np.arange(n_slc) * SLC_BLOCK
    overlap = np.clip(np.minimum(cs[:, None] + CMP_BLOCK, ss[None, :] + SLC_BLOCK)
                      - np.maximum(cs[:, None], ss[None, :]), 0, None) / CMP_BLOCK
    ovt = np.zeros((n_slc, n_cmp + 1), np.float32)
    ovt[:, :n_cmp] = overlap.T
    return jnp.asarray(ovt)


def _mask_tiles():
    ki = np.arange(TK)[:, None]
    qi = np.arange(TQ)[None, :]
    neg = np.float32(NEG_BIG)
    none = np.full((TK, TQ), neg, np.float32)
    far = np.where(ki > qi, np.float32(0), neg)
    full = np.zeros((TK, TQ), np.float32)
    causal = np.where(ki <= qi, np.float32(0), neg)
    return jnp.asarray(np.concatenate([none, far, full, causal], axis=0))


def _cmp_masks(seq):
    ncr = seq // CMP_STRIDE
    n_cmp = (seq - CMP_BLOCK) // CMP_STRIDE + 1
    n = np.arange(ncr)[None, :, None]
    t = (np.arange(seq // TQ)[:, None, None] * TQ + np.arange(TQ)[None, None, :])
    visible = (n * CMP_STRIDE + CMP_BLOCK - 1 <= t) & (n < n_cmp)
    return jnp.asarray(np.where(visible, np.float32(0), np.float32(NEG_BIG)).reshape(-1, TQ))


def _nsa(qt, kc, vct, ks, vst, kw, vwt, tailt, batch, seq):
    t = batch * seq
    G, R, D = NSA_KV_HEADS, NSA_REP, NSA_HEAD_DIM
    nq = seq // TQ
    ncr = seq // CMP_STRIDE
    n_slc = seq // SLC_BLOCK
    assert TQ == TK and WINDOW == 2 * TK, "window branch visits exactly three key chunks"
    assert TK // SLC_BLOCK <= BIAS_ROWS and D + BIAS_ROWS <= KSEL
    P = NSA_GROUPS_PER_STEP
    const = lambda b, g, i: (0, 0)
    vtspec = pl.BlockSpec((P, D, seq), lambda b, g, i: (g, 0, b))
    in_specs = [
        pl.BlockSpec((P * R, D, TQ), lambda b, g, i: (g, 0, b * nq + i)),
        pl.BlockSpec((P, ncr, D), lambda b, g, i: (g, b, 0)),
        pl.BlockSpec((P, D, ncr), lambda b, g, i: (g, 0, b)),
        pl.BlockSpec((P, seq, KSEL), lambda b, g, i: (g, b, 0)),
        vtspec,
        pl.BlockSpec((P, seq, D), lambda b, g, i: (g, b, 0)),
        vtspec,
        pl.BlockSpec((TAIL, TQ), lambda b, g, i: (0, b * nq + i)),
        pl.BlockSpec((n_slc, ncr), const),
        pl.BlockSpec((4 * TK, TQ), const),
        pl.BlockSpec((nq * ncr, TQ), const),
    ]
    return pl.pallas_call(
        _nsa_kernel, grid=(batch, G // P, nq), in_specs=in_specs,
        out_specs=pl.BlockSpec((TQ, P * R * D), lambda b, g, i: (b * nq + i, g)),
        out_shape=jax.ShapeDtypeStruct((t, NSA_Q), BF16),
        scratch_shapes=[pltpu.VMEM((P, seq // TK * BIAS_ROWS, R * TQ), BF16),
                        pltpu.VMEM((P, 1, R * TQ), F32),
                        pltpu.VMEM((P, D + ONES_ROWS, R * TQ), F32),
                        pltpu.VMEM((P, D, R * TQ), F32),
                        pltpu.VMEM((P, 2 * TK, R * TQ), F32),
                        pltpu.VMEM((P, 2 * TK, R * TQ), F32),
                        pltpu.VMEM((P, 8, R * TQ), F32),
                        pltpu.VMEM((P, 8, R * TQ), F32)],
        compiler_params=_cparams(("arbitrary", "arbitrary", "arbitrary")), name="nsa",
    )(qt, kc, vct, ks, vst, kw, vwt, tailt, _overlap_t(seq), _mask_tiles(), _cmp_masks(seq))


def _outproj_kernel(y_ref, o_ref, x_ref, w_ref, ag_ref, ng_ref, x1_ref, h2_ref):
    yn = _rms(o_ref[...].astype(F32), ag_ref[...], NORM_EPS).astype(BF16)
    x1 = (x_ref[...]
          + jnp.dot(y_ref[...], w_ref[0:SSM_D_INNER, :], preferred_element_type=F32)
          + jnp.dot(yn, w_ref[SSM_D_INNER:, :], preferred_element_type=F32))
    x1_ref[...] = x1
    h2_ref[...] = _rms(x1, ng_ref[...], NORM_EPS).astype(BF16)


def _out_proj(y_ssm, o_nsa, x2, w_out, attn_g, norm2_g):
    t = x2.shape[0]
    tm = TM_OUT_PROJ
    row = lambda i: (i, 0)
    const = lambda i: (0, 0)
    tok = pl.BlockSpec((tm, D_MODEL), row)
    vec = pl.BlockSpec((1, D_MODEL), const)
    return pl.pallas_call(
        _outproj_kernel, grid=(t // tm,),
        in_specs=[tok, tok, tok, pl.BlockSpec((SSM_D_INNER + NSA_Q, D_MODEL), const), vec, vec],
        out_specs=(tok, tok),
        out_shape=(jax.ShapeDtypeStruct((t, D_MODEL), F32), jax.ShapeDtypeStruct((t, D_MODEL), BF16)),
        compiler_params=_cparams(("arbitrary",)), name="out_proj",
    )(y_ssm, o_nsa, x2, w_out, attn_g, norm2_g)


def _ffn_kernel(h_ref, halo_ref, x1_ref, wup_ref, cw_ref, cb_ref, wd_ref, fg_ref, out_ref, act_ref,
                *, tiles_per_seq):
    i = pl.program_id(0)
    tm = h_ref.shape[0]
    pad = halo_ref.shape[0]
    tn = TN_FFN
    halo = halo_ref[...]
    halo = jnp.where(i % tiles_per_seq == 0, jnp.zeros_like(halo), halo)
    hc = jnp.concatenate([halo, h_ref[...]], axis=0)

    def branch(c0):
        cols = slice(c0, c0 + tn)
        u = jnp.dot(hc, wup_ref[:, cols], preferred_element_type=F32)
        out = cb_ref[:, cols] + cw_ref[FFN_CONV - 1:FFN_CONV, cols] * u[pad:pad + tm, :]
        for k in range(FFN_CONV - 1):
            shifted = pltpu.roll(u, FFN_CONV - 1 - k, 0)
            out = out + cw_ref[k:k + 1, cols] * shifted[pad:pad + tm, :]
        return out

    for j in range(D_FF // tn):
        act = _silu(branch(j * tn)) * branch(D_FF + j * tn)
        act_ref[:, j * tn:(j + 1) * tn] = act.astype(BF16)
    down = jnp.dot(act_ref[...], wd_ref[...], preferred_element_type=F32)
    out_ref[...] = _rms(x1_ref[...] + down, fg_ref[...], NORM_EPS)


def _ffn(h2, x1, w_up, conv_w, conv_b, w_down, final_g, seq):
    t = h2.shape[0]
    tm = TM_FFN
    pad = BF16_SUBLANES
    tok = pl.BlockSpec((tm, D_MODEL), lambda i: (i, 0))

    def resident(shape):
        return pl.BlockSpec(shape, lambda i: (0, 0), pipeline_mode=pl.Buffered(1))

    in_specs = [
        tok,
        pl.BlockSpec((pad, D_MODEL), lambda i: (jnp.maximum(i * (tm // pad) - 1, 0), 0)),
        tok,
        resident((D_MODEL, 2 * D_FF)),
        resident((FFN_CONV, 2 * D_FF)),
        resident((1, 2 * D_FF)),
        resident((D_FF, D_MODEL)),
        resident((1, D_MODEL)),
    ]
    return pl.pallas_call(
        functools.partial(_ffn_kernel, tiles_per_seq=seq // tm), grid=(t // tm,),
        in_specs=in_specs, out_specs=tok,
        out_shape=jax.ShapeDtypeStruct((t, D_MODEL), F32),
        scratch_shapes=[pltpu.VMEM((tm, D_FF), BF16)],
        compiler_params=_cparams(("arbitrary",)), name="ffn",
    )(h2, h2, x1, w_up, conv_w, conv_b, w_down, final_g)


def _rope_tables(seq):
    half = NSA_HEAD_DIM // 2
    inv_freq = 1.0 / (ROPE_THETA ** (jnp.arange(0, NSA_HEAD_DIM, 2, dtype=F32) / NSA_HEAD_DIM))
    ang = jnp.arange(seq).astype(F32)[:, None] * inv_freq[None, :]
    cos, sin = jnp.cos(ang), jnp.sin(ang)
    reps = LANES // NSA_HEAD_DIM
    cos_t = jnp.tile(jnp.concatenate([cos, cos], axis=1), (1, reps))
    sin_t = jnp.tile(jnp.concatenate([-sin, sin], axis=1), (1, reps))
    return cos_t, sin_t


def _split_w_in(w):
    o_z, o_xbc, o_dt, o_q, o_kv, o_g = np.cumsum([0, SSM_D_INNER, SSM_XBC, SSM_HEADS, NSA_Q, 6 * NSA_KV]).tolist()
    per = 3 * NSA_REP
    tail_cols = [w[:, o_dt:o_q]]
    for g in range(NSA_KV_HEADS):
        tail_cols += [w[:, o_g + g * per:o_g + (g + 1) * per], jnp.zeros((w.shape[0], GATE_SLOT - per), w.dtype)]
    tail_cols.append(jnp.zeros((w.shape[0], TAIL - GATE_OFF - NSA_KV_HEADS * GATE_SLOT), w.dtype))
    return (w[:, o_z:o_dt].astype(BF16), w[:, o_q:o_g].astype(BF16),
            jnp.concatenate(tail_cols, axis=1).astype(BF16))


def kernel(x, norm1_g, w_in, ssm_conv_w, ssm_conv_b, ssm_dt_bias, ssm_a_log, ssm_d, ssm_norm_g, cmp_k_pos, cmp_k_w1, cmp_k_b1, cmp_k_w2, cmp_v_pos, cmp_v_w1, cmp_v_b1, cmp_v_w2, attn_norm_g, w_out, norm2_g, ffn_w_up, ffn_conv_w, ffn_conv_b, ffn_w_down, final_norm_g):
    batch, seq, d = x.shape
    assert w_in.shape[0] == 1, "single-layer problem"
    l = 0
    cos_t, sin_t = _rope_tables(seq)
    x2 = x.reshape(batch * seq, d)
    zs, xbc, qt, kc_r, vc_r, ks, vst, kw, vwt, tail, tailt = _in_proj(
        x2, norm1_g[l].reshape(1, d), _split_w_in(w_in.reshape(d, -1)), cos_t, sin_t, seq)
    y_ssm = _ssd(xbc, zs, tail, tailt, ssm_conv_w[l], ssm_conv_b[l].reshape(1, -1), ssm_dt_bias[l],
                 ssm_a_log[l], ssm_d[l], ssm_norm_g[l], batch, seq)
    kc, vct = _compress(kc_r, vc_r,
                        (cmp_k_pos[l], cmp_k_w1[l], cmp_k_b1[l], cmp_k_w2[l]),
                        (cmp_v_pos[l], cmp_v_w1[l], cmp_v_b1[l], cmp_v_w2[l]), batch, seq)
    o_nsa = _nsa(qt, kc, vct, ks, vst, kw, vwt, tailt, batch, seq)
    x1, h2 = _out_proj(y_ssm, o_nsa, x2, w_out.reshape(-1, d).astype(BF16), attn_norm_g[l].reshape(1, d),
                       norm2_g[l].reshape(1, d))
    out = _ffn(h2, x1, ffn_w_up.reshape(d, -1).astype(BF16), ffn_conv_w[l], ffn_conv_b[l].reshape(1, -1),
               ffn_w_down.reshape(-1, d).astype(BF16), final_norm_g.reshape(1, d), seq)
    return out.reshape(batch, seq, d)
```

```python
import functools
import math

import numpy as np
import jax
import jax.numpy as jnp
from jax import lax
from jax.experimental import pallas as pl
from jax.experimental.pallas import tpu as pltpu

F32 = jnp.float32
BF16 = jnp.bfloat16

D_MODEL = 1024
SSM_D_INNER = 1024
SSM_HEAD_DIM = 64
SSM_HEADS = 16
SSM_GROUPS = 2
SSM_STATE = 128
SSM_CONV = 4
SSM_CHUNK = 128
SSM_XBC = SSM_D_INNER + 2 * SSM_GROUPS * SSM_STATE
SSM_NORM_EPS = 1e-5
NSA_HEADS = 16
NSA_KV_HEADS = 4
NSA_REP = NSA_HEADS // NSA_KV_HEADS
NSA_HEAD_DIM = 64
NSA_Q = NSA_HEADS * NSA_HEAD_DIM
NSA_KV = NSA_KV_HEADS * NSA_HEAD_DIM
CMP_BLOCK = 32
CMP_STRIDE = 16
SLC_BLOCK = 64
SLC_TOP_N = 16
N_LOCAL_BLOCKS = 2
FORCED_SCORE = 1e4
WINDOW = 512
ROPE_THETA = 10000.0
D_FF = 2816
FFN_CONV = 3
NORM_EPS = 1e-6
NEG_BIG = -1e30

LANES = 128
BF16_SUBLANES = 16
TAIL = LANES
GATE_OFF = SSM_HEADS
GATE_SLOT = 16

TM_IN_PROJ = 256
TM_OUT_PROJ = 512
TQ = 256
TK = 256
NSA_GROUPS_PER_STEP = 4
ONES_ROWS = 16
KSEL = LANES
BIAS_ROWS = 16
MT_NONE, MT_FAR, MT_ALL, MT_CAUSAL = 0, TK, 2 * TK, 3 * TK
TM_FFN = 512
TN_FFN = 256
VMEM_LIMIT = 56 * 1024 * 1024

NT_DIMS = (((1,), (1,)), ((), ()))


def _cparams(sem):
    return pltpu.CompilerParams(dimension_semantics=sem, vmem_limit_bytes=VMEM_LIMIT)


def _rms(x, g, eps):
    return x * lax.rsqrt(jnp.mean(x * x, axis=-1, keepdims=True) + eps) * g


def _silu(x):
    return x * jax.nn.sigmoid(x)


def _inproj_kernel(x_ref, g_ref, wa_ref, wb_ref, wt_ref, cos_ref, sin_ref,
                   zs_ref, xbc_ref, qt_ref, kc_ref, vc_ref, ks_ref, vst_ref, kw_ref, vwt_ref,
                   tail_ref, tailt_ref, kbuf_ref, vbuf_ref, *, tiles_per_seq):
    tm = x_ref.shape[0]
    seq_tile = pl.program_id(0) % tiles_per_seq
    h = _rms(x_ref[...], g_ref[...], NORM_EPS).astype(BF16)

    def mm(w_ref, lo, hi):
        return jnp.dot(h, w_ref[:, lo:hi], preferred_element_type=F32)

    tail = mm(wt_ref, 0, TAIL)
    tail_ref[...] = tail
    tailt_ref[...] = tail.T

    cos = cos_ref[...]
    sin = sin_ref[...]
    lane = lax.broadcasted_iota(jnp.int32, (tm, LANES), 1)
    first_half = (lane % NSA_HEAD_DIM) < (NSA_HEAD_DIM // 2)

    def rope(xc):
        partner = jnp.where(first_half, pltpu.roll(xc, LANES - 32, 1), pltpu.roll(xc, 32, 1))
        return xc * cos + partner * sin

    q = mm(wb_ref, 0, NSA_Q)
    scale = NSA_HEAD_DIM ** -0.5 * math.log2(math.e)
    for c in range(NSA_Q // LANES):
        rt = (rope(q[:, c * LANES:(c + 1) * LANES]) * scale).T.astype(BF16)
        qt_ref[2 * c] = rt[:NSA_HEAD_DIM]
        qt_ref[2 * c + 1] = rt[NSA_HEAD_DIM:]

    kv = mm(wb_ref, NSA_Q, NSA_Q + 6 * NSA_KV)

    def seg(i):
        return kv[:, i * NSA_KV:(i + 1) * NSA_KV]

    def rope_seg(x):
        return jnp.concatenate([rope(x[:, :LANES]), rope(x[:, LANES:])], axis=1)

    def store_heads(ref, x):
        for g in range(NSA_KV_HEADS):
            ref[g] = x[:, g * NSA_HEAD_DIM:(g + 1) * NSA_HEAD_DIM].astype(BF16)

    def store_heads_t(ref, x):
        xt = x.T.astype(BF16)
        for g in range(NSA_KV_HEADS):
            ref[g] = xt[g * NSA_HEAD_DIM:(g + 1) * NSA_HEAD_DIM]

    kc_tile = rope_seg(seg(0))
    vc_tile = seg(1)
    for c in range(NSA_KV // LANES):
        kbuf_ref[c] = kc_tile[:, c * LANES:(c + 1) * LANES]
        vbuf_ref[c] = vc_tile[:, c * LANES:(c + 1) * LANES]
    gw = CMP_STRIDE * NSA_HEAD_DIM
    for l in range(CMP_STRIDE):
        for c in range(NSA_KV // LANES):
            k_rows = kbuf_ref[c, pl.ds(l, tm // CMP_STRIDE, stride=CMP_STRIDE), :]
            v_rows = vbuf_ref[c, pl.ds(l, tm // CMP_STRIDE, stride=CMP_STRIDE), :]
            for half in range(LANES // NSA_HEAD_DIM):
                g = c * (LANES // NSA_HEAD_DIM) + half
                src = slice(half * NSA_HEAD_DIM, (half + 1) * NSA_HEAD_DIM)
                dst = slice(g * gw + l * NSA_HEAD_DIM, g * gw + (l + 1) * NSA_HEAD_DIM)
                kc_ref[:, dst] = k_rows[:, src]
                vc_ref[:, dst] = v_rows[:, src]
    pos = seq_tile * tm + lax.broadcasted_iota(jnp.int32, (tm, NSA_HEAD_DIM), 0)
    block_in_chunk = (pos // SLC_BLOCK) % (TK // SLC_BLOCK)
    onehot = jnp.where(lax.broadcasted_iota(jnp.int32, (tm, NSA_HEAD_DIM), 1) == block_in_chunk, 1.0, 0.0)
    ks = rope_seg(seg(2))
    for g in range(NSA_KV_HEADS):
        kg = ks[:, g * NSA_HEAD_DIM:(g + 1) * NSA_HEAD_DIM]
        ks_ref[g] = jnp.concatenate([kg, onehot], axis=1).astype(BF16)
    store_heads_t(vst_ref, seg(3))
    store_heads(kw_ref, rope_seg(seg(4)))
    store_heads_t(vwt_ref, seg(5))

    zs_ref[...] = _silu(mm(wa_ref, 0, SSM_D_INNER))
    xbc_ref[...] = mm(wa_ref, SSM_D_INNER, SSM_D_INNER + SSM_XBC)


def _in_proj(x2, norm_g, w_parts, cos_t, sin_t, seq):
    t = x2.shape[0]
    tm = TM_IN_PROJ
    nseq = seq // tm
    row = lambda i: (i, 0)
    const = lambda i: (0, 0)
    heads = lambda i: (0, i, 0)
    heads_t = lambda i: (0, 0, i)
    out_shape = (
        jax.ShapeDtypeStruct((t, SSM_D_INNER), F32),
        jax.ShapeDtypeStruct((t, SSM_XBC), F32),
        jax.ShapeDtypeStruct((NSA_HEADS, NSA_HEAD_DIM, t), BF16),
        jax.ShapeDtypeStruct((t // CMP_STRIDE, CMP_STRIDE * NSA_KV), F32),
        jax.ShapeDtypeStruct((t // CMP_STRIDE, CMP_STRIDE * NSA_KV), F32),
        jax.ShapeDtypeStruct((NSA_KV_HEADS, t, KSEL), BF16),
        jax.ShapeDtypeStruct((NSA_KV_HEADS, NSA_HEAD_DIM, t), BF16),
        jax.ShapeDtypeStruct((NSA_KV_HEADS, t, NSA_HEAD_DIM), BF16),
        jax.ShapeDtypeStruct((NSA_KV_HEADS, NSA_HEAD_DIM, t), BF16),
        jax.ShapeDtypeStruct((t, TAIL), F32),
        jax.ShapeDtypeStruct((TAIL, t), F32),
    )
    kvh = pl.BlockSpec((NSA_KV_HEADS, tm, NSA_HEAD_DIM), heads)
    kvh_t = pl.BlockSpec((NSA_KV_HEADS, NSA_HEAD_DIM, tm), heads_t)
    out_specs = (
        pl.BlockSpec((tm, SSM_D_INNER), row),
        pl.BlockSpec((tm, SSM_XBC), row),
        pl.BlockSpec((NSA_HEADS, NSA_HEAD_DIM, tm), heads_t),
        pl.BlockSpec((tm // CMP_STRIDE, CMP_STRIDE * NSA_KV), row),
        pl.BlockSpec((tm // CMP_STRIDE, CMP_STRIDE * NSA_KV), row),
        pl.BlockSpec((NSA_KV_HEADS, tm, KSEL), heads), kvh_t, kvh, kvh_t,
        pl.BlockSpec((tm, TAIL), row),
        pl.BlockSpec((TAIL, tm), lambda i: (0, i)),
    )
    in_specs = [
        pl.BlockSpec((tm, D_MODEL), row),
        pl.BlockSpec((1, D_MODEL), const),
        pl.BlockSpec((D_MODEL, SSM_D_INNER + SSM_XBC), const),
        pl.BlockSpec((D_MODEL, NSA_Q + 6 * NSA_KV), const),
        pl.BlockSpec((D_MODEL, TAIL), const),
        pl.BlockSpec((tm, LANES), lambda i: (i % nseq, 0)),
        pl.BlockSpec((tm, LANES), lambda i: (i % nseq, 0)),
    ]
    return pl.pallas_call(
        functools.partial(_inproj_kernel, tiles_per_seq=nseq), grid=(t // tm,), in_specs=in_specs,
        out_specs=out_specs,
        out_shape=out_shape,
        scratch_shapes=[pltpu.VMEM((NSA_KV // LANES, tm, LANES), F32),
                        pltpu.VMEM((NSA_KV // LANES, tm, LANES), F32)],
        compiler_params=_cparams(("arbitrary",)), name="in_proj",
    )(x2, norm_g, *w_parts, cos_t, sin_t)


def _ssd_kernel(xbc_ref, zs_ref, tail_ref, tailt_ref, cw_ref, cb_ref, dtb_ref, dtbt_ref,
                alog_ref, alogt_ref, dskip_ref, ng_ref,
                y_ref, xcat_ref, state_ref, ybuf_ref):
    L, P, N, H, G = SSM_CHUNK, SSM_HEAD_DIM, SSM_STATE, SSM_HEADS, SSM_GROUPS
    HG = H // G
    halo = 8

    @pl.when(pl.program_id(1) == 0)
    def _():
        xcat_ref[0:halo, :] = jnp.zeros((halo, SSM_XBC), F32)
        state_ref[...] = jnp.zeros_like(state_ref)

    xcat_ref[halo:halo + L, :] = xbc_ref[...]
    xcat = xcat_ref[...]
    conv = cb_ref[...] + cw_ref[SSM_CONV - 1:SSM_CONV, :] * xcat[halo:halo + L]
    for k in range(SSM_CONV - 1):
        shifted = pltpu.roll(xcat, SSM_CONV - 1 - k, 0)
        conv = conv + cw_ref[k:k + 1, :] * shifted[halo:halo + L]
    xcat_ref[0:halo, :] = xcat[L:L + halo]
    u = _silu(conv)
    xs = u[:, :SSM_D_INNER]
    bm = u[:, SSM_D_INNER:SSM_D_INNER + G * N]
    cm = u[:, SSM_D_INNER + G * N:]

    def softplus(v):
        return jnp.maximum(v, 0.0) + jnp.log1p(jnp.exp(-jnp.abs(v)))

    dt = softplus(tail_ref[:, 0:H] + dtb_ref[...])
    dtt = softplus(tailt_ref[0:H, :] + dtbt_ref[...])
    da = dt * (-jnp.exp(alog_ref[...]))
    dat = dtt * (-jnp.exp(alogt_ref[...]))
    ri = lax.broadcasted_iota(jnp.int32, (L, L), 0)
    ci = lax.broadcasted_iota(jnp.int32, (L, L), 1)
    tri = ci <= ri
    hi = lax.Precision.HIGHEST
    acs = jnp.dot(tri.astype(F32), da, precision=hi, preferred_element_type=F32)
    acst = jnp.dot(dat, (ri <= ci).astype(F32), precision=hi, preferred_element_type=F32)
    last = acs[L - 1:L, :]
    w_state = dt * jnp.exp(last - acs)
    eacs = jnp.exp(acs)
    cdec = jnp.exp(last)

    hrow = lax.broadcasted_iota(jnp.int32, (H, SSM_D_INNER), 0)
    hcol = lax.broadcasted_iota(jnp.int32, (H, SSM_D_INNER), 1)
    expand = jnp.where(hcol // P == hrow, 1.0, 0.0).astype(BF16)

    def split3(v):
        v_hi = v.astype(BF16)
        r1 = v - v_hi.astype(F32)
        v_mid = r1.astype(BF16)
        v_lo = (r1 - v_mid.astype(F32)).astype(BF16)
        return jnp.concatenate([v_hi, v_mid, v_lo], axis=1)

    small = jnp.concatenate([cdec, dskip_ref[...], jnp.zeros((6, H), F32)], axis=0)
    per_head = jnp.concatenate([dt, w_state, eacs, small], axis=0)
    spread = jnp.dot(split3(per_head), jnp.concatenate([expand] * 3, axis=0), preferred_element_type=F32)
    cdec_e = spread[3 * L:3 * L + 1, :]
    dskip_e = spread[3 * L + 1:3 * L + 2, :]
    xdt = (xs * spread[0:L]).astype(BF16)
    wst = (xs * spread[L:2 * L]).astype(BF16)
    eacs_e = spread[2 * L:3 * L]

    for g in range(G):
        bm_g = bm[:, g * N:(g + 1) * N]
        cm_g = cm[:, g * N:(g + 1) * N].astype(BF16)
        cb = lax.dot_general(cm_g, bm_g.astype(BF16), NT_DIMS, preferred_element_type=F32)
        cols = slice(g * HG * P, (g + 1) * HG * P)
        st = state_ref[:, cols]
        y_off = jnp.dot(cm_g, st.astype(BF16), preferred_element_type=F32) * eacs_e[:, cols]
        ybuf_ref[:, cols] = y_off
        bmt = bm_g.T.astype(BF16)
        state_ref[:, cols] = st * cdec_e[:, cols] + jnp.dot(bmt, wst[:, cols], preferred_element_type=F32)
        for r in range(HG):
            hh = g * HG + r
            diff = acs[:, hh:hh + 1] - acst[hh:hh + 1, :]
            seg = jnp.exp(jnp.where(tri, diff, -jnp.inf))
            lmat = (cb * seg).astype(BF16)
            hc = slice(hh * P, (hh + 1) * P)
            ybuf_ref[:, hc] = ybuf_ref[:, hc] + jnp.dot(lmat, xdt[:, hc], preferred_element_type=F32)

    y = (ybuf_ref[...] + xs * dskip_e) * zs_ref[...]
    gw = SSM_D_INNER // G
    parts = []
    for g in range(G):
        yg = y[:, g * gw:(g + 1) * gw]
        parts.append(yg * lax.rsqrt(jnp.mean(yg * yg, axis=-1, keepdims=True) + SSM_NORM_EPS))
    y_ref[...] = (jnp.concatenate(parts, axis=1) * ng_ref[...]).astype(y_ref.dtype)


def _ssd(xbc, zs, tail, tailt, conv_w, conv_b, dt_bias, a_log, d_skip, norm_g, batch, seq):
    t = xbc.shape[0]
    L = SSM_CHUNK
    nc = seq // L
    row = lambda b, c: (b * nc + c, 0)
    const = lambda b, c: (0, 0)
    H = SSM_HEADS
    in_specs = [
        pl.BlockSpec((L, SSM_XBC), row),
        pl.BlockSpec((L, SSM_D_INNER), row),
        pl.BlockSpec((L, TAIL), row),
        pl.BlockSpec((TAIL, L), lambda b, c: (0, b * nc + c)),
        pl.BlockSpec((SSM_CONV, SSM_XBC), const),
        pl.BlockSpec((1, SSM_XBC), const),
        pl.BlockSpec((1, H), const),
        pl.BlockSpec((H, 1), const),
        pl.BlockSpec((1, H), const),
        pl.BlockSpec((H, 1), const),
        pl.BlockSpec((1, H), const),
        pl.BlockSpec((1, SSM_D_INNER), const),
    ]
    return pl.pallas_call(
        _ssd_kernel, grid=(batch, nc), in_specs=in_specs,
        out_specs=pl.BlockSpec((L, SSM_D_INNER), row),
        out_shape=jax.ShapeDtypeStruct((t, SSM_D_INNER), BF16),
        scratch_shapes=[pltpu.VMEM((L + 8, SSM_XBC), F32),
                        pltpu.VMEM((SSM_STATE, SSM_D_INNER), F32),
                        pltpu.VMEM((L, SSM_D_INNER), F32)],
        compiler_params=_cparams(("arbitrary", "arbitrary")), name="ssd",
    )(xbc, zs, tail, tailt, conv_w, conv_b, dt_bias.reshape(1, H), dt_bias.reshape(H, 1),
      a_log.reshape(1, H), a_log.reshape(H, 1), d_skip.reshape(1, H), norm_g.reshape(1, -1))


def _compress_kernel(k_ref, v_ref, kpos_ref, vpos_ref, kw1_ref, vw1_ref, kb1_ref, vb1_ref,
                     kw2_ref, vw2_ref, kc_ref, vct_ref):
    D = NSA_HEAD_DIM
    gw = CMP_STRIDE * D

    def hidden(t_ref, pos_ref, w1_ref, b1_ref, g):
        t = t_ref[:, g * gw:(g + 1) * gw]
        n = t.shape[0]
        lo = jnp.dot((t + pos_ref[0:1, :]).astype(BF16), w1_ref[...], preferred_element_type=F32)
        hi = jnp.dot((t + pos_ref[1:2, :]).astype(BF16), w1_ref[...], preferred_element_type=F32)
        pre = lo + pltpu.roll(pltpu.roll(hi, n - 1, 0), D, 1)
        return _silu(pre[:, 0:D] + b1_ref[...]).astype(BF16)

    for g in range(NSA_KV_HEADS):
        kc = jnp.dot(hidden(k_ref, kpos_ref, kw1_ref, kb1_ref, g), kw2_ref[...], preferred_element_type=F32)
        kc_ref[g] = kc.astype(BF16)
        vct = lax.dot_general(vw2_ref[...], hidden(v_ref, vpos_ref, vw1_ref, vb1_ref, g), NT_DIMS,
                              preferred_element_type=F32)
        vct_ref[g] = vct.astype(BF16)


def _compress_weights(pos, w1, b1, w2):
    half = CMP_BLOCK // 2 * NSA_HEAD_DIM
    w1cat = jnp.concatenate([w1[:half], w1[half:]], axis=1).astype(BF16)
    return pos.reshape(2, half), w1cat, b1.reshape(1, -1), w2.astype(BF16)


def _compress(k16, v16, kparams, vparams, batch, seq):
    rows = seq // CMP_STRIDE
    width = CMP_STRIDE * NSA_KV
    D = NSA_HEAD_DIM
    gw = CMP_STRIDE * D
    kpos, kw1, kb1, kw2 = _compress_weights(*kparams)
    vpos, vw1, vb1, vw2 = _compress_weights(*vparams)
    vw2 = vw2.T
    c2 = lambda b: (0, 0)
    tok = pl.BlockSpec((rows, width), lambda b: (b, 0))
    in_specs = [tok, tok,
                pl.BlockSpec((2, gw), c2), pl.BlockSpec((2, gw), c2),
                pl.BlockSpec((gw, 2 * D), c2), pl.BlockSpec((gw, 2 * D), c2),
                pl.BlockSpec((1, D), c2), pl.BlockSpec((1, D), c2),
                pl.BlockSpec((D, D), c2), pl.BlockSpec((D, D), c2)]
    out = jax.ShapeDtypeStruct((NSA_KV_HEADS, batch * rows, D), BF16)
    out_t = jax.ShapeDtypeStruct((NSA_KV_HEADS, D, batch * rows), BF16)
    ospec = pl.BlockSpec((NSA_KV_HEADS, rows, D), lambda b: (0, b, 0))
    ospec_t = pl.BlockSpec((NSA_KV_HEADS, D, rows), lambda b: (0, 0, b))
    return pl.pallas_call(
        _compress_kernel, grid=(batch,), in_specs=in_specs, out_specs=(ospec, ospec_t),
        out_shape=(out, out_t), compiler_params=_cparams(("arbitrary",)), name="compress",
    )(k16, v16, kpos, vpos, kw1, vw1, kb1, vb1, kw2, vw2)


def _nsa_kernel(qt_ref, kc_ref, vct_ref, ks_ref, vst_ref, kw_ref, vwt_ref, tailt_ref, ovt_ref, mtab_ref, cmask_ref,
                o_ref, biasq_ref, m_ref, acc_ref, part_ref, sbuf0_ref, sbuf1_ref, mbuf0_ref, mbuf1_ref):
    R, D = NSA_REP, NSA_HEAD_DIM
    nl = R * TQ
    grp = range(kc_ref.shape[0])
    sbuf_refs = (sbuf0_ref, sbuf1_ref)
    mbuf_refs = (mbuf0_ref, mbuf1_ref)
    n_stages = (ks_ref.shape[1] // TK - 1) // 2
    gp = pl.program_id(1)
    qi = pl.program_id(2)
    t0 = qi * TQ
    qt = [jnp.concatenate([qt_ref[gi * R + r] for r in range(R)], axis=1) for gi in grp]
    tpos_row = t0 + lax.broadcasted_iota(jnp.int32, (1, nl), 1) % TQ

    def k_rows(k_ref, gi, start):
        return k_ref[gi, pl.ds(pl.multiple_of(start, TK), TK), :]

    def vt_cols(vt_ref, gi, start, n):
        return vt_ref[gi, :, pl.ds(pl.multiple_of(start, TK), n)]

    def pv(vt, p):
        vt1 = jnp.concatenate([vt, jnp.ones((ONES_ROWS, vt.shape[1]), BF16)], axis=0)
        return jnp.dot(vt1, p, preferred_element_type=F32)

    def mask_tile(off):
        return jnp.concatenate([mtab_ref[pl.ds(pl.multiple_of(off, TK), TK), :]] * R, axis=1)

    def col_max(s):
        return jnp.max(s, axis=0, keepdims=True)

    def online(state, s, m_chunk, vt):
        m_new = m_chunk if state is None else jnp.maximum(state[0], m_chunk)
        contrib = pv(vt, jnp.exp2(s - m_new).astype(BF16))
        if state is None:
            return m_new, contrib
        return m_new, jnp.exp2(state[0] - m_new) * state[1] + contrib

    causal = mask_tile(MT_CAUSAL)
    far0 = jnp.maximum(t0 - 2 * TK, 0)
    mid0 = jnp.maximum(t0 - TK, 0)
    window = []
    for gi in grp:
        s_dia = jnp.dot(k_rows(kw_ref, gi, t0), qt[gi], preferred_element_type=F32) + causal
        m_dia = col_max(s_dia)
        s_mid = (jnp.dot(k_rows(kw_ref, gi, mid0), qt[gi], preferred_element_type=F32)
                 + mask_tile(jnp.where(qi >= 1, MT_ALL, MT_NONE)))
        m_mid = col_max(s_mid)
        s_far = (jnp.dot(k_rows(kw_ref, gi, far0), qt[gi], preferred_element_type=F32)
                 + mask_tile(jnp.where(qi >= 2, MT_FAR, MT_NONE)))
        m_far = col_max(s_far)
        window.append(((s_dia, m_dia, t0), (s_mid, m_mid, mid0), (s_far, m_far, far0)))

    ncr = kc_ref.shape[1]
    n_slc = ovt_ref.shape[0]
    cmask = cmask_ref[pl.ds(pl.multiple_of(qi * ncr, ncr), ncr), :]
    cmask = jnp.concatenate([cmask] * R, axis=1)
    sees_any = tpos_row >= CMP_BLOCK - 1
    jj = lax.broadcasted_iota(jnp.int32, (n_slc, TQ), 0)
    tt = t0 + lax.broadcasted_iota(jnp.int32, (n_slc, TQ), 1)
    lag = tt // SLC_BLOCK - jj
    forced = (jj == 0) | ((lag >= 0) & (lag < N_LOCAL_BLOCKS))
    valid = jj * SLC_BLOCK <= tt
    rows8 = 8
    j8 = lax.broadcasted_iota(jnp.int32, (rows8, TQ), 0)
    per = TK // SLC_BLOCK
    zrows = jnp.zeros((BIAS_ROWS - per, TQ), F32)
    o_c, o_w = [], []
    for gi in grp:
        s_m = jnp.dot(kc_ref[gi], qt[gi], preferred_element_type=F32) + cmask
        mx = jnp.max(s_m, axis=0, keepdims=True)
        p = jnp.exp2(s_m - mx)
        den = jnp.sum(p, axis=0, keepdims=True)
        pc = p * jnp.where(sees_any, 1.0 / den, 0.0)
        o_c.append(jnp.dot(vct_ref[gi], pc.astype(BF16), preferred_element_type=F32))

        psum = pc[:, 0:TQ]
        for r in range(1, R):
            psum = psum + pc[:, r * TQ:(r + 1) * TQ]
        imp_t = jnp.dot(ovt_ref[...], psum, precision=lax.Precision.HIGHEST,
                        preferred_element_type=F32)
        score = jnp.where(forced, FORCED_SCORE, jnp.where(valid, imp_t, -1.0))
        groups = [score[a:a + rows8] for a in range(0, n_slc, rows8)]
        ranks = [jnp.zeros((rows8, TQ), F32) for _ in groups]
        window_steps = dict(zip((0, n_slc // 3, 2 * n_slc // 3), window[gi]))
        win = None
        for j2 in range(n_slc):
            if j2 in window_steps:
                s_w, m_w, start_w = window_steps[j2]
                win = online(win, s_w, m_w, vt_cols(vwt_ref, gi, start_w, TK))
            sj = score[j2:j2 + 1, :]
            for a, sg in enumerate(groups):
                lo = a * rows8
                if lo + rows8 - 1 < j2:
                    beats = sj > sg
                elif lo > j2:
                    beats = sj >= sg
                else:
                    beats = (sj > sg) | ((sj == sg) & (j8 + lo > j2))
                ranks[a] = ranks[a] + jnp.where(beats, 1.0, 0.0)
        rank = jnp.concatenate(ranks, axis=0)
        selected = (rank < float(min(SLC_TOP_N, n_slc))) & (score >= 0.0)
        bias_t = jnp.where(selected, 0.0, NEG_BIG)
        for kb in range(n_slc // per):
            blk = jnp.concatenate([bias_t[kb * per:(kb + 1) * per], zrows], axis=0)
            biasq_ref[gi, kb * BIAS_ROWS:(kb + 1) * BIAS_ROWS, :] = (
                jnp.concatenate([blk] * R, axis=1).astype(BF16))
        accw = win[1]
        o_w.append(accw[0:D] / accw[D:D + 1])
    qpad = jnp.zeros((KSEL - D - BIAS_ROWS, nl), BF16)

    def q_sel(gi, kb):
        rows = biasq_ref[gi, pl.ds(pl.multiple_of(kb * BIAS_ROWS, BIAS_ROWS), BIAS_ROWS), :]
        return jnp.concatenate([qt[gi], rows, qpad], axis=0)

    def slc_scores(gi, kb):
        return jnp.dot(k_rows(ks_ref, gi, kb * TK), q_sel(gi, kb), preferred_element_type=F32)

    def fill(gi, slot, chunks):
        for c, kb in enumerate(chunks):
            s = slc_scores(gi, kb)
            sbuf_refs[slot][gi, c * TK:(c + 1) * TK, :] = s
            mbuf_refs[slot][gi, c:c + 1, :] = col_max(s)

    npairs = qi // 2
    diag = []
    for gi in grp:
        s_sel = slc_scores(gi, qi) + causal
        diag.append((s_sel, col_max(s_sel)))
        fill(gi, 0, (0, 1))

    def gates(gi):
        gate_row = pl.multiple_of(GATE_OFF + GATE_SLOT * (gp * len(grp) + gi), 8)
        sig = jax.nn.sigmoid(tailt_ref[pl.ds(gate_row, GATE_SLOT), :])
        return [jnp.concatenate([sig[3 * r + c:3 * r + c + 1, :] for r in range(R)], axis=1)
                for c in range(3)]

    for gi in grp:
        gate = gates(gi)
        part_ref[gi] = gate[0] * o_c[gi] + gate[2] * o_w[gi]

    for gi in grp:
        m_sel, acc_sel = online(None, diag[gi][0], diag[gi][1], vt_cols(vst_ref, gi, t0, TK))
        m_ref[gi] = m_sel
        acc_ref[gi] = acc_sel

    def absorb(gi, slot, first, n):
        state = (m_ref[gi], acc_ref[gi])
        for c in range(n):
            state = online(state, sbuf_refs[slot][gi, c * TK:(c + 1) * TK, :], mbuf_refs[slot][gi, c:c + 1, :],
                           vt_cols(vst_ref, gi, (first + c) * TK, TK))
        m_ref[gi] = state[0]
        acc_ref[gi] = state[1]

    for k in range(n_stages):
        @pl.when(npairs > k)
        def _(k=k):
            for gi in grp:
                fill(gi, (k + 1) % 2, (jnp.minimum(2 * k + 2, qi - 1), jnp.minimum(2 * k + 3, qi - 1)))
            for gi in grp:
                absorb(gi, k % 2, 2 * k, 2)

    for parity in range(2):
        @pl.when((qi % 2 == 1) & (npairs % 2 == parity))
        def _(parity=parity):
            for gi in grp:
                absorb(gi, parity, qi - 1, 1)

    for gi in grp:
        acc = acc_ref[gi]
        ot = part_ref[gi] + gates(gi)[1] * (acc[0:D] / acc[D:D + 1])
        stacked = jnp.concatenate([ot[:, r * TQ:(r + 1) * TQ] for r in range(R)], axis=0)
        o_ref[:, gi * R * D:(gi + 1) * R * D] = stacked.T.astype(o_ref.dtype)


def _overlap_t(seq):
    n_cmp = (seq - CMP_BLOCK) // CMP_STRIDE + 1
    n_slc = seq // SLC_BLOCK
    cs = np.arange(n_cmp) * CMP_STRIDE
    ss = np.arange(n_slc) * SLC_BLOCK
    overlap = np.clip(np.minimum(cs[:, None] + CMP_BLOCK, ss[None, :] + SLC_BLOCK)
                      - np.maximum(cs[:, None], ss[None, :]), 0, None) / CMP_BLOCK
    ovt = np.zeros((n_slc, n_cmp + 1), np.float32)
    ovt[:, :n_cmp] = overlap.T
    return jnp.asarray(ovt)


def _mask_tiles():
    ki = np.arange(TK)[:, None]
    qi = np.arange(TQ)[None, :]
    neg = np.float32(NEG_BIG)
    none = np.full((TK, TQ), neg, np.float32)
    far = np.where(ki > qi, np.float32(0), neg)
    full = np.zeros((TK, TQ), np.float32)
    causal = np.where(ki <= qi, np.float32(0), neg)
    return jnp.asarray(np.concatenate([none, far, full, causal], axis=0))


def _cmp_masks(seq):
    ncr = seq // CMP_STRIDE
    n_cmp = (seq - CMP_BLOCK) // CMP_STRIDE + 1
    n = np.arange(ncr)[None, :, None]
    t = (np.arange(seq // TQ)[:, None, None] * TQ + np.arange(TQ)[None, None, :])
    visible = (n * CMP_STRIDE + CMP_BLOCK - 1 <= t) & (n < n_cmp)
    return jnp.asarray(np.where(visible, np.float32(0), np.float32(NEG_BIG)).reshape(-1, TQ))


def _nsa(qt, kc, vct, ks, vst, kw, vwt, tailt, batch, seq):
    t = batch * seq
    G, R, D = NSA_KV_HEADS, NSA_REP, NSA_HEAD_DIM
    nq = seq // TQ
    ncr = seq // CMP_STRIDE
    n_slc = seq // SLC_BLOCK
    assert TQ == TK and WINDOW == 2 * TK, "window branch visits exactly three key chunks"
    assert TK // SLC_BLOCK <= BIAS_ROWS and D + BIAS_ROWS <= KSEL
    P = NSA_GROUPS_PER_STEP
    const = lambda b, g, i: (0, 0)
    vtspec = pl.BlockSpec((P, D, seq), lambda b, g, i: (g, 0, b))
    in_specs = [
        pl.BlockSpec((P * R, D, TQ), lambda b, g, i: (g, 0, b * nq + i)),
        pl.BlockSpec((P, ncr, D), lambda b, g, i: (g, b, 0)),
        pl.BlockSpec((P, D, ncr), lambda b, g, i: (g, 0, b)),
        pl.BlockSpec((P, seq, KSEL), lambda b, g, i: (g, b, 0)),
        vtspec,
        pl.BlockSpec((P, seq, D), lambda b, g, i: (g, b, 0)),
        vtspec,
        pl.BlockSpec((TAIL, TQ), lambda b, g, i: (0, b * nq + i)),
        pl.BlockSpec((n_slc, ncr), const),
        pl.BlockSpec((4 * TK, TQ), const),
        pl.BlockSpec((nq * ncr, TQ), const),
    ]
    return pl.pallas_call(
        _nsa_kernel, grid=(batch, G // P, nq), in_specs=in_specs,
        out_specs=pl.BlockSpec((TQ, P * R * D), lambda b, g, i: (b * nq + i, g)),
        out_shape=jax.ShapeDtypeStruct((t, NSA_Q), BF16),
        scratch_shapes=[pltpu.VMEM((P, seq // TK * BIAS_ROWS, R * TQ), BF16),
                        pltpu.VMEM((P, 1, R * TQ), F32),
                        pltpu.VMEM((P, D + ONES_ROWS, R * TQ), F32),
                        pltpu.VMEM((P, D, R * TQ), F32),
                        pltpu.VMEM((P, 2 * TK, R * TQ), F32),
                        pltpu.VMEM((P, 2 * TK, R * TQ), F32),
                        pltpu.VMEM((P, 8, R * TQ), F32),
                        pltpu.VMEM((P, 8, R * TQ), F32)],
        compiler_params=_cparams(("arbitrary", "arbitrary", "arbitrary")), name="nsa",
    )(qt, kc, vct, ks, vst, kw, vwt, tailt, _overlap_t(seq), _mask_tiles(), _cmp_masks(seq))


def _outproj_kernel(y_ref, o_ref, x_ref, w_ref, ag_ref, ng_ref, x1_ref, h2_ref):
    yn = _rms(o_ref[...].astype(F32), ag_ref[...], NORM_EPS).astype(BF16)
    x1 = (x_ref[...]
          + jnp.dot(y_ref[...], w_ref[0:SSM_D_INNER, :], preferred_element_type=F32)
          + jnp.dot(yn, w_ref[SSM_D_INNER:, :], preferred_element_type=F32))
    x1_ref[...] = x1
    h2_ref[...] = _rms(x1, ng_ref[...], NORM_EPS).astype(BF16)


def _out_proj(y_ssm, o_nsa, x2, w_out, attn_g, norm2_g):
    t = x2.shape[0]
    tm = TM_OUT_PROJ
    row = lambda i: (i, 0)
    const = lambda i: (0, 0)
    tok = pl.BlockSpec((tm, D_MODEL), row)
    vec = pl.BlockSpec((1, D_MODEL), const)
    return pl.pallas_call(
        _outproj_kernel, grid=(t // tm,),
        in_specs=[tok, tok, tok, pl.BlockSpec((SSM_D_INNER + NSA_Q, D_MODEL), const), vec, vec],
        out_specs=(tok, tok),
        out_shape=(jax.ShapeDtypeStruct((t, D_MODEL), F32), jax.ShapeDtypeStruct((t, D_MODEL), BF16)),
        compiler_params=_cparams(("arbitrary",)), name="out_proj",
    )(y_ssm, o_nsa, x2, w_out, attn_g, norm2_g)


def _ffn_kernel(h_ref, halo_ref, x1_ref, wup_ref, cw_ref, cb_ref, wd_ref, fg_ref, out_ref, act_ref,
                *, tiles_per_seq):
    i = pl.program_id(0)
    tm = h_ref.shape[0]
    pad = halo_ref.shape[0]
    tn = TN_FFN
    halo = halo_ref[...]
    halo = jnp.where(i % tiles_per_seq == 0, jnp.zeros_like(halo), halo)
    hc = jnp.concatenate([halo, h_ref[...]], axis=0)

    def branch(c0):
        cols = slice(c0, c0 + tn)
        u = jnp.dot(hc, wup_ref[:, cols], preferred_element_type=F32)
        out = cb_ref[:, cols] + cw_ref[FFN_CONV - 1:FFN_CONV, cols] * u[pad:pad + tm, :]
        for k in range(FFN_CONV - 1):
            shifted = pltpu.roll(u, FFN_CONV - 1 - k, 0)
            out = out + cw_ref[k:k + 1, cols] * shifted[pad:pad + tm, :]
        return out

    for j in range(D_FF // tn):
        act = _silu(branch(j * tn)) * branch(D_FF + j * tn)
        act_ref[:, j * tn:(j + 1) * tn] = act.astype(BF16)
    down = jnp.dot(act_ref[...], wd_ref[...], preferred_element_type=F32)
    out_ref[...] = _rms(x1_ref[...] + down, fg_ref[...], NORM_EPS)


def _ffn(h2, x1, w_up, conv_w, conv_b, w_down, final_g, seq):
    t = h2.shape[0]
    tm = TM_FFN
    pad = BF16_SUBLANES
    tok = pl.BlockSpec((tm, D_MODEL), lambda i: (i, 0))

    def resident(shape):
        return pl.BlockSpec(shape, lambda i: (0, 0), pipeline_mode=pl.Buffered(1))

    in_specs = [
        tok,
        pl.BlockSpec((pad, D_MODEL), lambda i: (jnp.maximum(i * (tm // pad) - 1, 0), 0)),
        tok,
        resident((D_MODEL, 2 * D_FF)),
        resident((FFN_CONV, 2 * D_FF)),
        resident((1, 2 * D_FF)),
        resident((D_FF, D_MODEL)),
        resident((1, D_MODEL)),
    ]
    return pl.pallas_call(
        functools.partial(_ffn_kernel, tiles_per_seq=seq // tm), grid=(t // tm,),
        in_specs=in_specs, out_specs=tok,
        out_shape=jax.ShapeDtypeStruct((t, D_MODEL), F32),
        scratch_shapes=[pltpu.VMEM((tm, D_FF), BF16)],
        compiler_params=_cparams(("arbitrary",)), name="ffn",
    )(h2, h2, x1, w_up, conv_w, conv_b, w_down, final_g)


def _rope_tables(seq):
    half = NSA_HEAD_DIM // 2
    inv_freq = 1.0 / (ROPE_THETA ** (jnp.arange(0, NSA_HEAD_DIM, 2, dtype=F32) / NSA_HEAD_DIM))
    ang = jnp.arange(seq).astype(F32)[:, None] * inv_freq[None, :]
    cos, sin = jnp.cos(ang), jnp.sin(ang)
    reps = LANES // NSA_HEAD_DIM
    cos_t = jnp.tile(jnp.concatenate([cos, cos], axis=1), (1, reps))
    sin_t = jnp.tile(jnp.concatenate([-sin, sin], axis=1), (1, reps))
    return cos_t, sin_t


def _split_w_in(w):
    o_z, o_xbc, o_dt, o_q, o_kv, o_g = np.cumsum([0, SSM_D_INNER, SSM_XBC, SSM_HEADS, NSA_Q, 6 * NSA_KV]).tolist()
    per = 3 * NSA_REP
    tail_cols = [w[:, o_dt:o_q]]
    for g in range(NSA_KV_HEADS):
        tail_cols += [w[:, o_g + g * per:o_g + (g + 1) * per], jnp.zeros((w.shape[0], GATE_SLOT - per), w.dtype)]
    tail_cols.append(jnp.zeros((w.shape[0], TAIL - GATE_OFF - NSA_KV_HEADS * GATE_SLOT), w.dtype))
    return (w[:, o_z:o_dt].astype(BF16), w[:, o_q:o_g].astype(BF16),
            jnp.concatenate(tail_cols, axis=1).astype(BF16))


def kernel(x, norm1_g, w_in, ssm_conv_w, ssm_conv_b, ssm_dt_bias, ssm_a_log, ssm_d, ssm_norm_g, cmp_k_pos, cmp_k_w1, cmp_k_b1, cmp_k_w2, cmp_v_pos, cmp_v_w1, cmp_v_b1, cmp_v_w2, attn_norm_g, w_out, norm2_g, ffn_w_up, ffn_conv_w, ffn_conv_b, ffn_w_down, final_norm_g):
    batch, seq, d = x.shape
    assert w_in.shape[0] == 1, "single-layer problem"
    l = 0
    cos_t, sin_t = _rope_tables(seq)
    x2 = x.reshape(batch * seq, d)
    zs, xbc, qt, kc_r, vc_r, ks, vst, kw, vwt, tail, tailt = _in_proj(
        x2, norm1_g[l].reshape(1, d), _split_w_in(w_in.reshape(d, -1)), cos_t, sin_t, seq)
    y_ssm = _ssd(xbc, zs, tail, tailt, ssm_conv_w[l], ssm_conv_b[l].reshape(1, -1), ssm_dt_bias[l],
                 ssm_a_log[l], ssm_d[l], ssm_norm_g[l], batch, seq)
    kc, vct = _compress(kc_r, vc_r,
                        (cmp_k_pos[l], cmp_k_w1[l], cmp_k_b1[l], cmp_k_w2[l]),
                        (cmp_v_pos[l], cmp_v_w1[l], cmp_v_b1[l], cmp_v_w2[l]), batch, seq)
    o_nsa = _nsa(qt, kc, vct, ks, vst, kw, vwt, tailt, batch, seq)
    x1, h2 = _out_proj(y_ssm, o_nsa, x2, w_out.reshape(-1, d).astype(BF16), attn_norm_g[l].reshape(1, d),
                       norm2_g[l].reshape(1, d))
    out = _ffn(h2, x1, ffn_w_up.reshape(d, -1).astype(BF16), ffn_conv_w[l], ffn_conv_b[l].reshape(1, -1),
               ffn_w_down.reshape(-1, d).astype(BF16), final_norm_g.reshape(1, d), seq)
    return out.reshape(batch, seq, d)
```

```python
import functools
import math

import numpy as np
import jax
import jax.numpy as jnp
from jax import lax
from jax.experimental import pallas as pl
from jax.experimental.pallas import tpu as pltpu

F32 = jnp.float32
BF16 = jnp.bfloat16

D_MODEL = 1024
SSM_D_INNER = 1024
SSM_HEAD_DIM = 64
SSM_HEADS = 16
SSM_GROUPS = 2
SSM_STATE = 128
SSM_CONV = 4
SSM_CHUNK = 128
SSM_XBC = SSM_D_INNER + 2 * SSM_GROUPS * SSM_STATE
SSM_NORM_EPS = 1e-5
NSA_HEADS = 16
NSA_KV_HEADS = 4
NSA_REP = NSA_HEADS // NSA_KV_HEADS
NSA_HEAD_DIM = 64
NSA_Q = NSA_HEADS * NSA_HEAD_DIM
NSA_KV = NSA_KV_HEADS * NSA_HEAD_DIM
CMP_BLOCK = 32
CMP_STRIDE = 16
SLC_BLOCK = 64
SLC_TOP_N = 16
N_LOCAL_BLOCKS = 2
FORCED_SCORE = 1e4
WINDOW = 512
ROPE_THETA = 10000.0
D_FF = 2816
FFN_CONV = 3
NORM_EPS = 1e-6
NEG_BIG = -1e30

LANES = 128
SUBLANES = 8
BF16_SUBLANES = 16
TAIL = LANES
GATE_OFF = SSM_HEADS
GATE_SLOT = 16
W_QKV_OFF = SSM_D_INNER + SSM_XBC + SSM_HEADS

TM_IN_PROJ = 256
TM_OUT_PROJ = 512
TQ = 256
TK = 256
NSA_GROUPS_PER_STEP = 4
ONES_ROWS = 16
KSEL = LANES
BIAS_ROWS = 16
MT_NONE, MT_FAR, MT_ALL, MT_CAUSAL = 0, TK, 2 * TK, 3 * TK
TM_FFN = 512
TN_FFN = 256
VMEM_LIMIT = 56 * 1024 * 1024

NT_DIMS = (((1,), (1,)), ((), ()))


def _cparams(sem):
    return pltpu.CompilerParams(dimension_semantics=sem, vmem_limit_bytes=VMEM_LIMIT)


def _rms(x, g, eps):
    return x * lax.rsqrt(jnp.mean(x * x, axis=-1, keepdims=True) + eps) * g


def _silu(x):
    h = 0.5 * x
    return h + h * jnp.tanh(h)


def _inproj_kernel(x_ref, g_ref, w_ref, wt_ref, cos_ref, sin_ref,
                   zs_ref, xbc_ref, qt_ref, kc_ref, vc_ref, ks_ref, vst_ref, kw_ref, vwt_ref,
                   tail_ref, tailt_ref, kbuf_ref, vbuf_ref, wa_ref, wb_ref, *, tiles_per_seq):
    tm = x_ref.shape[0]

    @pl.when(pl.program_id(0) == 0)
    def _():
        strip = 2 * LANES
        for r0 in range(0, wa_ref.shape[0], strip):
            wa_ref[r0:r0 + strip, :] = w_ref[r0:r0 + strip, :].astype(BF16)
            wb_ref[r0:r0 + strip, :] = w_ref[W_QKV_OFF + r0:W_QKV_OFF + r0 + strip, :].astype(BF16)
    seq_tile = pl.program_id(0) % tiles_per_seq
    h = _rms(x_ref[...], g_ref[...], NORM_EPS).astype(BF16)

    def mm(w_ref, lo, hi):
        return lax.dot_general(h, w_ref[lo:hi, :], NT_DIMS, preferred_element_type=F32)

    tail = mm(wt_ref, 0, TAIL)
    tail_ref[...] = tail
    tailt_ref[...] = tail.T

    cos = cos_ref[...]
    sin = sin_ref[...]
    lane = lax.broadcasted_iota(jnp.int32, (tm, LANES), 1)
    first_half = (lane % NSA_HEAD_DIM) < (NSA_HEAD_DIM // 2)

    def rope(xc):
        partner = jnp.where(first_half, pltpu.roll(xc, LANES - 32, 1), pltpu.roll(xc, 32, 1))
        return xc * cos + partner * sin

    q = mm(wb_ref, 0, NSA_Q)
    scale = NSA_HEAD_DIM ** -0.5 * math.log2(math.e)
    for c in range(NSA_Q // LANES):
        rt = (rope(q[:, c * LANES:(c + 1) * LANES]) * scale).T.astype(BF16)
        qt_ref[2 * c] = rt[:NSA_HEAD_DIM]
        qt_ref[2 * c + 1] = rt[NSA_HEAD_DIM:]

    kv = mm(wb_ref, NSA_Q, NSA_Q + 6 * NSA_KV)

    def seg(i):
        return kv[:, i * NSA_KV:(i + 1) * NSA_KV]

    def rope_seg(x):
        return jnp.concatenate([rope(x[:, :LANES]), rope(x[:, LANES:])], axis=1)

    def store_heads(ref, x):
        for g in range(NSA_KV_HEADS):
            ref[g] = x[:, g * NSA_HEAD_DIM:(g + 1) * NSA_HEAD_DIM].astype(BF16)

    def store_heads_t(ref, x):
        xt = x.T.astype(BF16)
        for g in range(NSA_KV_HEADS):
            ref[g] = xt[g * NSA_HEAD_DIM:(g + 1) * NSA_HEAD_DIM]

    kc_tile = rope_seg(seg(0))
    vc_tile = seg(1)
    for c in range(NSA_KV // LANES):
        kbuf_ref[c] = kc_tile[:, c * LANES:(c + 1) * LANES]
        vbuf_ref[c] = vc_tile[:, c * LANES:(c + 1) * LANES]
    gw = CMP_STRIDE * NSA_HEAD_DIM
    for l in range(CMP_STRIDE):
        for c in range(NSA_KV // LANES):
            k_rows = kbuf_ref[c, pl.ds(l, tm // CMP_STRIDE, stride=CMP_STRIDE), :]
            v_rows = vbuf_ref[c, pl.ds(l, tm // CMP_STRIDE, stride=CMP_STRIDE), :]
            for half in range(LANES // NSA_HEAD_DIM):
                g = c * (LANES // NSA_HEAD_DIM) + half
                src = slice(half * NSA_HEAD_DIM, (half + 1) * NSA_HEAD_DIM)
                dst = slice(g * gw + l * NSA_HEAD_DIM, g * gw + (l + 1) * NSA_HEAD_DIM)
                kc_ref[:, dst] = k_rows[:, src]
                vc_ref[:, dst] = v_rows[:, src]
    pos = seq_tile * tm + lax.broadcasted_iota(jnp.int32, (tm, NSA_HEAD_DIM), 0)
    block_in_chunk = (pos // SLC_BLOCK) % (TK // SLC_BLOCK)
    onehot = jnp.where(lax.broadcasted_iota(jnp.int32, (tm, NSA_HEAD_DIM), 1) == block_in_chunk, 1.0, 0.0)
    ks = rope_seg(seg(2))
    for g in range(NSA_KV_HEADS):
        kg = ks[:, g * NSA_HEAD_DIM:(g + 1) * NSA_HEAD_DIM]
        ks_ref[g] = jnp.concatenate([kg, onehot], axis=1).astype(BF16)
    store_heads_t(vst_ref, seg(3))
    store_heads(kw_ref, rope_seg(seg(4)))
    store_heads_t(vwt_ref, seg(5))

    zs_ref[...] = _silu(mm(wa_ref, 0, SSM_D_INNER))
    xbc_ref[...] = mm(wa_ref, SSM_D_INNER, SSM_D_INNER + SSM_XBC)


def _in_proj(x2, norm_g, w_parts, cos_t, sin_t, seq):
    t = x2.shape[0]
    tm = TM_IN_PROJ
    nseq = seq // tm
    row = lambda i: (i, 0)
    const = lambda i: (0, 0)
    heads = lambda i: (0, i, 0)
    heads_t = lambda i: (0, 0, i)
    out_shape = (
        jax.ShapeDtypeStruct((t, SSM_D_INNER), F32),
        jax.ShapeDtypeStruct((t, SSM_XBC), F32),
        jax.ShapeDtypeStruct((NSA_HEADS, NSA_HEAD_DIM, t), BF16),
        jax.ShapeDtypeStruct((t // CMP_STRIDE, CMP_STRIDE * NSA_KV), F32),
        jax.ShapeDtypeStruct((t // CMP_STRIDE, CMP_STRIDE * NSA_KV), F32),
        jax.ShapeDtypeStruct((NSA_KV_HEADS, t, KSEL), BF16),
        jax.ShapeDtypeStruct((NSA_KV_HEADS, NSA_HEAD_DIM, t), BF16),
        jax.ShapeDtypeStruct((NSA_KV_HEADS, t, NSA_HEAD_DIM), BF16),
        jax.ShapeDtypeStruct((NSA_KV_HEADS, NSA_HEAD_DIM, t), BF16),
        jax.ShapeDtypeStruct((t, TAIL), F32),
        jax.ShapeDtypeStruct((TAIL, t), F32),
    )
    kvh = pl.BlockSpec((NSA_KV_HEADS, tm, NSA_HEAD_DIM), heads)
    kvh_t = pl.BlockSpec((NSA_KV_HEADS, NSA_HEAD_DIM, tm), heads_t)
    out_specs = (
        pl.BlockSpec((tm, SSM_D_INNER), row),
        pl.BlockSpec((tm, SSM_XBC), row),
        pl.BlockSpec((NSA_HEADS, NSA_HEAD_DIM, tm), heads_t),
        pl.BlockSpec((tm // CMP_STRIDE, CMP_STRIDE * NSA_KV), row),
        pl.BlockSpec((tm // CMP_STRIDE, CMP_STRIDE * NSA_KV), row),
        pl.BlockSpec((NSA_KV_HEADS, tm, KSEL), heads), kvh_t, kvh, kvh_t,
        pl.BlockSpec((tm, TAIL), row),
        pl.BlockSpec((TAIL, tm), lambda i: (0, i)),
    )
    in_specs = [
        pl.BlockSpec((tm, D_MODEL), row),
        pl.BlockSpec((1, D_MODEL), const),
        pl.BlockSpec(w_parts[0].shape, const, pipeline_mode=pl.Buffered(1)),
        pl.BlockSpec((TAIL, D_MODEL), const),
        pl.BlockSpec((tm, LANES), lambda i: (i % nseq, 0)),
        pl.BlockSpec((tm, LANES), lambda i: (i % nseq, 0)),
    ]
    return pl.pallas_call(
        functools.partial(_inproj_kernel, tiles_per_seq=nseq), grid=(t // tm,), in_specs=in_specs,
        out_specs=out_specs,
        out_shape=out_shape,
        scratch_shapes=[pltpu.VMEM((NSA_KV // LANES, tm, LANES), F32),
                        pltpu.VMEM((NSA_KV // LANES, tm, LANES), F32),
                        pltpu.VMEM((SSM_D_INNER + SSM_XBC, D_MODEL), BF16),
                        pltpu.VMEM((NSA_Q + 6 * NSA_KV, D_MODEL), BF16)],
        compiler_params=_cparams(("arbitrary",)), name="in_proj",
    )(x2, norm_g, *w_parts, cos_t, sin_t)


def _ssd_kernel(xbc_ref, zs_ref, tail_ref, tailt_ref, cw_ref, cb_ref, dtb_ref, dtbt_ref,
                alog_ref, alogt_ref, dskip_ref, ng_ref,
                y_ref, xcat_ref, state_ref, ybuf_ref):
    L, P, N, H, G = SSM_CHUNK, SSM_HEAD_DIM, SSM_STATE, SSM_HEADS, SSM_GROUPS
    HG = H // G
    halo = SUBLANES

    @pl.when(pl.program_id(1) == 0)
    def _():
        xcat_ref[0:halo, :] = jnp.zeros((halo, SSM_XBC), F32)
        state_ref[...] = jnp.zeros_like(state_ref)

    xcat_ref[halo:halo + L, :] = xbc_ref[...]
    xcat = xcat_ref[...]
    conv = cb_ref[...] + cw_ref[SSM_CONV - 1:SSM_CONV, :] * xcat[halo:halo + L]
    for k in range(SSM_CONV - 1):
        shifted = pltpu.roll(xcat, SSM_CONV - 1 - k, 0)
        conv = conv + cw_ref[k:k + 1, :] * shifted[halo:halo + L]
    xcat_ref[0:halo, :] = xcat[L:L + halo]
    u = _silu(conv)
    xs = u[:, :SSM_D_INNER]
    bm = u[:, SSM_D_INNER:SSM_D_INNER + G * N]
    cm = u[:, SSM_D_INNER + G * N:]

    def softplus(v):
        return jnp.maximum(v, 0.0) + jnp.log1p(jnp.exp(-jnp.abs(v)))

    dt = softplus(tail_ref[:, 0:H] + dtb_ref[...])
    dtt = softplus(tailt_ref[0:H, :] + dtbt_ref[...])
    da = dt * (-jnp.exp(alog_ref[...]))
    dat = dtt * (-jnp.exp(alogt_ref[...]))
    ri = lax.broadcasted_iota(jnp.int32, (L, L), 0)
    ci = lax.broadcasted_iota(jnp.int32, (L, L), 1)
    tri = ci <= ri
    hi = lax.Precision.HIGHEST
    acs = jnp.dot(tri.astype(F32), da, precision=hi, preferred_element_type=F32)
    acst = jnp.dot(dat, (ri <= ci).astype(F32), precision=hi, preferred_element_type=F32)
    last = acs[L - 1:L, :]
    w_state = dt * jnp.exp(last - acs)
    eacs = jnp.exp(acs)
    cdec = jnp.exp(last)

    hrow = lax.broadcasted_iota(jnp.int32, (H, SSM_D_INNER), 0)
    hcol = lax.broadcasted_iota(jnp.int32, (H, SSM_D_INNER), 1)
    expand = jnp.where(hcol // P == hrow, 1.0, 0.0).astype(BF16)

    def split3(v):
        v_hi = v.astype(BF16)
        r1 = v - v_hi.astype(F32)
        v_mid = r1.astype(BF16)
        v_lo = (r1 - v_mid.astype(F32)).astype(BF16)
        return jnp.concatenate([v_hi, v_mid, v_lo], axis=1)

    small = jnp.concatenate([cdec, dskip_ref[...], jnp.zeros((6, H), F32)], axis=0)
    per_head = jnp.concatenate([dt, w_state, eacs, small], axis=0)
    spread = jnp.dot(split3(per_head), jnp.concatenate([expand] * 3, axis=0), preferred_element_type=F32)
    cdec_e = spread[3 * L:3 * L + 1, :]
    dskip_e = spread[3 * L + 1:3 * L + 2, :]
    xdt = (xs * spread[0:L]).astype(BF16)
    wst = (xs * spread[L:2 * L]).astype(BF16)
    eacs_e = spread[2 * L:3 * L]

    for g in range(G):
        bm_g = bm[:, g * N:(g + 1) * N]
        cm_g = cm[:, g * N:(g + 1) * N].astype(BF16)
        cb = lax.dot_general(cm_g, bm_g.astype(BF16), NT_DIMS, preferred_element_type=F32)
        cols = slice(g * HG * P, (g + 1) * HG * P)
        st = state_ref[:, cols]
        y_off = jnp.dot(cm_g, st.astype(BF16), preferred_element_type=F32) * eacs_e[:, cols]
        ybuf_ref[:, cols] = y_off
        bmt = bm_g.T.astype(BF16)
        state_ref[:, cols] = st * cdec_e[:, cols] + jnp.dot(bmt, wst[:, cols], preferred_element_type=F32)
        for r in range(HG):
            hh = g * HG + r
            diff = acs[:, hh:hh + 1] - acst[hh:hh + 1, :]
            seg = jnp.exp(jnp.where(tri, diff, -jnp.inf))
            lmat = (cb * seg).astype(BF16)
            hc = slice(hh * P, (hh + 1) * P)
            ybuf_ref[:, hc] = ybuf_ref[:, hc] + jnp.dot(lmat, xdt[:, hc], preferred_element_type=F32)

    y = (ybuf_ref[...] + xs * dskip_e) * zs_ref[...]
    gw = SSM_D_INNER // G
    parts = []
    for g in range(G):
        yg = y[:, g * gw:(g + 1) * gw]
        parts.append(yg * lax.rsqrt(jnp.mean(yg * yg, axis=-1, keepdims=True) + SSM_NORM_EPS))
    y_ref[...] = (jnp.concatenate(parts, axis=1) * ng_ref[...]).astype(y_ref.dtype)


def _ssd(xbc, zs, tail, tailt, conv_w, conv_b, dt_bias, a_log, d_skip, norm_g, batch, seq):
    t = xbc.shape[0]
    L = SSM_CHUNK
    nc = seq // L
    row = lambda b, c: (b * nc + c, 0)
    const = lambda b, c: (0, 0)
    H = SSM_HEADS
    in_specs = [
        pl.BlockSpec((L, SSM_XBC), row),
        pl.BlockSpec((L, SSM_D_INNER), row),
        pl.BlockSpec((L, TAIL), row),
        pl.BlockSpec((TAIL, L), lambda b, c: (0, b * nc + c)),
        pl.BlockSpec((SSM_CONV, SSM_XBC), const),
        pl.BlockSpec((1, SSM_XBC), const),
        pl.BlockSpec((1, H), const),
        pl.BlockSpec((H, 1), const),
        pl.BlockSpec((1, H), const),
        pl.BlockSpec((H, 1), const),
        pl.BlockSpec((1, H), const),
        pl.BlockSpec((1, SSM_D_INNER), const),
    ]
    return pl.pallas_call(
        _ssd_kernel, grid=(batch, nc), in_specs=in_specs,
        out_specs=pl.BlockSpec((L, SSM_D_INNER), row),
        out_shape=jax.ShapeDtypeStruct((t, SSM_D_INNER), BF16),
        scratch_shapes=[pltpu.VMEM((L + SUBLANES, SSM_XBC), F32),
                        pltpu.VMEM((SSM_STATE, SSM_D_INNER), F32),
                        pltpu.VMEM((L, SSM_D_INNER), F32)],
        compiler_params=_cparams(("arbitrary", "arbitrary")), name="ssd",
    )(xbc, zs, tail, tailt, conv_w, conv_b, dt_bias.reshape(1, H), dt_bias.reshape(H, 1),
      a_log.reshape(1, H), a_log.reshape(H, 1), d_skip.reshape(1, H), norm_g.reshape(1, -1))


def _compress_kernel(k_ref, v_ref, kpos_ref, vpos_ref, kw1_ref, vw1_ref, kb1_ref, vb1_ref,
                     kw2_ref, vw2_ref, kc_ref, vct_ref):
    D = NSA_HEAD_DIM
    gw = CMP_STRIDE * D

    def hidden(t_ref, pos_ref, w1_ref, b1_ref, g):
        t = t_ref[:, g * gw:(g + 1) * gw]
        n = t.shape[0]
        lo = jnp.dot((t + pos_ref[0:1, :]).astype(BF16), w1_ref[...], preferred_element_type=F32)
        hi = jnp.dot((t + pos_ref[1:2, :]).astype(BF16), w1_ref[...], preferred_element_type=F32)
        pre = lo + pltpu.roll(pltpu.roll(hi, n - 1, 0), D, 1)
        return _silu(pre[:, 0:D] + b1_ref[...]).astype(BF16)

    for g in range(NSA_KV_HEADS):
        kc = jnp.dot(hidden(k_ref, kpos_ref, kw1_ref, kb1_ref, g), kw2_ref[...], preferred_element_type=F32)
        kc_ref[g] = kc.astype(BF16)
        vct = lax.dot_general(vw2_ref[...], hidden(v_ref, vpos_ref, vw1_ref, vb1_ref, g), NT_DIMS,
                              preferred_element_type=F32)
        vct_ref[g] = vct.astype(BF16)


def _compress_weights(pos, w1, b1, w2):
    half = CMP_BLOCK // 2 * NSA_HEAD_DIM
    w1cat = jnp.concatenate([w1[:half], w1[half:]], axis=1).astype(BF16)
    return pos.reshape(2, half), w1cat, b1.reshape(1, -1), w2.astype(BF16)


def _compress(k16, v16, kparams, vparams, batch, seq):
    rows = seq // CMP_STRIDE
    width = CMP_STRIDE * NSA_KV
    D = NSA_HEAD_DIM
    gw = CMP_STRIDE * D
    kpos, kw1, kb1, kw2 = _compress_weights(*kparams)
    vpos, vw1, vb1, vw2 = _compress_weights(*vparams)
    vw2 = vw2.T
    c2 = lambda b: (0, 0)
    tok = pl.BlockSpec((rows, width), lambda b: (b, 0))
    in_specs = [tok, tok,
                pl.BlockSpec((2, gw), c2), pl.BlockSpec((2, gw), c2),
                pl.BlockSpec((gw, 2 * D), c2), pl.BlockSpec((gw, 2 * D), c2),
                pl.BlockSpec((1, D), c2), pl.BlockSpec((1, D), c2),
                pl.BlockSpec((D, D), c2), pl.BlockSpec((D, D), c2)]
    out = jax.ShapeDtypeStruct((NSA_KV_HEADS, batch * rows, D), BF16)
    out_t = jax.ShapeDtypeStruct((NSA_KV_HEADS, D, batch * rows), BF16)
    ospec = pl.BlockSpec((NSA_KV_HEADS, rows, D), lambda b: (0, b, 0))
    ospec_t = pl.BlockSpec((NSA_KV_HEADS, D, rows), lambda b: (0, 0, b))
    return pl.pallas_call(
        _compress_kernel, grid=(batch,), in_specs=in_specs, out_specs=(ospec, ospec_t),
        out_shape=(out, out_t), compiler_params=_cparams(("arbitrary",)), name="compress",
    )(k16, v16, kpos, vpos, kw1, vw1, kb1, vb1, kw2, vw2)


def _nsa_kernel(qt_ref, kc_ref, vct_ref, ks_ref, vst_ref, kw_ref, vwt_ref, tailt_ref, ovt_ref, mtab_ref, cmask_ref,
                o_ref, biasq_ref, m_ref, acc_ref, part_ref, sbuf0_ref, sbuf1_ref, mbuf0_ref, mbuf1_ref):
    R, D = NSA_REP, NSA_HEAD_DIM
    nl = R * TQ
    grp = range(kc_ref.shape[0])
    sbuf_refs = (sbuf0_ref, sbuf1_ref)
    mbuf_refs = (mbuf0_ref, mbuf1_ref)
    n_stages = (ks_ref.shape[1] // TK - 1) // 2
    gp = pl.program_id(1)
    qi = pl.program_id(2)
    t0 = qi * TQ
    qt = [jnp.concatenate([qt_ref[gi * R + r] for r in range(R)], axis=1) for gi in grp]
    tpos_row = t0 + lax.broadcasted_iota(jnp.int32, (1, nl), 1) % TQ

    def k_rows(k_ref, gi, start):
        return k_ref[gi, pl.ds(pl.multiple_of(start, TK), TK), :]

    def vt_cols(vt_ref, gi, start, n):
        return vt_ref[gi, :, pl.ds(pl.multiple_of(start, TK), n)]

    def pv(vt, p):
        vt1 = jnp.concatenate([vt, jnp.ones((ONES_ROWS, vt.shape[1]), BF16)], axis=0)
        return jnp.dot(vt1, p, preferred_element_type=F32)

    def mask_tile(off):
        return jnp.concatenate([mtab_ref[pl.ds(pl.multiple_of(off, TK), TK), :]] * R, axis=1)

    def col_max(s):
        return jnp.max(s, axis=0, keepdims=True)

    def online(state, s, m_chunk, vt):
        m_new = m_chunk if state is None else jnp.maximum(state[0], m_chunk)
        contrib = pv(vt, jnp.exp2(s - m_new).astype(BF16))
        if state is None:
            return m_new, contrib
        return m_new, jnp.exp2(state[0] - m_new) * state[1] + contrib

    causal = mask_tile(MT_CAUSAL)
    far0 = jnp.maximum(t0 - 2 * TK, 0)
    mid0 = jnp.maximum(t0 - TK, 0)
    window = []
    for gi in grp:
        s_dia = jnp.dot(k_rows(kw_ref, gi, t0), qt[gi], preferred_element_type=F32) + causal
        m_dia = col_max(s_dia)
        s_mid = (jnp.dot(k_rows(kw_ref, gi, mid0), qt[gi], preferred_element_type=F32)
                 + mask_tile(jnp.where(qi >= 1, MT_ALL, MT_NONE)))
        m_mid = col_max(s_mid)
        s_far = (jnp.dot(k_rows(kw_ref, gi, far0), qt[gi], preferred_element_type=F32)
                 + mask_tile(jnp.where(qi >= 2, MT_FAR, MT_NONE)))
        m_far = col_max(s_far)
        window.append(((s_dia, m_dia, t0), (s_mid, m_mid, mid0), (s_far, m_far, far0)))

    ncr = kc_ref.shape[1]
    n_slc = ovt_ref.shape[0]
    cmask = cmask_ref[pl.ds(pl.multiple_of(qi * ncr, ncr), ncr), :]
    cmask = jnp.concatenate([cmask] * R, axis=1)
    sees_any = tpos_row >= CMP_BLOCK - 1
    jj = lax.broadcasted_iota(jnp.int32, (n_slc, TQ), 0)
    tt = t0 + lax.broadcasted_iota(jnp.int32, (n_slc, TQ), 1)
    lag = tt // SLC_BLOCK - jj
    forced = (jj == 0) | ((lag >= 0) & (lag < N_LOCAL_BLOCKS))
    valid = jj * SLC_BLOCK <= tt
    rows8 = SUBLANES
    j8 = lax.broadcasted_iota(jnp.int32, (rows8, TQ), 0)
    per = TK // SLC_BLOCK
    zrows = jnp.zeros((BIAS_ROWS - per, TQ), F32)
    o_c, o_w = [], []
    for gi in grp:
        s_m = jnp.dot(kc_ref[gi], qt[gi], preferred_element_type=F32) + cmask
        mx = jnp.max(s_m, axis=0, keepdims=True)
        p = jnp.exp2(s_m - mx)
        den = jnp.sum(p, axis=0, keepdims=True)
        pc = p * jnp.where(sees_any, 1.0 / den, 0.0)
        o_c.append(jnp.dot(vct_ref[gi], pc.astype(BF16), preferred_element_type=F32))

        psum = pc[:, 0:TQ]
        for r in range(1, R):
            psum = psum + pc[:, r * TQ:(r + 1) * TQ]
        imp_t = jnp.dot(ovt_ref[...], psum, precision=lax.Precision.HIGHEST,
                        preferred_element_type=F32)
        score = jnp.where(forced, FORCED_SCORE, jnp.where(valid, imp_t, -1.0))
        groups = [score[a:a + rows8] for a in range(0, n_slc, rows8)]
        ranks = [jnp.zeros((rows8, TQ), F32) for _ in groups]
        window_steps = dict(zip((0, n_slc // 3, 2 * n_slc // 3), window[gi]))
        win = None
        for j2 in range(n_slc):
            if j2 in window_steps:
                s_w, m_w, start_w = window_steps[j2]
                win = online(win, s_w, m_w, vt_cols(vwt_ref, gi, start_w, TK))
            sj = score[j2:j2 + 1, :]
            for a, sg in enumerate(groups):
                lo = a * rows8
                if lo + rows8 - 1 < j2:
                    beats = sj > sg
                elif lo > j2:
                    beats = sj >= sg
                else:
                    beats = (sj > sg) | ((sj == sg) & (j8 + lo > j2))
                ranks[a] = ranks[a] + jnp.where(beats, 1.0, 0.0)
        rank = jnp.concatenate(ranks, axis=0)
        selected = (rank < float(min(SLC_TOP_N, n_slc))) & (score >= 0.0)
        bias_t = jnp.where(selected, 0.0, NEG_BIG)
        for kb in range(n_slc // per):
            blk = jnp.concatenate([bias_t[kb * per:(kb + 1) * per], zrows], axis=0)
            biasq_ref[gi, kb * BIAS_ROWS:(kb + 1) * BIAS_ROWS, :] = (
                jnp.concatenate([blk] * R, axis=1).astype(BF16))
        accw = win[1]
        o_w.append(accw[0:D] / accw[D:D + 1])
    qpad = jnp.zeros((KSEL - D - BIAS_ROWS, nl), BF16)

    def q_sel(gi, kb):
        rows = biasq_ref[gi, pl.ds(pl.multiple_of(kb * BIAS_ROWS, BIAS_ROWS), BIAS_ROWS), :]
        return jnp.concatenate([qt[gi], rows, qpad], axis=0)

    def slc_scores(gi, kb):
        return jnp.dot(k_rows(ks_ref, gi, kb * TK), q_sel(gi, kb), preferred_element_type=F32)

    def fill(gi, slot, chunks):
        for c, kb in enumerate(chunks):
            s = slc_scores(gi, kb)
            sbuf_refs[slot][gi, c * TK:(c + 1) * TK, :] = s
            mbuf_refs[slot][gi, c:c + 1, :] = col_max(s)

    npairs = qi // 2
    diag = []
    for gi in grp:
        s_sel = slc_scores(gi, qi) + causal
        diag.append((s_sel, col_max(s_sel)))
        fill(gi, 0, (0, 1))

    def gates(gi):
        gate_row = pl.multiple_of(GATE_OFF + GATE_SLOT * (gp * len(grp) + gi), SUBLANES)
        sig = jax.nn.sigmoid(tailt_ref[pl.ds(gate_row, GATE_SLOT), :])
        return [jnp.concatenate([sig[3 * r + c:3 * r + c + 1, :] for r in range(R)], axis=1)
                for c in range(3)]

    for gi in grp:
        gate = gates(gi)
        part_ref[gi] = gate[0] * o_c[gi] + gate[2] * o_w[gi]

    for gi in grp:
        m_sel, acc_sel = online(None, diag[gi][0], diag[gi][1], vt_cols(vst_ref, gi, t0, TK))
        m_ref[gi] = m_sel
        acc_ref[gi] = acc_sel

    def absorb(gi, slot, first, n):
        state = (m_ref[gi], acc_ref[gi])
        for c in range(n):
            state = online(state, sbuf_refs[slot][gi, c * TK:(c + 1) * TK, :], mbuf_refs[slot][gi, c:c + 1, :],
                           vt_cols(vst_ref, gi, (first + c) * TK, TK))
        m_ref[gi] = state[0]
        acc_ref[gi] = state[1]

    for k in range(n_stages):
        @pl.when(npairs > k)
        def _(k=k):
            for gi in grp:
                fill(gi, (k + 1) % 2, (jnp.minimum(2 * k + 2, qi - 1), jnp.minimum(2 * k + 3, qi - 1)))
            for gi in grp:
                absorb(gi, k % 2, 2 * k, 2)

    for parity in range(2):
        @pl.when((qi % 2 == 1) & (npairs % 2 == parity))
        def _(parity=parity):
            for gi in grp:
                absorb(gi, parity, qi - 1, 1)

    for gi in grp:
        acc = acc_ref[gi]
        ot = part_ref[gi] + gates(gi)[1] * (acc[0:D] / acc[D:D + 1])
        stacked = jnp.concatenate([ot[:, r * TQ:(r + 1) * TQ] for r in range(R)], axis=0)
        o_ref[:, gi * R * D:(gi + 1) * R * D] = stacked.T.astype(o_ref.dtype)


def _overlap_t(seq):
    n_cmp = (seq - CMP_BLOCK) // CMP_STRIDE + 1
    n_slc = seq // SLC_BLOCK
    cs = np.arange(n_cmp) * CMP_STRIDE
    ss = np.arange(n_slc) * SLC_BLOCK
    overlap = np.clip(np.minimum(cs[:, None] + CMP_BLOCK, ss[None, :] + SLC_BLOCK)
                      - np.maximum(cs[:, None], ss[None, :]), 0, None) / CMP_BLOCK
    ovt = np.zeros((n_slc, n_cmp + 1), np.float32)
    ovt[:, :n_cmp] = overlap.T
    return jnp.asarray(ovt)


def _mask_tiles():
    ki = np.arange(TK)[:, None]
    qi = np.arange(TQ)[None, :]
    neg = np.float32(NEG_BIG)
    none = np.full((TK, TQ), neg, np.float32)
    far = np.where(ki > qi, np.float32(0), neg)
    full = np.zeros((TK, TQ), np.float32)
    causal = np.where(ki <= qi, np.float32(0), neg)
    return jnp.asarray(np.concatenate([none, far, full, causal], axis=0))


def _cmp_masks(seq):
    ncr = seq // CMP_STRIDE
    n_cmp = (seq - CMP_BLOCK) // CMP_STRIDE + 1
    n = np.arange(ncr)[None, :, None]
    t = (np.arange(seq // TQ)[:, None, None] * TQ + np.arange(TQ)[None, None, :])
    visible = (n * CMP_STRIDE + CMP_BLOCK - 1 <= t) & (n < n_cmp)
    return jnp.asarray(np.where(visible, np.float32(0), np.float32(NEG_BIG)).reshape(-1, TQ))


def _nsa(qt, kc, vct, ks, vst, kw, vwt, tailt, batch, seq):
    t = batch * seq
    G, R, D = NSA_KV_HEADS, NSA_REP, NSA_HEAD_DIM
    nq = seq // TQ
    ncr = seq // CMP_STRIDE
    n_slc = seq // SLC_BLOCK
    assert TQ == TK and WINDOW == 2 * TK, "window branch visits exactly three key chunks"
    assert TK // SLC_BLOCK <= BIAS_ROWS and D + BIAS_ROWS <= KSEL
    P = NSA_GROUPS_PER_STEP
    const = lambda b, g, i: (0, 0)
    vtspec = pl.BlockSpec((P, D, seq), lambda b, g, i: (g, 0, b))
    in_specs = [
        pl.BlockSpec((P * R, D, TQ), lambda b, g, i: (g, 0, b * nq + i)),
        pl.BlockSpec((P, ncr, D), lambda b, g, i: (g, b, 0)),
        pl.BlockSpec((P, D, ncr), lambda b, g, i: (g, 0, b)),
        pl.BlockSpec((P, seq, KSEL), lambda b, g, i: (g, b, 0)),
        vtspec,
        pl.BlockSpec((P, seq, D), lambda b, g, i: (g, b, 0)),
        vtspec,
        pl.BlockSpec((TAIL, TQ), lambda b, g, i: (0, b * nq + i)),
        pl.BlockSpec((n_slc, ncr), const),
        pl.BlockSpec((4 * TK, TQ), const),
        pl.BlockSpec((nq * ncr, TQ), const),
    ]
    return pl.pallas_call(
        _nsa_kernel, grid=(batch, G // P, nq), in_specs=in_specs,
        out_specs=pl.BlockSpec((TQ, P * R * D), lambda b, g, i: (b * nq + i, g)),
        out_shape=jax.ShapeDtypeStruct((t, NSA_Q), BF16),
        scratch_shapes=[pltpu.VMEM((P, seq // TK * BIAS_ROWS, R * TQ), BF16),
                        pltpu.VMEM((P, 1, R * TQ), F32),
                        pltpu.VMEM((P, D + ONES_ROWS, R * TQ), F32),
                        pltpu.VMEM((P, D, R * TQ), F32),
                        pltpu.VMEM((P, 2 * TK, R * TQ), F32),
                        pltpu.VMEM((P, 2 * TK, R * TQ), F32),
                        pltpu.VMEM((P, SUBLANES, R * TQ), F32),
                        pltpu.VMEM((P, SUBLANES, R * TQ), F32)],
        compiler_params=_cparams(("arbitrary", "arbitrary", "arbitrary")), name="nsa",
    )(qt, kc, vct, ks, vst, kw, vwt, tailt, _overlap_t(seq), _mask_tiles(), _cmp_masks(seq))


def _outproj_kernel(y_ref, o_ref, x_ref, w_ref, ag_ref, ng_ref, x1_ref, h2_ref):
    yn = _rms(o_ref[...].astype(F32), ag_ref[...], NORM_EPS).astype(BF16)
    x1 = (x_ref[...]
          + jnp.dot(y_ref[...], w_ref[0:SSM_D_INNER, :], preferred_element_type=F32)
          + jnp.dot(yn, w_ref[SSM_D_INNER:, :], preferred_element_type=F32))
    x1_ref[...] = x1
    h2_ref[...] = _rms(x1, ng_ref[...], NORM_EPS).astype(BF16)


def _out_proj(y_ssm, o_nsa, x2, w_out, attn_g, norm2_g):
    t = x2.shape[0]
    tm = TM_OUT_PROJ
    row = lambda i: (i, 0)
    const = lambda i: (0, 0)
    tok = pl.BlockSpec((tm, D_MODEL), row)
    vec = pl.BlockSpec((1, D_MODEL), const)
    return pl.pallas_call(
        _outproj_kernel, grid=(t // tm,),
        in_specs=[tok, tok, tok, pl.BlockSpec((SSM_D_INNER + NSA_Q, D_MODEL), const), vec, vec],
        out_specs=(tok, tok),
        out_shape=(jax.ShapeDtypeStruct((t, D_MODEL), F32), jax.ShapeDtypeStruct((t, D_MODEL), BF16)),
        compiler_params=_cparams(("arbitrary",)), name="out_proj",
    )(y_ssm, o_nsa, x2, w_out, attn_g, norm2_g)


def _ffn_kernel(h_ref, halo_ref, x1_ref, wup_ref, cw_ref, cb_ref, wd_ref, fg_ref, out_ref, act_ref,
                *, tiles_per_seq):
    i = pl.program_id(0)
    tm = h_ref.shape[0]
    pad = halo_ref.shape[0]
    tn = TN_FFN
    halo = halo_ref[...]
    halo = jnp.where(i % tiles_per_seq == 0, jnp.zeros_like(halo), halo)
    hc = jnp.concatenate([halo, h_ref[...]], axis=0)

    def branch(c0):
        cols = slice(c0, c0 + tn)
        u = jnp.dot(hc, wup_ref[:, cols], preferred_element_type=F32)
        out = cb_ref[:, cols] + cw_ref[FFN_CONV - 1:FFN_CONV, cols] * u[pad:pad + tm, :]
        for k in range(FFN_CONV - 1):
            shifted = pltpu.roll(u, FFN_CONV - 1 - k, 0)
            out = out + cw_ref[k:k + 1, cols] * shifted[pad:pad + tm, :]
        return out

    for j in range(D_FF // tn):
        act = _silu(branch(j * tn)) * branch(D_FF + j * tn)
        act_ref[:, j * tn:(j + 1) * tn] = act.astype(BF16)
    down = jnp.dot(act_ref[...], wd_ref[...], preferred_element_type=F32)
    out_ref[...] = _rms(x1_ref[...] + down, fg_ref[...], NORM_EPS)


def _ffn(h2, x1, w_up, conv_w, conv_b, w_down, final_g, seq):
    t = h2.shape[0]
    tm = TM_FFN
    pad = BF16_SUBLANES
    tok = pl.BlockSpec((tm, D_MODEL), lambda i: (i, 0))

    def resident(shape):
        return pl.BlockSpec(shape, lambda i: (0, 0), pipeline_mode=pl.Buffered(1))

    in_specs = [
        tok,
        pl.BlockSpec((pad, D_MODEL), lambda i: (jnp.maximum(i * (tm // pad) - 1, 0), 0)),
        tok,
        resident((D_MODEL, 2 * D_FF)),
        resident((FFN_CONV, 2 * D_FF)),
        resident((1, 2 * D_FF)),
        resident((D_FF, D_MODEL)),
        resident((1, D_MODEL)),
    ]
    return pl.pallas_call(
        functools.partial(_ffn_kernel, tiles_per_seq=seq // tm), grid=(t // tm,),
        in_specs=in_specs, out_specs=tok,
        out_shape=jax.ShapeDtypeStruct((t, D_MODEL), F32),
        scratch_shapes=[pltpu.VMEM((tm, D_FF), BF16)],
        compiler_params=_cparams(("arbitrary",)), name="ffn",
    )(h2, h2, x1, w_up, conv_w, conv_b, w_down, final_g)


def _rope_tables(seq):
    half = NSA_HEAD_DIM // 2
    inv_freq = 1.0 / (ROPE_THETA ** (jnp.arange(0, NSA_HEAD_DIM, 2, dtype=F32) / NSA_HEAD_DIM))
    ang = jnp.arange(seq).astype(F32)[:, None] * inv_freq[None, :]
    cos, sin = jnp.cos(ang), jnp.sin(ang)
    reps = LANES // NSA_HEAD_DIM
    cos_t = jnp.tile(jnp.concatenate([cos, cos], axis=1), (1, reps))
    sin_t = jnp.tile(jnp.concatenate([-sin, sin], axis=1), (1, reps))
    return cos_t, sin_t


def _split_w_in(w):
    o_z, o_xbc, o_dt, o_q, o_kv, o_g = np.cumsum([0, SSM_D_INNER, SSM_XBC, SSM_HEADS, NSA_Q, 6 * NSA_KV]).tolist()
    assert (o_dt, o_q, o_g - o_q) == (SSM_D_INNER + SSM_XBC, W_QKV_OFF, NSA_Q + 6 * NSA_KV)
    wt = w.T
    per = 3 * NSA_REP
    tail_rows = [wt[o_dt:o_q]]
    for g in range(NSA_KV_HEADS):
        tail_rows += [wt[o_g + g * per:o_g + (g + 1) * per], jnp.zeros((GATE_SLOT - per, wt.shape[1]), w.dtype)]
    tail_rows.append(jnp.zeros((TAIL - GATE_OFF - NSA_KV_HEADS * GATE_SLOT, wt.shape[1]), w.dtype))
    return wt, jnp.concatenate(tail_rows, axis=0).astype(BF16)


def kernel(x, norm1_g, w_in, ssm_conv_w, ssm_conv_b, ssm_dt_bias, ssm_a_log, ssm_d, ssm_norm_g, cmp_k_pos, cmp_k_w1, cmp_k_b1, cmp_k_w2, cmp_v_pos, cmp_v_w1, cmp_v_b1, cmp_v_w2, attn_norm_g, w_out, norm2_g, ffn_w_up, ffn_conv_w, ffn_conv_b, ffn_w_down, final_norm_g):
    batch, seq, d = x.shape
    assert w_in.shape[0] == 1, "single-layer problem"
    l = 0
    cos_t, sin_t = _rope_tables(seq)
    x2 = x.reshape(batch * seq, d)
    zs, xbc, qt, kc_r, vc_r, ks, vst, kw, vwt, tail, tailt = _in_proj(
        x2, norm1_g[l].reshape(1, d), _split_w_in(w_in.reshape(d, -1)), cos_t, sin_t, seq)
    y_ssm = _ssd(xbc, zs, tail, tailt, ssm_conv_w[l], ssm_conv_b[l].reshape(1, -1), ssm_dt_bias[l],
                 ssm_a_log[l], ssm_d[l], ssm_norm_g[l], batch, seq)
    kc, vct = _compress(kc_r, vc_r,
                        (cmp_k_pos[l], cmp_k_w1[l], cmp_k_b1[l], cmp_k_w2[l]),
                        (cmp_v_pos[l], cmp_v_w1[l], cmp_v_b1[l], cmp_v_w2[l]), batch, seq)
    o_nsa = _nsa(qt, kc, vct, ks, vst, kw, vwt, tailt, batch, seq)
    x1, h2 = _out_proj(y_ssm, o_nsa, x2, w_out.reshape(-1, d).astype(BF16), attn_norm_g[l].reshape(1, d),
                       norm2_g[l].reshape(1, d))
    out = _ffn(h2, x1, ffn_w_up.reshape(d, -1).astype(BF16), ffn_conv_w[l], ffn_conv_b[l].reshape(1, -1),
               ffn_w_down.reshape(-1, d).astype(BF16), final_norm_g.reshape(1, d), seq)
    return out.reshape(batch, seq, d)
```

```python
import functools
import math

import numpy as np
import jax
import jax.numpy as jnp
from jax import lax
from jax.experimental import pallas as pl
from jax.experimental.pallas import tpu as pltpu

F32 = jnp.float32
BF16 = jnp.bfloat16

D_MODEL = 1024
SSM_D_INNER = 1024
SSM_HEAD_DIM = 64
SSM_HEADS = 16
SSM_GROUPS = 2
SSM_STATE = 128
SSM_CONV = 4
SSM_CHUNK = 128
SSM_XBC = SSM_D_INNER + 2 * SSM_GROUPS * SSM_STATE
SSM_NORM_EPS = 1e-5
NSA_HEADS = 16
NSA_KV_HEADS = 4
NSA_REP = NSA_HEADS // NSA_KV_HEADS
NSA_HEAD_DIM = 64
NSA_Q = NSA_HEADS * NSA_HEAD_DIM
NSA_KV = NSA_KV_HEADS * NSA_HEAD_DIM
CMP_BLOCK = 32
CMP_STRIDE = 16
SLC_BLOCK = 64
SLC_TOP_N = 16
N_LOCAL_BLOCKS = 2
FORCED_SCORE = 1e4
WINDOW = 512
ROPE_THETA = 10000.0
D_FF = 2816
FFN_CONV = 3
NORM_EPS = 1e-6
NEG_BIG = -1e30

LANES = 128
SUBLANES = 8
BF16_SUBLANES = 16
TAIL = LANES
GATE_OFF = SSM_HEADS
GATE_SLOT = 16
W_QKV_OFF = SSM_D_INNER + SSM_XBC + SSM_HEADS

TM_IN_PROJ = 256
TM_OUT_PROJ = 512
TQ = 256
TK = 256
NSA_GROUPS_PER_STEP = 4
ONES_ROWS = 16
KSEL = LANES
BIAS_ROWS = 16
MT_NONE, MT_FAR, MT_ALL, MT_CAUSAL = 0, TK, 2 * TK, 3 * TK
TM_FFN = 512
TN_FFN = 256
VMEM_LIMIT = 56 * 1024 * 1024

NT_DIMS = (((1,), (1,)), ((), ()))


def _cparams(sem):
    return pltpu.CompilerParams(dimension_semantics=sem, vmem_limit_bytes=VMEM_LIMIT)


def _rms(x, g, eps):
    return x * lax.rsqrt(jnp.mean(x * x, axis=-1, keepdims=True) + eps) * g


def _silu(x):
    h = 0.5 * x
    return h + h * jnp.tanh(h)


def _inproj_kernel(x_ref, g_ref, w_ref, wt_ref, cos_ref, sin_ref,
                   zs_ref, xbc_ref, qt_ref, kc_ref, vc_ref, ks_ref, vst_ref, kw_ref, vwt_ref,
                   tail_ref, tailt_ref, kbuf_ref, vbuf_ref, wa_ref, wb_ref, *, tiles_per_seq):
    tm = x_ref.shape[0]

    @pl.when(pl.program_id(0) == 0)
    def _():
        strip = 2 * LANES
        for r0 in range(0, wa_ref.shape[0], strip):
            wa_ref[r0:r0 + strip, :] = w_ref[r0:r0 + strip, :].astype(BF16)
            wb_ref[r0:r0 + strip, :] = w_ref[W_QKV_OFF + r0:W_QKV_OFF + r0 + strip, :].astype(BF16)
    seq_tile = pl.program_id(0) % tiles_per_seq
    h = _rms(x_ref[...], g_ref[...], NORM_EPS).astype(BF16)

    def mm(w_ref, lo, hi):
        return lax.dot_general(h, w_ref[lo:hi, :], NT_DIMS, preferred_element_type=F32)

    tail = mm(wt_ref, 0, TAIL)
    tail_ref[...] = tail
    tailt_ref[...] = tail.T

    cos = cos_ref[...]
    sin = sin_ref[...]
    lane = lax.broadcasted_iota(jnp.int32, (tm, LANES), 1)
    first_half = (lane % NSA_HEAD_DIM) < (NSA_HEAD_DIM // 2)

    def rope(xc):
        partner = jnp.where(first_half, pltpu.roll(xc, LANES - 32, 1), pltpu.roll(xc, 32, 1))
        return xc * cos + partner * sin

    q = mm(wb_ref, 0, NSA_Q)
    scale = NSA_HEAD_DIM ** -0.5 * math.log2(math.e)
    for c in range(NSA_Q // LANES):
        rt = (rope(q[:, c * LANES:(c + 1) * LANES]) * scale).T.astype(BF16)
        qt_ref[2 * c] = rt[:NSA_HEAD_DIM]
        qt_ref[2 * c + 1] = rt[NSA_HEAD_DIM:]

    kv = mm(wb_ref, NSA_Q, NSA_Q + 6 * NSA_KV)

    def seg(i):
        return kv[:, i * NSA_KV:(i + 1) * NSA_KV]

    def rope_seg(x):
        return jnp.concatenate([rope(x[:, :LANES]), rope(x[:, LANES:])], axis=1)

    def store_heads(ref, x):
        for g in range(NSA_KV_HEADS):
            ref[g] = x[:, g * NSA_HEAD_DIM:(g + 1) * NSA_HEAD_DIM].astype(BF16)

    def store_heads_t(ref, x):
        xt = x.T.astype(BF16)
        for g in range(NSA_KV_HEADS):
            ref[g] = xt[g * NSA_HEAD_DIM:(g + 1) * NSA_HEAD_DIM]

    kc_tile = rope_seg(seg(0))
    vc_tile = seg(1)
    for c in range(NSA_KV // LANES):
        kbuf_ref[c] = kc_tile[:, c * LANES:(c + 1) * LANES]
        vbuf_ref[c] = vc_tile[:, c * LANES:(c + 1) * LANES]
    gw = CMP_STRIDE * NSA_HEAD_DIM
    for l in range(CMP_STRIDE):
        for c in range(NSA_KV // LANES):
            k_rows = kbuf_ref[c, pl.ds(l, tm // CMP_STRIDE, stride=CMP_STRIDE), :]
            v_rows = vbuf_ref[c, pl.ds(l, tm // CMP_STRIDE, stride=CMP_STRIDE), :]
            for half in range(LANES // NSA_HEAD_DIM):
                g = c * (LANES // NSA_HEAD_DIM) + half
                src = slice(half * NSA_HEAD_DIM, (half + 1) * NSA_HEAD_DIM)
                dst = slice(g * gw + l * NSA_HEAD_DIM, g * gw + (l + 1) * NSA_HEAD_DIM)
                kc_ref[:, dst] = k_rows[:, src]
                vc_ref[:, dst] = v_rows[:, src]
    pos = seq_tile * tm + lax.broadcasted_iota(jnp.int32, (tm, NSA_HEAD_DIM), 0)
    block_in_chunk = (pos // SLC_BLOCK) % (TK // SLC_BLOCK)
    onehot = jnp.where(lax.broadcasted_iota(jnp.int32, (tm, NSA_HEAD_DIM), 1) == block_in_chunk, 1.0, 0.0)
    ks = rope_seg(seg(2))
    for g in range(NSA_KV_HEADS):
        kg = ks[:, g * NSA_HEAD_DIM:(g + 1) * NSA_HEAD_DIM]
        ks_ref[g] = jnp.concatenate([kg, onehot], axis=1).astype(BF16)
    store_heads_t(vst_ref, seg(3))
    store_heads(kw_ref, rope_seg(seg(4)))
    store_heads_t(vwt_ref, seg(5))

    zs_ref[...] = _silu(mm(wa_ref, 0, SSM_D_INNER))
    xbc_ref[...] = mm(wa_ref, SSM_D_INNER, SSM_D_INNER + SSM_XBC)


def _in_proj(x2, norm_g, w_parts, cos_t, sin_t, seq):
    t = x2.shape[0]
    tm = TM_IN_PROJ
    nseq = seq // tm
    row = lambda i: (i, 0)
    const = lambda i: (0, 0)
    heads = lambda i: (0, i, 0)
    heads_t = lambda i: (0, 0, i)
    out_shape = (
        jax.ShapeDtypeStruct((t, SSM_D_INNER), F32),
        jax.ShapeDtypeStruct((t, SSM_XBC), F32),
        jax.ShapeDtypeStruct((NSA_HEADS, NSA_HEAD_DIM, t), BF16),
        jax.ShapeDtypeStruct((t // CMP_STRIDE, CMP_STRIDE * NSA_KV), F32),
        jax.ShapeDtypeStruct((t // CMP_STRIDE, CMP_STRIDE * NSA_KV), F32),
        jax.ShapeDtypeStruct((NSA_KV_HEADS, t, KSEL), BF16),
        jax.ShapeDtypeStruct((NSA_KV_HEADS, NSA_HEAD_DIM, t), BF16),
        jax.ShapeDtypeStruct((NSA_KV_HEADS, t, NSA_HEAD_DIM), BF16),
        jax.ShapeDtypeStruct((NSA_KV_HEADS, NSA_HEAD_DIM, t), BF16),
        jax.ShapeDtypeStruct((t, TAIL), F32),
        jax.ShapeDtypeStruct((TAIL, t), F32),
    )
    kvh = pl.BlockSpec((NSA_KV_HEADS, tm, NSA_HEAD_DIM), heads)
    kvh_t = pl.BlockSpec((NSA_KV_HEADS, NSA_HEAD_DIM, tm), heads_t)
    out_specs = (
        pl.BlockSpec((tm, SSM_D_INNER), row),
        pl.BlockSpec((tm, SSM_XBC), row),
        pl.BlockSpec((NSA_HEADS, NSA_HEAD_DIM, tm), heads_t),
        pl.BlockSpec((tm // CMP_STRIDE, CMP_STRIDE * NSA_KV), row),
        pl.BlockSpec((tm // CMP_STRIDE, CMP_STRIDE * NSA_KV), row),
        pl.BlockSpec((NSA_KV_HEADS, tm, KSEL), heads), kvh_t, kvh, kvh_t,
        pl.BlockSpec((tm, TAIL), row),
        pl.BlockSpec((TAIL, tm), lambda i: (0, i)),
    )
    in_specs = [
        pl.BlockSpec((tm, D_MODEL), row),
        pl.BlockSpec((1, D_MODEL), const),
        pl.BlockSpec(w_parts[0].shape, const, pipeline_mode=pl.Buffered(1)),
        pl.BlockSpec((TAIL, D_MODEL), const),
        pl.BlockSpec((tm, LANES), lambda i: (i % nseq, 0)),
        pl.BlockSpec((tm, LANES), lambda i: (i % nseq, 0)),
    ]
    return pl.pallas_call(
        functools.partial(_inproj_kernel, tiles_per_seq=nseq), grid=(t // tm,), in_specs=in_specs,
        out_specs=out_specs,
        out_shape=out_shape,
        scratch_shapes=[pltpu.VMEM((NSA_KV // LANES, tm, LANES), F32),
                        pltpu.VMEM((NSA_KV // LANES, tm, LANES), F32),
                        pltpu.VMEM((SSM_D_INNER + SSM_XBC, D_MODEL), BF16),
                        pltpu.VMEM((NSA_Q + 6 * NSA_KV, D_MODEL), BF16)],
        compiler_params=_cparams(("arbitrary",)), name="in_proj",
    )(x2, norm_g, *w_parts, cos_t, sin_t)


def _ssd_kernel(xbc_ref, zs_ref, tail_ref, tailt_ref, cw_ref, cb_ref, dtb_ref, dtbt_ref,
                alog_ref, alogt_ref, dskip_ref, ng_ref,
                y_ref, xcat_ref, state_ref, ybuf_ref):
    L, P, N, H, G = SSM_CHUNK, SSM_HEAD_DIM, SSM_STATE, SSM_HEADS, SSM_GROUPS
    HG = H // G
    halo = SUBLANES

    @pl.when(pl.program_id(1) == 0)
    def _():
        xcat_ref[0:halo, :] = jnp.zeros((halo, SSM_XBC), F32)
        state_ref[...] = jnp.zeros_like(state_ref)

    xcat_ref[halo:halo + L, :] = xbc_ref[...]
    xcat = xcat_ref[...]
    conv = cb_ref[...] + cw_ref[SSM_CONV - 1:SSM_CONV, :] * xcat[halo:halo + L]
    for k in range(SSM_CONV - 1):
        shifted = pltpu.roll(xcat, SSM_CONV - 1 - k, 0)
        conv = conv + cw_ref[k:k + 1, :] * shifted[halo:halo + L]
    xcat_ref[0:halo, :] = xcat[L:L + halo]
    u = _silu(conv)
    xs = u[:, :SSM_D_INNER]
    bm = u[:, SSM_D_INNER:SSM_D_INNER + G * N]
    cm = u[:, SSM_D_INNER + G * N:]

    def softplus(v):
        return jnp.maximum(v, 0.0) + jnp.log1p(jnp.exp(-jnp.abs(v)))

    dt = softplus(tail_ref[:, 0:H] + dtb_ref[...])
    dtt = softplus(tailt_ref[0:H, :] + dtbt_ref[...])
    da = dt * (-jnp.exp(alog_ref[...]))
    dat = dtt * (-jnp.exp(alogt_ref[...]))
    ri = lax.broadcasted_iota(jnp.int32, (L, L), 0)
    ci = lax.broadcasted_iota(jnp.int32, (L, L), 1)
    tri = ci <= ri
    hi = lax.Precision.HIGHEST
    acs = jnp.dot(tri.astype(F32), da, precision=hi, preferred_element_type=F32)
    acst = jnp.dot(dat, (ri <= ci).astype(F32), precision=hi, preferred_element_type=F32)
    last = acs[L - 1:L, :]
    w_state = dt * jnp.exp(last - acs)
    eacs = jnp.exp(acs)
    cdec = jnp.exp(last)

    hrow = lax.broadcasted_iota(jnp.int32, (H, SSM_D_INNER), 0)
    hcol = lax.broadcasted_iota(jnp.int32, (H, SSM_D_INNER), 1)
    expand = jnp.where(hcol // P == hrow, 1.0, 0.0).astype(BF16)

    def split3(v):
        v_hi = v.astype(BF16)
        r1 = v - v_hi.astype(F32)
        v_mid = r1.astype(BF16)
        v_lo = (r1 - v_mid.astype(F32)).astype(BF16)
        return jnp.concatenate([v_hi, v_mid, v_lo], axis=1)

    small = jnp.concatenate([cdec, dskip_ref[...], jnp.zeros((6, H), F32)], axis=0)
    per_head = jnp.concatenate([dt, w_state, eacs, small], axis=0)
    spread = jnp.dot(split3(per_head), jnp.concatenate([expand] * 3, axis=0), preferred_element_type=F32)
    cdec_e = spread[3 * L:3 * L + 1, :]
    dskip_e = spread[3 * L + 1:3 * L + 2, :]
    xdt = (xs * spread[0:L]).astype(BF16)
    wst = (xs * spread[L:2 * L]).astype(BF16)
    eacs_e = spread[2 * L:3 * L]

    for g in range(G):
        bm_g = bm[:, g * N:(g + 1) * N]
        cm_g = cm[:, g * N:(g + 1) * N].astype(BF16)
        cb = lax.dot_general(cm_g, bm_g.astype(BF16), NT_DIMS, preferred_element_type=F32)
        cols = slice(g * HG * P, (g + 1) * HG * P)
        st = state_ref[:, cols]
        y_off = jnp.dot(cm_g, st.astype(BF16), preferred_element_type=F32) * eacs_e[:, cols]
        ybuf_ref[:, cols] = y_off
        bmt = bm_g.T.astype(BF16)
        state_ref[:, cols] = st * cdec_e[:, cols] + jnp.dot(bmt, wst[:, cols], preferred_element_type=F32)
        for r in range(HG):
            hh = g * HG + r
            diff = acs[:, hh:hh + 1] - acst[hh:hh + 1, :]
            seg = jnp.exp(jnp.where(tri, diff, -jnp.inf))
            lmat = (cb * seg).astype(BF16)
            hc = slice(hh * P, (hh + 1) * P)
            ybuf_ref[:, hc] = ybuf_ref[:, hc] + jnp.dot(lmat, xdt[:, hc], preferred_element_type=F32)

    y = (ybuf_ref[...] + xs * dskip_e) * zs_ref[...]
    gw = SSM_D_INNER // G
    parts = []
    for g in range(G):
        yg = y[:, g * gw:(g + 1) * gw]
        parts.append(yg * lax.rsqrt(jnp.mean(yg * yg, axis=-1, keepdims=True) + SSM_NORM_EPS))
    y_ref[...] = (jnp.concatenate(parts, axis=1) * ng_ref[...]).astype(y_ref.dtype)


def _ssd(xbc, zs, tail, tailt, conv_w, conv_b, dt_bias, a_log, d_skip, norm_g, batch, seq):
    t = xbc.shape[0]
    L = SSM_CHUNK
    nc = seq // L
    row = lambda b, c: (b * nc + c, 0)
    const = lambda b, c: (0, 0)
    H = SSM_HEADS
    in_specs = [
        pl.BlockSpec((L, SSM_XBC), row),
        pl.BlockSpec((L, SSM_D_INNER), row),
        pl.BlockSpec((L, TAIL), row),
        pl.BlockSpec((TAIL, L), lambda b, c: (0, b * nc + c)),
        pl.BlockSpec((SSM_CONV, SSM_XBC), const),
        pl.BlockSpec((1, SSM_XBC), const),
        pl.BlockSpec((1, H), const),
        pl.BlockSpec((H, 1), const),
        pl.BlockSpec((1, H), const),
        pl.BlockSpec((H, 1), const),
        pl.BlockSpec((1, H), const),
        pl.BlockSpec((1, SSM_D_INNER), const),
    ]
    return pl.pallas_call(
        _ssd_kernel, grid=(batch, nc), in_specs=in_specs,
        out_specs=pl.BlockSpec((L, SSM_D_INNER), row),
        out_shape=jax.ShapeDtypeStruct((t, SSM_D_INNER), BF16),
        scratch_shapes=[pltpu.VMEM((L + SUBLANES, SSM_XBC), F32),
                        pltpu.VMEM((SSM_STATE, SSM_D_INNER), F32),
                        pltpu.VMEM((L, SSM_D_INNER), F32)],
        compiler_params=_cparams(("arbitrary", "arbitrary")), name="ssd",
    )(xbc, zs, tail, tailt, conv_w, conv_b, dt_bias.reshape(1, H), dt_bias.reshape(H, 1),
      a_log.reshape(1, H), a_log.reshape(H, 1), d_skip.reshape(1, H), norm_g.reshape(1, -1))


def _compress_kernel(k_ref, v_ref, kpos_ref, vpos_ref, kw1_ref, vw1_ref, kb1_ref, vb1_ref,
                     kw2_ref, vw2_ref, kc_ref, vct_ref):
    D = NSA_HEAD_DIM
    gw = CMP_STRIDE * D

    def hidden(t_ref, pos_ref, w1_ref, b1_ref, g):
        t = t_ref[:, g * gw:(g + 1) * gw]
        n = t.shape[0]
        lo = jnp.dot((t + pos_ref[0:1, :]).astype(BF16), w1_ref[...], preferred_element_type=F32)
        hi = jnp.dot((t + pos_ref[1:2, :]).astype(BF16), w1_ref[...], preferred_element_type=F32)
        pre = lo + pltpu.roll(pltpu.roll(hi, n - 1, 0), D, 1)
        return _silu(pre[:, 0:D] + b1_ref[...]).astype(BF16)

    for g in range(NSA_KV_HEADS):
        kc = jnp.dot(hidden(k_ref, kpos_ref, kw1_ref, kb1_ref, g), kw2_ref[...], preferred_element_type=F32)
        kc_ref[g] = kc.astype(BF16)
        vct = lax.dot_general(vw2_ref[...], hidden(v_ref, vpos_ref, vw1_ref, vb1_ref, g), NT_DIMS,
                              preferred_element_type=F32)
        vct_ref[g] = vct.astype(BF16)


def _compress_weights(pos, w1, b1, w2):
    half = CMP_BLOCK // 2 * NSA_HEAD_DIM
    w1cat = jnp.concatenate([w1[:half], w1[half:]], axis=1).astype(BF16)
    return pos.reshape(2, half), w1cat, b1.reshape(1, -1), w2.astype(BF16)


def _compress(k16, v16, kparams, vparams, batch, seq):
    rows = seq // CMP_STRIDE
    width = CMP_STRIDE * NSA_KV
    D = NSA_HEAD_DIM
    gw = CMP_STRIDE * D
    kpos, kw1, kb1, kw2 = _compress_weights(*kparams)
    vpos, vw1, vb1, vw2 = _compress_weights(*vparams)
    vw2 = vw2.T
    c2 = lambda b: (0, 0)
    tok = pl.BlockSpec((rows, width), lambda b: (b, 0))
    in_specs = [tok, tok,
                pl.BlockSpec((2, gw), c2), pl.BlockSpec((2, gw), c2),
                pl.BlockSpec((gw, 2 * D), c2), pl.BlockSpec((gw, 2 * D), c2),
                pl.BlockSpec((1, D), c2), pl.BlockSpec((1, D), c2),
                pl.BlockSpec((D, D), c2), pl.BlockSpec((D, D), c2)]
    out = jax.ShapeDtypeStruct((NSA_KV_HEADS, batch * rows, D), BF16)
    out_t = jax.ShapeDtypeStruct((NSA_KV_HEADS, D, batch * rows), BF16)
    ospec = pl.BlockSpec((NSA_KV_HEADS, rows, D), lambda b: (0, b, 0))
    ospec_t = pl.BlockSpec((NSA_KV_HEADS, D, rows), lambda b: (0, 0, b))
    return pl.pallas_call(
        _compress_kernel, grid=(batch,), in_specs=in_specs, out_specs=(ospec, ospec_t),
        out_shape=(out, out_t), compiler_params=_cparams(("arbitrary",)), name="compress",
    )(k16, v16, kpos, vpos, kw1, vw1, kb1, vb1, kw2, vw2)


def _nsa_kernel(qt_ref, kc_ref, vct_ref, ks_ref, vst_ref, kw_ref, vwt_ref, tailt_ref, ovt_ref, mtab_ref, cmask_ref,
                o_ref, biasq_ref, m_ref, acc_ref, part_ref, sbuf0_ref, sbuf1_ref, mbuf0_ref, mbuf1_ref):
    R, D = NSA_REP, NSA_HEAD_DIM
    nl = R * TQ
    grp = range(kc_ref.shape[0])
    sbuf_refs = (sbuf0_ref, sbuf1_ref)
    mbuf_refs = (mbuf0_ref, mbuf1_ref)
    n_stages = (ks_ref.shape[1] // TK - 1) // 2
    gp = pl.program_id(1)
    qi = pl.program_id(2)
    t0 = qi * TQ
    qt = [jnp.concatenate([qt_ref[gi * R + r] for r in range(R)], axis=1) for gi in grp]
    tpos_row = t0 + lax.broadcasted_iota(jnp.int32, (1, nl), 1) % TQ

    def k_rows(k_ref, gi, start):
        return k_ref[gi, pl.ds(pl.multiple_of(start, TK), TK), :]

    def vt_cols(vt_ref, gi, start, n):
        return vt_ref[gi, :, pl.ds(pl.multiple_of(start, TK), n)]

    def pv(vt, p):
        vt1 = jnp.concatenate([vt, jnp.ones((ONES_ROWS, vt.shape[1]), BF16)], axis=0)
        return jnp.dot(vt1, p, preferred_element_type=F32)

    def mask_tile(off):
        return jnp.concatenate([mtab_ref[pl.ds(pl.multiple_of(off, TK), TK), :]] * R, axis=1)

    def col_max(s):
        return jnp.max(s, axis=0, keepdims=True)

    def online(state, s, m_chunk, vt):
        m_new = m_chunk if state is None else jnp.maximum(state[0], m_chunk)
        contrib = pv(vt, jnp.exp2(s - m_new).astype(BF16))
        if state is None:
            return m_new, contrib
        return m_new, jnp.exp2(state[0] - m_new) * state[1] + contrib

    causal = mask_tile(MT_CAUSAL)
    far0 = jnp.maximum(t0 - 2 * TK, 0)
    mid0 = jnp.maximum(t0 - TK, 0)
    window = []
    for gi in grp:
        s_dia = jnp.dot(k_rows(kw_ref, gi, t0), qt[gi], preferred_element_type=F32) + causal
        m_dia = col_max(s_dia)
        s_mid = (jnp.dot(k_rows(kw_ref, gi, mid0), qt[gi], preferred_element_type=F32)
                 + mask_tile(jnp.where(qi >= 1, MT_ALL, MT_NONE)))
        m_mid = col_max(s_mid)
        s_far = (jnp.dot(k_rows(kw_ref, gi, far0), qt[gi], preferred_element_type=F32)
                 + mask_tile(jnp.where(qi >= 2, MT_FAR, MT_NONE)))
        m_far = col_max(s_far)
        window.append(((s_dia, m_dia, t0), (s_mid, m_mid, mid0), (s_far, m_far, far0)))

    ncr = kc_ref.shape[1]
    n_slc = ovt_ref.shape[0]
    cmask = cmask_ref[pl.ds(pl.multiple_of(qi * ncr, ncr), ncr), :]
    cmask = jnp.concatenate([cmask] * R, axis=1)
    sees_any = tpos_row >= CMP_BLOCK - 1
    jj = lax.broadcasted_iota(jnp.int32, (n_slc, TQ), 0)
    tt = t0 + lax.broadcasted_iota(jnp.int32, (n_slc, TQ), 1)
    lag = tt // SLC_BLOCK - jj
    forced = (jj == 0) | ((lag >= 0) & (lag < N_LOCAL_BLOCKS))
    valid = jj * SLC_BLOCK <= tt
    rows8 = SUBLANES
    j8 = lax.broadcasted_iota(jnp.int32, (rows8, TQ), 0)
    per = TK // SLC_BLOCK
    zrows = jnp.zeros((BIAS_ROWS - per, TQ), F32)
    o_c, o_w = [], []
    for gi in grp:
        s_m = jnp.dot(kc_ref[gi], qt[gi], preferred_element_type=F32) + cmask
        mx = jnp.max(s_m, axis=0, keepdims=True)
        p = jnp.exp2(s_m - mx)
        den = jnp.sum(p, axis=0, keepdims=True)
        pc = p * jnp.where(sees_any, 1.0 / den, 0.0)
        o_c.append(jnp.dot(vct_ref[gi], pc.astype(BF16), preferred_element_type=F32))

        psum = pc[:, 0:TQ]
        for r in range(1, R):
            psum = psum + pc[:, r * TQ:(r + 1) * TQ]
        imp_t = jnp.dot(ovt_ref[...], psum, precision=lax.Precision.HIGHEST,
                        preferred_element_type=F32)
        score = jnp.where(forced, FORCED_SCORE, jnp.where(valid, imp_t, -1.0))
        groups = [score[a:a + rows8] for a in range(0, n_slc, rows8)]
        ranks = [jnp.zeros((rows8, TQ), F32) for _ in groups]
        window_steps = dict(zip((0, n_slc // 3, 2 * n_slc // 3), window[gi]))
        win = None
        for j2 in range(n_slc):
            if j2 in window_steps:
                s_w, m_w, start_w = window_steps[j2]
                win = online(win, s_w, m_w, vt_cols(vwt_ref, gi, start_w, TK))
            sj = score[j2:j2 + 1, :]
            for a, sg in enumerate(groups):
                lo = a * rows8
                if lo + rows8 - 1 < j2:
                    beats = sj > sg
                elif lo > j2:
                    beats = sj >= sg
                else:
                    beats = (sj > sg) | ((sj == sg) & (j8 + lo > j2))
                ranks[a] = ranks[a] + jnp.where(beats, 1.0, 0.0)
        rank = jnp.concatenate(ranks, axis=0)
        selected = (rank < float(min(SLC_TOP_N, n_slc))) & (score >= 0.0)
        bias_t = jnp.where(selected, 0.0, NEG_BIG)
        for kb in range(n_slc // per):
            blk = jnp.concatenate([bias_t[kb * per:(kb + 1) * per], zrows], axis=0)
            biasq_ref[gi, kb * BIAS_ROWS:(kb + 1) * BIAS_ROWS, :] = (
                jnp.concatenate([blk] * R, axis=1).astype(BF16))
        accw = win[1]
        o_w.append(accw[0:D] / accw[D:D + 1])
    qpad = jnp.zeros((KSEL - D - BIAS_ROWS, nl), BF16)

    def q_sel(gi, kb):
        rows = biasq_ref[gi, pl.ds(pl.multiple_of(kb * BIAS_ROWS, BIAS_ROWS), BIAS_ROWS), :]
        return jnp.concatenate([qt[gi], rows, qpad], axis=0)

    def slc_scores(gi, kb):
        return jnp.dot(k_rows(ks_ref, gi, kb * TK), q_sel(gi, kb), preferred_element_type=F32)

    def fill(gi, slot, chunks):
        for c, kb in enumerate(chunks):
            s = slc_scores(gi, kb)
            sbuf_refs[slot][gi, c * TK:(c + 1) * TK, :] = s
            mbuf_refs[slot][gi, c:c + 1, :] = col_max(s)

    npairs = qi // 2
    diag = []
    for gi in grp:
        s_sel = slc_scores(gi, qi) + causal
        diag.append((s_sel, col_max(s_sel)))
        fill(gi, 0, (0, 1))

    def gates(gi):
        gate_row = pl.multiple_of(GATE_OFF + GATE_SLOT * (gp * len(grp) + gi), SUBLANES)
        sig = jax.nn.sigmoid(tailt_ref[pl.ds(gate_row, GATE_SLOT), :])
        return [jnp.concatenate([sig[3 * r + c:3 * r + c + 1, :] for r in range(R)], axis=1)
                for c in range(3)]

    for gi in grp:
        gate = gates(gi)
        part_ref[gi] = gate[0] * o_c[gi] + gate[2] * o_w[gi]

    for gi in grp:
        m_sel, acc_sel = online(None, diag[gi][0], diag[gi][1], vt_cols(vst_ref, gi, t0, TK))
        m_ref[gi] = m_sel
        acc_ref[gi] = acc_sel

    def absorb(gi, slot, first, n):
        state = (m_ref[gi], acc_ref[gi])
        for c in range(n):
            state = online(state, sbuf_refs[slot][gi, c * TK:(c + 1) * TK, :], mbuf_refs[slot][gi, c:c + 1, :],
                           vt_cols(vst_ref, gi, (first + c) * TK, TK))
        m_ref[gi] = state[0]
        acc_ref[gi] = state[1]

    for k in range(n_stages):
        @pl.when(npairs > k)
        def _(k=k):
            for gi in grp:
                fill(gi, (k + 1) % 2, (jnp.minimum(2 * k + 2, qi - 1), jnp.minimum(2 * k + 3, qi - 1)))
            for gi in grp:
                absorb(gi, k % 2, 2 * k, 2)

    for parity in range(2):
        @pl.when((qi % 2 == 1) & (npairs % 2 == parity))
        def _(parity=parity):
            for gi in grp:
                absorb(gi, parity, qi - 1, 1)

    for gi in grp:
        acc = acc_ref[gi]
        ot = part_ref[gi] + gates(gi)[1] * (acc[0:D] / acc[D:D + 1])
        stacked = jnp.concatenate([ot[:, r * TQ:(r + 1) * TQ] for r in range(R)], axis=0)
        o_ref[:, gi * R * D:(gi + 1) * R * D] = stacked.T.astype(o_ref.dtype)


def _overlap_t(seq):
    n_cmp = (seq - CMP_BLOCK) // CMP_STRIDE + 1
    n_slc = seq // SLC_BLOCK
    cs = np.arange(n_cmp) * CMP_STRIDE
    ss = np.arange(n_slc) * SLC_BLOCK
    overlap = np.clip(np.minimum(cs[:, None] + CMP_BLOCK, ss[None, :] + SLC_BLOCK)
                      - np.maximum(cs[:, None], ss[None, :]), 0, None) / CMP_BLOCK
    ovt = np.zeros((n_slc, n_cmp + 1), np.float32)
    ovt[:, :n_cmp] = overlap.T
    return jnp.asarray(ovt)


def _mask_tiles():
    ki = np.arange(TK)[:, None]
    qi = np.arange(TQ)[None, :]
    neg = np.float32(NEG_BIG)
    none = np.full((TK, TQ), neg, np.float32)
    far = np.where(ki > qi, np.float32(0), neg)
    full = np.zeros((TK, TQ), np.float32)
    causal = np.where(ki <= qi, np.float32(0), neg)
    return jnp.asarray(np.concatenate([none, far, full, causal], axis=0))


def _cmp_masks(seq):
    ncr = seq // CMP_STRIDE
    n_cmp = (seq - CMP_BLOCK) // CMP_STRIDE + 1
    n = np.arange(ncr)[None, :, None]
    t = (np.arange(seq // TQ)[:, None, None] * TQ + np.arange(TQ)[None, None, :])
    visible = (n * CMP_STRIDE + CMP_BLOCK - 1 <= t) & (n < n_cmp)
    return jnp.asarray(np.where(visible, np.float32(0), np.float32(NEG_BIG)).reshape(-1, TQ))


def _nsa(qt, kc, vct, ks, vst, kw, vwt, tailt, batch, seq):
    t = batch * seq
    G, R, D = NSA_KV_HEADS, NSA_REP, NSA_HEAD_DIM
    nq = seq // TQ
    ncr = seq // CMP_STRIDE
    n_slc = seq // SLC_BLOCK
    assert TQ == TK and WINDOW == 2 * TK, "window branch visits exactly three key chunks"
    assert TK // SLC_BLOCK <= BIAS_ROWS and D + BIAS_ROWS <= KSEL
    P = NSA_GROUPS_PER_STEP
    const = lambda b, g, i: (0, 0)
    vtspec = pl.BlockSpec((P, D, seq), lambda b, g, i: (g, 0, b))
    in_specs = [
        pl.BlockSpec((P * R, D, TQ), lambda b, g, i: (g, 0, b * nq + i)),
        pl.BlockSpec((P, ncr, D), lambda b, g, i: (g, b, 0)),
        pl.BlockSpec((P, D, ncr), lambda b, g, i: (g, 0, b)),
        pl.BlockSpec((P, seq, KSEL), lambda b, g, i: (g, b, 0)),
        vtspec,
        pl.BlockSpec((P, seq, D), lambda b, g, i: (g, b, 0)),
        vtspec,
        pl.BlockSpec((TAIL, TQ), lambda b, g, i: (0, b * nq + i)),
        pl.BlockSpec((n_slc, ncr), const),
        pl.BlockSpec((4 * TK, TQ), const),
        pl.BlockSpec((nq * ncr, TQ), const),
    ]
    return pl.pallas_call(
        _nsa_kernel, grid=(batch, G // P, nq), in_specs=in_specs,
        out_specs=pl.BlockSpec((TQ, P * R * D), lambda b, g, i: (b * nq + i, g)),
        out_shape=jax.ShapeDtypeStruct((t, NSA_Q), BF16),
        scratch_shapes=[pltpu.VMEM((P, seq // TK * BIAS_ROWS, R * TQ), BF16),
                        pltpu.VMEM((P, 1, R * TQ), F32),
                        pltpu.VMEM((P, D + ONES_ROWS, R * TQ), F32),
                        pltpu.VMEM((P, D, R * TQ), F32),
                        pltpu.VMEM((P, 2 * TK, R * TQ), F32),
                        pltpu.VMEM((P, 2 * TK, R * TQ), F32),
                        pltpu.VMEM((P, SUBLANES, R * TQ), F32),
                        pltpu.VMEM((P, SUBLANES, R * TQ), F32)],
        compiler_params=_cparams(("arbitrary", "arbitrary", "arbitrary")), name="nsa",
    )(qt, kc, vct, ks, vst, kw, vwt, tailt, _overlap_t(seq), _mask_tiles(), _cmp_masks(seq))


def _outproj_kernel(y_ref, o_ref, x_ref, w_ref, ag_ref, ng_ref, x1_ref, h2_ref):
    yn = _rms(o_ref[...].astype(F32), ag_ref[...], NORM_EPS).astype(BF16)
    x1 = (x_ref[...]
          + jnp.dot(y_ref[...], w_ref[0:SSM_D_INNER, :], preferred_element_type=F32)
          + jnp.dot(yn, w_ref[SSM_D_INNER:, :], preferred_element_type=F32))
    x1_ref[...] = x1
    h2_ref[...] = _rms(x1, ng_ref[...], NORM_EPS).astype(BF16)


def _out_proj(y_ssm, o_nsa, x2, w_out, attn_g, norm2_g):
    t = x2.shape[0]
    tm = TM_OUT_PROJ
    row = lambda i: (i, 0)
    const = lambda i: (0, 0)
    tok = pl.BlockSpec((tm, D_MODEL), row)
    vec = pl.BlockSpec((1, D_MODEL), const)
    return pl.pallas_call(
        _outproj_kernel, grid=(t // tm,),
        in_specs=[tok, tok, tok, pl.BlockSpec((SSM_D_INNER + NSA_Q, D_MODEL), const), vec, vec],
        out_specs=(tok, tok),
        out_shape=(jax.ShapeDtypeStruct((t, D_MODEL), F32), jax.ShapeDtypeStruct((t, D_MODEL), BF16)),
        compiler_params=_cparams(("arbitrary",)), name="out_proj",
    )(y_ssm, o_nsa, x2, w_out, attn_g, norm2_g)


def _ffn_kernel(h_ref, halo_ref, x1_ref, wup_ref, cw_ref, cb_ref, wd_ref, fg_ref, out_ref, act_ref,
                *, tiles_per_seq):
    i = pl.program_id(0)
    tm = h_ref.shape[0]
    pad = halo_ref.shape[0]
    tn = TN_FFN
    halo = halo_ref[...]
    halo = jnp.where(i % tiles_per_seq == 0, jnp.zeros_like(halo), halo)
    hc = jnp.concatenate([halo, h_ref[...]], axis=0)

    def branch(c0):
        cols = slice(c0, c0 + tn)
        u = jnp.dot(hc, wup_ref[:, cols], preferred_element_type=F32)
        out = cb_ref[:, cols] + cw_ref[FFN_CONV - 1:FFN_CONV, cols] * u[pad:pad + tm, :]
        for k in range(FFN_CONV - 1):
            shifted = pltpu.roll(u, FFN_CONV - 1 - k, 0)
            out = out + cw_ref[k:k + 1, cols] * shifted[pad:pad + tm, :]
        return out

    for j in range(D_FF // tn):
        act = _silu(branch(j * tn)) * branch(D_FF + j * tn)
        act_ref[:, j * tn:(j + 1) * tn] = act.astype(BF16)
    down = jnp.dot(act_ref[...], wd_ref[...], preferred_element_type=F32)
    out_ref[...] = _rms(x1_ref[...] + down, fg_ref[...], NORM_EPS)


def _ffn(h2, x1, w_up, conv_w, conv_b, w_down, final_g, seq):
    t = h2.shape[0]
    tm = TM_FFN
    pad = BF16_SUBLANES
    tok = pl.BlockSpec((tm, D_MODEL), lambda i: (i, 0))

    def resident(shape):
        return pl.BlockSpec(shape, lambda i: (0, 0), pipeline_mode=pl.Buffered(1))

    in_specs = [
        tok,
        pl.BlockSpec((pad, D_MODEL), lambda i: (jnp.maximum(i * (tm // pad) - 1, 0), 0)),
        tok,
        resident((D_MODEL, 2 * D_FF)),
        resident((FFN_CONV, 2 * D_FF)),
        resident((1, 2 * D_FF)),
        resident((D_FF, D_MODEL)),
        resident((1, D_MODEL)),
    ]
    return pl.pallas_call(
        functools.partial(_ffn_kernel, tiles_per_seq=seq // tm), grid=(t // tm,),
        in_specs=in_specs, out_specs=tok,
        out_shape=jax.ShapeDtypeStruct((t, D_MODEL), F32),
        scratch_shapes=[pltpu.VMEM((tm, D_FF), BF16)],
        compiler_params=_cparams(("arbitrary",)), name="ffn",
    )(h2, h2, x1, w_up, conv_w, conv_b, w_down, final_g)


def _rope_tables(seq):
    half = NSA_HEAD_DIM // 2
    inv_freq = 1.0 / (ROPE_THETA ** (jnp.arange(0, NSA_HEAD_DIM, 2, dtype=F32) / NSA_HEAD_DIM))
    ang = jnp.arange(seq).astype(F32)[:, None] * inv_freq[None, :]
    cos, sin = jnp.cos(ang), jnp.sin(ang)
    reps = LANES // NSA_HEAD_DIM
    cos_t = jnp.tile(jnp.concatenate([cos, cos], axis=1), (1, reps))
    sin_t = jnp.tile(jnp.concatenate([-sin, sin], axis=1), (1, reps))
    return cos_t, sin_t


def _split_w_in(w):
    o_z, o_xbc, o_dt, o_q, o_kv, o_g = np.cumsum([0, SSM_D_INNER, SSM_XBC, SSM_HEADS, NSA_Q, 6 * NSA_KV]).tolist()
    assert (o_dt, o_q, o_g - o_q) == (SSM_D_INNER + SSM_XBC, W_QKV_OFF, NSA_Q + 6 * NSA_KV)
    wt = w.T
    per = 3 * NSA_REP
    rows = np.zeros((TAIL,), np.int32)
    keep = np.zeros((TAIL, 1), np.float32)
    rows[:SSM_HEADS] = np.arange(o_dt, o_q)
    keep[:SSM_HEADS] = 1.0
    for g in range(NSA_KV_HEADS):
        lo = GATE_OFF + g * GATE_SLOT
        rows[lo:lo + per] = np.arange(o_g + g * per, o_g + (g + 1) * per)
        keep[lo:lo + per] = 1.0
    tail = jnp.take(wt, jnp.asarray(rows), axis=0) * jnp.asarray(keep)
    return wt, tail.astype(BF16)


def kernel(x, norm1_g, w_in, ssm_conv_w, ssm_conv_b, ssm_dt_bias, ssm_a_log, ssm_d, ssm_norm_g, cmp_k_pos, cmp_k_w1, cmp_k_b1, cmp_k_w2, cmp_v_pos, cmp_v_w1, cmp_v_b1, cmp_v_w2, attn_norm_g, w_out, norm2_g, ffn_w_up, ffn_conv_w, ffn_conv_b, ffn_w_down, final_norm_g):
    batch, seq, d = x.shape
    assert w_in.shape[0] == 1, "single-layer problem"
    l = 0
    cos_t, sin_t = _rope_tables(seq)
    x2 = x.reshape(batch * seq, d)
    zs, xbc, qt, kc_r, vc_r, ks, vst, kw, vwt, tail, tailt = _in_proj(
        x2, norm1_g[l].reshape(1, d), _split_w_in(w_in.reshape(d, -1)), cos_t, sin_t, seq)
    y_ssm = _ssd(xbc, zs, tail, tailt, ssm_conv_w[l], ssm_conv_b[l].reshape(1, -1), ssm_dt_bias[l],
                 ssm_a_log[l], ssm_d[l], ssm_norm_g[l], batch, seq)
    kc, vct = _compress(kc_r, vc_r,
                        (cmp_k_pos[l], cmp_k_w1[l], cmp_k_b1[l], cmp_k_w2[l]),
                        (cmp_v_pos[l], cmp_v_w1[l], cmp_v_b1[l], cmp_v_w2[l]), batch, seq)
    o_nsa = _nsa(qt, kc, vct, ks, vst, kw, vwt, tailt, batch, seq)
    x1, h2 = _out_proj(y_ssm, o_nsa, x2, w_out.reshape(-1, d).astype(BF16), attn_norm_g[l].reshape(1, d),
                       norm2_g[l].reshape(1, d))
    out = _ffn(h2, x1, ffn_w_up.reshape(d, -1).astype(BF16), ffn_conv_w[l], ffn_conv_b[l].reshape(1, -1),
               ffn_w_down.reshape(-1, d).astype(BF16), final_norm_g.reshape(1, d), seq)
    return out.reshape(batch, seq, d)
```

```python
import functools
import math

import numpy as np
import jax
import jax.numpy as jnp
from jax import lax
from jax.experimental import pallas as pl
from jax.experimental.pallas import tpu as pltpu

F32 = jnp.float32
BF16 = jnp.bfloat16

D_MODEL = 1024
SSM_D_INNER = 1024
SSM_HEAD_DIM = 64
SSM_HEADS = 16
SSM_GROUPS = 2
SSM_STATE = 128
SSM_CONV = 4
SSM_CHUNK = 128
SSM_XBC = SSM_D_INNER + 2 * SSM_GROUPS * SSM_STATE
SSM_NORM_EPS = 1e-5
NSA_HEADS = 16
NSA_KV_HEADS = 4
NSA_REP = NSA_HEADS // NSA_KV_HEADS
NSA_HEAD_DIM = 64
NSA_Q = NSA_HEADS * NSA_HEAD_DIM
NSA_KV = NSA_KV_HEADS * NSA_HEAD_DIM
CMP_BLOCK = 32
CMP_STRIDE = 16
SLC_BLOCK = 64
SLC_TOP_N = 16
N_LOCAL_BLOCKS = 2
FORCED_SCORE = 1e4
WINDOW = 512
ROPE_THETA = 10000.0
D_FF = 2816
FFN_CONV = 3
NORM_EPS = 1e-6
NEG_BIG = -1e30

LANES = 128
SUBLANES = 8
BF16_SUBLANES = 16
TAIL = LANES
GATE_OFF = SSM_HEADS
GATE_SLOT = 16
W_QKV_OFF = SSM_D_INNER + SSM_XBC + SSM_HEADS

TM_IN_PROJ = 256
TM_OUT_PROJ = 512
TQ = 256
TK = 256
NSA_GROUPS_PER_STEP = 4
ONES_ROWS = 16
KSEL = LANES
BIAS_ROWS = 16
MT_NONE, MT_FAR, MT_ALL, MT_CAUSAL = 0, TK, 2 * TK, 3 * TK
TM_FFN = 1024
TN_FFN = 256
VMEM_LIMIT = 56 * 1024 * 1024

NT_DIMS = (((1,), (1,)), ((), ()))


def _cparams(sem):
    return pltpu.CompilerParams(dimension_semantics=sem, vmem_limit_bytes=VMEM_LIMIT)


def _rms(x, g, eps):
    return x * lax.rsqrt(jnp.mean(x * x, axis=-1, keepdims=True) + eps) * g


def _silu(x):
    h = 0.5 * x
    return h + h * jnp.tanh(h)


def _inproj_kernel(x_ref, g_ref, w_ref, wt_ref, cos_ref, sin_ref,
                   zs_ref, xbc_ref, qt_ref, kc_ref, vc_ref, ks_ref, vst_ref, kw_ref, vwt_ref,
                   tail_ref, tailt_ref, kbuf_ref, vbuf_ref, wa_ref, wb_ref, *, tiles_per_seq):
    tm = x_ref.shape[0]

    @pl.when(pl.program_id(0) == 0)
    def _():
        strip = 2 * LANES
        for r0 in range(0, wa_ref.shape[0], strip):
            wa_ref[r0:r0 + strip, :] = w_ref[r0:r0 + strip, :].astype(BF16)
            wb_ref[r0:r0 + strip, :] = w_ref[W_QKV_OFF + r0:W_QKV_OFF + r0 + strip, :].astype(BF16)
    seq_tile = pl.program_id(0) % tiles_per_seq
    h = _rms(x_ref[...], g_ref[...], NORM_EPS).astype(BF16)

    def mm(w_ref, lo, hi):
        return lax.dot_general(h, w_ref[lo:hi, :], NT_DIMS, preferred_element_type=F32)

    tail = mm(wt_ref, 0, TAIL)
    tail_ref[...] = tail
    tailt_ref[...] = tail.T

    cos = cos_ref[...]
    sin = sin_ref[...]
    lane = lax.broadcasted_iota(jnp.int32, (tm, LANES), 1)
    first_half = (lane % NSA_HEAD_DIM) < (NSA_HEAD_DIM // 2)

    def rope(xc):
        partner = jnp.where(first_half, pltpu.roll(xc, LANES - 32, 1), pltpu.roll(xc, 32, 1))
        return xc * cos + partner * sin

    q = mm(wb_ref, 0, NSA_Q)
    scale = NSA_HEAD_DIM ** -0.5 * math.log2(math.e)
    for c in range(NSA_Q // LANES):
        rt = (rope(q[:, c * LANES:(c + 1) * LANES]) * scale).T.astype(BF16)
        qt_ref[2 * c] = rt[:NSA_HEAD_DIM]
        qt_ref[2 * c + 1] = rt[NSA_HEAD_DIM:]

    kv = mm(wb_ref, NSA_Q, NSA_Q + 6 * NSA_KV)

    def seg(i):
        return kv[:, i * NSA_KV:(i + 1) * NSA_KV]

    def rope_seg(x):
        return jnp.concatenate([rope(x[:, :LANES]), rope(x[:, LANES:])], axis=1)

    def store_heads(ref, x):
        for g in range(NSA_KV_HEADS):
            ref[g] = x[:, g * NSA_HEAD_DIM:(g + 1) * NSA_HEAD_DIM].astype(BF16)

    def store_heads_t(ref, x):
        xt = x.T.astype(BF16)
        for g in range(NSA_KV_HEADS):
            ref[g] = xt[g * NSA_HEAD_DIM:(g + 1) * NSA_HEAD_DIM]

    kc_tile = rope_seg(seg(0))
    vc_tile = seg(1)
    for c in range(NSA_KV // LANES):
        kbuf_ref[c] = kc_tile[:, c * LANES:(c + 1) * LANES]
        vbuf_ref[c] = vc_tile[:, c * LANES:(c + 1) * LANES]
    gw = CMP_STRIDE * NSA_HEAD_DIM
    for l in range(CMP_STRIDE):
        for c in range(NSA_KV // LANES):
            k_rows = kbuf_ref[c, pl.ds(l, tm // CMP_STRIDE, stride=CMP_STRIDE), :]
            v_rows = vbuf_ref[c, pl.ds(l, tm // CMP_STRIDE, stride=CMP_STRIDE), :]
            for half in range(LANES // NSA_HEAD_DIM):
                g = c * (LANES // NSA_HEAD_DIM) + half
                src = slice(half * NSA_HEAD_DIM, (half + 1) * NSA_HEAD_DIM)
                dst = slice(g * gw + l * NSA_HEAD_DIM, g * gw + (l + 1) * NSA_HEAD_DIM)
                kc_ref[:, dst] = k_rows[:, src]
                vc_ref[:, dst] = v_rows[:, src]
    pos = seq_tile * tm + lax.broadcasted_iota(jnp.int32, (tm, NSA_HEAD_DIM), 0)
    block_in_chunk = (pos // SLC_BLOCK) % (TK // SLC_BLOCK)
    onehot = jnp.where(lax.broadcasted_iota(jnp.int32, (tm, NSA_HEAD_DIM), 1) == block_in_chunk, 1.0, 0.0)
    ks = rope_seg(seg(2))
    for g in range(NSA_KV_HEADS):
        kg = ks[:, g * NSA_HEAD_DIM:(g + 1) * NSA_HEAD_DIM]
        ks_ref[g] = jnp.concatenate([kg, onehot], axis=1).astype(BF16)
    store_heads_t(vst_ref, seg(3))
    store_heads(kw_ref, rope_seg(seg(4)))
    store_heads_t(vwt_ref, seg(5))

    zs_ref[...] = _silu(mm(wa_ref, 0, SSM_D_INNER))
    xbc_ref[...] = mm(wa_ref, SSM_D_INNER, SSM_D_INNER + SSM_XBC)


def _in_proj(x2, norm_g, w_parts, cos_t, sin_t, seq):
    t = x2.shape[0]
    tm = TM_IN_PROJ
    nseq = seq // tm
    row = lambda i: (i, 0)
    const = lambda i: (0, 0)
    heads = lambda i: (0, i, 0)
    heads_t = lambda i: (0, 0, i)
    out_shape = (
        jax.ShapeDtypeStruct((t, SSM_D_INNER), F32),
        jax.ShapeDtypeStruct((t, SSM_XBC), F32),
        jax.ShapeDtypeStruct((NSA_HEADS, NSA_HEAD_DIM, t), BF16),
        jax.ShapeDtypeStruct((t // CMP_STRIDE, CMP_STRIDE * NSA_KV), F32),
        jax.ShapeDtypeStruct((t // CMP_STRIDE, CMP_STRIDE * NSA_KV), F32),
        jax.ShapeDtypeStruct((NSA_KV_HEADS, t, KSEL), BF16),
        jax.ShapeDtypeStruct((NSA_KV_HEADS, NSA_HEAD_DIM, t), BF16),
        jax.ShapeDtypeStruct((NSA_KV_HEADS, t, NSA_HEAD_DIM), BF16),
        jax.ShapeDtypeStruct((NSA_KV_HEADS, NSA_HEAD_DIM, t), BF16),
        jax.ShapeDtypeStruct((t, TAIL), F32),
        jax.ShapeDtypeStruct((TAIL, t), F32),
    )
    kvh = pl.BlockSpec((NSA_KV_HEADS, tm, NSA_HEAD_DIM), heads)
    kvh_t = pl.BlockSpec((NSA_KV_HEADS, NSA_HEAD_DIM, tm), heads_t)
    out_specs = (
        pl.BlockSpec((tm, SSM_D_INNER), row),
        pl.BlockSpec((tm, SSM_XBC), row),
        pl.BlockSpec((NSA_HEADS, NSA_HEAD_DIM, tm), heads_t),
        pl.BlockSpec((tm // CMP_STRIDE, CMP_STRIDE * NSA_KV), row),
        pl.BlockSpec((tm // CMP_STRIDE, CMP_STRIDE * NSA_KV), row),
        pl.BlockSpec((NSA_KV_HEADS, tm, KSEL), heads), kvh_t, kvh, kvh_t,
        pl.BlockSpec((tm, TAIL), row),
        pl.BlockSpec((TAIL, tm), lambda i: (0, i)),
    )
    in_specs = [
        pl.BlockSpec((tm, D_MODEL), row),
        pl.BlockSpec((1, D_MODEL), const),
        pl.BlockSpec(w_parts[0].shape, const, pipeline_mode=pl.Buffered(1)),
        pl.BlockSpec((TAIL, D_MODEL), const),
        pl.BlockSpec((tm, LANES), lambda i: (i % nseq, 0)),
        pl.BlockSpec((tm, LANES), lambda i: (i % nseq, 0)),
    ]
    return pl.pallas_call(
        functools.partial(_inproj_kernel, tiles_per_seq=nseq), grid=(t // tm,), in_specs=in_specs,
        out_specs=out_specs,
        out_shape=out_shape,
        scratch_shapes=[pltpu.VMEM((NSA_KV // LANES, tm, LANES), F32),
                        pltpu.VMEM((NSA_KV // LANES, tm, LANES), F32),
                        pltpu.VMEM((SSM_D_INNER + SSM_XBC, D_MODEL), BF16),
                        pltpu.VMEM((NSA_Q + 6 * NSA_KV, D_MODEL), BF16)],
        compiler_params=_cparams(("arbitrary",)), name="in_proj",
    )(x2, norm_g, *w_parts, cos_t, sin_t)


def _ssd_kernel(xbc_ref, zs_ref, tail_ref, tailt_ref, cw_ref, cb_ref, dtb_ref, dtbt_ref,
                alog_ref, alogt_ref, dskip_ref, ng_ref,
                y_ref, xcat_ref, state_ref, ybuf_ref):
    L, P, N, H, G = SSM_CHUNK, SSM_HEAD_DIM, SSM_STATE, SSM_HEADS, SSM_GROUPS
    HG = H // G
    halo = SUBLANES

    @pl.when(pl.program_id(1) == 0)
    def _():
        xcat_ref[0:halo, :] = jnp.zeros((halo, SSM_XBC), F32)
        state_ref[...] = jnp.zeros_like(state_ref)

    xcat_ref[halo:halo + L, :] = xbc_ref[...]
    xcat = xcat_ref[...]
    conv = cb_ref[...] + cw_ref[SSM_CONV - 1:SSM_CONV, :] * xcat[halo:halo + L]
    for k in range(SSM_CONV - 1):
        shifted = pltpu.roll(xcat, SSM_CONV - 1 - k, 0)
        conv = conv + cw_ref[k:k + 1, :] * shifted[halo:halo + L]
    xcat_ref[0:halo, :] = xcat[L:L + halo]
    u = _silu(conv)
    xs = u[:, :SSM_D_INNER]
    bm = u[:, SSM_D_INNER:SSM_D_INNER + G * N]
    cm = u[:, SSM_D_INNER + G * N:]

    def softplus(v):
        return jnp.maximum(v, 0.0) + jnp.log1p(jnp.exp(-jnp.abs(v)))

    dt = softplus(tail_ref[:, 0:H] + dtb_ref[...])
    dtt = softplus(tailt_ref[0:H, :] + dtbt_ref[...])
    da = dt * (-jnp.exp(alog_ref[...]))
    dat = dtt * (-jnp.exp(alogt_ref[...]))
    ri = lax.broadcasted_iota(jnp.int32, (L, L), 0)
    ci = lax.broadcasted_iota(jnp.int32, (L, L), 1)
    tri = ci <= ri
    hi = lax.Precision.HIGHEST
    acs = jnp.dot(tri.astype(F32), da, precision=hi, preferred_element_type=F32)
    acst = jnp.dot(dat, (ri <= ci).astype(F32), precision=hi, preferred_element_type=F32)
    last = acs[L - 1:L, :]
    w_state = dt * jnp.exp(last - acs)
    eacs = jnp.exp(acs)
    cdec = jnp.exp(last)

    hrow = lax.broadcasted_iota(jnp.int32, (H, SSM_D_INNER), 0)
    hcol = lax.broadcasted_iota(jnp.int32, (H, SSM_D_INNER), 1)
    expand = jnp.where(hcol // P == hrow, 1.0, 0.0).astype(BF16)

    def split3(v):
        v_hi = v.astype(BF16)
        r1 = v - v_hi.astype(F32)
        v_mid = r1.astype(BF16)
        v_lo = (r1 - v_mid.astype(F32)).astype(BF16)
        return jnp.concatenate([v_hi, v_mid, v_lo], axis=1)

    small = jnp.concatenate([cdec, dskip_ref[...], jnp.zeros((6, H), F32)], axis=0)
    per_head = jnp.concatenate([dt, w_state, eacs, small], axis=0)
    spread = jnp.dot(split3(per_head), jnp.concatenate([expand] * 3, axis=0), preferred_element_type=F32)
    cdec_e = spread[3 * L:3 * L + 1, :]
    dskip_e = spread[3 * L + 1:3 * L + 2, :]
    xdt = (xs * spread[0:L]).astype(BF16)
    wst = (xs * spread[L:2 * L]).astype(BF16)
    eacs_e = spread[2 * L:3 * L]

    for g in range(G):
        bm_g = bm[:, g * N:(g + 1) * N]
        cm_g = cm[:, g * N:(g + 1) * N].astype(BF16)
        cb = lax.dot_general(cm_g, bm_g.astype(BF16), NT_DIMS, preferred_element_type=F32)
        cols = slice(g * HG * P, (g + 1) * HG * P)
        st = state_ref[:, cols]
        y_off = jnp.dot(cm_g, st.astype(BF16), preferred_element_type=F32) * eacs_e[:, cols]
        ybuf_ref[:, cols] = y_off
        bmt = bm_g.T.astype(BF16)
        state_ref[:, cols] = st * cdec_e[:, cols] + jnp.dot(bmt, wst[:, cols], preferred_element_type=F32)
        for r in range(HG):
            hh = g * HG + r
            diff = acs[:, hh:hh + 1] - acst[hh:hh + 1, :]
            seg = jnp.exp(jnp.where(tri, diff, -jnp.inf))
            lmat = (cb * seg).astype(BF16)
            hc = slice(hh * P, (hh + 1) * P)
            ybuf_ref[:, hc] = ybuf_ref[:, hc] + jnp.dot(lmat, xdt[:, hc], preferred_element_type=F32)

    y = (ybuf_ref[...] + xs * dskip_e) * zs_ref[...]
    gw = SSM_D_INNER // G
    parts = []
    for g in range(G):
        yg = y[:, g * gw:(g + 1) * gw]
        parts.append(yg * lax.rsqrt(jnp.mean(yg * yg, axis=-1, keepdims=True) + SSM_NORM_EPS))
    y_ref[...] = (jnp.concatenate(parts, axis=1) * ng_ref[...]).astype(y_ref.dtype)


def _ssd(xbc, zs, tail, tailt, conv_w, conv_b, dt_bias, a_log, d_skip, norm_g, batch, seq):
    t = xbc.shape[0]
    L = SSM_CHUNK
    nc = seq // L
    row = lambda b, c: (b * nc + c, 0)
    const = lambda b, c: (0, 0)
    H = SSM_HEADS
    in_specs = [
        pl.BlockSpec((L, SSM_XBC), row),
        pl.BlockSpec((L, SSM_D_INNER), row),
        pl.BlockSpec((L, TAIL), row),
        pl.BlockSpec((TAIL, L), lambda b, c: (0, b * nc + c)),
        pl.BlockSpec((SSM_CONV, SSM_XBC), const),
        pl.BlockSpec((1, SSM_XBC), const),
        pl.BlockSpec((1, H), const),
        pl.BlockSpec((H, 1), const),
        pl.BlockSpec((1, H), const),
        pl.BlockSpec((H, 1), const),
        pl.BlockSpec((1, H), const),
        pl.BlockSpec((1, SSM_D_INNER), const),
    ]
    return pl.pallas_call(
        _ssd_kernel, grid=(batch, nc), in_specs=in_specs,
        out_specs=pl.BlockSpec((L, SSM_D_INNER), row),
        out_shape=jax.ShapeDtypeStruct((t, SSM_D_INNER), BF16),
        scratch_shapes=[pltpu.VMEM((L + SUBLANES, SSM_XBC), F32),
                        pltpu.VMEM((SSM_STATE, SSM_D_INNER), F32),
                        pltpu.VMEM((L, SSM_D_INNER), F32)],
        compiler_params=_cparams(("arbitrary", "arbitrary")), name="ssd",
    )(xbc, zs, tail, tailt, conv_w, conv_b, dt_bias.reshape(1, H), dt_bias.reshape(H, 1),
      a_log.reshape(1, H), a_log.reshape(H, 1), d_skip.reshape(1, H), norm_g.reshape(1, -1))


def _compress_kernel(k_ref, v_ref, kpos_ref, vpos_ref, kw1_ref, vw1_ref, kb1_ref, vb1_ref,
                     kw2_ref, vw2_ref, kc_ref, vct_ref):
    D = NSA_HEAD_DIM
    gw = CMP_STRIDE * D

    def hidden(t_ref, pos_ref, w1_ref, b1_ref, g):
        t = t_ref[:, g * gw:(g + 1) * gw]
        n = t.shape[0]
        lo = jnp.dot((t + pos_ref[0:1, :]).astype(BF16), w1_ref[...], preferred_element_type=F32)
        hi = jnp.dot((t + pos_ref[1:2, :]).astype(BF16), w1_ref[...], preferred_element_type=F32)
        pre = lo + pltpu.roll(pltpu.roll(hi, n - 1, 0), D, 1)
        return _silu(pre[:, 0:D] + b1_ref[...]).astype(BF16)

    for g in range(NSA_KV_HEADS):
        kc = jnp.dot(hidden(k_ref, kpos_ref, kw1_ref, kb1_ref, g), kw2_ref[...], preferred_element_type=F32)
        kc_ref[g] = kc.astype(BF16)
        vct = lax.dot_general(vw2_ref[...], hidden(v_ref, vpos_ref, vw1_ref, vb1_ref, g), NT_DIMS,
                              preferred_element_type=F32)
        vct_ref[g] = vct.astype(BF16)


def _compress_weights(pos, w1, b1, w2):
    half = CMP_BLOCK // 2 * NSA_HEAD_DIM
    w1cat = jnp.concatenate([w1[:half], w1[half:]], axis=1).astype(BF16)
    return pos.reshape(2, half), w1cat, b1.reshape(1, -1), w2.astype(BF16)


def _compress(k16, v16, kparams, vparams, batch, seq):
    rows = seq // CMP_STRIDE
    width = CMP_STRIDE * NSA_KV
    D = NSA_HEAD_DIM
    gw = CMP_STRIDE * D
    kpos, kw1, kb1, kw2 = _compress_weights(*kparams)
    vpos, vw1, vb1, vw2 = _compress_weights(*vparams)
    vw2 = vw2.T
    c2 = lambda b: (0, 0)
    tok = pl.BlockSpec((rows, width), lambda b: (b, 0))
    in_specs = [tok, tok,
                pl.BlockSpec((2, gw), c2), pl.BlockSpec((2, gw), c2),
                pl.BlockSpec((gw, 2 * D), c2), pl.BlockSpec((gw, 2 * D), c2),
                pl.BlockSpec((1, D), c2), pl.BlockSpec((1, D), c2),
                pl.BlockSpec((D, D), c2), pl.BlockSpec((D, D), c2)]
    out = jax.ShapeDtypeStruct((NSA_KV_HEADS, batch * rows, D), BF16)
    out_t = jax.ShapeDtypeStruct((NSA_KV_HEADS, D, batch * rows), BF16)
    ospec = pl.BlockSpec((NSA_KV_HEADS, rows, D), lambda b: (0, b, 0))
    ospec_t = pl.BlockSpec((NSA_KV_HEADS, D, rows), lambda b: (0, 0, b))
    return pl.pallas_call(
        _compress_kernel, grid=(batch,), in_specs=in_specs, out_specs=(ospec, ospec_t),
        out_shape=(out, out_t), compiler_params=_cparams(("arbitrary",)), name="compress",
    )(k16, v16, kpos, vpos, kw1, vw1, kb1, vb1, kw2, vw2)


def _nsa_kernel(qt_ref, kc_ref, vct_ref, ks_ref, vst_ref, kw_ref, vwt_ref, tailt_ref, ovt_ref, mtab_ref, cmask_ref,
                o_ref, biasq_ref, m_ref, acc_ref, part_ref, sbuf0_ref, sbuf1_ref, mbuf0_ref, mbuf1_ref):
    R, D = NSA_REP, NSA_HEAD_DIM
    nl = R * TQ
    grp = range(kc_ref.shape[0])
    sbuf_refs = (sbuf0_ref, sbuf1_ref)
    mbuf_refs = (mbuf0_ref, mbuf1_ref)
    n_stages = (ks_ref.shape[1] // TK - 1) // 2
    gp = pl.program_id(1)
    qi = pl.program_id(2)
    t0 = qi * TQ
    qt = [jnp.concatenate([qt_ref[gi * R + r] for r in range(R)], axis=1) for gi in grp]
    tpos_row = t0 + lax.broadcasted_iota(jnp.int32, (1, nl), 1) % TQ

    def k_rows(k_ref, gi, start):
        return k_ref[gi, pl.ds(pl.multiple_of(start, TK), TK), :]

    def vt_cols(vt_ref, gi, start, n):
        return vt_ref[gi, :, pl.ds(pl.multiple_of(start, TK), n)]

    def pv(vt, p):
        vt1 = jnp.concatenate([vt, jnp.ones((ONES_ROWS, vt.shape[1]), BF16)], axis=0)
        return jnp.dot(vt1, p, preferred_element_type=F32)

    def mask_tile(off):
        return jnp.concatenate([mtab_ref[pl.ds(pl.multiple_of(off, TK), TK), :]] * R, axis=1)

    def col_max(s):
        return jnp.max(s, axis=0, keepdims=True)

    def online(state, s, m_chunk, vt):
        m_new = m_chunk if state is None else jnp.maximum(state[0], m_chunk)
        contrib = pv(vt, jnp.exp2(s - m_new).astype(BF16))
        if state is None:
            return m_new, contrib
        return m_new, jnp.exp2(state[0] - m_new) * state[1] + contrib

    causal = mask_tile(MT_CAUSAL)
    far0 = jnp.maximum(t0 - 2 * TK, 0)
    mid0 = jnp.maximum(t0 - TK, 0)
    window = []
    for gi in grp:
        s_dia = jnp.dot(k_rows(kw_ref, gi, t0), qt[gi], preferred_element_type=F32) + causal
        m_dia = col_max(s_dia)
        s_mid = (jnp.dot(k_rows(kw_ref, gi, mid0), qt[gi], preferred_element_type=F32)
                 + mask_tile(jnp.where(qi >= 1, MT_ALL, MT_NONE)))
        m_mid = col_max(s_mid)
        s_far = (jnp.dot(k_rows(kw_ref, gi, far0), qt[gi], preferred_element_type=F32)
                 + mask_tile(jnp.where(qi >= 2, MT_FAR, MT_NONE)))
        m_far = col_max(s_far)
        window.append(((s_dia, m_dia, t0), (s_mid, m_mid, mid0), (s_far, m_far, far0)))

    ncr = kc_ref.shape[1]
    n_slc = ovt_ref.shape[0]
    cmask = cmask_ref[pl.ds(pl.multiple_of(qi * ncr, ncr), ncr), :]
    cmask = jnp.concatenate([cmask] * R, axis=1)
    sees_any = tpos_row >= CMP_BLOCK - 1
    jj = lax.broadcasted_iota(jnp.int32, (n_slc, TQ), 0)
    tt = t0 + lax.broadcasted_iota(jnp.int32, (n_slc, TQ), 1)
    lag = tt // SLC_BLOCK - jj
    forced = (jj == 0) | ((lag >= 0) & (lag < N_LOCAL_BLOCKS))
    valid = jj * SLC_BLOCK <= tt
    rows8 = SUBLANES
    j8 = lax.broadcasted_iota(jnp.int32, (rows8, TQ), 0)
    per = TK // SLC_BLOCK
    zrows = jnp.zeros((BIAS_ROWS - per, TQ), F32)
    o_c, o_w = [], []
    for gi in grp:
        s_m = jnp.dot(kc_ref[gi], qt[gi], preferred_element_type=F32) + cmask
        mx = jnp.max(s_m, axis=0, keepdims=True)
        p = jnp.exp2(s_m - mx)
        den = jnp.sum(p, axis=0, keepdims=True)
        pc = p * jnp.where(sees_any, 1.0 / den, 0.0)
        o_c.append(jnp.dot(vct_ref[gi], pc.astype(BF16), preferred_element_type=F32))

        psum = pc[:, 0:TQ]
        for r in range(1, R):
            psum = psum + pc[:, r * TQ:(r + 1) * TQ]
        imp_t = jnp.dot(ovt_ref[...], psum, precision=lax.Precision.HIGHEST,
                        preferred_element_type=F32)
        score = jnp.where(forced, FORCED_SCORE, jnp.where(valid, imp_t, -1.0))
        groups = [score[a:a + rows8] for a in range(0, n_slc, rows8)]
        ranks = [jnp.zeros((rows8, TQ), F32) for _ in groups]
        window_steps = dict(zip((0, n_slc // 3, 2 * n_slc // 3), window[gi]))
        win = None
        for j2 in range(n_slc):
            if j2 in window_steps:
                s_w, m_w, start_w = window_steps[j2]
                win = online(win, s_w, m_w, vt_cols(vwt_ref, gi, start_w, TK))
            sj = score[j2:j2 + 1, :]
            for a, sg in enumerate(groups):
                lo = a * rows8
                if lo + rows8 - 1 < j2:
                    beats = sj > sg
                elif lo > j2:
                    beats = sj >= sg
                else:
                    beats = (sj > sg) | ((sj == sg) & (j8 + lo > j2))
                ranks[a] = ranks[a] + jnp.where(beats, 1.0, 0.0)
        rank = jnp.concatenate(ranks, axis=0)
        selected = (rank < float(min(SLC_TOP_N, n_slc))) & (score >= 0.0)
        bias_t = jnp.where(selected, 0.0, NEG_BIG)
        for kb in range(n_slc // per):
            blk = jnp.concatenate([bias_t[kb * per:(kb + 1) * per], zrows], axis=0)
            biasq_ref[gi, kb * BIAS_ROWS:(kb + 1) * BIAS_ROWS, :] = (
                jnp.concatenate([blk] * R, axis=1).astype(BF16))
        accw = win[1]
        o_w.append(accw[0:D] / accw[D:D + 1])
    qpad = jnp.zeros((KSEL - D - BIAS_ROWS, nl), BF16)

    def q_sel(gi, kb):
        rows = biasq_ref[gi, pl.ds(pl.multiple_of(kb * BIAS_ROWS, BIAS_ROWS), BIAS_ROWS), :]
        return jnp.concatenate([qt[gi], rows, qpad], axis=0)

    def slc_scores(gi, kb):
        return jnp.dot(k_rows(ks_ref, gi, kb * TK), q_sel(gi, kb), preferred_element_type=F32)

    def fill(gi, slot, chunks):
        for c, kb in enumerate(chunks):
            s = slc_scores(gi, kb)
            sbuf_refs[slot][gi, c * TK:(c + 1) * TK, :] = s
            mbuf_refs[slot][gi, c:c + 1, :] = col_max(s)

    npairs = qi // 2
    diag = []
    for gi in grp:
        s_sel = slc_scores(gi, qi) + causal
        diag.append((s_sel, col_max(s_sel)))
        fill(gi, 0, (0, 1))

    def gates(gi):
        gate_row = pl.multiple_of(GATE_OFF + GATE_SLOT * (gp * len(grp) + gi), SUBLANES)
        sig = jax.nn.sigmoid(tailt_ref[pl.ds(gate_row, GATE_SLOT), :])
        return [jnp.concatenate([sig[3 * r + c:3 * r + c + 1, :] for r in range(R)], axis=1)
                for c in range(3)]

    for gi in grp:
        gate = gates(gi)
        part_ref[gi] = gate[0] * o_c[gi] + gate[2] * o_w[gi]

    for gi in grp:
        m_sel, acc_sel = online(None, diag[gi][0], diag[gi][1], vt_cols(vst_ref, gi, t0, TK))
        m_ref[gi] = m_sel
        acc_ref[gi] = acc_sel

    def absorb(gi, slot, first, n):
        state = (m_ref[gi], acc_ref[gi])
        for c in range(n):
            state = online(state, sbuf_refs[slot][gi, c * TK:(c + 1) * TK, :], mbuf_refs[slot][gi, c:c + 1, :],
                           vt_cols(vst_ref, gi, (first + c) * TK, TK))
        m_ref[gi] = state[0]
        acc_ref[gi] = state[1]

    for k in range(n_stages):
        @pl.when(npairs > k)
        def _(k=k):
            for gi in grp:
                fill(gi, (k + 1) % 2, (jnp.minimum(2 * k + 2, qi - 1), jnp.minimum(2 * k + 3, qi - 1)))
            for gi in grp:
                absorb(gi, k % 2, 2 * k, 2)

    for parity in range(2):
        @pl.when((qi % 2 == 1) & (npairs % 2 == parity))
        def _(parity=parity):
            for gi in grp:
                absorb(gi, parity, qi - 1, 1)

    for gi in grp:
        acc = acc_ref[gi]
        ot = part_ref[gi] + gates(gi)[1] * (acc[0:D] / acc[D:D + 1])
        stacked = jnp.concatenate([ot[:, r * TQ:(r + 1) * TQ] for r in range(R)], axis=0)
        o_ref[:, gi * R * D:(gi + 1) * R * D] = stacked.T.astype(o_ref.dtype)


def _overlap_t(seq):
    n_cmp = (seq - CMP_BLOCK) // CMP_STRIDE + 1
    n_slc = seq // SLC_BLOCK
    cs = np.arange(n_cmp) * CMP_STRIDE
    ss = np.arange(n_slc) * SLC_BLOCK
    overlap = np.clip(np.minimum(cs[:, None] + CMP_BLOCK, ss[None, :] + SLC_BLOCK)
                      - np.maximum(cs[:, None], ss[None, :]), 0, None) / CMP_BLOCK
    ovt = np.zeros((n_slc, n_cmp + 1), np.float32)
    ovt[:, :n_cmp] = overlap.T
    return jnp.asarray(ovt)


def _mask_tiles():
    ki = np.arange(TK)[:, None]
    qi = np.arange(TQ)[None, :]
    neg = np.float32(NEG_BIG)
    none = np.full((TK, TQ), neg, np.float32)
    far = np.where(ki > qi, np.float32(0), neg)
    full = np.zeros((TK, TQ), np.float32)
    causal = np.where(ki <= qi, np.float32(0), neg)
    return jnp.asarray(np.concatenate([none, far, full, causal], axis=0))


def _cmp_masks(seq):
    ncr = seq // CMP_STRIDE
    n_cmp = (seq - CMP_BLOCK) // CMP_STRIDE + 1
    n = np.arange(ncr)[None, :, None]
    t = (np.arange(seq // TQ)[:, None, None] * TQ + np.arange(TQ)[None, None, :])
    visible = (n * CMP_STRIDE + CMP_BLOCK - 1 <= t) & (n < n_cmp)
    return jnp.asarray(np.where(visible, np.float32(0), np.float32(NEG_BIG)).reshape(-1, TQ))


def _nsa(qt, kc, vct, ks, vst, kw, vwt, tailt, batch, seq):
    t = batch * seq
    G, R, D = NSA_KV_HEADS, NSA_REP, NSA_HEAD_DIM
    nq = seq // TQ
    ncr = seq // CMP_STRIDE
    n_slc = seq // SLC_BLOCK
    assert TQ == TK and WINDOW == 2 * TK, "window branch visits exactly three key chunks"
    assert TK // SLC_BLOCK <= BIAS_ROWS and D + BIAS_ROWS <= KSEL
    P = NSA_GROUPS_PER_STEP
    const = lambda b, g, i: (0, 0)
    vtspec = pl.BlockSpec((P, D, seq), lambda b, g, i: (g, 0, b))
    in_specs = [
        pl.BlockSpec((P * R, D, TQ), lambda b, g, i: (g, 0, b * nq + i)),
        pl.BlockSpec((P, ncr, D), lambda b, g, i: (g, b, 0)),
        pl.BlockSpec((P, D, ncr), lambda b, g, i: (g, 0, b)),
        pl.BlockSpec((P, seq, KSEL), lambda b, g, i: (g, b, 0)),
        vtspec,
        pl.BlockSpec((P, seq, D), lambda b, g, i: (g, b, 0)),
        vtspec,
        pl.BlockSpec((TAIL, TQ), lambda b, g, i: (0, b * nq + i)),
        pl.BlockSpec((n_slc, ncr), const),
        pl.BlockSpec((4 * TK, TQ), const),
        pl.BlockSpec((nq * ncr, TQ), const),
    ]
    return pl.pallas_call(
        _nsa_kernel, grid=(batch, G // P, nq), in_specs=in_specs,
        out_specs=pl.BlockSpec((TQ, P * R * D), lambda b, g, i: (b * nq + i, g)),
        out_shape=jax.ShapeDtypeStruct((t, NSA_Q), BF16),
        scratch_shapes=[pltpu.VMEM((P, seq // TK * BIAS_ROWS, R * TQ), BF16),
                        pltpu.VMEM((P, 1, R * TQ), F32),
                        pltpu.VMEM((P, D + ONES_ROWS, R * TQ), F32),
                        pltpu.VMEM((P, D, R * TQ), F32),
                        pltpu.VMEM((P, 2 * TK, R * TQ), F32),
                        pltpu.VMEM((P, 2 * TK, R * TQ), F32),
                        pltpu.VMEM((P, SUBLANES, R * TQ), F32),
                        pltpu.VMEM((P, SUBLANES, R * TQ), F32)],
        compiler_params=_cparams(("arbitrary", "arbitrary", "arbitrary")), name="nsa",
    )(qt, kc, vct, ks, vst, kw, vwt, tailt, _overlap_t(seq), _mask_tiles(), _cmp_masks(seq))


def _outproj_kernel(y_ref, o_ref, x_ref, w_ref, ag_ref, ng_ref, x1_ref, h2_ref):
    yn = _rms(o_ref[...].astype(F32), ag_ref[...], NORM_EPS).astype(BF16)
    x1 = (x_ref[...]
          + jnp.dot(y_ref[...], w_ref[0:SSM_D_INNER, :], preferred_element_type=F32)
          + jnp.dot(yn, w_ref[SSM_D_INNER:, :], preferred_element_type=F32))
    x1_ref[...] = x1
    h2_ref[...] = _rms(x1, ng_ref[...], NORM_EPS).astype(BF16)


def _out_proj(y_ssm, o_nsa, x2, w_out, attn_g, norm2_g):
    t = x2.shape[0]
    tm = TM_OUT_PROJ
    row = lambda i: (i, 0)
    const = lambda i: (0, 0)
    tok = pl.BlockSpec((tm, D_MODEL), row)
    vec = pl.BlockSpec((1, D_MODEL), const)
    return pl.pallas_call(
        _outproj_kernel, grid=(t // tm,),
        in_specs=[tok, tok, tok, pl.BlockSpec((SSM_D_INNER + NSA_Q, D_MODEL), const), vec, vec],
        out_specs=(tok, tok),
        out_shape=(jax.ShapeDtypeStruct((t, D_MODEL), F32), jax.ShapeDtypeStruct((t, D_MODEL), BF16)),
        compiler_params=_cparams(("arbitrary",)), name="out_proj",
    )(y_ssm, o_nsa, x2, w_out, attn_g, norm2_g)


def _ffn_kernel(h_ref, halo_ref, x1_ref, wup_ref, cw_ref, cb_ref, wd_ref, fg_ref, out_ref, act_ref,
                *, tiles_per_seq):
    i = pl.program_id(0)
    tm = h_ref.shape[0]
    pad = halo_ref.shape[0]
    tn = TN_FFN
    halo = halo_ref[...]
    halo = jnp.where(i % tiles_per_seq == 0, jnp.zeros_like(halo), halo)
    hc = jnp.concatenate([halo, h_ref[...]], axis=0)

    def branch(c0):
        cols = slice(c0, c0 + tn)
        u = jnp.dot(hc, wup_ref[:, cols], preferred_element_type=F32)
        out = cb_ref[:, cols] + cw_ref[FFN_CONV - 1:FFN_CONV, cols] * u[pad:pad + tm, :]
        for k in range(FFN_CONV - 1):
            shifted = pltpu.roll(u, FFN_CONV - 1 - k, 0)
            out = out + cw_ref[k:k + 1, cols] * shifted[pad:pad + tm, :]
        return out

    for j in range(D_FF // tn):
        act = _silu(branch(j * tn)) * branch(D_FF + j * tn)
        act_ref[:, j * tn:(j + 1) * tn] = act.astype(BF16)
    down = jnp.dot(act_ref[...], wd_ref[...], preferred_element_type=F32)
    out_ref[...] = _rms(x1_ref[...] + down, fg_ref[...], NORM_EPS)


def _ffn(h2, x1, w_up, conv_w, conv_b, w_down, final_g, seq):
    t = h2.shape[0]
    tm = TM_FFN
    pad = BF16_SUBLANES
    tok = pl.BlockSpec((tm, D_MODEL), lambda i: (i, 0))

    def resident(shape):
        return pl.BlockSpec(shape, lambda i: (0, 0), pipeline_mode=pl.Buffered(1))

    in_specs = [
        tok,
        pl.BlockSpec((pad, D_MODEL), lambda i: (jnp.maximum(i * (tm // pad) - 1, 0), 0)),
        tok,
        resident((D_MODEL, 2 * D_FF)),
        resident((FFN_CONV, 2 * D_FF)),
        resident((1, 2 * D_FF)),
        resident((D_FF, D_MODEL)),
        resident((1, D_MODEL)),
    ]
    return pl.pallas_call(
        functools.partial(_ffn_kernel, tiles_per_seq=seq // tm), grid=(t // tm,),
        in_specs=in_specs, out_specs=tok,
        out_shape=jax.ShapeDtypeStruct((t, D_MODEL), F32),
        scratch_shapes=[pltpu.VMEM((tm, D_FF), BF16)],
        compiler_params=_cparams(("arbitrary",)), name="ffn",
    )(h2, h2, x1, w_up, conv_w, conv_b, w_down, final_g)


def _rope_tables(seq):
    half = NSA_HEAD_DIM // 2
    inv_freq = 1.0 / (ROPE_THETA ** (jnp.arange(0, NSA_HEAD_DIM, 2, dtype=F32) / NSA_HEAD_DIM))
    ang = jnp.arange(seq).astype(F32)[:, None] * inv_freq[None, :]
    cos, sin = jnp.cos(ang), jnp.sin(ang)
    reps = LANES // NSA_HEAD_DIM
    cos_t = jnp.tile(jnp.concatenate([cos, cos], axis=1), (1, reps))
    sin_t = jnp.tile(jnp.concatenate([-sin, sin], axis=1), (1, reps))
    return cos_t, sin_t


def _split_w_in(w):
    o_z, o_xbc, o_dt, o_q, o_kv, o_g = np.cumsum([0, SSM_D_INNER, SSM_XBC, SSM_HEADS, NSA_Q, 6 * NSA_KV]).tolist()
    assert (o_dt, o_q, o_g - o_q) == (SSM_D_INNER + SSM_XBC, W_QKV_OFF, NSA_Q + 6 * NSA_KV)
    wt = w.T
    per = 3 * NSA_REP
    rows = np.zeros((TAIL,), np.int32)
    keep = np.zeros((TAIL, 1), np.float32)
    rows[:SSM_HEADS] = np.arange(o_dt, o_q)
    keep[:SSM_HEADS] = 1.0
    for g in range(NSA_KV_HEADS):
        lo = GATE_OFF + g * GATE_SLOT
        rows[lo:lo + per] = np.arange(o_g + g * per, o_g + (g + 1) * per)
        keep[lo:lo + per] = 1.0
    tail = jnp.take(wt, jnp.asarray(rows), axis=0) * jnp.asarray(keep)
    return wt, tail.astype(BF16)


def kernel(x, norm1_g, w_in, ssm_conv_w, ssm_conv_b, ssm_dt_bias, ssm_a_log, ssm_d, ssm_norm_g, cmp_k_pos, cmp_k_w1, cmp_k_b1, cmp_k_w2, cmp_v_pos, cmp_v_w1, cmp_v_b1, cmp_v_w2, attn_norm_g, w_out, norm2_g, ffn_w_up, ffn_conv_w, ffn_conv_b, ffn_w_down, final_norm_g):
    batch, seq, d = x.shape
    assert w_in.shape[0] == 1, "single-layer problem"
    l = 0
    cos_t, sin_t = _rope_tables(seq)
    x2 = x.reshape(batch * seq, d)
    zs, xbc, qt, kc_r, vc_r, ks, vst, kw, vwt, tail, tailt = _in_proj(
        x2, norm1_g[l].reshape(1, d), _split_w_in(w_in.reshape(d, -1)), cos_t, sin_t, seq)
    y_ssm = _ssd(xbc, zs, tail, tailt, ssm_conv_w[l], ssm_conv_b[l].reshape(1, -1), ssm_dt_bias[l],
                 ssm_a_log[l], ssm_d[l], ssm_norm_g[l], batch, seq)
    kc, vct = _compress(kc_r, vc_r,
                        (cmp_k_pos[l], cmp_k_w1[l], cmp_k_b1[l], cmp_k_w2[l]),
                        (cmp_v_pos[l], cmp_v_w1[l], cmp_v_b1[l], cmp_v_w2[l]), batch, seq)
    o_nsa = _nsa(qt, kc, vct, ks, vst, kw, vwt, tailt, batch, seq)
    x1, h2 = _out_proj(y_ssm, o_nsa, x2, w_out.reshape(-1, d).astype(BF16), attn_norm_g[l].reshape(1, d),
                       norm2_g[l].reshape(1, d))
    out = _ffn(h2, x1, ffn_w_up.reshape(d, -1).astype(BF16), ffn_conv_w[l], ffn_conv_b[l].reshape(1, -1),
               ffn_w_down.reshape(-1, d).astype(BF16), final_norm_g.reshape(1, d), seq)
    return out.reshape(batch, seq, d)
```

```python
import functools
import math

import numpy as np
import jax
import jax.numpy as jnp
from jax import lax
from jax.experimental import pallas as pl
from jax.experimental.pallas import tpu as pltpu

F32 = jnp.float32
BF16 = jnp.bfloat16

D_MODEL = 1024
SSM_D_INNER = 1024
SSM_HEAD_DIM = 64
SSM_HEADS = 16
SSM_GROUPS = 2
SSM_STATE = 128
SSM_CONV = 4
SSM_CHUNK = 128
SSM_XBC = SSM_D_INNER + 2 * SSM_GROUPS * SSM_STATE
SSM_NORM_EPS = 1e-5
NSA_HEADS = 16
NSA_KV_HEADS = 4
NSA_REP = NSA_HEADS // NSA_KV_HEADS
NSA_HEAD_DIM = 64
NSA_Q = NSA_HEADS * NSA_HEAD_DIM
NSA_KV = NSA_KV_HEADS * NSA_HEAD_DIM
CMP_BLOCK = 32
CMP_STRIDE = 16
SLC_BLOCK = 64
SLC_TOP_N = 16
N_LOCAL_BLOCKS = 2
FORCED_SCORE = 1e4
WINDOW = 512
ROPE_THETA = 10000.0
D_FF = 2816
FFN_CONV = 3
NORM_EPS = 1e-6
NEG_BIG = -1e30

LANES = 128
SUBLANES = 8
BF16_SUBLANES = 16
TAIL = LANES
GATE_OFF = SSM_HEADS
GATE_SLOT = 16
W_QKV_OFF = SSM_D_INNER + SSM_XBC + SSM_HEADS

TM_IN_PROJ = 256
TM_OUT_PROJ = 512
TQ = 256
TK = 256
NSA_GROUPS_PER_STEP = 4
ONES_ROWS = 16
KSEL = LANES
BIAS_ROWS = 16
MT_NONE, MT_FAR, MT_ALL, MT_CAUSAL = 0, TK, 2 * TK, 3 * TK
TM_FFN = 512
TN_FFN = 256
VMEM_LIMIT = 56 * 1024 * 1024

NT_DIMS = (((1,), (1,)), ((), ()))


def _cparams(sem):
    return pltpu.CompilerParams(dimension_semantics=sem, vmem_limit_bytes=VMEM_LIMIT)


def _rms(x, g, eps):
    return x * lax.rsqrt(jnp.mean(x * x, axis=-1, keepdims=True) + eps) * g


def _silu(x):
    h = 0.5 * x
    return h + h * jnp.tanh(h)


def _inproj_kernel(x_ref, g_ref, w_ref, wt_ref, cos_ref, sin_ref,
                   zs_ref, xbc_ref, qt_ref, kc_ref, vc_ref, ks_ref, vst_ref, kw_ref, vwt_ref,
                   tail_ref, tailt_ref, kbuf_ref, vbuf_ref, wa_ref, wb_ref, *, tiles_per_seq):
    tm = x_ref.shape[0]

    @pl.when(pl.program_id(0) == 0)
    def _():
        strip = 2 * LANES
        for r0 in range(0, wa_ref.shape[0], strip):
            wa_ref[r0:r0 + strip, :] = w_ref[r0:r0 + strip, :].astype(BF16)
            wb_ref[r0:r0 + strip, :] = w_ref[W_QKV_OFF + r0:W_QKV_OFF + r0 + strip, :].astype(BF16)
    seq_tile = pl.program_id(0) % tiles_per_seq
    h = _rms(x_ref[...], g_ref[...], NORM_EPS).astype(BF16)

    def mm(w_ref, lo, hi):
        return lax.dot_general(h, w_ref[lo:hi, :], NT_DIMS, preferred_element_type=F32)

    tail = mm(wt_ref, 0, TAIL)
    tail_ref[...] = tail
    tailt_ref[...] = tail.T

    cos = cos_ref[...]
    sin = sin_ref[...]
    lane = lax.broadcasted_iota(jnp.int32, (tm, LANES), 1)
    first_half = (lane % NSA_HEAD_DIM) < (NSA_HEAD_DIM // 2)

    def rope(xc):
        partner = jnp.where(first_half, pltpu.roll(xc, LANES - 32, 1), pltpu.roll(xc, 32, 1))
        return xc * cos + partner * sin

    q = mm(wb_ref, 0, NSA_Q)
    scale = NSA_HEAD_DIM ** -0.5 * math.log2(math.e)
    for c in range(NSA_Q // LANES):
        rt = (rope(q[:, c * LANES:(c + 1) * LANES]) * scale).T.astype(BF16)
        qt_ref[2 * c] = rt[:NSA_HEAD_DIM]
        qt_ref[2 * c + 1] = rt[NSA_HEAD_DIM:]

    kv = mm(wb_ref, NSA_Q, NSA_Q + 6 * NSA_KV)

    def seg(i):
        return kv[:, i * NSA_KV:(i + 1) * NSA_KV]

    def rope_seg(x):
        return jnp.concatenate([rope(x[:, :LANES]), rope(x[:, LANES:])], axis=1)

    def store_heads(ref, x):
        for g in range(NSA_KV_HEADS):
            ref[g] = x[:, g * NSA_HEAD_DIM:(g + 1) * NSA_HEAD_DIM].astype(BF16)

    def store_heads_t(ref, x):
        xt = x.T.astype(BF16)
        for g in range(NSA_KV_HEADS):
            ref[g] = xt[g * NSA_HEAD_DIM:(g + 1) * NSA_HEAD_DIM]

    kc_tile = rope_seg(seg(0))
    vc_tile = seg(1)
    for c in range(NSA_KV // LANES):
        kbuf_ref[c] = kc_tile[:, c * LANES:(c + 1) * LANES]
        vbuf_ref[c] = vc_tile[:, c * LANES:(c + 1) * LANES]
    gw = CMP_STRIDE * NSA_HEAD_DIM
    for l in range(CMP_STRIDE):
        for c in range(NSA_KV // LANES):
            k_rows = kbuf_ref[c, pl.ds(l, tm // CMP_STRIDE, stride=CMP_STRIDE), :]
            v_rows = vbuf_ref[c, pl.ds(l, tm // CMP_STRIDE, stride=CMP_STRIDE), :]
            for half in range(LANES // NSA_HEAD_DIM):
                g = c * (LANES // NSA_HEAD_DIM) + half
                src = slice(half * NSA_HEAD_DIM, (half + 1) * NSA_HEAD_DIM)
                dst = slice(g * gw + l * NSA_HEAD_DIM, g * gw + (l + 1) * NSA_HEAD_DIM)
                kc_ref[:, dst] = k_rows[:, src]
                vc_ref[:, dst] = v_rows[:, src]
    pos = seq_tile * tm + lax.broadcasted_iota(jnp.int32, (tm, NSA_HEAD_DIM), 0)
    block_in_chunk = (pos // SLC_BLOCK) % (TK // SLC_BLOCK)
    onehot = jnp.where(lax.broadcasted_iota(jnp.int32, (tm, NSA_HEAD_DIM), 1) == block_in_chunk, 1.0, 0.0)
    ks = rope_seg(seg(2))
    for g in range(NSA_KV_HEADS):
        kg = ks[:, g * NSA_HEAD_DIM:(g + 1) * NSA_HEAD_DIM]
        ks_ref[g] = jnp.concatenate([kg, onehot], axis=1).astype(BF16)
    store_heads_t(vst_ref, seg(3))
    store_heads(kw_ref, rope_seg(seg(4)))
    store_heads_t(vwt_ref, seg(5))

    zs_ref[...] = _silu(mm(wa_ref, 0, SSM_D_INNER))
    xbc_ref[...] = mm(wa_ref, SSM_D_INNER, SSM_D_INNER + SSM_XBC)


def _in_proj(x2, norm_g, w_parts, cos_t, sin_t, seq):
    t = x2.shape[0]
    tm = TM_IN_PROJ
    nseq = seq // tm
    row = lambda i: (i, 0)
    const = lambda i: (0, 0)
    heads = lambda i: (0, i, 0)
    heads_t = lambda i: (0, 0, i)
    out_shape = (
        jax.ShapeDtypeStruct((t, SSM_D_INNER), F32),
        jax.ShapeDtypeStruct((t, SSM_XBC), F32),
        jax.ShapeDtypeStruct((NSA_HEADS, NSA_HEAD_DIM, t), BF16),
        jax.ShapeDtypeStruct((t // CMP_STRIDE, CMP_STRIDE * NSA_KV), F32),
        jax.ShapeDtypeStruct((t // CMP_STRIDE, CMP_STRIDE * NSA_KV), F32),
        jax.ShapeDtypeStruct((NSA_KV_HEADS, t, KSEL), BF16),
        jax.ShapeDtypeStruct((NSA_KV_HEADS, NSA_HEAD_DIM, t), BF16),
        jax.ShapeDtypeStruct((NSA_KV_HEADS, t, NSA_HEAD_DIM), BF16),
        jax.ShapeDtypeStruct((NSA_KV_HEADS, NSA_HEAD_DIM, t), BF16),
        jax.ShapeDtypeStruct((t, TAIL), F32),
        jax.ShapeDtypeStruct((TAIL, t), F32),
    )
    kvh = pl.BlockSpec((NSA_KV_HEADS, tm, NSA_HEAD_DIM), heads)
    kvh_t = pl.BlockSpec((NSA_KV_HEADS, NSA_HEAD_DIM, tm), heads_t)
    out_specs = (
        pl.BlockSpec((tm, SSM_D_INNER), row),
        pl.BlockSpec((tm, SSM_XBC), row),
        pl.BlockSpec((NSA_HEADS, NSA_HEAD_DIM, tm), heads_t),
        pl.BlockSpec((tm // CMP_STRIDE, CMP_STRIDE * NSA_KV), row),
        pl.BlockSpec((tm // CMP_STRIDE, CMP_STRIDE * NSA_KV), row),
        pl.BlockSpec((NSA_KV_HEADS, tm, KSEL), heads), kvh_t, kvh, kvh_t,
        pl.BlockSpec((tm, TAIL), row),
        pl.BlockSpec((TAIL, tm), lambda i: (0, i)),
    )
    in_specs = [
        pl.BlockSpec((tm, D_MODEL), row),
        pl.BlockSpec((1, D_MODEL), const),
        pl.BlockSpec(w_parts[0].shape, const, pipeline_mode=pl.Buffered(1)),
        pl.BlockSpec((TAIL, D_MODEL), const),
        pl.BlockSpec((tm, LANES), lambda i: (i % nseq, 0)),
        pl.BlockSpec((tm, LANES), lambda i: (i % nseq, 0)),
    ]
    return pl.pallas_call(
        functools.partial(_inproj_kernel, tiles_per_seq=nseq), grid=(t // tm,), in_specs=in_specs,
        out_specs=out_specs,
        out_shape=out_shape,
        scratch_shapes=[pltpu.VMEM((NSA_KV // LANES, tm, LANES), F32),
                        pltpu.VMEM((NSA_KV // LANES, tm, LANES), F32),
                        pltpu.VMEM((SSM_D_INNER + SSM_XBC, D_MODEL), BF16),
                        pltpu.VMEM((NSA_Q + 6 * NSA_KV, D_MODEL), BF16)],
        compiler_params=_cparams(("arbitrary",)), name="in_proj",
    )(x2, norm_g, *w_parts, cos_t, sin_t)


def _ssd_kernel(xbc_ref, zs_ref, tail_ref, tailt_ref, cw_ref, cb_ref, dtb_ref, dtbt_ref,
                alog_ref, alogt_ref, dskip_ref, ng_ref,
                y_ref, xcat_ref, state_ref, ybuf_ref):
    L, P, N, H, G = SSM_CHUNK, SSM_HEAD_DIM, SSM_STATE, SSM_HEADS, SSM_GROUPS
    HG = H // G
    halo = SUBLANES

    @pl.when(pl.program_id(1) == 0)
    def _():
        xcat_ref[0:halo, :] = jnp.zeros((halo, SSM_XBC), F32)
        state_ref[...] = jnp.zeros_like(state_ref)

    xcat_ref[halo:halo + L, :] = xbc_ref[...]
    xcat = xcat_ref[...]
    conv = cb_ref[...] + cw_ref[SSM_CONV - 1:SSM_CONV, :] * xcat[halo:halo + L]
    for k in range(SSM_CONV - 1):
        shifted = pltpu.roll(xcat, SSM_CONV - 1 - k, 0)
        conv = conv + cw_ref[k:k + 1, :] * shifted[halo:halo + L]
    xcat_ref[0:halo, :] = xcat[L:L + halo]
    u = _silu(conv)
    xs = u[:, :SSM_D_INNER]
    bm = u[:, SSM_D_INNER:SSM_D_INNER + G * N]
    cm = u[:, SSM_D_INNER + G * N:]

    def softplus(v):
        return jnp.maximum(v, 0.0) + jnp.log1p(jnp.exp(-jnp.abs(v)))

    dt = softplus(tail_ref[:, 0:H] + dtb_ref[...])
    dtt = softplus(tailt_ref[0:H, :] + dtbt_ref[...])
    da = dt * (-jnp.exp(alog_ref[...]))
    dat = dtt * (-jnp.exp(alogt_ref[...]))
    ri = lax.broadcasted_iota(jnp.int32, (L, L), 0)
    ci = lax.broadcasted_iota(jnp.int32, (L, L), 1)
    tri = ci <= ri
    hi = lax.Precision.HIGHEST
    acs = jnp.dot(tri.astype(F32), da, precision=hi, preferred_element_type=F32)
    acst = jnp.dot(dat, (ri <= ci).astype(F32), precision=hi, preferred_element_type=F32)
    last = acs[L - 1:L, :]
    w_state = dt * jnp.exp(last - acs)
    eacs = jnp.exp(acs)
    cdec = jnp.exp(last)

    hrow = lax.broadcasted_iota(jnp.int32, (H, SSM_D_INNER), 0)
    hcol = lax.broadcasted_iota(jnp.int32, (H, SSM_D_INNER), 1)
    expand = jnp.where(hcol // P == hrow, 1.0, 0.0).astype(BF16)

    def split3(v):
        v_hi = v.astype(BF16)
        r1 = v - v_hi.astype(F32)
        v_mid = r1.astype(BF16)
        v_lo = (r1 - v_mid.astype(F32)).astype(BF16)
        return jnp.concatenate([v_hi, v_mid, v_lo], axis=1)

    small = jnp.concatenate([cdec, dskip_ref[...], jnp.zeros((6, H), F32)], axis=0)
    per_head = jnp.concatenate([dt, w_state, eacs, small], axis=0)
    spread = jnp.dot(split3(per_head), jnp.concatenate([expand] * 3, axis=0), preferred_element_type=F32)
    cdec_e = spread[3 * L:3 * L + 1, :]
    dskip_e = spread[3 * L + 1:3 * L + 2, :]
    xdt = (xs * spread[0:L]).astype(BF16)
    wst = (xs * spread[L:2 * L]).astype(BF16)
    eacs_e = spread[2 * L:3 * L]

    for g in range(G):
        bm_g = bm[:, g * N:(g + 1) * N]
        cm_g = cm[:, g * N:(g + 1) * N].astype(BF16)
        cb = lax.dot_general(cm_g, bm_g.astype(BF16), NT_DIMS, preferred_element_type=F32)
        cols = slice(g * HG * P, (g + 1) * HG * P)
        st = state_ref[:, cols]
        y_off = jnp.dot(cm_g, st.astype(BF16), preferred_element_type=F32) * eacs_e[:, cols]
        ybuf_ref[:, cols] = y_off
        bmt = bm_g.T.astype(BF16)
        state_ref[:, cols] = st * cdec_e[:, cols] + jnp.dot(bmt, wst[:, cols], preferred_element_type=F32)
        for r in range(HG):
            hh = g * HG + r
            diff = acs[:, hh:hh + 1] - acst[hh:hh + 1, :]
            seg = jnp.exp(jnp.where(tri, diff, -jnp.inf))
            lmat = (cb * seg).astype(BF16)
            hc = slice(hh * P, (hh + 1) * P)
            ybuf_ref[:, hc] = ybuf_ref[:, hc] + jnp.dot(lmat, xdt[:, hc], preferred_element_type=F32)

    y = (ybuf_ref[...] + xs * dskip_e) * zs_ref[...]
    gw = SSM_D_INNER // G
    parts = []
    for g in range(G):
        yg = y[:, g * gw:(g + 1) * gw]
        parts.append(yg * lax.rsqrt(jnp.mean(yg * yg, axis=-1, keepdims=True) + SSM_NORM_EPS))
    y_ref[...] = (jnp.concatenate(parts, axis=1) * ng_ref[...]).astype(y_ref.dtype)


def _ssd(xbc, zs, tail, tailt, conv_w, conv_b, dt_bias, a_log, d_skip, norm_g, batch, seq):
    t = xbc.shape[0]
    L = SSM_CHUNK
    nc = seq // L
    row = lambda b, c: (b * nc + c, 0)
    const = lambda b, c: (0, 0)
    H = SSM_HEADS
    in_specs = [
        pl.BlockSpec((L, SSM_XBC), row),
        pl.BlockSpec((L, SSM_D_INNER), row),
        pl.BlockSpec((L, TAIL), row),
        pl.BlockSpec((TAIL, L), lambda b, c: (0, b * nc + c)),
        pl.BlockSpec((SSM_CONV, SSM_XBC), const),
        pl.BlockSpec((1, SSM_XBC), const),
        pl.BlockSpec((1, H), const),
        pl.BlockSpec((H, 1), const),
        pl.BlockSpec((1, H), const),
        pl.BlockSpec((H, 1), const),
        pl.BlockSpec((1, H), const),
        pl.BlockSpec((1, SSM_D_INNER), const),
    ]
    return pl.pallas_call(
        _ssd_kernel, grid=(batch, nc), in_specs=in_specs,
        out_specs=pl.BlockSpec((L, SSM_D_INNER), row),
        out_shape=jax.ShapeDtypeStruct((t, SSM_D_INNER), BF16),
        scratch_shapes=[pltpu.VMEM((L + SUBLANES, SSM_XBC), F32),
                        pltpu.VMEM((SSM_STATE, SSM_D_INNER), F32),
                        pltpu.VMEM((L, SSM_D_INNER), F32)],
        compiler_params=_cparams(("arbitrary", "arbitrary")), name="ssd",
    )(xbc, zs, tail, tailt, conv_w, conv_b, dt_bias.reshape(1, H), dt_bias.reshape(H, 1),
      a_log.reshape(1, H), a_log.reshape(H, 1), d_skip.reshape(1, H), norm_g.reshape(1, -1))


def _compress_kernel(k_ref, v_ref, kpos_ref, vpos_ref, kw1_ref, vw1_ref, kb1_ref, vb1_ref,
                     kw2_ref, vw2_ref, kc_ref, vct_ref):
    D = NSA_HEAD_DIM
    gw = CMP_STRIDE * D

    def hidden(t_ref, pos_ref, w1_ref, b1_ref, g):
        t = t_ref[:, g * gw:(g + 1) * gw]
        n = t.shape[0]
        lo = jnp.dot((t + pos_ref[0:1, :]).astype(BF16), w1_ref[...], preferred_element_type=F32)
        hi = jnp.dot((t + pos_ref[1:2, :]).astype(BF16), w1_ref[...], preferred_element_type=F32)
        pre = lo + pltpu.roll(pltpu.roll(hi, n - 1, 0), D, 1)
        return _silu(pre[:, 0:D] + b1_ref[...]).astype(BF16)

    for g in range(NSA_KV_HEADS):
        kc = jnp.dot(hidden(k_ref, kpos_ref, kw1_ref, kb1_ref, g), kw2_ref[...], preferred_element_type=F32)
        kc_ref[g] = kc.astype(BF16)
        vct = lax.dot_general(vw2_ref[...], hidden(v_ref, vpos_ref, vw1_ref, vb1_ref, g), NT_DIMS,
                              preferred_element_type=F32)
        vct_ref[g] = vct.astype(BF16)


def _compress_weights(pos, w1, b1, w2):
    half = CMP_BLOCK // 2 * NSA_HEAD_DIM
    w1cat = jnp.concatenate([w1[:half], w1[half:]], axis=1).astype(BF16)
    return pos.reshape(2, half), w1cat, b1.reshape(1, -1), w2.astype(BF16)


def _compress(k16, v16, kparams, vparams, batch, seq):
    rows = seq // CMP_STRIDE
    width = CMP_STRIDE * NSA_KV
    D = NSA_HEAD_DIM
    gw = CMP_STRIDE * D
    kpos, kw1, kb1, kw2 = _compress_weights(*kparams)
    vpos, vw1, vb1, vw2 = _compress_weights(*vparams)
    vw2 = vw2.T
    c2 = lambda b: (0, 0)
    tok = pl.BlockSpec((rows, width), lambda b: (b, 0))
    in_specs = [tok, tok,
                pl.BlockSpec((2, gw), c2), pl.BlockSpec((2, gw), c2),
                pl.BlockSpec((gw, 2 * D), c2), pl.BlockSpec((gw, 2 * D), c2),
                pl.BlockSpec((1, D), c2), pl.BlockSpec((1, D), c2),
                pl.BlockSpec((D, D), c2), pl.BlockSpec((D, D), c2)]
    out = jax.ShapeDtypeStruct((NSA_KV_HEADS, batch * rows, D), BF16)
    out_t = jax.ShapeDtypeStruct((NSA_KV_HEADS, D, batch * rows), BF16)
    ospec = pl.BlockSpec((NSA_KV_HEADS, rows, D), lambda b: (0, b, 0))
    ospec_t = pl.BlockSpec((NSA_KV_HEADS, D, rows), lambda b: (0, 0, b))
    return pl.pallas_call(
        _compress_kernel, grid=(batch,), in_specs=in_specs, out_specs=(ospec, ospec_t),
        out_shape=(out, out_t), compiler_params=_cparams(("arbitrary",)), name="compress",
    )(k16, v16, kpos, vpos, kw1, vw1, kb1, vb1, kw2, vw2)


def _nsa_kernel(qt_ref, kc_ref, vct_ref, ks_ref, vst_ref, kw_ref, vwt_ref, tailt_ref, ovt_ref, mtab_ref, cmask_ref,
                o_ref, biasq_ref, m_ref, acc_ref, part_ref, sbuf0_ref, sbuf1_ref, mbuf0_ref, mbuf1_ref):
    R, D = NSA_REP, NSA_HEAD_DIM
    nl = R * TQ
    grp = range(kc_ref.shape[0])
    sbuf_refs = (sbuf0_ref, sbuf1_ref)
    mbuf_refs = (mbuf0_ref, mbuf1_ref)
    n_stages = (ks_ref.shape[1] // TK - 1) // 2
    gp = pl.program_id(1)
    qi = pl.program_id(2)
    t0 = qi * TQ
    qt = [jnp.concatenate([qt_ref[gi * R + r] for r in range(R)], axis=1) for gi in grp]
    tpos_row = t0 + lax.broadcasted_iota(jnp.int32, (1, nl), 1) % TQ

    def k_rows(k_ref, gi, start):
        return k_ref[gi, pl.ds(pl.multiple_of(start, TK), TK), :]

    def vt_cols(vt_ref, gi, start, n):
        return vt_ref[gi, :, pl.ds(pl.multiple_of(start, TK), n)]

    def pv(vt, p):
        vt1 = jnp.concatenate([vt, jnp.ones((ONES_ROWS, vt.shape[1]), BF16)], axis=0)
        return jnp.dot(vt1, p, preferred_element_type=F32)

    def mask_tile(off):
        return jnp.concatenate([mtab_ref[pl.ds(pl.multiple_of(off, TK), TK), :]] * R, axis=1)

    def col_max(s):
        return jnp.max(s, axis=0, keepdims=True)

    def online(state, s, m_chunk, vt):
        m_new = m_chunk if state is None else jnp.maximum(state[0], m_chunk)
        contrib = pv(vt, jnp.exp2(s - m_new).astype(BF16))
        if state is None:
            return m_new, contrib
        return m_new, jnp.exp2(state[0] - m_new) * state[1] + contrib

    causal = mask_tile(MT_CAUSAL)
    far0 = jnp.maximum(t0 - 2 * TK, 0)
    mid0 = jnp.maximum(t0 - TK, 0)
    window = []
    for gi in grp:
        s_dia = jnp.dot(k_rows(kw_ref, gi, t0), qt[gi], preferred_element_type=F32) + causal
        m_dia = col_max(s_dia)
        s_mid = (jnp.dot(k_rows(kw_ref, gi, mid0), qt[gi], preferred_element_type=F32)
                 + mask_tile(jnp.where(qi >= 1, MT_ALL, MT_NONE)))
        m_mid = col_max(s_mid)
        s_far = (jnp.dot(k_rows(kw_ref, gi, far0), qt[gi], preferred_element_type=F32)
                 + mask_tile(jnp.where(qi >= 2, MT_FAR, MT_NONE)))
        m_far = col_max(s_far)
        window.append(((s_dia, m_dia, t0), (s_mid, m_mid, mid0), (s_far, m_far, far0)))

    ncr = kc_ref.shape[1]
    n_slc = ovt_ref.shape[0]
    cmask = cmask_ref[pl.ds(pl.multiple_of(qi * ncr, ncr), ncr), :]
    cmask = jnp.concatenate([cmask] * R, axis=1)
    sees_any = tpos_row >= CMP_BLOCK - 1
    jj = lax.broadcasted_iota(jnp.int32, (n_slc, TQ), 0)
    tt = t0 + lax.broadcasted_iota(jnp.int32, (n_slc, TQ), 1)
    lag = tt // SLC_BLOCK - jj
    forced = (jj == 0) | ((lag >= 0) & (lag < N_LOCAL_BLOCKS))
    valid = jj * SLC_BLOCK <= tt
    rows8 = SUBLANES
    j8 = lax.broadcasted_iota(jnp.int32, (rows8, TQ), 0)
    per = TK // SLC_BLOCK
    zrows = jnp.zeros((BIAS_ROWS - per, TQ), F32)
    o_c, o_w = [], []
    for gi in grp:
        s_m = jnp.dot(kc_ref[gi], qt[gi], preferred_element_type=F32) + cmask
        mx = jnp.max(s_m, axis=0, keepdims=True)
        p = jnp.exp2(s_m - mx)
        den = jnp.sum(p, axis=0, keepdims=True)
        pc = p * jnp.where(sees_any, 1.0 / den, 0.0)
        o_c.append(jnp.dot(vct_ref[gi], pc.astype(BF16), preferred_element_type=F32))

        psum = pc[:, 0:TQ]
        for r in range(1, R):
            psum = psum + pc[:, r * TQ:(r + 1) * TQ]
        imp_t = jnp.dot(ovt_ref[...], psum, precision=lax.Precision.HIGHEST,
                        preferred_element_type=F32)
        score = jnp.where(forced, FORCED_SCORE, jnp.where(valid, imp_t, -1.0))
        groups = [score[a:a + rows8] for a in range(0, n_slc, rows8)]
        ranks = [jnp.zeros((rows8, TQ), F32) for _ in groups]
        window_steps = dict(zip((0, n_slc // 3, 2 * n_slc // 3), window[gi]))
        win = None
        for j2 in range(n_slc):
            if j2 in window_steps:
                s_w, m_w, start_w = window_steps[j2]
                win = online(win, s_w, m_w, vt_cols(vwt_ref, gi, start_w, TK))
            sj = score[j2:j2 + 1, :]
            for a, sg in enumerate(groups):
                lo = a * rows8
                if lo + rows8 - 1 < j2:
                    beats = sj > sg
                elif lo > j2:
                    beats = sj >= sg
                else:
                    beats = (sj > sg) | ((sj == sg) & (j8 + lo > j2))
                ranks[a] = ranks[a] + jnp.where(beats, 1.0, 0.0)
        rank = jnp.concatenate(ranks, axis=0)
        selected = (rank < float(min(SLC_TOP_N, n_slc))) & (score >= 0.0)
        bias_t = jnp.where(selected, 0.0, NEG_BIG)
        for kb in range(n_slc // per):
            blk = jnp.concatenate([bias_t[kb * per:(kb + 1) * per], zrows], axis=0)
            biasq_ref[gi, kb * BIAS_ROWS:(kb + 1) * BIAS_ROWS, :] = (
                jnp.concatenate([blk] * R, axis=1).astype(BF16))
        accw = win[1]
        o_w.append(accw[0:D] / accw[D:D + 1])
    qpad = jnp.zeros((KSEL - D - BIAS_ROWS, nl), BF16)

    def q_sel(gi, kb):
        rows = biasq_ref[gi, pl.ds(pl.multiple_of(kb * BIAS_ROWS, BIAS_ROWS), BIAS_ROWS), :]
        return jnp.concatenate([qt[gi], rows, qpad], axis=0)

    def slc_scores(gi, kb):
        return jnp.dot(k_rows(ks_ref, gi, kb * TK), q_sel(gi, kb), preferred_element_type=F32)

    def fill(gi, slot, chunks):
        for c, kb in enumerate(chunks):
            s = slc_scores(gi, kb)
            sbuf_refs[slot][gi, c * TK:(c + 1) * TK, :] = s
            mbuf_refs[slot][gi, c:c + 1, :] = col_max(s)

    npairs = qi // 2
    diag = []
    for gi in grp:
        s_sel = slc_scores(gi, qi) + causal
        diag.append((s_sel, col_max(s_sel)))
        fill(gi, 0, (0, 1))

    def gates(gi):
        gate_row = pl.multiple_of(GATE_OFF + GATE_SLOT * (gp * len(grp) + gi), SUBLANES)
        sig = jax.nn.sigmoid(tailt_ref[pl.ds(gate_row, GATE_SLOT), :])
        return [jnp.concatenate([sig[3 * r + c:3 * r + c + 1, :] for r in range(R)], axis=1)
                for c in range(3)]

    for gi in grp:
        gate = gates(gi)
        part_ref[gi] = gate[0] * o_c[gi] + gate[2] * o_w[gi]

    for gi in grp:
        m_sel, acc_sel = online(None, diag[gi][0], diag[gi][1], vt_cols(vst_ref, gi, t0, TK))
        m_ref[gi] = m_sel
        acc_ref[gi] = acc_sel

    def absorb(gi, slot, first, n):
        state = (m_ref[gi], acc_ref[gi])
        for c in range(n):
            state = online(state, sbuf_refs[slot][gi, c * TK:(c + 1) * TK, :], mbuf_refs[slot][gi, c:c + 1, :],
                           vt_cols(vst_ref, gi, (first + c) * TK, TK))
        m_ref[gi] = state[0]
        acc_ref[gi] = state[1]

    for k in range(n_stages):
        @pl.when(npairs > k)
        def _(k=k):
            for gi in grp:
                fill(gi, (k + 1) % 2, (jnp.minimum(2 * k + 2, qi - 1), jnp.minimum(2 * k + 3, qi - 1)))
                absorb(gi, k % 2, 2 * k, 2)

    for parity in range(2):
        @pl.when((qi % 2 == 1) & (npairs % 2 == parity))
        def _(parity=parity):
            for gi in grp:
                absorb(gi, parity, qi - 1, 1)

    for gi in grp:
        acc = acc_ref[gi]
        ot = part_ref[gi] + gates(gi)[1] * (acc[0:D] / acc[D:D + 1])
        stacked = jnp.concatenate([ot[:, r * TQ:(r + 1) * TQ] for r in range(R)], axis=0)
        o_ref[:, gi * R * D:(gi + 1) * R * D] = stacked.T.astype(o_ref.dtype)


def _overlap_t(seq):
    n_cmp = (seq - CMP_BLOCK) // CMP_STRIDE + 1
    n_slc = seq // SLC_BLOCK
    cs = np.arange(n_cmp) * CMP_STRIDE
    ss = np.arange(n_slc) * SLC_BLOCK
    overlap = np.clip(np.minimum(cs[:, None] + CMP_BLOCK, ss[None, :] + SLC_BLOCK)
                      - np.maximum(cs[:, None], ss[None, :]), 0, None) / CMP_BLOCK
    ovt = np.zeros((n_slc, n_cmp + 1), np.float32)
    ovt[:, :n_cmp] = overlap.T
    return jnp.asarray(ovt)


def _mask_tiles():
    ki = np.arange(TK)[:, None]
    qi = np.arange(TQ)[None, :]
    neg = np.float32(NEG_BIG)
    none = np.full((TK, TQ), neg, np.float32)
    far = np.where(ki > qi, np.float32(0), neg)
    full = np.zeros((TK, TQ), np.float32)
    causal = np.where(ki <= qi, np.float32(0), neg)
    return jnp.asarray(np.concatenate([none, far, full, causal], axis=0))


def _cmp_masks(seq):
    ncr = seq // CMP_STRIDE
    n_cmp = (seq - CMP_BLOCK) // CMP_STRIDE + 1
    n = np.arange(ncr)[None, :, None]
    t = (np.arange(seq // TQ)[:, None, None] * TQ + np.arange(TQ)[None, None, :])
    visible = (n * CMP_STRIDE + CMP_BLOCK - 1 <= t) & (n < n_cmp)
    return jnp.asarray(np.where(visible, np.float32(0), np.float32(NEG_BIG)).reshape(-1, TQ))


def _nsa(qt, kc, vct, ks, vst, kw, vwt, tailt, batch, seq):
    t = batch * seq
    G, R, D = NSA_KV_HEADS, NSA_REP, NSA_HEAD_DIM
    nq = seq // TQ
    ncr = seq // CMP_STRIDE
    n_slc = seq // SLC_BLOCK
    assert TQ == TK and WINDOW == 2 * TK, "window branch visits exactly three key chunks"
    assert TK // SLC_BLOCK <= BIAS_ROWS and D + BIAS_ROWS <= KSEL
    P = NSA_GROUPS_PER_STEP
    const = lambda b, g, i: (0, 0)
    vtspec = pl.BlockSpec((P, D, seq), lambda b, g, i: (g, 0, b))
    in_specs = [
        pl.BlockSpec((P * R, D, TQ), lambda b, g, i: (g, 0, b * nq + i)),
        pl.BlockSpec((P, ncr, D), lambda b, g, i: (g, b, 0)),
        pl.BlockSpec((P, D, ncr), lambda b, g, i: (g, 0, b)),
        pl.BlockSpec((P, seq, KSEL), lambda b, g, i: (g, b, 0)),
        vtspec,
        pl.BlockSpec((P, seq, D), lambda b, g, i: (g, b, 0)),
        vtspec,
        pl.BlockSpec((TAIL, TQ), lambda b, g, i: (0, b * nq + i)),
        pl.BlockSpec((n_slc, ncr), const),
        pl.BlockSpec((4 * TK, TQ), const),
        pl.BlockSpec((nq * ncr, TQ), const),
    ]
    return pl.pallas_call(
        _nsa_kernel, grid=(batch, G // P, nq), in_specs=in_specs,
        out_specs=pl.BlockSpec((TQ, P * R * D), lambda b, g, i: (b * nq + i, g)),
        out_shape=jax.ShapeDtypeStruct((t, NSA_Q), BF16),
        scratch_shapes=[pltpu.VMEM((P, seq // TK * BIAS_ROWS, R * TQ), BF16),
                        pltpu.VMEM((P, 1, R * TQ), F32),
                        pltpu.VMEM((P, D + ONES_ROWS, R * TQ), F32),
                        pltpu.VMEM((P, D, R * TQ), F32),
                        pltpu.VMEM((P, 2 * TK, R * TQ), F32),
                        pltpu.VMEM((P, 2 * TK, R * TQ), F32),
                        pltpu.VMEM((P, SUBLANES, R * TQ), F32),
                        pltpu.VMEM((P, SUBLANES, R * TQ), F32)],
        compiler_params=_cparams(("arbitrary", "arbitrary", "arbitrary")), name="nsa",
    )(qt, kc, vct, ks, vst, kw, vwt, tailt, _overlap_t(seq), _mask_tiles(), _cmp_masks(seq))


def _outproj_kernel(y_ref, o_ref, x_ref, w_ref, ag_ref, ng_ref, x1_ref, h2_ref):
    yn = _rms(o_ref[...].astype(F32), ag_ref[...], NORM_EPS).astype(BF16)
    x1 = (x_ref[...]
          + jnp.dot(y_ref[...], w_ref[0:SSM_D_INNER, :], preferred_element_type=F32)
          + jnp.dot(yn, w_ref[SSM_D_INNER:, :], preferred_element_type=F32))
    x1_ref[...] = x1
    h2_ref[...] = _rms(x1, ng_ref[...], NORM_EPS).astype(BF16)


def _out_proj(y_ssm, o_nsa, x2, w_out, attn_g, norm2_g):
    t = x2.shape[0]
    tm = TM_OUT_PROJ
    row = lambda i: (i, 0)
    const = lambda i: (0, 0)
    tok = pl.BlockSpec((tm, D_MODEL), row)
    vec = pl.BlockSpec((1, D_MODEL), const)
    return pl.pallas_call(
        _outproj_kernel, grid=(t // tm,),
        in_specs=[tok, tok, tok, pl.BlockSpec((SSM_D_INNER + NSA_Q, D_MODEL), const), vec, vec],
        out_specs=(tok, tok),
        out_shape=(jax.ShapeDtypeStruct((t, D_MODEL), F32), jax.ShapeDtypeStruct((t, D_MODEL), BF16)),
        compiler_params=_cparams(("arbitrary",)), name="out_proj",
    )(y_ssm, o_nsa, x2, w_out, attn_g, norm2_g)


def _ffn_kernel(h_ref, halo_ref, x1_ref, wup_ref, cw_ref, cb_ref, wd_ref, fg_ref, out_ref, act_ref,
                *, tiles_per_seq):
    i = pl.program_id(0)
    tm = h_ref.shape[0]
    pad = halo_ref.shape[0]
    tn = TN_FFN
    halo = halo_ref[...]
    halo = jnp.where(i % tiles_per_seq == 0, jnp.zeros_like(halo), halo)
    hc = jnp.concatenate([halo, h_ref[...]], axis=0)

    def branch(c0):
        cols = slice(c0, c0 + tn)
        u = jnp.dot(hc, wup_ref[:, cols], preferred_element_type=F32)
        out = cb_ref[:, cols] + cw_ref[FFN_CONV - 1:FFN_CONV, cols] * u[pad:pad + tm, :]
        for k in range(FFN_CONV - 1):
            shifted = pltpu.roll(u, FFN_CONV - 1 - k, 0)
            out = out + cw_ref[k:k + 1, cols] * shifted[pad:pad + tm, :]
        return out

    for j in range(D_FF // tn):
        act = _silu(branch(j * tn)) * branch(D_FF + j * tn)
        act_ref[:, j * tn:(j + 1) * tn] = act.astype(BF16)
    down = jnp.dot(act_ref[...], wd_ref[...], preferred_element_type=F32)
    out_ref[...] = _rms(x1_ref[...] + down, fg_ref[...], NORM_EPS)


def _ffn(h2, x1, w_up, conv_w, conv_b, w_down, final_g, seq):
    t = h2.shape[0]
    tm = TM_FFN
    pad = BF16_SUBLANES
    tok = pl.BlockSpec((tm, D_MODEL), lambda i: (i, 0))

    def resident(shape):
        return pl.BlockSpec(shape, lambda i: (0, 0), pipeline_mode=pl.Buffered(1))

    in_specs = [
        tok,
        pl.BlockSpec((pad, D_MODEL), lambda i: (jnp.maximum(i * (tm // pad) - 1, 0), 0)),
        tok,
        resident((D_MODEL, 2 * D_FF)),
        resident((FFN_CONV, 2 * D_FF)),
        resident((1, 2 * D_FF)),
        resident((D_FF, D_MODEL)),
        resident((1, D_MODEL)),
    ]
    return pl.pallas_call(
        functools.partial(_ffn_kernel, tiles_per_seq=seq // tm), grid=(t // tm,),
        in_specs=in_specs, out_specs=tok,
        out_shape=jax.ShapeDtypeStruct((t, D_MODEL), F32),
        scratch_shapes=[pltpu.VMEM((tm, D_FF), BF16)],
        compiler_params=_cparams(("arbitrary",)), name="ffn",
    )(h2, h2, x1, w_up, conv_w, conv_b, w_down, final_g)


def _rope_tables(seq):
    half = NSA_HEAD_DIM // 2
    inv_freq = 1.0 / (ROPE_THETA ** (jnp.arange(0, NSA_HEAD_DIM, 2, dtype=F32) / NSA_HEAD_DIM))
    ang = jnp.arange(seq).astype(F32)[:, None] * inv_freq[None, :]
    cos, sin = jnp.cos(ang), jnp.sin(ang)
    reps = LANES // NSA_HEAD_DIM
    cos_t = jnp.tile(jnp.concatenate([cos, cos], axis=1), (1, reps))
    sin_t = jnp.tile(jnp.concatenate([-sin, sin], axis=1), (1, reps))
    return cos_t, sin_t


def _split_w_in(w):
    o_z, o_xbc, o_dt, o_q, o_kv, o_g = np.cumsum([0, SSM_D_INNER, SSM_XBC, SSM_HEADS, NSA_Q, 6 * NSA_KV]).tolist()
    assert (o_dt, o_q, o_g - o_q) == (SSM_D_INNER + SSM_XBC, W_QKV_OFF, NSA_Q + 6 * NSA_KV)
    wt = w.T
    per = 3 * NSA_REP
    rows = np.zeros((TAIL,), np.int32)
    keep = np.zeros((TAIL, 1), np.float32)
    rows[:SSM_HEADS] = np.arange(o_dt, o_q)
    keep[:SSM_HEADS] = 1.0
    for g in range(NSA_KV_HEADS):
        lo = GATE_OFF + g * GATE_SLOT
        rows[lo:lo + per] = np.arange(o_g + g * per, o_g + (g + 1) * per)
        keep[lo:lo + per] = 1.0
    tail = jnp.take(wt, jnp.asarray(rows), axis=0) * jnp.asarray(keep)
    return wt, tail.astype(BF16)


def kernel(x, norm1_g, w_in, ssm_conv_w, ssm_conv_b, ssm_dt_bias, ssm_a_log, ssm_d, ssm_norm_g, cmp_k_pos, cmp_k_w1, cmp_k_b1, cmp_k_w2, cmp_v_pos, cmp_v_w1, cmp_v_b1, cmp_v_w2, attn_norm_g, w_out, norm2_g, ffn_w_up, ffn_conv_w, ffn_conv_b, ffn_w_down, final_norm_g):
    batch, seq, d = x.shape
    assert w_in.shape[0] == 1, "single-layer problem"
    l = 0
    cos_t, sin_t = _rope_tables(seq)
    x2 = x.reshape(batch * seq, d)
    zs, xbc, qt, kc_r, vc_r, ks, vst, kw, vwt, tail, tailt = _in_proj(
        x2, norm1_g[l].reshape(1, d), _split_w_in(w_in.reshape(d, -1)), cos_t, sin_t, seq)
    y_ssm = _ssd(xbc, zs, tail, tailt, ssm_conv_w[l], ssm_conv_b[l].reshape(1, -1), ssm_dt_bias[l],
                 ssm_a_log[l], ssm_d[l], ssm_norm_g[l], batch, seq)
    kc, vct = _compress(kc_r, vc_r,
                        (cmp_k_pos[l], cmp_k_w1[l], cmp_k_b1[l], cmp_k_w2[l]),
                        (cmp_v_pos[l], cmp_v_w1[l], cmp_v_b1[l], cmp_v_w2[l]), batch, seq)
    o_nsa = _nsa(qt, kc, vct, ks, vst, kw, vwt, tailt, batch, seq)
    x1, h2 = _out_proj(y_ssm, o_nsa, x2, w_out.reshape(-1, d).astype(BF16), attn_norm_g[l].reshape(1, d),
                       norm2_g[l].reshape(1, d))
    out = _ffn(h2, x1, ffn_w_up.reshape(d, -1).astype(BF16), ffn_conv_w[l], ffn_conv_b[l].reshape(1, -1),
               ffn_w_down.reshape(-1, d).astype(BF16), final_norm_g.reshape(1, d), seq)
    return out.reshape(batch, seq, d)
```

```python
import functools
import math

import numpy as np
import jax
import jax.numpy as jnp
from jax import lax
from jax.experimental import pallas as pl
from jax.experimental.pallas import tpu as pltpu

F32 = jnp.float32
BF16 = jnp.bfloat16

D_MODEL = 1024
SSM_D_INNER = 1024
SSM_HEAD_DIM = 64
SSM_HEADS = 16
SSM_GROUPS = 2
SSM_STATE = 128
SSM_CONV = 4
SSM_CHUNK = 128
SSM_XBC = SSM_D_INNER + 2 * SSM_GROUPS * SSM_STATE
SSM_NORM_EPS = 1e-5
NSA_HEADS = 16
NSA_KV_HEADS = 4
NSA_REP = NSA_HEADS // NSA_KV_HEADS
NSA_HEAD_DIM = 64
NSA_Q = NSA_HEADS * NSA_HEAD_DIM
NSA_KV = NSA_KV_HEADS * NSA_HEAD_DIM
CMP_BLOCK = 32
CMP_STRIDE = 16
SLC_BLOCK = 64
SLC_TOP_N = 16
N_LOCAL_BLOCKS = 2
FORCED_SCORE = 1e4
WINDOW = 512
ROPE_THETA = 10000.0
D_FF = 2816
FFN_CONV = 3
NORM_EPS = 1e-6
NEG_BIG = -1e30

LANES = 128
SUBLANES = 8
BF16_SUBLANES = 16
TAIL = LANES
GATE_OFF = SSM_HEADS
GATE_SLOT = 16
W_QKV_OFF = SSM_D_INNER + SSM_XBC + SSM_HEADS

TM_IN_PROJ = 256
TM_OUT_PROJ = 1024
TQ = 256
TK = 256
NSA_GROUPS_PER_STEP = 4
ONES_ROWS = 16
KSEL = LANES
BIAS_ROWS = 16
MT_NONE, MT_FAR, MT_ALL, MT_CAUSAL = 0, TK, 2 * TK, 3 * TK
TM_FFN = 512
TN_FFN = 256
VMEM_LIMIT = 56 * 1024 * 1024

NT_DIMS = (((1,), (1,)), ((), ()))


def _cparams(sem):
    return pltpu.CompilerParams(dimension_semantics=sem, vmem_limit_bytes=VMEM_LIMIT)


def _rms(x, g, eps):
    return x * lax.rsqrt(jnp.mean(x * x, axis=-1, keepdims=True) + eps) * g


def _silu(x):
    h = 0.5 * x
    return h + h * jnp.tanh(h)


def _inproj_kernel(x_ref, g_ref, w_ref, wt_ref, cos_ref, sin_ref,
                   zs_ref, xbc_ref, qt_ref, kc_ref, vc_ref, ks_ref, vst_ref, kw_ref, vwt_ref,
                   tail_ref, tailt_ref, kbuf_ref, vbuf_ref, wa_ref, wb_ref, *, tiles_per_seq):
    tm = x_ref.shape[0]

    @pl.when(pl.program_id(0) == 0)
    def _():
        strip = 2 * LANES
        for r0 in range(0, wa_ref.shape[0], strip):
            wa_ref[r0:r0 + strip, :] = w_ref[r0:r0 + strip, :].astype(BF16)
            wb_ref[r0:r0 + strip, :] = w_ref[W_QKV_OFF + r0:W_QKV_OFF + r0 + strip, :].astype(BF16)
    seq_tile = pl.program_id(0) % tiles_per_seq
    h = _rms(x_ref[...], g_ref[...], NORM_EPS).astype(BF16)

    def mm(w_ref, lo, hi):
        return lax.dot_general(h, w_ref[lo:hi, :], NT_DIMS, preferred_element_type=F32)

    tail = mm(wt_ref, 0, TAIL)
    tail_ref[...] = tail
    tailt_ref[...] = tail.T

    cos = cos_ref[...]
    sin = sin_ref[...]
    lane = lax.broadcasted_iota(jnp.int32, (tm, LANES), 1)
    first_half = (lane % NSA_HEAD_DIM) < (NSA_HEAD_DIM // 2)

    def rope(xc):
        partner = jnp.where(first_half, pltpu.roll(xc, LANES - 32, 1), pltpu.roll(xc, 32, 1))
        return xc * cos + partner * sin

    q = mm(wb_ref, 0, NSA_Q)
    scale = NSA_HEAD_DIM ** -0.5 * math.log2(math.e)
    for c in range(NSA_Q // LANES):
        rt = (rope(q[:, c * LANES:(c + 1) * LANES]) * scale).T.astype(BF16)
        qt_ref[2 * c] = rt[:NSA_HEAD_DIM]
        qt_ref[2 * c + 1] = rt[NSA_HEAD_DIM:]

    kv = mm(wb_ref, NSA_Q, NSA_Q + 6 * NSA_KV)

    def seg(i):
        return kv[:, i * NSA_KV:(i + 1) * NSA_KV]

    def rope_seg(x):
        return jnp.concatenate([rope(x[:, :LANES]), rope(x[:, LANES:])], axis=1)

    def store_heads(ref, x):
        for g in range(NSA_KV_HEADS):
            ref[g] = x[:, g * NSA_HEAD_DIM:(g + 1) * NSA_HEAD_DIM].astype(BF16)

    def store_heads_t(ref, x):
        xt = x.T.astype(BF16)
        for g in range(NSA_KV_HEADS):
            ref[g] = xt[g * NSA_HEAD_DIM:(g + 1) * NSA_HEAD_DIM]

    kc_tile = rope_seg(seg(0))
    vc_tile = seg(1)
    for c in range(NSA_KV // LANES):
        kbuf_ref[c] = kc_tile[:, c * LANES:(c + 1) * LANES]
        vbuf_ref[c] = vc_tile[:, c * LANES:(c + 1) * LANES]
    gw = CMP_STRIDE * NSA_HEAD_DIM
    for l in range(CMP_STRIDE):
        for c in range(NSA_KV // LANES):
            k_rows = kbuf_ref[c, pl.ds(l, tm // CMP_STRIDE, stride=CMP_STRIDE), :]
            v_rows = vbuf_ref[c, pl.ds(l, tm // CMP_STRIDE, stride=CMP_STRIDE), :]
            for half in range(LANES // NSA_HEAD_DIM):
                g = c * (LANES // NSA_HEAD_DIM) + half
                src = slice(half * NSA_HEAD_DIM, (half + 1) * NSA_HEAD_DIM)
                dst = slice(g * gw + l * NSA_HEAD_DIM, g * gw + (l + 1) * NSA_HEAD_DIM)
                kc_ref[:, dst] = k_rows[:, src]
                vc_ref[:, dst] = v_rows[:, src]
    pos = seq_tile * tm + lax.broadcasted_iota(jnp.int32, (tm, NSA_HEAD_DIM), 0)
    block_in_chunk = (pos // SLC_BLOCK) % (TK // SLC_BLOCK)
    onehot = jnp.where(lax.broadcasted_iota(jnp.int32, (tm, NSA_HEAD_DIM), 1) == block_in_chunk, 1.0, 0.0)
    ks = rope_seg(seg(2))
    for g in range(NSA_KV_HEADS):
        kg = ks[:, g * NSA_HEAD_DIM:(g + 1) * NSA_HEAD_DIM]
        ks_ref[g] = jnp.concatenate([kg, onehot], axis=1).astype(BF16)
    store_heads_t(vst_ref, seg(3))
    store_heads(kw_ref, rope_seg(seg(4)))
    store_heads_t(vwt_ref, seg(5))

    zs_ref[...] = _silu(mm(wa_ref, 0, SSM_D_INNER))
    xbc_ref[...] = mm(wa_ref, SSM_D_INNER, SSM_D_INNER + SSM_XBC)


def _in_proj(x2, norm_g, w_parts, cos_t, sin_t, seq):
    t = x2.shape[0]
    tm = TM_IN_PROJ
    nseq = seq // tm
    row = lambda i: (i, 0)
    const = lambda i: (0, 0)
    heads = lambda i: (0, i, 0)
    heads_t = lambda i: (0, 0, i)
    out_shape = (
        jax.ShapeDtypeStruct((t, SSM_D_INNER), F32),
        jax.ShapeDtypeStruct((t, SSM_XBC), F32),
        jax.ShapeDtypeStruct((NSA_HEADS, NSA_HEAD_DIM, t), BF16),
        jax.ShapeDtypeStruct((t // CMP_STRIDE, CMP_STRIDE * NSA_KV), F32),
        jax.ShapeDtypeStruct((t // CMP_STRIDE, CMP_STRIDE * NSA_KV), F32),
        jax.ShapeDtypeStruct((NSA_KV_HEADS, t, KSEL), BF16),
        jax.ShapeDtypeStruct((NSA_KV_HEADS, NSA_HEAD_DIM, t), BF16),
        jax.ShapeDtypeStruct((NSA_KV_HEADS, t, NSA_HEAD_DIM), BF16),
        jax.ShapeDtypeStruct((NSA_KV_HEADS, NSA_HEAD_DIM, t), BF16),
        jax.ShapeDtypeStruct((t, TAIL), F32),
        jax.ShapeDtypeStruct((TAIL, t), F32),
    )
    kvh = pl.BlockSpec((NSA_KV_HEADS, tm, NSA_HEAD_DIM), heads)
    kvh_t = pl.BlockSpec((NSA_KV_HEADS, NSA_HEAD_DIM, tm), heads_t)
    out_specs = (
        pl.BlockSpec((tm, SSM_D_INNER), row),
        pl.BlockSpec((tm, SSM_XBC), row),
        pl.BlockSpec((NSA_HEADS, NSA_HEAD_DIM, tm), heads_t),
        pl.BlockSpec((tm // CMP_STRIDE, CMP_STRIDE * NSA_KV), row),
        pl.BlockSpec((tm // CMP_STRIDE, CMP_STRIDE * NSA_KV), row),
        pl.BlockSpec((NSA_KV_HEADS, tm, KSEL), heads), kvh_t, kvh, kvh_t,
        pl.BlockSpec((tm, TAIL), row),
        pl.BlockSpec((TAIL, tm), lambda i: (0, i)),
    )
    in_specs = [
        pl.BlockSpec((tm, D_MODEL), row),
        pl.BlockSpec((1, D_MODEL), const),
        pl.BlockSpec(w_parts[0].shape, const, pipeline_mode=pl.Buffered(1)),
        pl.BlockSpec((TAIL, D_MODEL), const),
        pl.BlockSpec((tm, LANES), lambda i: (i % nseq, 0)),
        pl.BlockSpec((tm, LANES), lambda i: (i % nseq, 0)),
    ]
    return pl.pallas_call(
        functools.partial(_inproj_kernel, tiles_per_seq=nseq), grid=(t // tm,), in_specs=in_specs,
        out_specs=out_specs,
        out_shape=out_shape,
        scratch_shapes=[pltpu.VMEM((NSA_KV // LANES, tm, LANES), F32),
                        pltpu.VMEM((NSA_KV // LANES, tm, LANES), F32),
                        pltpu.VMEM((SSM_D_INNER + SSM_XBC, D_MODEL), BF16),
                        pltpu.VMEM((NSA_Q + 6 * NSA_KV, D_MODEL), BF16)],
        compiler_params=_cparams(("arbitrary",)), name="in_proj",
    )(x2, norm_g, *w_parts, cos_t, sin_t)


def _ssd_kernel(xbc_ref, zs_ref, tail_ref, tailt_ref, cw_ref, cb_ref, dtb_ref, dtbt_ref,
                alog_ref, alogt_ref, dskip_ref, ng_ref,
                y_ref, xcat_ref, state_ref, ybuf_ref):
    L, P, N, H, G = SSM_CHUNK, SSM_HEAD_DIM, SSM_STATE, SSM_HEADS, SSM_GROUPS
    HG = H // G
    halo = SUBLANES

    @pl.when(pl.program_id(1) == 0)
    def _():
        xcat_ref[0:halo, :] = jnp.zeros((halo, SSM_XBC), F32)
        state_ref[...] = jnp.zeros_like(state_ref)

    xcat_ref[halo:halo + L, :] = xbc_ref[...]
    xcat = xcat_ref[...]
    conv = cb_ref[...] + cw_ref[SSM_CONV - 1:SSM_CONV, :] * xcat[halo:halo + L]
    for k in range(SSM_CONV - 1):
        shifted = pltpu.roll(xcat, SSM_CONV - 1 - k, 0)
        conv = conv + cw_ref[k:k + 1, :] * shifted[halo:halo + L]
    xcat_ref[0:halo, :] = xcat[L:L + halo]
    u = _silu(conv)
    xs = u[:, :SSM_D_INNER]
    bm = u[:, SSM_D_INNER:SSM_D_INNER + G * N]
    cm = u[:, SSM_D_INNER + G * N:]

    def softplus(v):
        return jnp.maximum(v, 0.0) + jnp.log1p(jnp.exp(-jnp.abs(v)))

    dt = softplus(tail_ref[:, 0:H] + dtb_ref[...])
    dtt = softplus(tailt_ref[0:H, :] + dtbt_ref[...])
    da = dt * (-jnp.exp(alog_ref[...]))
    dat = dtt * (-jnp.exp(alogt_ref[...]))
    ri = lax.broadcasted_iota(jnp.int32, (L, L), 0)
    ci = lax.broadcasted_iota(jnp.int32, (L, L), 1)
    tri = ci <= ri
    hi = lax.Precision.HIGHEST
    acs = jnp.dot(tri.astype(F32), da, precision=hi, preferred_element_type=F32)
    acst = jnp.dot(dat, (ri <= ci).astype(F32), precision=hi, preferred_element_type=F32)
    last = acs[L - 1:L, :]
    w_state = dt * jnp.exp(last - acs)
    eacs = jnp.exp(acs)
    cdec = jnp.exp(last)

    hrow = lax.broadcasted_iota(jnp.int32, (H, SSM_D_INNER), 0)
    hcol = lax.broadcasted_iota(jnp.int32, (H, SSM_D_INNER), 1)
    expand = jnp.where(hcol // P == hrow, 1.0, 0.0).astype(BF16)

    def split3(v):
        v_hi = v.astype(BF16)
        r1 = v - v_hi.astype(F32)
        v_mid = r1.astype(BF16)
        v_lo = (r1 - v_mid.astype(F32)).astype(BF16)
        return jnp.concatenate([v_hi, v_mid, v_lo], axis=1)

    small = jnp.concatenate([cdec, dskip_ref[...], jnp.zeros((6, H), F32)], axis=0)
    per_head = jnp.concatenate([dt, w_state, eacs, small], axis=0)
    spread = jnp.dot(split3(per_head), jnp.concatenate([expand] * 3, axis=0), preferred_element_type=F32)
    cdec_e = spread[3 * L:3 * L + 1, :]
    dskip_e = spread[3 * L + 1:3 * L + 2, :]
    xdt = (xs * spread[0:L]).astype(BF16)
    wst = (xs * spread[L:2 * L]).astype(BF16)
    eacs_e = spread[2 * L:3 * L]

    for g in range(G):
        bm_g = bm[:, g * N:(g + 1) * N]
        cm_g = cm[:, g * N:(g + 1) * N].astype(BF16)
        cb = lax.dot_general(cm_g, bm_g.astype(BF16), NT_DIMS, preferred_element_type=F32)
        cols = slice(g * HG * P, (g + 1) * HG * P)
        st = state_ref[:, cols]
        y_off = jnp.dot(cm_g, st.astype(BF16), preferred_element_type=F32) * eacs_e[:, cols]
        ybuf_ref[:, cols] = y_off
        bmt = bm_g.T.astype(BF16)
        state_ref[:, cols] = st * cdec_e[:, cols] + jnp.dot(bmt, wst[:, cols], preferred_element_type=F32)
        for r in range(HG):
            hh = g * HG + r
            diff = acs[:, hh:hh + 1] - acst[hh:hh + 1, :]
            seg = jnp.exp(jnp.where(tri, diff, -jnp.inf))
            lmat = (cb * seg).astype(BF16)
            hc = slice(hh * P, (hh + 1) * P)
            ybuf_ref[:, hc] = ybuf_ref[:, hc] + jnp.dot(lmat, xdt[:, hc], preferred_element_type=F32)

    y = (ybuf_ref[...] + xs * dskip_e) * zs_ref[...]
    gw = SSM_D_INNER // G
    parts = []
    for g in range(G):
        yg = y[:, g * gw:(g + 1) * gw]
        parts.append(yg * lax.rsqrt(jnp.mean(yg * yg, axis=-1, keepdims=True) + SSM_NORM_EPS))
    y_ref[...] = (jnp.concatenate(parts, axis=1) * ng_ref[...]).astype(y_ref.dtype)


def _ssd(xbc, zs, tail, tailt, conv_w, conv_b, dt_bias, a_log, d_skip, norm_g, batch, seq):
    t = xbc.shape[0]
    L = SSM_CHUNK
    nc = seq // L
    row = lambda b, c: (b * nc + c, 0)
    const = lambda b, c: (0, 0)
    H = SSM_HEADS
    in_specs = [
        pl.BlockSpec((L, SSM_XBC), row),
        pl.BlockSpec((L, SSM_D_INNER), row),
        pl.BlockSpec((L, TAIL), row),
        pl.BlockSpec((TAIL, L), lambda b, c: (0, b * nc + c)),
        pl.BlockSpec((SSM_CONV, SSM_XBC), const),
        pl.BlockSpec((1, SSM_XBC), const),
        pl.BlockSpec((1, H), const),
        pl.BlockSpec((H, 1), const),
        pl.BlockSpec((1, H), const),
        pl.BlockSpec((H, 1), const),
        pl.BlockSpec((1, H), const),
        pl.BlockSpec((1, SSM_D_INNER), const),
    ]
    return pl.pallas_call(
        _ssd_kernel, grid=(batch, nc), in_specs=in_specs,
        out_specs=pl.BlockSpec((L, SSM_D_INNER), row),
        out_shape=jax.ShapeDtypeStruct((t, SSM_D_INNER), BF16),
        scratch_shapes=[pltpu.VMEM((L + SUBLANES, SSM_XBC), F32),
                        pltpu.VMEM((SSM_STATE, SSM_D_INNER), F32),
                        pltpu.VMEM((L, SSM_D_INNER), F32)],
        compiler_params=_cparams(("arbitrary", "arbitrary")), name="ssd",
    )(xbc, zs, tail, tailt, conv_w, conv_b, dt_bias.reshape(1, H), dt_bias.reshape(H, 1),
      a_log.reshape(1, H), a_log.reshape(H, 1), d_skip.reshape(1, H), norm_g.reshape(1, -1))


def _compress_kernel(k_ref, v_ref, kpos_ref, vpos_ref, kw1_ref, vw1_ref, kb1_ref, vb1_ref,
                     kw2_ref, vw2_ref, kc_ref, vct_ref):
    D = NSA_HEAD_DIM
    gw = CMP_STRIDE * D

    def hidden(t_ref, pos_ref, w1_ref, b1_ref, g):
        t = t_ref[:, g * gw:(g + 1) * gw]
        n = t.shape[0]
        lo = jnp.dot((t + pos_ref[0:1, :]).astype(BF16), w1_ref[...], preferred_element_type=F32)
        hi = jnp.dot((t + pos_ref[1:2, :]).astype(BF16), w1_ref[...], preferred_element_type=F32)
        pre = lo + pltpu.roll(pltpu.roll(hi, n - 1, 0), D, 1)
        return _silu(pre[:, 0:D] + b1_ref[...]).astype(BF16)

    for g in range(NSA_KV_HEADS):
        kc = jnp.dot(hidden(k_ref, kpos_ref, kw1_ref, kb1_ref, g), kw2_ref[...], preferred_element_type=F32)
        kc_ref[g] = kc.astype(BF16)
        vct = lax.dot_general(vw2_ref[...], hidden(v_ref, vpos_ref, vw1_ref, vb1_ref, g), NT_DIMS,
                              preferred_element_type=F32)
        vct_ref[g] = vct.astype(BF16)


def _compress_weights(pos, w1, b1, w2):
    half = CMP_BLOCK // 2 * NSA_HEAD_DIM
    w1cat = jnp.concatenate([w1[:half], w1[half:]], axis=1).astype(BF16)
    return pos.reshape(2, half), w1cat, b1.reshape(1, -1), w2.astype(BF16)


def _compress(k16, v16, kparams, vparams, batch, seq):
    rows = seq // CMP_STRIDE
    width = CMP_STRIDE * NSA_KV
    D = NSA_HEAD_DIM
    gw = CMP_STRIDE * D
    kpos, kw1, kb1, kw2 = _compress_weights(*kparams)
    vpos, vw1, vb1, vw2 = _compress_weights(*vparams)
    vw2 = vw2.T
    c2 = lambda b: (0, 0)
    tok = pl.BlockSpec((rows, width), lambda b: (b, 0))
    in_specs = [tok, tok,
                pl.BlockSpec((2, gw), c2), pl.BlockSpec((2, gw), c2),
                pl.BlockSpec((gw, 2 * D), c2), pl.BlockSpec((gw, 2 * D), c2),
                pl.BlockSpec((1, D), c2), pl.BlockSpec((1, D), c2),
                pl.BlockSpec((D, D), c2), pl.BlockSpec((D, D), c2)]
    out = jax.ShapeDtypeStruct((NSA_KV_HEADS, batch * rows, D), BF16)
    out_t = jax.ShapeDtypeStruct((NSA_KV_HEADS, D, batch * rows), BF16)
    ospec = pl.BlockSpec((NSA_KV_HEADS, rows, D), lambda b: (0, b, 0))
    ospec_t = pl.BlockSpec((NSA_KV_HEADS, D, rows), lambda b: (0, 0, b))
    return pl.pallas_call(
        _compress_kernel, grid=(batch,), in_specs=in_specs, out_specs=(ospec, ospec_t),
        out_shape=(out, out_t), compiler_params=_cparams(("arbitrary",)), name="compress",
    )(k16, v16, kpos, vpos, kw1, vw1, kb1, vb1, kw2, vw2)


def _nsa_kernel(qt_ref, kc_ref, vct_ref, ks_ref, vst_ref, kw_ref, vwt_ref, tailt_ref, ovt_ref, mtab_ref, cmask_ref,
                o_ref, biasq_ref, m_ref, acc_ref, part_ref, sbuf0_ref, sbuf1_ref, mbuf0_ref, mbuf1_ref):
    R, D = NSA_REP, NSA_HEAD_DIM
    nl = R * TQ
    grp = range(kc_ref.shape[0])
    sbuf_refs = (sbuf0_ref, sbuf1_ref)
    mbuf_refs = (mbuf0_ref, mbuf1_ref)
    n_stages = (ks_ref.shape[1] // TK - 1) // 2
    gp = pl.program_id(1)
    qi = pl.program_id(2)
    t0 = qi * TQ
    qt = [jnp.concatenate([qt_ref[gi * R + r] for r in range(R)], axis=1) for gi in grp]
    tpos_row = t0 + lax.broadcasted_iota(jnp.int32, (1, nl), 1) % TQ

    def k_rows(k_ref, gi, start):
        return k_ref[gi, pl.ds(pl.multiple_of(start, TK), TK), :]

    def vt_cols(vt_ref, gi, start, n):
        return vt_ref[gi, :, pl.ds(pl.multiple_of(start, TK), n)]

    def pv(vt, p):
        vt1 = jnp.concatenate([vt, jnp.ones((ONES_ROWS, vt.shape[1]), BF16)], axis=0)
        return jnp.dot(vt1, p, preferred_element_type=F32)

    def mask_tile(off):
        return jnp.concatenate([mtab_ref[pl.ds(pl.multiple_of(off, TK), TK), :]] * R, axis=1)

    def col_max(s):
        return jnp.max(s, axis=0, keepdims=True)

    def online(state, s, m_chunk, vt):
        m_new = m_chunk if state is None else jnp.maximum(state[0], m_chunk)
        contrib = pv(vt, jnp.exp2(s - m_new).astype(BF16))
        if state is None:
            return m_new, contrib
        return m_new, jnp.exp2(state[0] - m_new) * state[1] + contrib

    causal = mask_tile(MT_CAUSAL)
    far0 = jnp.maximum(t0 - 2 * TK, 0)
    mid0 = jnp.maximum(t0 - TK, 0)
    window = []
    for gi in grp:
        s_dia = jnp.dot(k_rows(kw_ref, gi, t0), qt[gi], preferred_element_type=F32) + causal
        m_dia = col_max(s_dia)
        s_mid = (jnp.dot(k_rows(kw_ref, gi, mid0), qt[gi], preferred_element_type=F32)
                 + mask_tile(jnp.where(qi >= 1, MT_ALL, MT_NONE)))
        m_mid = col_max(s_mid)
        s_far = (jnp.dot(k_rows(kw_ref, gi, far0), qt[gi], preferred_element_type=F32)
                 + mask_tile(jnp.where(qi >= 2, MT_FAR, MT_NONE)))
        m_far = col_max(s_far)
        window.append(((s_dia, m_dia, t0), (s_mid, m_mid, mid0), (s_far, m_far, far0)))

    ncr = kc_ref.shape[1]
    n_slc = ovt_ref.shape[0]
    cmask = cmask_ref[pl.ds(pl.multiple_of(qi * ncr, ncr), ncr), :]
    cmask = jnp.concatenate([cmask] * R, axis=1)
    sees_any = tpos_row >= CMP_BLOCK - 1
    jj = lax.broadcasted_iota(jnp.int32, (n_slc, TQ), 0)
    tt = t0 + lax.broadcasted_iota(jnp.int32, (n_slc, TQ), 1)
    lag = tt // SLC_BLOCK - jj
    forced = (jj == 0) | ((lag >= 0) & (lag < N_LOCAL_BLOCKS))
    valid = jj * SLC_BLOCK <= tt
    rows8 = SUBLANES
    j8 = lax.broadcasted_iota(jnp.int32, (rows8, TQ), 0)
    per = TK // SLC_BLOCK
    zrows = jnp.zeros((BIAS_ROWS - per, TQ), F32)
    o_c, o_w = [], []
    for gi in grp:
        s_m = jnp.dot(kc_ref[gi], qt[gi], preferred_element_type=F32) + cmask
        mx = jnp.max(s_m, axis=0, keepdims=True)
        p = jnp.exp2(s_m - mx)
        den = jnp.sum(p, axis=0, keepdims=True)
        pc = p * jnp.where(sees_any, 1.0 / den, 0.0)
        o_c.append(jnp.dot(vct_ref[gi], pc.astype(BF16), preferred_element_type=F32))

        psum = pc[:, 0:TQ]
        for r in range(1, R):
            psum = psum + pc[:, r * TQ:(r + 1) * TQ]
        imp_t = jnp.dot(ovt_ref[...], psum, precision=lax.Precision.HIGHEST,
                        preferred_element_type=F32)
        score = jnp.where(forced, FORCED_SCORE, jnp.where(valid, imp_t, -1.0))
        groups = [score[a:a + rows8] for a in range(0, n_slc, rows8)]
        ranks = [jnp.zeros((rows8, TQ), F32) for _ in groups]
        window_steps = dict(zip((0, n_slc // 3, 2 * n_slc // 3), window[gi]))
        win = None
        for j2 in range(n_slc):
            if j2 in window_steps:
                s_w, m_w, start_w = window_steps[j2]
                win = online(win, s_w, m_w, vt_cols(vwt_ref, gi, start_w, TK))
            sj = score[j2:j2 + 1, :]
            for a, sg in enumerate(groups):
                lo = a * rows8
                if lo + rows8 - 1 < j2:
                    beats = sj > sg
                elif lo > j2:
                    beats = sj >= sg
                else:
                    beats = (sj > sg) | ((sj == sg) & (j8 + lo > j2))
                ranks[a] = ranks[a] + jnp.where(beats, 1.0, 0.0)
        rank = jnp.concatenate(ranks, axis=0)
        selected = (rank < float(min(SLC_TOP_N, n_slc))) & (score >= 0.0)
        bias_t = jnp.where(selected, 0.0, NEG_BIG)
        for kb in range(n_slc // per):
            blk = jnp.concatenate([bias_t[kb * per:(kb + 1) * per], zrows], axis=0)
            biasq_ref[gi, kb * BIAS_ROWS:(kb + 1) * BIAS_ROWS, :] = (
                jnp.concatenate([blk] * R, axis=1).astype(BF16))
        accw = win[1]
        o_w.append(accw[0:D] / accw[D:D + 1])
    qpad = jnp.zeros((KSEL - D - BIAS_ROWS, nl), BF16)

    def q_sel(gi, kb):
        rows = biasq_ref[gi, pl.ds(pl.multiple_of(kb * BIAS_ROWS, BIAS_ROWS), BIAS_ROWS), :]
        return jnp.concatenate([qt[gi], rows, qpad], axis=0)

    def slc_scores(gi, kb):
        return jnp.dot(k_rows(ks_ref, gi, kb * TK), q_sel(gi, kb), preferred_element_type=F32)

    def fill(gi, slot, chunks):
        for c, kb in enumerate(chunks):
            s = slc_scores(gi, kb)
            sbuf_refs[slot][gi, c * TK:(c + 1) * TK, :] = s
            mbuf_refs[slot][gi, c:c + 1, :] = col_max(s)

    npairs = qi // 2
    diag = []
    for gi in grp:
        s_sel = slc_scores(gi, qi) + causal
        diag.append((s_sel, col_max(s_sel)))

    def gates(gi):
        gate_row = pl.multiple_of(GATE_OFF + GATE_SLOT * (gp * len(grp) + gi), SUBLANES)
        sig = jax.nn.sigmoid(tailt_ref[pl.ds(gate_row, GATE_SLOT), :])
        return [jnp.concatenate([sig[3 * r + c:3 * r + c + 1, :] for r in range(R)], axis=1)
                for c in range(3)]

    for gi in grp:
        gate = gates(gi)
        part_ref[gi] = gate[0] * o_c[gi] + gate[2] * o_w[gi]

    for gi in grp:
        fill(gi, 0, (0, 1))
        m_sel, acc_sel = online(None, diag[gi][0], diag[gi][1], vt_cols(vst_ref, gi, t0, TK))
        m_ref[gi] = m_sel
        acc_ref[gi] = acc_sel

    def absorb(gi, slot, first, n):
        state = (m_ref[gi], acc_ref[gi])
        for c in range(n):
            state = online(state, sbuf_refs[slot][gi, c * TK:(c + 1) * TK, :], mbuf_refs[slot][gi, c:c + 1, :],
                           vt_cols(vst_ref, gi, (first + c) * TK, TK))
        m_ref[gi] = state[0]
        acc_ref[gi] = state[1]

    for k in range(n_stages):
        @pl.when(npairs > k)
        def _(k=k):
            for gi in grp:
                fill(gi, (k + 1) % 2, (jnp.minimum(2 * k + 2, qi - 1), jnp.minimum(2 * k + 3, qi - 1)))
                absorb(gi, k % 2, 2 * k, 2)

    for parity in range(2):
        @pl.when((qi % 2 == 1) & (npairs % 2 == parity))
        def _(parity=parity):
            for gi in grp:
                absorb(gi, parity, qi - 1, 1)

    for gi in grp:
        acc = acc_ref[gi]
        ot = part_ref[gi] + gates(gi)[1] * (acc[0:D] / acc[D:D + 1])
        stacked = jnp.concatenate([ot[:, r * TQ:(r + 1) * TQ] for r in range(R)], axis=0)
        o_ref[:, gi * R * D:(gi + 1) * R * D] = stacked.T.astype(o_ref.dtype)


def _overlap_t(seq):
    n_cmp = (seq - CMP_BLOCK) // CMP_STRIDE + 1
    n_slc = seq // SLC_BLOCK
    cs = np.arange(n_cmp) * CMP_STRIDE
    ss = np.arange(n_slc) * SLC_BLOCK
    overlap = np.clip(np.minimum(cs[:, None] + CMP_BLOCK, ss[None, :] + SLC_BLOCK)
                      - np.maximum(cs[:, None], ss[None, :]), 0, None) / CMP_BLOCK
    ovt = np.zeros((n_slc, n_cmp + 1), np.float32)
    ovt[:, :n_cmp] = overlap.T
    return jnp.asarray(ovt)


def _mask_tiles():
    ki = np.arange(TK)[:, None]
    qi = np.arange(TQ)[None, :]
    neg = np.float32(NEG_BIG)
    none = np.full((TK, TQ), neg, np.float32)
    far = np.where(ki > qi, np.float32(0), neg)
    full = np.zeros((TK, TQ), np.float32)
    causal = np.where(ki <= qi, np.float32(0), neg)
    return jnp.asarray(np.concatenate([none, far, full, causal], axis=0))


def _cmp_masks(seq):
    ncr = seq // CMP_STRIDE
    n_cmp = (seq - CMP_BLOCK) // CMP_STRIDE + 1
    n = np.arange(ncr)[None, :, None]
    t = (np.arange(seq // TQ)[:, None, None] * TQ + np.arange(TQ)[None, None, :])
    visible = (n * CMP_STRIDE + CMP_BLOCK - 1 <= t) & (n < n_cmp)
    return jnp.asarray(np.where(visible, np.float32(0), np.float32(NEG_BIG)).reshape(-1, TQ))


def _nsa(qt, kc, vct, ks, vst, kw, vwt, tailt, batch, seq):
    t = batch * seq
    G, R, D = NSA_KV_HEADS, NSA_REP, NSA_HEAD_DIM
    nq = seq // TQ
    ncr = seq // CMP_STRIDE
    n_slc = seq // SLC_BLOCK
    assert TQ == TK and WINDOW == 2 * TK, "window branch visits exactly three key chunks"
    assert TK // SLC_BLOCK <= BIAS_ROWS and D + BIAS_ROWS <= KSEL
    P = NSA_GROUPS_PER_STEP
    const = lambda b, g, i: (0, 0)
    vtspec = pl.BlockSpec((P, D, seq), lambda b, g, i: (g, 0, b))
    in_specs = [
        pl.BlockSpec((P * R, D, TQ), lambda b, g, i: (g, 0, b * nq + i)),
        pl.BlockSpec((P, ncr, D), lambda b, g, i: (g, b, 0)),
        pl.BlockSpec((P, D, ncr), lambda b, g, i: (g, 0, b)),
        pl.BlockSpec((P, seq, KSEL), lambda b, g, i: (g, b, 0)),
        vtspec,
        pl.BlockSpec((P, seq, D), lambda b, g, i: (g, b, 0)),
        vtspec,
        pl.BlockSpec((TAIL, TQ), lambda b, g, i: (0, b * nq + i)),
        pl.BlockSpec((n_slc, ncr), const),
        pl.BlockSpec((4 * TK, TQ), const),
        pl.BlockSpec((nq * ncr, TQ), const),
    ]
    return pl.pallas_call(
        _nsa_kernel, grid=(batch, G // P, nq), in_specs=in_specs,
        out_specs=pl.BlockSpec((TQ, P * R * D), lambda b, g, i: (b * nq + i, g)),
        out_shape=jax.ShapeDtypeStruct((t, NSA_Q), BF16),
        scratch_shapes=[pltpu.VMEM((P, seq // TK * BIAS_ROWS, R * TQ), BF16),
                        pltpu.VMEM((P, 1, R * TQ), F32),
                        pltpu.VMEM((P, D + ONES_ROWS, R * TQ), F32),
                        pltpu.VMEM((P, D, R * TQ), F32),
                        pltpu.VMEM((P, 2 * TK, R * TQ), F32),
                        pltpu.VMEM((P, 2 * TK, R * TQ), F32),
                        pltpu.VMEM((P, SUBLANES, R * TQ), F32),
                        pltpu.VMEM((P, SUBLANES, R * TQ), F32)],
        compiler_params=_cparams(("arbitrary", "arbitrary", "arbitrary")), name="nsa",
    )(qt, kc, vct, ks, vst, kw, vwt, tailt, _overlap_t(seq), _mask_tiles(), _cmp_masks(seq))


def _outproj_kernel(y_ref, o_ref, x_ref, w_ref, ag_ref, ng_ref, x1_ref, h2_ref):
    yn = _rms(o_ref[...].astype(F32), ag_ref[...], NORM_EPS).astype(BF16)
    x1 = (x_ref[...]
          + jnp.dot(y_ref[...], w_ref[0:SSM_D_INNER, :], preferred_element_type=F32)
          + jnp.dot(yn, w_ref[SSM_D_INNER:, :], preferred_element_type=F32))
    x1_ref[...] = x1
    h2_ref[...] = _rms(x1, ng_ref[...], NORM_EPS).astype(BF16)


def _out_proj(y_ssm, o_nsa, x2, w_out, attn_g, norm2_g):
    t = x2.shape[0]
    tm = TM_OUT_PROJ
    row = lambda i: (i, 0)
    const = lambda i: (0, 0)
    tok = pl.BlockSpec((tm, D_MODEL), row)
    vec = pl.BlockSpec((1, D_MODEL), const)
    return pl.pallas_call(
        _outproj_kernel, grid=(t // tm,),
        in_specs=[tok, tok, tok, pl.BlockSpec((SSM_D_INNER + NSA_Q, D_MODEL), const), vec, vec],
        out_specs=(tok, tok),
        out_shape=(jax.ShapeDtypeStruct((t, D_MODEL), F32), jax.ShapeDtypeStruct((t, D_MODEL), BF16)),
        compiler_params=_cparams(("arbitrary",)), name="out_proj",
    )(y_ssm, o_nsa, x2, w_out, attn_g, norm2_g)


def _ffn_kernel(h_ref, halo_ref, x1_ref, wup_ref, cw_ref, cb_ref, wd_ref, fg_ref, out_ref, act_ref,
                *, tiles_per_seq):
    i = pl.program_id(0)
    tm = h_ref.shape[0]
    pad = halo_ref.shape[0]
    tn = TN_FFN
    halo = halo_ref[...]
    halo = jnp.where(i % tiles_per_seq == 0, jnp.zeros_like(halo), halo)
    hc = jnp.concatenate([halo, h_ref[...]], axis=0)

    def branch(c0):
        cols = slice(c0, c0 + tn)
        u = jnp.dot(hc, wup_ref[:, cols], preferred_element_type=F32)
        out = cb_ref[:, cols] + cw_ref[FFN_CONV - 1:FFN_CONV, cols] * u[pad:pad + tm, :]
        for k in range(FFN_CONV - 1):
            shifted = pltpu.roll(u, FFN_CONV - 1 - k, 0)
            out = out + cw_ref[k:k + 1, cols] * shifted[pad:pad + tm, :]
        return out

    for j in range(D_FF // tn):
        act = _silu(branch(j * tn)) * branch(D_FF + j * tn)
        act_ref[:, j * tn:(j + 1) * tn] = act.astype(BF16)
    down = jnp.dot(act_ref[...], wd_ref[...], preferred_element_type=F32)
    out_ref[...] = _rms(x1_ref[...] + down, fg_ref[...], NORM_EPS)


def _ffn(h2, x1, w_up, conv_w, conv_b, w_down, final_g, seq):
    t = h2.shape[0]
    tm = TM_FFN
    pad = BF16_SUBLANES
    tok = pl.BlockSpec((tm, D_MODEL), lambda i: (i, 0))

    def resident(shape):
        return pl.BlockSpec(shape, lambda i: (0, 0), pipeline_mode=pl.Buffered(1))

    in_specs = [
        tok,
        pl.BlockSpec((pad, D_MODEL), lambda i: (jnp.maximum(i * (tm // pad) - 1, 0), 0)),
        tok,
        resident((D_MODEL, 2 * D_FF)),
        resident((FFN_CONV, 2 * D_FF)),
        resident((1, 2 * D_FF)),
        resident((D_FF, D_MODEL)),
        resident((1, D_MODEL)),
    ]
    return pl.pallas_call(
        functools.partial(_ffn_kernel, tiles_per_seq=seq // tm), grid=(t // tm,),
        in_specs=in_specs, out_specs=tok,
        out_shape=jax.ShapeDtypeStruct((t, D_MODEL), F32),
        scratch_shapes=[pltpu.VMEM((tm, D_FF), BF16)],
        compiler_params=_cparams(("arbitrary",)), name="ffn",
    )(h2, h2, x1, w_up, conv_w, conv_b, w_down, final_g)


def _rope_tables(seq):
    half = NSA_HEAD_DIM // 2
    inv_freq = 1.0 / (ROPE_THETA ** (jnp.arange(0, NSA_HEAD_DIM, 2, dtype=F32) / NSA_HEAD_DIM))
    ang = jnp.arange(seq).astype(F32)[:, None] * inv_freq[None, :]
    cos, sin = jnp.cos(ang), jnp.sin(ang)
    reps = LANES // NSA_HEAD_DIM
    cos_t = jnp.tile(jnp.concatenate([cos, cos], axis=1), (1, reps))
    sin_t = jnp.tile(jnp.concatenate([-sin, sin], axis=1), (1, reps))
    return cos_t, sin_t


def _split_w_in(w):
    o_z, o_xbc, o_dt, o_q, o_kv, o_g = np.cumsum([0, SSM_D_INNER, SSM_XBC, SSM_HEADS, NSA_Q, 6 * NSA_KV]).tolist()
    assert (o_dt, o_q, o_g - o_q) == (SSM_D_INNER + SSM_XBC, W_QKV_OFF, NSA_Q + 6 * NSA_KV)
    wt = w.T
    per = 3 * NSA_REP
    rows = np.zeros((TAIL,), np.int32)
    keep = np.zeros((TAIL, 1), np.float32)
    rows[:SSM_HEADS] = np.arange(o_dt, o_q)
    keep[:SSM_HEADS] = 1.0
    for g in range(NSA_KV_HEADS):
        lo = GATE_OFF + g * GATE_SLOT
        rows[lo:lo + per] = np.arange(o_g + g * per, o_g + (g + 1) * per)
        keep[lo:lo + per] = 1.0
    tail = jnp.take(wt, jnp.asarray(rows), axis=0) * jnp.asarray(keep)
    return wt, tail.astype(BF16)


def kernel(x, norm1_g, w_in, ssm_conv_w, ssm_conv_b, ssm_dt_bias, ssm_a_log, ssm_d, ssm_norm_g, cmp_k_pos, cmp_k_w1, cmp_k_b1, cmp_k_w2, cmp_v_pos, cmp_v_w1, cmp_v_b1, cmp_v_w2, attn_norm_g, w_out, norm2_g, ffn_w_up, ffn_conv_w, ffn_conv_b, ffn_w_down, final_norm_g):
    batch, seq, d = x.shape
    assert w_in.shape[0] == 1, "single-layer problem"
    l = 0
    cos_t, sin_t = _rope_tables(seq)
    x2 = x.reshape(batch * seq, d)
    zs, xbc, qt, kc_r, vc_r, ks, vst, kw, vwt, tail, tailt = _in_proj(
        x2, norm1_g[l].reshape(1, d), _split_w_in(w_in.reshape(d, -1)), cos_t, sin_t, seq)
    y_ssm = _ssd(xbc, zs, tail, tailt, ssm_conv_w[l], ssm_conv_b[l].reshape(1, -1), ssm_dt_bias[l],
                 ssm_a_log[l], ssm_d[l], ssm_norm_g[l], batch, seq)
    kc, vct = _compress(kc_r, vc_r,
                        (cmp_k_pos[l], cmp_k_w1[l], cmp_k_b1[l], cmp_k_w2[l]),
                        (cmp_v_pos[l], cmp_v_w1[l], cmp_v_b1[l], cmp_v_w2[l]), batch, seq)
    o_nsa = _nsa(qt, kc, vct, ks, vst, kw, vwt, tailt, batch, seq)
    x1, h2 = _out_proj(y_ssm, o_nsa, x2, w_out.reshape(-1, d).astype(BF16), attn_norm_g[l].reshape(1, d),
                       norm2_g[l].reshape(1, d))
    out = _ffn(h2, x1, ffn_w_up.reshape(d, -1).astype(BF16), ffn_conv_w[l], ffn_conv_b[l].reshape(1, -1),
               ffn_w_down.reshape(-1, d).astype(BF16), final_norm_g.reshape(1, d), seq)
    return out.reshape(batch, seq, d)
```
